```python
import math
import jax, jax.numpy as jnp
from jax import lax
import numpy as np

D_MODEL = 1024
BATCH = 2
SEQ = 8192
DEPTH = 4

N_HEADS = 16
HEAD_DIM = D_MODEL // N_HEADS
ATTN_SCALE = HEAD_DIM ** -0.5
ROPE_THETA = 10000.0
N_MIXERS = 3
PLE_DIM = 256
MAX_POS_OFFSET = 1024

FOX_Q_BLOCK = 128
FGATE_BIAS_LO = 1.0
FGATE_BIAS_HI = 6.0

NSA_Q_BLOCK = 64
NSA_KV_GROUPS = 4
NSA_HEADS_PER_GROUP = N_HEADS // NSA_KV_GROUPS
NSA_CMP_LEN = 32
NSA_CMP_STRIDE = 16
NSA_CMP_HIDDEN = 2 * HEAD_DIM
NSA_SEL_LEN = 64
NSA_SEL_TOPN = 16
NSA_WINDOW = 512
NSA_FORCE_BONUS = 1e4

MOBA_Q_BLOCK = 32
MOBA_BLOCK = 256
MOBA_TOPK = 3

N_EXPERTS = 64
EXPERT_DIM = 256
TOP_K = 8
N_GROUPS = 8
TOPK_GROUPS = 4
ROUTED_SCALE = 2.5
SHARED_DIM = 256
MOE_BLOCK = 256

DN_ALPHA = (2 * DEPTH) ** 0.25
DN_BETA = (8 * DEPTH) ** -0.25
LN_EPS = 1e-5
NEG = -1e30

N_FOX = (DEPTH + 2) // 3
N_NSA = (DEPTH + 1) // 3
N_MOBA = DEPTH // 3

kernel_name = "hybrid_fox_nsa_moba_moe_deepnorm"


def layer_norm(x, g, b):
    xf = x.astype(jnp.float32)
    mu = jnp.mean(xf, -1, keepdims=True)
    var = jnp.mean(jnp.square(xf - mu), -1, keepdims=True)
    return ((xf - mu) * lax.rsqrt(var + LN_EPS) * g + b).astype(x.dtype)


def rope_tables(positions):
    inv = 1.0 / (ROPE_THETA ** (jnp.arange(0, HEAD_DIM, 2, dtype=jnp.float32) / HEAD_DIM))
    ang = positions.astype(jnp.float32)[..., None] * inv
    return jnp.cos(ang), jnp.sin(ang)


def apply_rope(x, cos, sin):
    x1, x2 = jnp.split(x.astype(jnp.float32), 2, axis=-1)
    c = cos[:, :, None, :]
    s = sin[:, :, None, :]
    return jnp.concatenate([x1 * c - x2 * s, x2 * c + x1 * s], axis=-1).astype(x.dtype)


def masked_softmax(s, mask):
    s = jnp.where(mask, s.astype(jnp.float32), NEG)
    m = jnp.max(s, -1, keepdims=True)
    e = jnp.where(mask, jnp.exp(s - m), 0.0)
    return e / jnp.maximum(jnp.sum(e, -1, keepdims=True), 1e-30)


def merge_blocks(o):
    nb, B, H, Q, HD = o.shape
    return o.transpose(1, 0, 3, 2, 4).reshape(B, nb * Q, H * HD)


def fox_attention(x, w_in, b_f, w_out):
    B, S, D = x.shape
    H, HD = N_HEADS, HEAD_DIM
    q, k, v, fl = jnp.split(x @ w_in, [D, 2 * D, 3 * D], axis=-1)
    q = q.reshape(B, S, H, HD).transpose(0, 2, 1, 3)
    k = k.reshape(B, S, H, HD).transpose(0, 2, 1, 3)
    v = v.reshape(B, S, H, HD).transpose(0, 2, 1, 3)
    log_f = jax.nn.log_sigmoid((fl + b_f).astype(jnp.float32))
    c = jnp.cumsum(log_f, axis=1).transpose(0, 2, 1)
    k_pos = jnp.arange(S)

    def block(i):
        q0 = i * FOX_Q_BLOCK
        qb = lax.dynamic_slice_in_dim(q, q0, FOX_Q_BLOCK, axis=2)
        cq = lax.dynamic_slice_in_dim(c, q0, FOX_Q_BLOCK, axis=2)
        s = (jnp.einsum('bhqd,bhkd->bhqk', qb, k).astype(jnp.float32) * ATTN_SCALE
             + cq[..., :, None] - c[..., None, :])
        mask = (q0 + jnp.arange(FOX_Q_BLOCK))[:, None] >= k_pos[None, :]
        pr = masked_softmax(s, mask)
        return jnp.einsum('bhqk,bhkd->bhqd', pr.astype(v.dtype), v)

    o = lax.map(block, jnp.arange(S // FOX_Q_BLOCK))
    return merge_blocks(o) @ w_out


def nsa_compress(t, pe, w1, w2):
    B, S, G, HD = t.shape
    n_chunks = S // NSA_CMP_STRIDE
    r = NSA_CMP_LEN // NSA_CMP_STRIDE
    n_cmp = n_chunks - r + 1
    ch = t.reshape(B, n_chunks, NSA_CMP_STRIDE, G, HD)
    blocks = jnp.concatenate([ch[:, j:j + n_cmp] for j in range(r)], axis=2)
    blocks = blocks + pe[None, None, :, None, :]
    flat = blocks.transpose(0, 1, 3, 2, 4).reshape(B, n_cmp, G, NSA_CMP_LEN * HD)
    h = jax.nn.gelu(flat @ w1)
    return (h @ w2).transpose(0, 2, 1, 3)


def nsa_attention(x, cos, sin, w_in, b_gate, pe_k, pe_v, ck_w1, ck_w2, cv_w1, cv_w2, w_out):
    B, S, D = x.shape
    G, HG, HD, H = NSA_KV_GROUPS, NSA_HEADS_PER_GROUP, HEAD_DIM, N_HEADS
    QB = NSA_Q_BLOCK
    kvw = G * HD
    splits = [int(v) for v in np.cumsum([D] + [kvw] * 6)]
    q, kc, vc, ks, vs, kw, vw, gl = jnp.split(x @ w_in, splits, axis=-1)

    def to_bghsd(t):
        return t.reshape(B, S, G, HG, HD).transpose(0, 2, 3, 1, 4)

    def to_bsgd(t):
        return t.reshape(B, S, G, HD)

    q_c = to_bghsd(q)
    q_r = to_bghsd(apply_rope(q.reshape(B, S, H, HD), cos, sin))
    k_sel = apply_rope(to_bsgd(ks), cos, sin).transpose(0, 2, 1, 3)
    v_sel = to_bsgd(vs).transpose(0, 2, 1, 3)
    k_win = apply_rope(to_bsgd(kw), cos, sin).transpose(0, 2, 1, 3)
    v_win = to_bsgd(vw).transpose(0, 2, 1, 3)
    gates = jax.nn.sigmoid((gl + b_gate).astype(jnp.float32))
    gates = gates.reshape(B, S, G, HG, 3).transpose(0, 2, 3, 1, 4)

    k_cmp = nsa_compress(to_bsgd(kc), pe_k, ck_w1, ck_w2)
    v_cmp = nsa_compress(to_bsgd(vc), pe_v, cv_w1, cv_w2)
    n_cmp = k_cmp.shape[2]
    cmp_start = jnp.arange(n_cmp) * NSA_CMP_STRIDE
    cmp_end = cmp_start + NSA_CMP_LEN - 1

    n_sel = S // NSA_SEL_LEN
    n_top = min(NSA_SEL_TOPN, n_sel)
    sel_start = jnp.arange(n_sel) * NSA_SEL_LEN
    overlap = ((cmp_start[:, None] < sel_start[None, :] + NSA_SEL_LEN)
               & (cmp_start[:, None] + NSA_CMP_LEN > sel_start[None, :])).astype(jnp.float32)
    ks_blk = k_sel.reshape(B, G, n_sel, NSA_SEL_LEN, HD)
    vs_blk = v_sel.reshape(B, G, n_sel, NSA_SEL_LEN, HD)
    pad = ((0, 0), (0, 0), (NSA_WINDOW, 0), (0, 0))
    kw_pad = jnp.pad(k_win, pad)
    vw_pad = jnp.pad(v_win, pad)
    bi = jnp.arange(B)[:, None, None, None]
    gi = jnp.arange(G)[None, :, None, None]
    jsel = jnp.arange(n_sel)

    def block(i):
        q0 = i * QB
        tq = q0 + jnp.arange(QB)
        qc = lax.dynamic_slice_in_dim(q_c, q0, QB, axis=3)
        qr = lax.dynamic_slice_in_dim(q_r, q0, QB, axis=3)
        gb = lax.dynamic_slice_in_dim(gates, q0, QB, axis=3)
        s_c = jnp.einsum('bghqd,bgnd->bghqn', qc, k_cmp) * ATTN_SCALE
        p_c = masked_softmax(s_c, cmp_end[None, :] <= tq[:, None])
        o_c = jnp.einsum('bghqn,bgnd->bghqd', p_c.astype(v_cmp.dtype), v_cmp)
        imp = jnp.einsum('bghqn,nj->bgqj', p_c, overlap)
        cur = tq // NSA_SEL_LEN
        forced = ((jsel[None, :] == 0) | (jsel[None, :] == cur[:, None])
                  | (jsel[None, :] == cur[:, None] - 1))
        pri = jnp.where(forced, imp + NSA_FORCE_BONUS, imp)
        pri = jnp.where(jsel[None, :] <= cur[:, None], pri, NEG)
        _, idx = lax.top_k(pri, n_top)
        kg = ks_blk[bi, gi, idx].reshape(B, G, QB, n_top * NSA_SEL_LEN, HD)
        vg = vs_blk[bi, gi, idx].reshape(B, G, QB, n_top * NSA_SEL_LEN, HD)
        kpos = (idx[..., None] * NSA_SEL_LEN + jnp.arange(NSA_SEL_LEN)).reshape(B, G, QB, -1)
        m_s = (kpos <= tq[:, None])[:, :, None]
        s_s = jnp.einsum('bghqd,bgqkd->bghqk', qr, kg) * ATTN_SCALE
        p_s = masked_softmax(s_s, m_s)
        o_s = jnp.einsum('bghqk,bgqkd->bghqd', p_s.astype(vg.dtype), vg)
        kwb = lax.dynamic_slice_in_dim(kw_pad, q0, NSA_WINDOW + QB, axis=2)
        vwb = lax.dynamic_slice_in_dim(vw_pad, q0, NSA_WINDOW + QB, axis=2)
        kpw = q0 - NSA_WINDOW + jnp.arange(NSA_WINDOW + QB)
        dlt = tq[:, None] - kpw[None, :]
        m_w = (dlt >= 0) & (dlt < NSA_WINDOW) & (kpw[None, :] >= 0)
        s_w = jnp.einsum('bghqd,bgkd->bghqk', qr, kwb) * ATTN_SCALE
        p_w = masked_softmax(s_w, m_w)
        o_w = jnp.einsum('bghqk,bgkd->bghqd', p_w.astype(vwb.dtype), vwb)
        o = gb[..., 0:1] * o_c + gb[..., 1:2] * o_s + gb[..., 2:3] * o_w
        return o.astype(x.dtype)

    o = lax.map(block, jnp.arange(S // QB))
    o = o.transpose(1, 0, 4, 2, 3, 5).reshape(B, S, D)
    return o @ w_out


def moba_attention(x, cos, sin, w_in, w_out):
    B, S, D = x.shape
    H, HD, BLK, QB = N_HEADS, HEAD_DIM, MOBA_BLOCK, MOBA_Q_BLOCK
    q, k, v = jnp.split(x @ w_in, 3, axis=-1)
    q = apply_rope(q.reshape(B, S, H, HD), cos, sin).transpose(0, 2, 1, 3)
    k = apply_rope(k.reshape(B, S, H, HD), cos, sin).transpose(0, 2, 1, 3)
    v = v.reshape(B, S, H, HD).transpose(0, 2, 1, 3)
    nb = -(-S // BLK)
    s_pad = nb * BLK
    pad = ((0, 0), (0, 0), (0, s_pad - S), (0, 0))
    k_pad = jnp.pad(k, pad)
    v_pad = jnp.pad(v, pad)
    kb = k_pad.reshape(B, H, nb, BLK, HD)
    vb = v_pad.reshape(B, H, nb, BLK, HD)
    k_mean = jnp.mean(kb.astype(jnp.float32), axis=3).astype(k.dtype)
    n_top = min(MOBA_TOPK, nb)
    bi = jnp.arange(B)[:, None, None, None]
    hi = jnp.arange(H)[None, :, None, None]
    jblk = jnp.arange(nb)

    def block(i):
        q0 = i * QB
        tq = q0 + jnp.arange(QB)
        cb = q0 // BLK
        qb = lax.dynamic_slice_in_dim(q, q0, QB, axis=2)
        gs = jnp.einsum('bhqd,bhnd->bhqn', qb, k_mean).astype(jnp.float32)
        gs = jnp.where(jblk < cb, gs, NEG)
        _, idx = lax.top_k(gs, n_top)
        m_sel = jnp.repeat(idx < cb, BLK, axis=-1)
        kg = kb[bi, hi, idx].reshape(B, H, QB, n_top * BLK, HD)
        vg = vb[bi, hi, idx].reshape(B, H, QB, n_top * BLK, HD)
        ko = lax.dynamic_slice_in_dim(k_pad, cb * BLK, BLK, axis=2)
        vo = lax.dynamic_slice_in_dim(v_pad, cb * BLK, BLK, axis=2)
        m_own = jnp.broadcast_to((cb * BLK + jnp.arange(BLK))[None, :] <= tq[:, None], (B, H, QB, BLK))
        s = jnp.concatenate([jnp.einsum('bhqd,bhqkd->bhqk', qb, kg),
                             jnp.einsum('bhqd,bhkd->bhqk', qb, ko)], axis=-1) * ATTN_SCALE
        pr = masked_softmax(s, jnp.concatenate([m_sel, m_own], axis=-1)).astype(v.dtype)
        n_s = n_top * BLK
        return (jnp.einsum('bhqk,bhqkd->bhqd', pr[..., :n_s], vg)
                + jnp.einsum('bhqk,bhkd->bhqd', pr[..., n_s:], vo))

    o = lax.map(block, jnp.arange(S // QB))
    return merge_blocks(o) @ w_out


def moe_ffn(x, router_w, router_b, w_gate, w_up, w_down, s_gate, s_up, s_down):
    B, S, D = x.shape
    T = B * S
    xt = x.reshape(T, D)
    scores = jax.nn.sigmoid((xt @ router_w).astype(jnp.float32))
    biased = scores + router_b
    grp = biased.reshape(T, N_GROUPS, N_EXPERTS // N_GROUPS)
    grp_score = jnp.sum(lax.top_k(grp, 2)[0], axis=-1)
    _, gidx = lax.top_k(grp_score, TOPK_GROUPS)
    gmask = jnp.any(gidx[..., None] == jnp.arange(N_GROUPS), axis=-2)
    emask = jnp.repeat(gmask, N_EXPERTS // N_GROUPS, axis=-1)
    _, eidx = lax.top_k(jnp.where(emask, biased, NEG), TOP_K)
    gw = jnp.take_along_axis(scores, eidx, axis=-1)
    gw = gw / jnp.sum(gw, -1, keepdims=True) * ROUTED_SCALE

    R = T * TOP_K
    e_flat = eidx.reshape(R)
    tok_flat = jnp.repeat(jnp.arange(T, dtype=jnp.int32), TOP_K)
    order = jnp.argsort(e_flat)
    e_sorted = e_flat[order]
    counts = jnp.bincount(e_flat, length=N_EXPERTS)
    starts = jnp.cumsum(counts) - counts
    padded = (counts + MOE_BLOCK - 1) // MOE_BLOCK * MOE_BLOCK
    pends = jnp.cumsum(padded)
    pstarts = pends - padded
    dest = pstarts[e_sorted] + (jnp.arange(R) - starts[e_sorted])
    P = -(-(R + N_EXPERTS * (MOE_BLOCK - 1)) // MOE_BLOCK) * MOE_BLOCK
    NB = P // MOE_BLOCK
    buf_tok = jnp.full((P,), T, jnp.int32).at[dest].set(tok_flat[order])
    buf_w = jnp.zeros((P,), jnp.float32).at[dest].set(gw.reshape(R)[order])
    blk_e = jnp.minimum(jnp.searchsorted(pends, jnp.arange(NB) * MOE_BLOCK, side='right'),
                        N_EXPERTS - 1)
    x_pad = jnp.concatenate([xt, jnp.zeros((1, D), xt.dtype)], axis=0)

    def expert_block(args):
        e, toks, wts = args
        xb = x_pad[toks]
        h = jax.nn.silu(xb @ w_gate[e]) * (xb @ w_up[e])
        return (h @ w_down[e]) * wts[:, None].astype(xb.dtype)

    yb = lax.map(expert_block, (blk_e, buf_tok.reshape(NB, MOE_BLOCK), buf_w.reshape(NB, MOE_BLOCK)))
    routed = jax.ops.segment_sum(yb.reshape(P, D), buf_tok, num_segments=T + 1)[:T]
    shared = (jax.nn.silu(xt @ s_gate) * (xt @ s_up)) @ s_down
    return (routed + shared).reshape(B, S, D).astype(x.dtype)


def _normal(key, shape, fan_in, gain=1.0):
    return jax.random.normal(key, shape, jnp.float32) * (gain * fan_in ** -0.5)


def setup_inputs(seed: int = 0) -> dict:
    key = jax.random.key(seed)
    k = jax.random.split(key, 48)
    D, H, HD = D_MODEL, N_HEADS, HEAD_DIM
    kvw = NSA_KV_GROUPS * HEAD_DIM
    L = NSA_CMP_LEN
    x = jax.random.normal(k[0], (BATCH, SEQ, D), jnp.float32)
    p = jax.random.normal(k[1], (DEPTH, BATCH, SEQ, PLE_DIM), jnp.float32)
    positions = (jax.random.randint(k[2], (BATCH, 1), 0, MAX_POS_OFFSET)
                 + jnp.arange(SEQ, dtype=jnp.int32)[None, :]).astype(jnp.int32)
    fox_w_in = jnp.concatenate([
        _normal(k[3], (N_FOX, D, 2 * D), D),
        _normal(k[4], (N_FOX, D, D), D, DN_BETA),
        _normal(k[5], (N_FOX, D, H), D)], axis=-1)
    fox_b_f = (jnp.linspace(FGATE_BIAS_LO, FGATE_BIAS_HI, H, dtype=jnp.float32)[None, :]
               + 0.1 * jax.random.normal(k[6], (N_FOX, H), jnp.float32))
    fox_w_out = _normal(k[7], (N_FOX, D, D), D, DN_BETA)
    nsa_w_in = jnp.concatenate([
        _normal(k[8], (N_NSA, D, D), D),
        _normal(k[9], (N_NSA, D, kvw), D),
        _normal(k[10], (N_NSA, D, kvw), D, DN_BETA),
        _normal(k[11], (N_NSA, D, kvw), D),
        _normal(k[12], (N_NSA, D, kvw), D, DN_BETA),
        _normal(k[13], (N_NSA, D, kvw), D),
        _normal(k[14], (N_NSA, D, kvw), D, DN_BETA),
        _normal(k[15], (N_NSA, D, 3 * H), D)], axis=-1)
    nsa_b_gate = 0.1 * jax.random.normal(k[16], (N_NSA, 3 * H), jnp.float32)
    nsa_pe_k = 0.02 * jax.random.normal(k[17], (N_NSA, L, HD), jnp.float32)
    nsa_pe_v = 0.02 * jax.random.normal(k[18], (N_NSA, L, HD), jnp.float32)
    nsa_cmp_k_w1 = _normal(k[19], (N_NSA, L * HD, NSA_CMP_HIDDEN), L * HD)
    nsa_cmp_k_w2 = _normal(k[20], (N_NSA, NSA_CMP_HIDDEN, HD), NSA_CMP_HIDDEN)
    nsa_cmp_v_w1 = _normal(k[21], (N_NSA, L * HD, NSA_CMP_HIDDEN), L * HD)
    nsa_cmp_v_w2 = _normal(k[22], (N_NSA, NSA_CMP_HIDDEN, HD), NSA_CMP_HIDDEN)
    nsa_w_out = _normal(k[23], (N_NSA, D, D), D, DN_BETA)
    moba_w_in = jnp.concatenate([
        _normal(k[24], (N_MOBA, D, 2 * D), D),
        _normal(k[25], (N_MOBA, D, D), D, DN_BETA)], axis=-1)
    moba_w_out = _normal(k[26], (N_MOBA, D, D), D, DN_BETA)
    ln1_g = 1.0 + 0.02 * jax.random.normal(k[27], (DEPTH, D), jnp.float32)
    ln1_b = 0.02 * jax.random.normal(k[28], (DEPTH, D), jnp.float32)
    router_w = _normal(k[29], (DEPTH, D, N_EXPERTS), D)
    router_b = 0.01 * jax.random.normal(k[30], (DEPTH, N_EXPERTS), jnp.float32)
    exp_w_gate = _normal(k[31], (DEPTH, N_EXPERTS, D, EXPERT_DIM), D)
    exp_w_up = _normal(k[32], (DEPTH, N_EXPERTS, D, EXPERT_DIM), D)
    exp_w_down = _normal(k[33], (DEPTH, N_EXPERTS, EXPERT_DIM, D), EXPERT_DIM, DN_BETA)
    sh_w_gate = _normal(k[34], (DEPTH, D, SHARED_DIM), D)
    sh_w_up = _normal(k[35], (DEPTH, D, SHARED_DIM), D)
    sh_w_down = _normal(k[36], (DEPTH, SHARED_DIM, D), SHARED_DIM, DN_BETA)
    ln2_g = 1.0 + 0.02 * jax.random.normal(k[37], (DEPTH, D), jnp.float32)
    ln2_b = 0.02 * jax.random.normal(k[38], (DEPTH, D), jnp.float32)
    ple_w_gate = _normal(k[39], (DEPTH, D, D), D)
    ple_w_proj = _normal(k[40], (DEPTH, PLE_DIM, D), PLE_DIM)
    return {"x": x, "p": p, "positions": positions,
            "fox_w_in": fox_w_in, "fox_b_f": fox_b_f, "fox_w_out": fox_w_out,
            "nsa_w_in": nsa_w_in, "nsa_b_gate": nsa_b_gate, "nsa_pe_k": nsa_pe_k, "nsa_pe_v": nsa_pe_v,
            "nsa_cmp_k_w1": nsa_cmp_k_w1, "nsa_cmp_k_w2": nsa_cmp_k_w2,
            "nsa_cmp_v_w1": nsa_cmp_v_w1, "nsa_cmp_v_w2": nsa_cmp_v_w2, "nsa_w_out": nsa_w_out,
            "moba_w_in": moba_w_in, "moba_w_out": moba_w_out,
            "ln1_g": ln1_g, "ln1_b": ln1_b, "router_w": router_w, "router_b": router_b,
            "exp_w_gate": exp_w_gate, "exp_w_up": exp_w_up, "exp_w_down": exp_w_down,
            "sh_w_gate": sh_w_gate, "sh_w_up": sh_w_up, "sh_w_down": sh_w_down,
            "ln2_g": ln2_g, "ln2_b": ln2_b, "ple_w_gate": ple_w_gate, "ple_w_proj": ple_w_proj}


def reference(x, p, positions, fox_w_in, fox_b_f, fox_w_out, nsa_w_in, nsa_b_gate, nsa_pe_k, nsa_pe_v,
              nsa_cmp_k_w1, nsa_cmp_k_w2, nsa_cmp_v_w1, nsa_cmp_v_w2, nsa_w_out, moba_w_in, moba_w_out,
              ln1_g, ln1_b, router_w, router_b, exp_w_gate, exp_w_up, exp_w_down,
              sh_w_gate, sh_w_up, sh_w_down, ln2_g, ln2_b, ple_w_gate, ple_w_proj):
    cos, sin = rope_tables(positions)
    for i in range(DEPTH):
        kind, j = i % N_MIXERS, i // N_MIXERS
        if kind == 0:
            mix = fox_attention(x, fox_w_in[j], fox_b_f[j], fox_w_out[j])
        elif kind == 1:
            mix = nsa_attention(x, cos, sin, nsa_w_in[j], nsa_b_gate[j], nsa_pe_k[j], nsa_pe_v[j],
                                nsa_cmp_k_w1[j], nsa_cmp_k_w2[j], nsa_cmp_v_w1[j], nsa_cmp_v_w2[j],
                                nsa_w_out[j])
        else:
            mix = moba_attention(x, cos, sin, moba_w_in[j], moba_w_out[j])
        x = layer_norm(DN_ALPHA * x + mix, ln1_g[i], ln1_b[i])
        ffn = moe_ffn(x, router_w[i], router_b[i], exp_w_gate[i], exp_w_up[i], exp_w_down[i],
                      sh_w_gate[i], sh_w_up[i], sh_w_down[i])
        x = layer_norm(DN_ALPHA * x + ffn, ln2_g[i], ln2_b[i])
        x = x + jax.nn.sigmoid(x @ ple_w_gate[i]) * (p[i] @ ple_w_proj[i])
    return x
```

```python
import functools
import math

import jax
import jax.numpy as jnp
import numpy as np
from jax import lax
from jax.experimental import pallas as pl
from jax.experimental.pallas import tpu as pltpu

D_MODEL = 1024
DEPTH = 4
N_HEADS = 16
HEAD_DIM = 64
ATTN_SCALE = HEAD_DIM ** -0.5
ROPE_THETA = 10000.0
N_MIXERS = 3

NSA_KV_GROUPS = 4
NSA_HEADS_PER_GROUP = N_HEADS // NSA_KV_GROUPS
NSA_CMP_LEN = 32
NSA_CMP_STRIDE = 16
NSA_SEL_LEN = 64
NSA_SEL_TOPN = 16
NSA_WINDOW = 512
NSA_FORCE_BONUS = 1e4

MOBA_BLOCK = 256
MOBA_TOPK = 3

N_EXPERTS = 64
EXPERT_DIM = 256
TOP_K = 8
N_GROUPS = 8
TOPK_GROUPS = 4
ROUTED_SCALE = 2.5
MOE_BLOCK = 256

DN_ALPHA = (2 * DEPTH) ** 0.25
LN_EPS = 1e-5
NEG = -1e30

LANES = 128
ATTN_TILE = 512
ROW_TILE = 512
VMEM_LIMIT = 48 * 1024 * 1024

F32 = jnp.float32
BF16 = jnp.bfloat16


def _cparams(sem):
    return pltpu.CompilerParams(dimension_semantics=sem, vmem_limit_bytes=VMEM_LIMIT)


def _nt_dot(a, b):
    return lax.dot_general(a, b, (((1,), (1,)), ((), ())), preferred_element_type=F32)


def _proj_kernel(*refs, mode, precision):
    if mode == "none":
        x_ref, w_ref, o_ref = refs
    elif mode == "rope":
        x_ref, w_ref, cos_ref, sin_ref, r_ref = refs
    else:
        x_ref, w_ref, cos_ref, sin_ref, o_ref, r_ref = refs
    x = x_ref[...]
    w = w_ref[...]
    if x.dtype != w.dtype:
        x = x.astype(w.dtype)
    acc = jnp.dot(x, w, preferred_element_type=F32, precision=precision)
    if mode in ("none", "both"):
        o_ref[...] = acc.astype(o_ref.dtype)
    if mode in ("rope", "both"):
        tn = acc.shape[1]
        rep = tn // LANES
        cosf = jnp.concatenate([cos_ref[...]] * rep, axis=1)
        sinf = jnp.concatenate([sin_ref[...]] * rep, axis=1)
        lane = lax.broadcasted_iota(jnp.int32, acc.shape, 1)
        first_half = (lane & (HEAD_DIM // 2)) == 0
        swapped = jnp.where(first_half, pltpu.roll(acc, tn - HEAD_DIM // 2, 1),
                            pltpu.roll(acc, HEAD_DIM // 2, 1))
        r_ref[...] = (acc * cosf + swapped * sinf).astype(r_ref.dtype)


def _proj(x, w, *, mode="none", cos=None, sin=None, out_dtype=BF16, tn=512, precision=None):
    T, K = x.shape
    N = w.shape[1]
    tm = min(ROW_TILE, T)
    tn = min(tn, N)
    assert T % tm == 0 and N % tn == 0
    grid = (T // tm, N // tn)
    in_specs = [pl.BlockSpec((tm, K), lambda i, j: (i, 0)),
                pl.BlockSpec((K, tn), lambda i, j: (0, j))]
    args = [x, w]
    if mode != "none":
        in_specs += [pl.BlockSpec((tm, LANES), lambda i, j: (i, 0))] * 2
        args += [cos, sin]
    o_spec = pl.BlockSpec((tm, tn), lambda i, j: (i, j))
    o_shape = jax.ShapeDtypeStruct((T, N), out_dtype)
    if mode == "both":
        out_specs, out_shape = [o_spec, o_spec], [o_shape, o_shape]
    else:
        out_specs, out_shape = o_spec, o_shape
    return pl.pallas_call(
        functools.partial(_proj_kernel, mode=mode, precision=precision),
        grid=grid, in_specs=in_specs, out_specs=out_specs, out_shape=out_shape,
        compiler_params=_cparams(("parallel", "parallel")), name=f"proj_{mode}")(*args)


def _pad_cols(w, n):
    return jnp.pad(w, ((0, 0), (0, n - w.shape[1])))


def _layer_norm(z, g, b):
    mu = jnp.mean(z, axis=-1, keepdims=True)
    zc = z - mu
    var = jnp.mean(zc * zc, axis=-1, keepdims=True)
    return zc * lax.rsqrt(var + LN_EPS) * g + b


def _outproj_ln_kernel(*refs, n_o):
    o_refs = refs[:n_o]
    w_ref, x_ref, g_ref, b_ref, y_ref = refs[n_o:]
    if n_o == 1:
        o = o_refs[0][...]
    else:
        o = o_refs[0][...].astype(F32)
        for r in o_refs[1:]:
            o = o + r[...].astype(F32)
        o = o.astype(BF16)
    mix = jnp.dot(o, w_ref[...], preferred_element_type=F32)
    z = DN_ALPHA * x_ref[...] + mix
    y_ref[...] = _layer_norm(z, g_ref[...], b_ref[...])


def _outproj_ln(o_list, w, x, g, b):
    T, D = x.shape
    tm = min(ROW_TILE, T)
    row = pl.BlockSpec((tm, D), lambda i: (i, 0))
    vec = pl.BlockSpec((1, D), lambda i: (0, 0))
    return pl.pallas_call(
        functools.partial(_outproj_ln_kernel, n_o=len(o_list)),
        grid=(T // tm,),
        in_specs=[row] * len(o_list) + [pl.BlockSpec((D, D), lambda i: (0, 0)), row, vec, vec],
        out_specs=row, out_shape=jax.ShapeDtypeStruct((T, D), F32),
        compiler_params=_cparams(("parallel",)), name="outproj_ln")(
            *o_list, w, x, g.reshape(1, D), b.reshape(1, D))


def _flash_init(m_ref, l_ref, acc_ref):
    m_ref[...] = jnp.full(m_ref.shape, NEG, F32)
    l_ref[...] = jnp.zeros(l_ref.shape, F32)
    acc_ref[...] = jnp.zeros(acc_ref.shape, F32)


def _flash_step(s, v, m_ref, l_ref, acc_ref, h):
    m_prev = m_ref[h]
    m_new = jnp.maximum(m_prev, jnp.max(s, axis=1, keepdims=True))
    p = jnp.exp(s - m_new)
    alpha = jnp.exp(m_prev - m_new)
    l_ref[h] = alpha * l_ref[h] + jnp.sum(p, axis=1, keepdims=True)
    acc_ref[h] = alpha * acc_ref[h] + jnp.dot(p.astype(BF16), v, preferred_element_type=F32)
    m_ref[h] = m_new


def _tri_mask(tq, tk, strict_upper=False):
    row = lax.broadcasted_iota(jnp.int32, (tq, tk), 0)
    col = lax.broadcasted_iota(jnp.int32, (tq, tk), 1)
    return (col > row) if strict_upper else (col <= row)


def _half_select(lo_half_val, hi_half_val):
    lane = lax.broadcasted_iota(jnp.int32, lo_half_val.shape, 1)
    return jnp.where(lane < HEAD_DIM, lo_half_val, hi_half_val)


def _head_of_pair(q_pair, which):
    lane = lax.broadcasted_iota(jnp.int32, q_pair.shape, 1)
    keep = (lane < HEAD_DIM) if which == 0 else (lane >= HEAD_DIM)
    return jnp.where(keep, q_pair, jnp.zeros_like(q_pair))


def _fox_kernel(q_ref, k_ref, v_ref, c_ref, o_ref, m_ref, l_ref, acc_ref, *, tq, tk):
    i = pl.program_id(2)
    _flash_init(m_ref, l_ref, acc_ref)
    q = q_ref[...] * jnp.asarray(ATTN_SCALE, q_ref.dtype)
    qh = [_head_of_pair(q, 0), _head_of_pair(q, 1)]

    def step(j, masked):
        off = pl.multiple_of(j * tk, tk)
        k = k_ref[pl.ds(off, tk), :]
        v = v_ref[pl.ds(off, tk), :]
        ck = c_ref[:, pl.ds(off, tk)]
        for h in range(2):
            s = _nt_dot(qh[h], k) - ck[h:h + 1, :]
            if masked:
                s = jnp.where(_tri_mask(tq, tk), s, NEG)
            _flash_step(s, v, m_ref, l_ref, acc_ref, h)

    def body(j, carry):
        step(j, False)
        return carry

    lax.fori_loop(0, i, body, 0)
    step(i, True)
    o0 = acc_ref[0] / l_ref[0]
    o1 = acc_ref[1] / l_ref[1]
    o_ref[...] = _half_select(o0, o1).astype(o_ref.dtype)


def _fox_attention(qkv, c2, B, S):
    T = qkv.shape[0]
    tq = tk = min(ATTN_TILE, S)
    nq = S // tq
    npair = N_HEADS // 2
    ncol = D_MODEL // LANES
    return pl.pallas_call(
        functools.partial(_fox_kernel, tq=tq, tk=tk),
        grid=(B, npair, nq),
        in_specs=[pl.BlockSpec((tq, LANES), lambda b, hp, i: (b * nq + i, hp)),
                  pl.BlockSpec((S, LANES), lambda b, hp, i: (b, ncol + hp)),
                  pl.BlockSpec((S, LANES), lambda b, hp, i: (b, 2 * ncol + hp)),
                  pl.BlockSpec((None, None, 2, S), lambda b, hp, i: (b, hp, 0, 0))],
        out_specs=pl.BlockSpec((tq, LANES), lambda b, hp, i: (b * nq + i, hp)),
        out_shape=jax.ShapeDtypeStruct((T, D_MODEL), BF16),
        scratch_shapes=[pltpu.VMEM((2, tq, 1), F32), pltpu.VMEM((2, tq, 1), F32),
                        pltpu.VMEM((2, tq, LANES), F32)],
        compiler_params=_cparams(("parallel", "parallel", "arbitrary")), name="fox_attn")(
            qkv, qkv, qkv, c2)


def _gate_expand(gl_ref, bg_ref, ex_ref):
    sig = jax.nn.sigmoid(gl_ref[...] + bg_ref[...])
    hi = sig.astype(BF16)
    lo = (sig - hi.astype(F32)).astype(BF16)
    ex = ex_ref[...]
    return jnp.dot(hi, ex, preferred_element_type=F32) + jnp.dot(lo, ex, preferred_element_type=F32)


def _nsa_compress_kernel(f_ref, pe_ref, w1_ref, w2_ref, o_ref):
    blk = (f_ref[...].astype(F32) + pe_ref[...]).astype(BF16)
    h = jax.nn.gelu(jnp.dot(blk, w1_ref[...], preferred_element_type=F32))
    o_ref[...] = jnp.dot(h.astype(BF16), w2_ref[...], preferred_element_type=F32)


def _nsa_compress(flat, pe_flat, w1, w2p):
    M, K = flat.shape
    tm = min(ROW_TILE, M)
    Hc = w1.shape[1]
    return pl.pallas_call(
        _nsa_compress_kernel, grid=(M // tm,),
        in_specs=[pl.BlockSpec((tm, K), lambda i: (i, 0)), pl.BlockSpec((1, K), lambda i: (0, 0)),
                  pl.BlockSpec((K, Hc), lambda i: (0, 0)), pl.BlockSpec((Hc, LANES), lambda i: (0, 0))],
        out_specs=pl.BlockSpec((tm, LANES), lambda i: (i, 0)),
        out_shape=jax.ShapeDtypeStruct((M, LANES), F32),
        compiler_params=_cparams(("parallel",)), name="nsa_compress")(flat, pe_flat, w1, w2p)


def _nsa_cmp_kernel(q_ref, kk_ref, vv_ref, ov_ref, gl_ref, bg_ref, ex_ref, o_ref, imp_ref, *, tq, ncp):
    i = pl.program_id(2)
    q = q_ref[...] * jnp.asarray(ATTN_SCALE, q_ref.dtype)
    kk = kk_ref[...]
    vv = vv_ref[...]
    t = i * tq + lax.broadcasted_iota(jnp.int32, (tq, ncp), 0)
    n = lax.broadcasted_iota(jnp.int32, (tq, ncp), 1)
    valid = n * NSA_CMP_STRIDE + (NSA_CMP_LEN - 1) <= t
    psum = jnp.zeros((tq, ncp), F32)
    outs = []
    for hg in range(NSA_HEADS_PER_GROUP):
        qh = _head_of_pair(q[:, LANES * (hg // 2):LANES * (hg // 2 + 1)], hg % 2)
        s = jnp.where(valid, _nt_dot(qh, kk), NEG)
        m = jnp.max(s, axis=1, keepdims=True)
        e = jnp.where(valid, jnp.exp(s - m), 0.0)
        p = e / jnp.maximum(jnp.sum(e, axis=1, keepdims=True), 1e-30)
        psum = psum + p
        outs.append(jnp.dot(p.astype(BF16), vv, preferred_element_type=F32))
    o = jnp.concatenate([_half_select(outs[0], outs[1]), _half_select(outs[2], outs[3])], axis=1)
    o_ref[...] = (o * _gate_expand(gl_ref, bg_ref, ex_ref)).astype(o_ref.dtype)
    hi = psum.astype(BF16)
    lo = (psum - hi.astype(F32)).astype(BF16)
    ov = ov_ref[...]
    imp_ref[...] = jnp.dot(hi, ov, preferred_element_type=F32) + jnp.dot(lo, ov, preferred_element_type=F32)


def _nsa_cmp_branch(q_c, kk_c, vv_c, overlap, gl, bg, ex, B, S):
    T = q_c.shape[0]
    G = NSA_KV_GROUPS
    tq = min(ATTN_TILE, S)
    nq = S // tq
    ncp = kk_c.shape[2]
    nsel = overlap.shape[1]
    return pl.pallas_call(
        functools.partial(_nsa_cmp_kernel, tq=tq, ncp=ncp),
        grid=(B, G, nq),
        in_specs=[pl.BlockSpec((tq, 2 * LANES), lambda b, g, i: (b * nq + i, g)),
                  pl.BlockSpec((None, None, ncp, LANES), lambda b, g, i: (b, g, 0, 0)),
                  pl.BlockSpec((None, None, ncp, LANES), lambda b, g, i: (b, g, 0, 0)),
                  pl.BlockSpec((ncp, nsel), lambda b, g, i: (0, 0)),
                  pl.BlockSpec((tq, LANES), lambda b, g, i: (b * nq + i, 0)),
                  pl.BlockSpec((1, LANES), lambda b, g, i: (0, 0)),
                  pl.BlockSpec((None, LANES, 2 * LANES), lambda b, g, i: (g, 0, 0))],
        out_specs=[pl.BlockSpec((tq, 2 * LANES), lambda b, g, i: (b * nq + i, g)),
                   pl.BlockSpec((None, None, tq, nsel), lambda b, g, i: (b, g, i, 0))],
        out_shape=[jax.ShapeDtypeStruct((T, D_MODEL), BF16),
                   jax.ShapeDtypeStruct((B, G, S, nsel), F32)],
        compiler_params=_cparams(("parallel", "parallel", "parallel")), name="nsa_cmp")(
            q_c, kk_c, vv_c, overlap, gl, bg, ex)


def _nsa_kv_kernel(*refs, mode, tq, tk):
    if mode == "sel":
        q_ref, kk_ref, vv_ref, oh_ref, sn_ref, gl_ref, bg_ref, ex_ref, o_ref, m_ref, l_ref, acc_ref = refs
    else:
        q_ref, kk_ref, vv_ref, gl_ref, bg_ref, ex_ref, o_ref, m_ref, l_ref, acc_ref = refs
    i = pl.program_id(2)
    _flash_init(m_ref, l_ref, acc_ref)
    q = q_ref[...] * jnp.asarray(ATTN_SCALE, q_ref.dtype)
    qh = []
    for hg in range(NSA_HEADS_PER_GROUP):
        qq = _head_of_pair(q[:, LANES * (hg // 2):LANES * (hg // 2 + 1)], hg % 2)
        if mode == "sel":
            qq = jnp.concatenate([qq, sn_ref[...]], axis=1)
        qh.append(qq)

    def step(j, mask):
        off = pl.multiple_of(j * tk, tk)
        k = kk_ref[pl.ds(off, tk), :]
        v = vv_ref[pl.ds(off, tk), :]
        if mode == "sel":
            k = jnp.concatenate([k, oh_ref[pl.ds(off, tk), :]], axis=1)
        for hg in range(NSA_HEADS_PER_GROUP):
            s = _nt_dot(qh[hg], k)
            if mask is not None:
                s = jnp.where(mask, s, NEG)
            _flash_step(s, v, m_ref, l_ref, acc_ref, hg)

    if mode == "sel":
        def body(j, carry):
            step(j, None)
            return carry
        lax.fori_loop(0, i, body, 0)
    else:
        @pl.when(i > 0)
        def _():
            step(i - 1, _tri_mask(tq, tk, strict_upper=True))
    step(i, _tri_mask(tq, tk))
    outs = [acc_ref[hg] / l_ref[hg] for hg in range(NSA_HEADS_PER_GROUP)]
    o = jnp.concatenate([_half_select(outs[0], outs[1]), _half_select(outs[2], outs[3])], axis=1)
    o_ref[...] = (o * _gate_expand(gl_ref, bg_ref, ex_ref)).astype(o_ref.dtype)


def _nsa_kv_branch(mode, q_r, kk, vv, gl, bg, ex, B, S, onehot=None, selneg=None):
    T = q_r.shape[0]
    G = NSA_KV_GROUPS
    tq = tk = min(ATTN_TILE, S)
    assert NSA_WINDOW == tk or mode == "sel"
    nq = S // tq
    kv_spec = pl.BlockSpec((None, None, S, LANES), lambda b, g, i: (b, g, 0, 0))
    in_specs = [pl.BlockSpec((tq, 2 * LANES), lambda b, g, i: (b * nq + i, g)), kv_spec, kv_spec]
    args = [q_r, kk, vv]
    if mode == "sel":
        in_specs += [pl.BlockSpec((S, LANES), lambda b, g, i: (0, 0)),
                     pl.BlockSpec((None, None, tq, LANES), lambda b, g, i: (b, g, i, 0))]
        args += [onehot, selneg]
    in_specs += [pl.BlockSpec((tq, LANES), lambda b, g, i: (b * nq + i, 0)),
                 pl.BlockSpec((1, LANES), lambda b, g, i: (0, 0)),
                 pl.BlockSpec((None, LANES, 2 * LANES), lambda b, g, i: (g, 0, 0))]
    args += [gl, bg, ex]
    nh = NSA_HEADS_PER_GROUP
    return pl.pallas_call(
        functools.partial(_nsa_kv_kernel, mode=mode, tq=tq, tk=tk),
        grid=(B, G, nq), in_specs=in_specs,
        out_specs=pl.BlockSpec((tq, 2 * LANES), lambda b, g, i: (b * nq + i, g)),
        out_shape=jax.ShapeDtypeStruct((T, D_MODEL), BF16),
        scratch_shapes=[pltpu.VMEM((nh, tq, 1), F32), pltpu.VMEM((nh, tq, 1), F32),
                        pltpu.VMEM((nh, tq, LANES), F32)],
        compiler_params=_cparams(("parallel", "parallel", "arbitrary")), name=f"nsa_{mode}")(*args)


def _kmean_kernel(k_ref, o_ref, *, nblk):
    k = k_ref[...].astype(F32)
    o_ref[...] = jnp.mean(k.reshape(nblk, MOBA_BLOCK, k.shape[1]), axis=1)


def _moba_kmean(k_rot):
    T, D = k_rot.shape
    nblk = 8
    rows = nblk * MOBA_BLOCK
    assert T % rows == 0
    return pl.pallas_call(
        functools.partial(_kmean_kernel, nblk=nblk), grid=(T // rows,),
        in_specs=[pl.BlockSpec((rows, D), lambda i: (i, 0))],
        out_specs=pl.BlockSpec((nblk, D), lambda i: (i, 0)),
        out_shape=jax.ShapeDtypeStruct((T // MOBA_BLOCK, D), F32),
        compiler_params=_cparams(("parallel",)), name="moba_kmean")(k_rot)


def _moba_gate_kernel(q_ref, r_ref, o_ref):
    o_ref[...] = jnp.dot(q_ref[...].astype(F32), r_ref[...], preferred_element_type=F32,
                         precision=lax.Precision.HIGHEST)


def _moba_gate_scores(q_rot, rmat, B, S):
    tq = min(ATTN_TILE, S)
    nq = S // tq
    npair = N_HEADS // 2
    return pl.pallas_call(
        _moba_gate_kernel, grid=(B, npair, nq),
        in_specs=[pl.BlockSpec((tq, LANES), lambda b, hp, i: (b * nq + i, hp)),
                  pl.BlockSpec((None, None, LANES, LANES), lambda b, hp, i: (b, hp, 0, 0))],
        out_specs=pl.BlockSpec((None, None, tq, LANES), lambda b, hp, i: (b, hp, i, 0)),
        out_shape=jax.ShapeDtypeStruct((B, npair, S, LANES), F32),
        compiler_params=_cparams(("parallel", "parallel", "parallel")), name="moba_gate")(q_rot, rmat)


def _moba_kernel(q_ref, k_ref, v_ref, oh_ref, sn_ref, o_ref, m_ref, l_ref, acc_ref, *, tq, tk, nblk):
    i = pl.program_id(2)
    _flash_init(m_ref, l_ref, acc_ref)
    q = q_ref[...] * jnp.asarray(ATTN_SCALE, q_ref.dtype)
    sn = sn_ref[...]
    lane = lax.broadcasted_iota(jnp.int32, sn.shape, 1)
    qh = []
    for h in range(2):
        mine = (lane >= h * nblk) & (lane < (h + 1) * nblk)
        qh.append(jnp.concatenate([_head_of_pair(q, h), jnp.where(mine, sn, jnp.zeros_like(sn))], axis=1))

    def step(j, masked):
        off = pl.multiple_of(j * tk, tk)
        k = jnp.concatenate([k_ref[pl.ds(off, tk), :], oh_ref[pl.ds(off, tk), :]], axis=1)
        v = v_ref[pl.ds(off, tk), :]
        for h in range(2):
            s = _nt_dot(qh[h], k)
            if masked:
                s = jnp.where(_tri_mask(tq, tk), s, NEG)
            _flash_step(s, v, m_ref, l_ref, acc_ref, h)

    def body(j, carry):
        step(j, False)
        return carry

    lax.fori_loop(0, i, body, 0)
    step(i, True)
    o_ref[...] = _half_select(acc_ref[0] / l_ref[0], acc_ref[1] / l_ref[1]).astype(o_ref.dtype)


def _moba_attention(qk_rot, v, onehot2, selneg, B, S):
    T = v.shape[0]
    tq = tk = min(ATTN_TILE, S)
    nq = S // tq
    npair = N_HEADS // 2
    ncol = D_MODEL // LANES
    nblk = S // MOBA_BLOCK
    return pl.pallas_call(
        functools.partial(_moba_kernel, tq=tq, tk=tk, nblk=nblk),
        grid=(B, npair, nq),
        in_specs=[pl.BlockSpec((tq, LANES), lambda b, hp, i: (b * nq + i, hp)),
                  pl.BlockSpec((S, LANES), lambda b, hp, i: (b, ncol + hp)),
                  pl.BlockSpec((S, LANES), lambda b, hp, i: (b, hp)),
                  pl.BlockSpec((S, LANES), lambda b, hp, i: (0, 0)),
                  pl.BlockSpec((None, None, tq, LANES), lambda b, hp, i: (b, hp, i, 0))],
        out_specs=pl.BlockSpec((tq, LANES), lambda b, hp, i: (b * nq + i, hp)),
        out_shape=jax.ShapeDtypeStruct((T, D_MODEL), BF16),
        scratch_shapes=[pltpu.VMEM((2, tq, 1), F32), pltpu.VMEM((2, tq, 1), F32),
                        pltpu.VMEM((2, tq, LANES), F32)],
        compiler_params=_cparams(("parallel", "parallel", "arbitrary")), name="moba_attn")(
            qk_rot, qk_rot, v, onehot2, selneg)


def _expert_kernel(be_ref, nb_ref, x_ref, wt_ref, wg_ref, wu_ref, wd_ref, o_ref):
    b = pl.program_id(0)

    @pl.when(b < nb_ref[0])
    def _():
        x = x_ref[...]
        h = jax.nn.silu(jnp.dot(x, wg_ref[...], preferred_element_type=F32)) * jnp.dot(
            x, wu_ref[...], preferred_element_type=F32)
        y = jnp.dot(h.astype(BF16), wd_ref[...], preferred_element_type=F32)
        o_ref[...] = (y * wt_ref[...]).astype(o_ref.dtype)

    @pl.when(b >= nb_ref[0])
    def _():
        o_ref[...] = jnp.zeros(o_ref.shape, o_ref.dtype)


def _expert_ffn(blk_e, nb_used, xs, wts, wg, wu, wd):
    P, D = xs.shape
    NB = P // MOE_BLOCK
    E = EXPERT_DIM
    grid_spec = pltpu.PrefetchScalarGridSpec(
        num_scalar_prefetch=2, grid=(NB,),
        in_specs=[pl.BlockSpec((MOE_BLOCK, D), lambda b, be, nb: (b, 0)),
                  pl.BlockSpec((MOE_BLOCK, 1), lambda b, be, nb: (b, 0)),
                  pl.BlockSpec((None, D, E), lambda b, be, nb: (be[b], 0, 0)),
                  pl.BlockSpec((None, D, E), lambda b, be, nb: (be[b], 0, 0)),
                  pl.BlockSpec((None, E, D), lambda b, be, nb: (be[b], 0, 0))],
        out_specs=pl.BlockSpec((MOE_BLOCK, D), lambda b, be, nb: (b, 0)))
    return pl.pallas_call(
        _expert_kernel, grid_spec=grid_spec, out_shape=jax.ShapeDtypeStruct((P, D), BF16),
        compiler_params=_cparams(("arbitrary",)), name="moe_experts")(
            blk_e, nb_used, xs, wts, wg, wu, wd)


def _post_moe_kernel(x_ref, r_ref, p_ref, sg_ref, su_ref, sd_ref, g_ref, b_ref, wg_ref, wp_ref, o_ref):
    x = x_ref[...]
    xb = x.astype(BF16)
    h = jax.nn.silu(jnp.dot(xb, sg_ref[...], preferred_element_type=F32)) * jnp.dot(
        xb, su_ref[...], preferred_element_type=F32)
    shared = jnp.dot(h.astype(BF16), sd_ref[...], preferred_element_type=F32)
    z = DN_ALPHA * x + (r_ref[...].astype(F32) + shared)
    x2 = _layer_norm(z, g_ref[...], b_ref[...])
    gate = jax.nn.sigmoid(jnp.dot(x2.astype(BF16), wg_ref[...], preferred_element_type=F32))
    proj = jnp.dot(p_ref[...].astype(BF16), wp_ref[...], preferred_element_type=F32)
    o_ref[...] = x2 + gate * proj


def _post_moe(x1, routed, p, sg, su, sd, g, b, wgate, wproj):
    T, D = x1.shape
    tm = min(ROW_TILE, T)
    PD = p.shape[1]
    SD = sg.shape[1]
    row = pl.BlockSpec((tm, D), lambda i: (i, 0))
    vec = pl.BlockSpec((1, D), lambda i: (0, 0))
    full = lambda r, c: pl.BlockSpec((r, c), lambda i: (0, 0))
    return pl.pallas_call(
        _post_moe_kernel, grid=(T // tm,),
        in_specs=[row, row, pl.BlockSpec((tm, PD), lambda i: (i, 0)), full(D, SD), full(D, SD),
                  full(SD, D), vec, vec, full(D, D), full(PD, D)],
        out_specs=row, out_shape=jax.ShapeDtypeStruct((T, D), F32),
        compiler_params=_cparams(("parallel",)), name="post_moe")(
            x1, routed, p, sg, su, sd, g.reshape(1, D), b.reshape(1, D), wgate, wproj)


def _rope_tables(positions):
    inv = 1.0 / (ROPE_THETA ** (jnp.arange(0, HEAD_DIM, 2, dtype=F32) / HEAD_DIM))
    ang = positions.astype(F32).reshape(-1)[:, None] * inv
    cos, sin = jnp.cos(ang), jnp.sin(ang)
    cosf = jnp.concatenate([cos] * (LANES // (HEAD_DIM // 2)), axis=1)
    sinf = jnp.concatenate([-sin, sin] * (LANES // HEAD_DIM), axis=1)
    return cosf, sinf


def _fox_mixer(x, B, S, w_in, b_f):
    D, H = D_MODEL, N_HEADS
    qkv = _proj(x, w_in[:, :3 * D].astype(BF16))
    fl = _proj(x, _pad_cols(w_in[:, 3 * D:], LANES), out_dtype=F32, tn=LANES,
               precision=lax.Precision.HIGHEST)[:, :H]
    log_f = jax.nn.log_sigmoid(fl + b_f)
    c = jnp.cumsum(log_f.reshape(B, S, H), axis=1)
    c2 = c.transpose(0, 2, 1).reshape(B, H // 2, 2, S)
    return [_fox_attention(qkv, c2, B, S)]


def _nsa_mixer(x, B, S, cosf, sinf, w_in, b_gate, pe_k, pe_v, ck_w1, ck_w2, cv_w1, cv_w2):
    D, H, G, HD = D_MODEL, N_HEADS, NSA_KV_GROUPS, HEAD_DIM
    HG = NSA_HEADS_PER_GROUP
    kvw = G * HD
    T = B * S
    wb = w_in[:, :D + 6 * kvw].astype(BF16)
    q_c, q_r = _proj(x, wb[:, :D], mode="both", cos=cosf, sin=sinf)
    w_rot = jnp.concatenate([wb[:, D + 2 * kvw:D + 3 * kvw], wb[:, D + 4 * kvw:D + 5 * kvw]], axis=1)
    k_rot = _proj(x, w_rot, mode="rope", cos=cosf, sin=sinf)
    w_plain = jnp.concatenate([wb[:, D:D + 2 * kvw], wb[:, D + 3 * kvw:D + 4 * kvw],
                               wb[:, D + 5 * kvw:D + 6 * kvw]], axis=1)
    plain = _proj(x, w_plain)
    gl = _proj(x, _pad_cols(w_in[:, D + 6 * kvw:], LANES), out_dtype=F32, tn=LANES,
               precision=lax.Precision.HIGHEST)
    bg = _pad_cols(b_gate.reshape(1, -1), LANES)

    def grouped(t2d):
        t = t2d.reshape(B, S, G, HD).transpose(0, 2, 1, 3)
        return jnp.concatenate([t, t], axis=-1)

    n_chunks = S // NSA_CMP_STRIDE
    n_cmp = n_chunks - NSA_CMP_LEN // NSA_CMP_STRIDE + 1
    ncp = n_chunks

    def compress(t2d, pe, w1, w2):
        ch = t2d.reshape(B, n_chunks, NSA_CMP_STRIDE, G, HD).transpose(0, 1, 3, 2, 4)
        ch = ch.reshape(B, n_chunks, G, NSA_CMP_STRIDE * HD)
        flat = jnp.concatenate([ch[:, :n_cmp], ch[:, 1:n_cmp + 1]], axis=-1)
        flat = jnp.pad(flat, ((0, 0), (0, ncp - n_cmp), (0, 0), (0, 0))).reshape(B * ncp * G, -1)
        out = _nsa_compress(flat, pe.reshape(1, -1), w1.astype(BF16), _pad_cols(w2, LANES).astype(BF16))
        out = out[:, :HD].reshape(B, ncp, G, HD).transpose(0, 2, 1, 3).astype(BF16)
        return jnp.concatenate([out, out], axis=-1)

    kk_c = compress(plain[:, :kvw], pe_k, ck_w1, ck_w2)
    vv_c = compress(plain[:, kvw:2 * kvw], pe_v, cv_w1, cv_w2)

    n_sel = S // NSA_SEL_LEN
    assert n_sel <= LANES
    cmp_start = np.arange(ncp) * NSA_CMP_STRIDE
    sel_start = np.arange(LANES) * NSA_SEL_LEN
    overlap = ((cmp_start[:, None] < sel_start[None, :] + NSA_SEL_LEN)
               & (cmp_start[:, None] + NSA_CMP_LEN > sel_start[None, :])
               & (np.arange(ncp)[:, None] < n_cmp) & (np.arange(LANES)[None, :] < n_sel))
    overlap = jnp.asarray(overlap, BF16)
    ex = np.zeros((3, G, LANES, HG * HD), np.float32)
    for br in range(3):
        for g in range(G):
            for hg in range(HG):
                ex[br, g, (g * HG + hg) * 3 + br, hg * HD:(hg + 1) * HD] = 1.0
    ex = jnp.asarray(ex, BF16)

    o_c, imp = _nsa_cmp_branch(q_c, kk_c, vv_c, overlap, gl, bg, ex[0], B, S)

    n_top = min(NSA_SEL_TOPN, n_sel)
    tq = jnp.arange(S)
    cur = tq // NSA_SEL_LEN
    jsel = jnp.arange(n_sel)
    forced = (jsel[None, :] == 0) | (jsel[None, :] == cur[:, None]) | (jsel[None, :] == cur[:, None] - 1)
    imp = imp[..., :n_sel]
    pri = jnp.where(forced, imp + NSA_FORCE_BONUS, imp)
    pri = jnp.where(jsel[None, :] <= cur[:, None], pri, NEG)
    _, idx = lax.top_k(pri, n_top)
    chosen = jnp.any(idx[..., None] == jsel, axis=-2) & (jsel[None, :] <= cur[:, None])
    selneg = jnp.where(chosen, 0.0, NEG).astype(BF16)
    selneg = jnp.pad(selneg, ((0, 0), (0, 0), (0, 0), (0, LANES - n_sel)))
    onehot = jnp.asarray((np.arange(S)[:, None] // NSA_SEL_LEN) == np.arange(LANES)[None, :], BF16)

    kk_s = grouped(k_rot[:, :kvw])
    kk_w = grouped(k_rot[:, kvw:])
    vv_s = grouped(plain[:, 2 * kvw:3 * kvw])
    vv_w = grouped(plain[:, 3 * kvw:])
    o_s = _nsa_kv_branch("sel", q_r, kk_s, vv_s, gl, bg, ex[1], B, S, onehot=onehot, selneg=selneg)
    o_w = _nsa_kv_branch("win", q_r, kk_w, vv_w, gl, bg, ex[2], B, S)
    return [o_c, o_s, o_w]


def _moba_mixer(x, B, S, cosf, sinf, w_in):
    D, H, HD = D_MODEL, N_HEADS, HEAD_DIM
    wb = w_in.astype(BF16)
    qk_rot = _proj(x, wb[:, :2 * D], mode="rope", cos=cosf, sin=sinf)
    v = _proj(x, wb[:, 2 * D:])
    nblk = S // MOBA_BLOCK
    assert S % MOBA_BLOCK == 0 and 2 * nblk <= LANES
    kmean = _moba_kmean(qk_rot[:, D:]).reshape(B, nblk, H // 2, 2, HD)
    km = kmean.transpose(0, 2, 3, 4, 1)
    rmat = jnp.zeros((B, H // 2, 2, HD, LANES), F32)
    rmat = rmat.at[:, :, 0, :, :nblk].set(km[:, :, 0]).at[:, :, 1, :, nblk:2 * nblk].set(km[:, :, 1])
    rmat = rmat.reshape(B, H // 2, LANES, LANES)
    gs = _moba_gate_scores(qk_rot, rmat, B, S)
    gs = gs[..., :2 * nblk].reshape(B, H // 2, S, 2, nblk)
    cb = (jnp.arange(S) // MOBA_BLOCK)[:, None, None]
    jblk = jnp.arange(nblk)
    gsm = jnp.where(jblk < cb, gs, NEG)
    _, idx = lax.top_k(gsm, min(MOBA_TOPK, nblk))
    chosen = (jnp.any(idx[..., None] == jblk, axis=-2) & (jblk < cb)) | (jblk == cb)
    selneg = jnp.where(chosen, 0.0, NEG).astype(BF16).reshape(B, H // 2, S, 2 * nblk)
    selneg = jnp.pad(selneg, ((0, 0), (0, 0), (0, 0), (0, LANES - 2 * nblk)))
    blk_of = np.arange(S) // MOBA_BLOCK
    oh = np.zeros((S, LANES), np.float32)
    oh[np.arange(S), blk_of] = 1.0
    oh[np.arange(S), nblk + blk_of] = 1.0
    return [_moba_attention(qk_rot, v, jnp.asarray(oh, BF16), selneg, B, S)]


def _moe_routed(x1, router_w, router_b, wg, wu, wd):
    T, D = x1.shape
    logits = _proj(x1, _pad_cols(router_w, LANES), out_dtype=F32, tn=LANES,
                   precision=lax.Precision.HIGHEST)[:, :N_EXPERTS]
    scores = jax.nn.sigmoid(logits)
    biased = scores + router_b
    grp = biased.reshape(T, N_GROUPS, N_EXPERTS // N_GROUPS)
    grp_score = jnp.sum(lax.top_k(grp, 2)[0], axis=-1)
    _, gidx = lax.top_k(grp_score, TOPK_GROUPS)
    gmask = jnp.any(gidx[..., None] == jnp.arange(N_GROUPS), axis=-2)
    emask = jnp.repeat(gmask, N_EXPERTS // N_GROUPS, axis=-1)
    _, eidx = lax.top_k(jnp.where(emask, biased, NEG), TOP_K)
    gw = jnp.take_along_axis(scores, eidx, axis=-1)
    gw = gw / jnp.sum(gw, -1, keepdims=True) * ROUTED_SCALE

    R = T * TOP_K
    e_flat = eidx.reshape(R)
    tok_flat = jnp.repeat(jnp.arange(T, dtype=jnp.int32), TOP_K)
    order = jnp.argsort(e_flat)
    e_sorted = e_flat[order]
    counts = jnp.bincount(e_flat, length=N_EXPERTS)
    starts = jnp.cumsum(counts) - counts
    padded = (counts + MOE_BLOCK - 1) // MOE_BLOCK * MOE_BLOCK
    pends = jnp.cumsum(padded)
    pstarts = pends - padded
    dest = (pstarts[e_sorted] + (jnp.arange(R) - starts[e_sorted])).astype(jnp.int32)
    P = -(-(R + N_EXPERTS * (MOE_BLOCK - 1)) // MOE_BLOCK) * MOE_BLOCK
    NB = P // MOE_BLOCK
    buf_tok = jnp.full((P,), T, jnp.int32).at[dest].set(tok_flat[order])
    buf_w = jnp.zeros((P,), F32).at[dest].set(gw.reshape(R)[order])
    blk_e = jnp.minimum(jnp.searchsorted(pends, jnp.arange(NB) * MOE_BLOCK, side="right"),
                        N_EXPERTS - 1).astype(jnp.int32)
    nb_used = (pends[-1] // MOE_BLOCK).astype(jnp.int32).reshape(1)
    pos = jnp.zeros((R,), jnp.int32).at[order].set(dest).reshape(T, TOP_K)

    x_pad = jnp.concatenate([x1.astype(BF16), jnp.zeros((1, D), BF16)], axis=0)
    xs = x_pad[buf_tok]
    yb = _expert_ffn(blk_e, nb_used, xs, buf_w.reshape(P, 1), wg, wu, wd)
    return jnp.sum(yb[pos].astype(F32), axis=1)


def kernel(x, p, positions, fox_w_in, fox_b_f, fox_w_out, nsa_w_in, nsa_b_gate, nsa_pe_k, nsa_pe_v,
           nsa_cmp_k_w1, nsa_cmp_k_w2, nsa_cmp_v_w1, nsa_cmp_v_w2, nsa_w_out, moba_w_in, moba_w_out,
           ln1_g, ln1_b, router_w, router_b, exp_w_gate, exp_w_up, exp_w_down,
           sh_w_gate, sh_w_up, sh_w_down, ln2_g, ln2_b, ple_w_gate, ple_w_proj):
    B, S, D = x.shape
    T = B * S
    depth = p.shape[0]
    cosf, sinf = _rope_tables(positions)
    xt = x.reshape(T, D)
    for i in range(depth):
        kind, j = i % N_MIXERS, i // N_MIXERS
        if kind == 0:
            o_list = _fox_mixer(xt, B, S, fox_w_in[j], fox_b_f[j])
            w_out = fox_w_out[j]
        elif kind == 1:
            o_list = _nsa_mixer(xt, B, S, cosf, sinf, nsa_w_in[j], nsa_b_gate[j], nsa_pe_k[j], nsa_pe_v[j],
                                nsa_cmp_k_w1[j], nsa_cmp_k_w2[j], nsa_cmp_v_w1[j], nsa_cmp_v_w2[j])
            w_out = nsa_w_out[j]
        else:
            o_list = _moba_mixer(xt, B, S, cosf, sinf, moba_w_in[j])
            w_out = moba_w_out[j]
        x1 = _outproj_ln(o_list, w_out.astype(BF16), xt, ln1_g[i], ln1_b[i])
        routed = _moe_routed(x1, router_w[i], router_b[i], exp_w_gate[i].astype(BF16),
                             exp_w_up[i].astype(BF16), exp_w_down[i].astype(BF16))
        xt = _post_moe(x1, routed, p[i].reshape(T, -1), sh_w_gate[i].astype(BF16), sh_w_up[i].astype(BF16),
                       sh_w_down[i].astype(BF16), ln2_g[i], ln2_b[i],
                       ple_w_gate[i].astype(BF16), ple_w_proj[i].astype(BF16))
    return xt.reshape(B, S, D)
```

```python
import functools

import jax
import jax.numpy as jnp
import numpy as np
from jax import lax
from jax.experimental import pallas as pl
from jax.experimental.pallas import tpu as pltpu

D_MODEL = 1024
DEPTH = 4
N_HEADS = 16
HEAD_DIM = 64
ATTN_SCALE = HEAD_DIM ** -0.5
ROPE_THETA = 10000.0
N_MIXERS = 3

NSA_KV_GROUPS = 4
NSA_HEADS_PER_GROUP = N_HEADS // NSA_KV_GROUPS
NSA_CMP_LEN = 32
NSA_CMP_STRIDE = 16
NSA_SEL_LEN = 64
NSA_SEL_TOPN = 16
NSA_WINDOW = 512
NSA_FORCE_BONUS = 1e4

MOBA_BLOCK = 256
MOBA_TOPK = 3

N_EXPERTS = 64
EXPERT_DIM = 256
TOP_K = 8
N_GROUPS = 8
TOPK_GROUPS = 4
ROUTED_SCALE = 2.5
MOE_BLOCK = 256

DN_ALPHA = (2 * DEPTH) ** 0.25
LN_EPS = 1e-5
NEG = -1e30

LANES = 128
SUBLANES = 8
ATTN_TILE = 512
ROW_TILE = 512
VMEM_LIMIT = 48 * 1024 * 1024
C_SPLIT = 3

F32 = jnp.float32
BF16 = jnp.bfloat16
HIGHEST = lax.Precision.HIGHEST


def _cparams(sem):
    return pltpu.CompilerParams(dimension_semantics=sem, vmem_limit_bytes=VMEM_LIMIT)


def _log2(n):
    assert n & (n - 1) == 0
    return n.bit_length() - 1


def _nt_dot(a, b, precision=None):
    return lax.dot_general(a, b, (((1,), (1,)), ((), ())), preferred_element_type=F32,
                           precision=precision)


def _topk_rows(work, n):
    rows = lax.broadcasted_iota(jnp.int32, work.shape, 0).astype(F32)
    chosen = jnp.zeros(work.shape, F32)
    for _ in range(n):
        m = jnp.max(work, axis=0, keepdims=True)
        idx = jnp.min(jnp.where(work == m, rows, float(work.shape[0])), axis=0, keepdims=True)
        pick = rows == idx
        chosen = jnp.where(pick, 1.0, chosen)
        work = jnp.where(pick, -jnp.inf, work)
    return chosen


def _proj_kernel(*refs, mode, precision):
    if mode == "none":
        x_ref, w_ref, o_ref = refs
    elif mode == "rope":
        x_ref, w_ref, cos_ref, sin_ref, r_ref = refs
    else:
        x_ref, w_ref, cos_ref, sin_ref, o_ref, r_ref = refs
    x = x_ref[...]
    w = w_ref[...]
    if x.dtype != w.dtype:
        x = x.astype(w.dtype)
    acc = jnp.dot(x, w, preferred_element_type=F32, precision=precision)
    if mode in ("none", "both"):
        o_ref[...] = acc.astype(o_ref.dtype)
    if mode in ("rope", "both"):
        tn = acc.shape[1]
        rep = tn // LANES
        cosf = jnp.concatenate([cos_ref[...]] * rep, axis=1)
        sinf = jnp.concatenate([sin_ref[...]] * rep, axis=1)
        lane = lax.broadcasted_iota(jnp.int32, acc.shape, 1)
        first_half = (lane & (HEAD_DIM // 2)) == 0
        swapped = jnp.where(first_half, pltpu.roll(acc, tn - HEAD_DIM // 2, 1),
                            pltpu.roll(acc, HEAD_DIM // 2, 1))
        r_ref[...] = (acc * cosf + swapped * sinf).astype(r_ref.dtype)


def _proj(x, w, *, mode="none", cos=None, sin=None, out_dtype=BF16, tn=512, precision=None):
    T, K = x.shape
    N = w.shape[1]
    tm = min(ROW_TILE, T)
    tn = min(tn, N)
    assert T % tm == 0 and N % tn == 0
    grid = (T // tm, N // tn)
    in_specs = [pl.BlockSpec((tm, K), lambda i, j: (i, 0)),
                pl.BlockSpec((K, tn), lambda i, j: (0, j))]
    args = [x, w]
    if mode != "none":
        in_specs += [pl.BlockSpec((tm, LANES), lambda i, j: (i, 0))] * 2
        args += [cos, sin]
    o_spec = pl.BlockSpec((tm, tn), lambda i, j: (i, j))
    o_shape = jax.ShapeDtypeStruct((T, N), out_dtype)
    if mode == "both":
        out_specs, out_shape = [o_spec, o_spec], [o_shape, o_shape]
    else:
        out_specs, out_shape = o_spec, o_shape
    return pl.pallas_call(
        functools.partial(_proj_kernel, mode=mode, precision=precision),
        grid=grid, in_specs=in_specs, out_specs=out_specs, out_shape=out_shape,
        compiler_params=_cparams(("parallel", "parallel")), name=f"proj_{mode}")(*args)


def _proj_t_kernel(wt_ref, x_ref, o_ref, *, precision):
    wt = wt_ref[...]
    x = x_ref[...]
    if x.dtype != wt.dtype:
        x = x.astype(wt.dtype)
    o_ref[...] = _nt_dot(wt, x, precision).astype(o_ref.dtype)


def _proj_t(wt, x, *, out_dtype=BF16, precision=None):
    N, K = wt.shape
    T = x.shape[0]
    tm = min(ROW_TILE, T)
    tn = min(512, N)
    assert T % tm == 0 and N % tn == 0
    return pl.pallas_call(
        functools.partial(_proj_t_kernel, precision=precision),
        grid=(T // tm, N // tn),
        in_specs=[pl.BlockSpec((tn, K), lambda i, j: (j, 0)), pl.BlockSpec((tm, K), lambda i, j: (i, 0))],
        out_specs=pl.BlockSpec((tn, tm), lambda i, j: (j, i)),
        out_shape=jax.ShapeDtypeStruct((N, T), out_dtype),
        compiler_params=_cparams(("parallel", "parallel")), name="proj_t")(wt, x)


def _pad_cols(w, n):
    return jnp.pad(w, ((0, 0), (0, n - w.shape[1])))


def _layer_norm(z, g, b):
    mu = jnp.mean(z, axis=-1, keepdims=True)
    zc = z - mu
    var = jnp.mean(zc * zc, axis=-1, keepdims=True)
    return zc * lax.rsqrt(var + LN_EPS) * g + b


def _outproj_ln_kernel(*refs, n_o):
    o_refs = refs[:n_o]
    w_ref, x_ref, g_ref, b_ref, y_ref = refs[n_o:]
    if n_o == 1:
        o = o_refs[0][...]
    else:
        o = o_refs[0][...].astype(F32)
        for r in o_refs[1:]:
            o = o + r[...].astype(F32)
        o = o.astype(BF16)
    mix = jnp.dot(o, w_ref[...], preferred_element_type=F32)
    z = DN_ALPHA * x_ref[...] + mix
    y_ref[...] = _layer_norm(z, g_ref[...], b_ref[...])


def _outproj_ln(o_list, w, x, g, b):
    T, D = x.shape
    tm = min(ROW_TILE, T)
    row = pl.BlockSpec((tm, D), lambda i: (i, 0))
    vec = pl.BlockSpec((1, D), lambda i: (0, 0))
    return pl.pallas_call(
        functools.partial(_outproj_ln_kernel, n_o=len(o_list)),
        grid=(T // tm,),
        in_specs=[row] * len(o_list) + [pl.BlockSpec((D, D), lambda i: (0, 0)), row, vec, vec],
        out_specs=row, out_shape=jax.ShapeDtypeStruct((T, D), F32),
        compiler_params=_cparams(("parallel",)), name="outproj_ln")(
            *o_list, w, x, g.reshape(1, D), b.reshape(1, D))


def _flash_init(m_ref, l_ref, acc_ref):
    m_ref[...] = jnp.full(m_ref.shape, NEG, F32)
    l_ref[...] = jnp.zeros(l_ref.shape, F32)
    acc_ref[...] = jnp.zeros(acc_ref.shape, F32)


def _flash_step(sT, vT, m_ref, l_ref, acc_ref, h):
    m_prev = m_ref[h]
    m_new = jnp.maximum(m_prev, jnp.max(sT, axis=0, keepdims=True))
    p = jnp.exp(sT - m_new)
    alpha = jnp.exp(m_prev - m_new)
    l_ref[h] = alpha * l_ref[h] + jnp.sum(p, axis=0, keepdims=True)
    acc_ref[h] = alpha * acc_ref[h] + jnp.dot(vT, p.astype(BF16), preferred_element_type=F32)
    m_ref[h] = m_new


def _causal_t(tk, tq, strict_lower=False):
    key = lax.broadcasted_iota(jnp.int32, (tk, tq), 0)
    qry = lax.broadcasted_iota(jnp.int32, (tk, tq), 1)
    return (key > qry) if strict_lower else (key <= qry)


def _head_of_pair(q_pair, which):
    lane = lax.broadcasted_iota(jnp.int32, q_pair.shape, 1)
    keep = (lane < HEAD_DIM) if which == 0 else (lane >= HEAD_DIM)
    return jnp.where(keep, q_pair, jnp.zeros_like(q_pair))


def _pair_rows(a_top, a_bot):
    row = lax.broadcasted_iota(jnp.int32, a_top.shape, 0)
    return jnp.where(row < HEAD_DIM, a_top, a_bot)


def _fox_gate_kernel(fl_ref, bf_ref, tri_ref, ex_ref, cp_ref, carry_ref):
    @pl.when(pl.program_id(1) == 0)
    def _():
        carry_ref[...] = jnp.zeros(carry_ref.shape, F32)

    z = fl_ref[...] + bf_ref[...]
    log_f = jnp.minimum(z, 0.0) - jnp.log1p(jnp.exp(-jnp.abs(z)))
    c = jnp.dot(tri_ref[...], log_f, preferred_element_type=F32, precision=HIGHEST) + carry_ref[...]
    carry_ref[...] = c[-1:, :]
    out = jnp.zeros(c.shape, F32)
    rem = c
    for piece in range(C_SPLIT):
        part = rem.astype(BF16)
        rem = rem - part.astype(F32)
        out = out + jnp.dot(part, ex_ref[piece], preferred_element_type=F32)
    cp_ref[...] = out.astype(BF16)


def _fox_gate(fl, b_f, B, S):
    T = fl.shape[0]
    tm = min(ROW_TILE, S)
    ns = S // tm
    tri = jnp.asarray(np.tril(np.ones((tm, tm), np.float32)))
    ex = np.zeros((C_SPLIT, LANES, LANES), np.float32)
    for h in range(N_HEADS):
        for j in range(C_SPLIT):
            ex[j, h, C_SPLIT * h + j] = 1.0
    bf = _pad_cols(b_f.reshape(1, -1), LANES)
    return pl.pallas_call(
        _fox_gate_kernel, grid=(B, ns),
        in_specs=[pl.BlockSpec((tm, LANES), lambda b, s: (b * ns + s, 0)),
                  pl.BlockSpec((1, LANES), lambda b, s: (0, 0)),
                  pl.BlockSpec((tm, tm), lambda b, s: (0, 0)),
                  pl.BlockSpec((C_SPLIT, LANES, LANES), lambda b, s: (0, 0, 0))],
        out_specs=pl.BlockSpec((tm, LANES), lambda b, s: (b * ns + s, 0)),
        out_shape=jax.ShapeDtypeStruct((T, LANES), BF16),
        scratch_shapes=[pltpu.VMEM((1, LANES), F32)],
        compiler_params=_cparams(("parallel", "arbitrary")), name="fox_gate")(
            fl, bf, tri, jnp.asarray(ex, BF16))


def _fox_kernel(q_ref, k_ref, cp_ref, vt_ref, o_ref, m_ref, l_ref, acc_ref, *, tq, tk):
    hp = pl.program_id(1)
    i = pl.program_id(2)
    _flash_init(m_ref, l_ref, acc_ref)
    q = q_ref[...] * jnp.asarray(ATTN_SCALE, q_ref.dtype)
    lane = lax.broadcasted_iota(jnp.int32, q.shape, 1)
    qa = []
    for h in range(2):
        lo = C_SPLIT * (2 * hp + h)
        minus_one = jnp.where((lane >= lo) & (lane < lo + C_SPLIT), -1.0, 0.0).astype(q.dtype)
        qa.append(jnp.concatenate([_head_of_pair(q, h), minus_one], axis=1))

    def step(j, masked):
        off = pl.multiple_of(j * tk, tk)
        k = jnp.concatenate([k_ref[pl.ds(off, tk), :], cp_ref[pl.ds(off, tk), :]], axis=1)
        vt = vt_ref[:, pl.ds(off, tk)]
        for h in range(2):
            sT = _nt_dot(k, qa[h])
            if masked:
                sT = jnp.where(_causal_t(tk, tq), sT, NEG)
            _flash_step(sT, vt, m_ref, l_ref, acc_ref, h)

    def body(j, carry):
        step(j, False)
        return carry

    lax.fori_loop(0, i, body, 0)
    step(i, True)
    oT = _pair_rows(acc_ref[0] / l_ref[0], acc_ref[1] / l_ref[1])
    o_ref[...] = oT.T.astype(o_ref.dtype)


def _fox_attention(qk, cp, vt, B, S):
    T = qk.shape[0]
    tq = tk = min(ATTN_TILE, S)
    nq = S // tq
    npair = N_HEADS // 2
    ncol = D_MODEL // LANES
    return pl.pallas_call(
        functools.partial(_fox_kernel, tq=tq, tk=tk),
        grid=(B, npair, nq),
        in_specs=[pl.BlockSpec((tq, LANES), lambda b, hp, i: (b * nq + i, hp)),
                  pl.BlockSpec((S, LANES), lambda b, hp, i: (b, ncol + hp)),
                  pl.BlockSpec((S, LANES), lambda b, hp, i: (b, 0)),
                  pl.BlockSpec((LANES, S), lambda b, hp, i: (hp, b))],
        out_specs=pl.BlockSpec((tq, LANES), lambda b, hp, i: (b * nq + i, hp)),
        out_shape=jax.ShapeDtypeStruct((T, D_MODEL), BF16),
        scratch_shapes=[pltpu.VMEM((2, 1, tq), F32), pltpu.VMEM((2, 1, tq), F32),
                        pltpu.VMEM((2, LANES, tq), F32)],
        compiler_params=_cparams(("parallel", "parallel", "arbitrary")), name="fox_attn")(
            qk, qk, cp, vt)


def _gate_expand(gl_ref, bg_ref, ex_ref):
    sig = jax.nn.sigmoid(gl_ref[...] + bg_ref[...])
    hi = sig.astype(BF16)
    lo = (sig - hi.astype(F32)).astype(BF16)
    ex = ex_ref[...]
    return jnp.dot(hi, ex, preferred_element_type=F32) + jnp.dot(lo, ex, preferred_element_type=F32)


def _nsa_compress_kernel(f_ref, pe_ref, w1_ref, w2_ref, o_ref):
    blk = (f_ref[...].astype(F32) + pe_ref[...]).astype(BF16)
    h = jax.nn.gelu(jnp.dot(blk, w1_ref[...], preferred_element_type=F32))
    o_ref[...] = jnp.dot(h.astype(BF16), w2_ref[...], preferred_element_type=F32)


def _nsa_compress(flat, pe_flat, w1, w2p):
    M, K = flat.shape
    tm = min(ROW_TILE, M)
    Hc = w1.shape[1]
    return pl.pallas_call(
        _nsa_compress_kernel, grid=(M // tm,),
        in_specs=[pl.BlockSpec((tm, K), lambda i: (i, 0)), pl.BlockSpec((1, K), lambda i: (0, 0)),
                  pl.BlockSpec((K, Hc), lambda i: (0, 0)), pl.BlockSpec((Hc, LANES), lambda i: (0, 0))],
        out_specs=pl.BlockSpec((tm, LANES), lambda i: (i, 0)),
        out_shape=jax.ShapeDtypeStruct((M, LANES), F32),
        compiler_params=_cparams(("parallel",)), name="nsa_compress")(flat, pe_flat, w1, w2p)


def _nsa_cmp_kernel(q_ref, kk_ref, vt_ref, ovt_ref, gl_ref, bg_ref, ex_ref, o_ref, sn_ref, *, tq, ncp, nselp):
    i = pl.program_id(2)
    q = q_ref[...] * jnp.asarray(ATTN_SCALE, q_ref.dtype)
    kk = kk_ref[...]
    vt = vt_ref[...]
    t = i * tq + lax.broadcasted_iota(jnp.int32, (ncp, tq), 1)
    n = lax.broadcasted_iota(jnp.int32, (ncp, tq), 0)
    valid = n * NSA_CMP_STRIDE + (NSA_CMP_LEN - 1) <= t
    psum = jnp.zeros((ncp, tq), F32)
    outs = []
    for hg in range(NSA_HEADS_PER_GROUP):
        qh = _head_of_pair(q[:, LANES * (hg // 2):LANES * (hg // 2 + 1)], hg % 2)
        sT = jnp.where(valid, _nt_dot(kk, qh), NEG)
        m = jnp.max(sT, axis=0, keepdims=True)
        e = jnp.where(valid, jnp.exp(sT - m), 0.0)
        p = e / jnp.maximum(jnp.sum(e, axis=0, keepdims=True), 1e-30)
        psum = psum + p
        outs.append(jnp.dot(vt, p.astype(BF16), preferred_element_type=F32))
    o = jnp.concatenate(outs, axis=0).T
    o_ref[...] = (o * _gate_expand(gl_ref, bg_ref, ex_ref)).astype(o_ref.dtype)
    hi = psum.astype(BF16)
    lo = (psum - hi.astype(F32)).astype(BF16)
    ovt = ovt_ref[...]
    imp = jnp.dot(ovt, hi, preferred_element_type=F32) + jnp.dot(ovt, lo, preferred_element_type=F32)
    jblk = lax.broadcasted_iota(jnp.int32, (nselp, tq), 0)
    cur = (i * tq + lax.broadcasted_iota(jnp.int32, (nselp, tq), 1)) >> _log2(NSA_SEL_LEN)
    forced = (jblk == 0) | (jblk == cur) | (jblk == cur - 1)
    pri = jnp.where(forced, imp + NSA_FORCE_BONUS, imp)
    past = jblk <= cur
    pri = jnp.where(past, pri, NEG)
    chosen = _topk_rows(pri, NSA_SEL_TOPN)
    sn = jnp.where((chosen > 0.0) & past, 0.0, NEG)
    sn_ref[...] = sn.T.astype(sn_ref.dtype)


def _nsa_cmp_branch(q_c, kk_c, vt_c, overlap_t, gl, bg, ex, B, S):
    T = q_c.shape[0]
    G = NSA_KV_GROUPS
    tq = min(ATTN_TILE, S)
    nq = S // tq
    ncp = kk_c.shape[2]
    nselp = overlap_t.shape[0]
    return pl.pallas_call(
        functools.partial(_nsa_cmp_kernel, tq=tq, ncp=ncp, nselp=nselp),
        grid=(B, G, nq),
        in_specs=[pl.BlockSpec((tq, 2 * LANES), lambda b, g, i: (b * nq + i, g)),
                  pl.BlockSpec((None, None, ncp, LANES), lambda b, g, i: (b, g, 0, 0)),
                  pl.BlockSpec((None, None, HEAD_DIM, ncp), lambda b, g, i: (b, g, 0, 0)),
                  pl.BlockSpec((nselp, ncp), lambda b, g, i: (0, 0)),
                  pl.BlockSpec((tq, LANES), lambda b, g, i: (b * nq + i, 0)),
                  pl.BlockSpec((1, LANES), lambda b, g, i: (0, 0)),
                  pl.BlockSpec((None, LANES, 2 * LANES), lambda b, g, i: (g, 0, 0))],
        out_specs=[pl.BlockSpec((tq, 2 * LANES), lambda b, g, i: (b * nq + i, g)),
                   pl.BlockSpec((None, None, tq, nselp), lambda b, g, i: (b, g, i, 0))],
        out_shape=[jax.ShapeDtypeStruct((T, D_MODEL), BF16),
                   jax.ShapeDtypeStruct((B, G, S, nselp), BF16)],
        compiler_params=_cparams(("parallel", "parallel", "parallel")), name="nsa_cmp")(
            q_c, kk_c, vt_c, overlap_t, gl, bg, ex)


def _nsa_kv_kernel(*refs, mode, tq, tk):
    if mode == "sel":
        q_ref, kk_ref, vt_ref, oh_ref, sn_ref, gl_ref, bg_ref, ex_ref, o_ref, m_ref, l_ref, acc_ref = refs
    else:
        q_ref, kk_ref, vt_ref, gl_ref, bg_ref, ex_ref, o_ref, m_ref, l_ref, acc_ref = refs
    i = pl.program_id(2)
    _flash_init(m_ref, l_ref, acc_ref)
    q = q_ref[...] * jnp.asarray(ATTN_SCALE, q_ref.dtype)
    qh = []
    for hg in range(NSA_HEADS_PER_GROUP):
        qq = _head_of_pair(q[:, LANES * (hg // 2):LANES * (hg // 2 + 1)], hg % 2)
        if mode == "sel":
            qq = jnp.concatenate([qq, sn_ref[...]], axis=1)
        qh.append(qq)

    def step(j, mask):
        off = pl.multiple_of(j * tk, tk)
        k = kk_ref[pl.ds(off, tk), :]
        vt = vt_ref[:, pl.ds(off, tk)]
        if mode == "sel":
            k = jnp.concatenate([k, oh_ref[pl.ds(off, tk), :]], axis=1)
        for hg in range(NSA_HEADS_PER_GROUP):
            sT = _nt_dot(k, qh[hg])
            if mask is not None:
                sT = jnp.where(mask, sT, NEG)
            _flash_step(sT, vt, m_ref, l_ref, acc_ref, hg)

    if mode == "sel":
        def body(j, carry):
            step(j, None)
            return carry
        lax.fori_loop(0, i, body, 0)
    else:
        @pl.when(i > 0)
        def _():
            step(i - 1, _causal_t(tk, tq, strict_lower=True))
    step(i, _causal_t(tk, tq))
    oT = jnp.concatenate([acc_ref[hg] / l_ref[hg] for hg in range(NSA_HEADS_PER_GROUP)], axis=0)
    o_ref[...] = (oT.T * _gate_expand(gl_ref, bg_ref, ex_ref)).astype(o_ref.dtype)


def _nsa_kv_branch(mode, q_r, kk, vt, gl, bg, ex, B, S, onehot=None, selneg=None):
    T = q_r.shape[0]
    G = NSA_KV_GROUPS
    tq = tk = min(ATTN_TILE, S)
    assert NSA_WINDOW == tk or mode == "sel"
    nq = S // tq
    in_specs = [pl.BlockSpec((tq, 2 * LANES), lambda b, g, i: (b * nq + i, g)),
                pl.BlockSpec((None, None, S, LANES), lambda b, g, i: (b, g, 0, 0)),
                pl.BlockSpec((HEAD_DIM, S), lambda b, g, i: (g, b))]
    args = [q_r, kk, vt]
    if mode == "sel":
        in_specs += [pl.BlockSpec((S, LANES), lambda b, g, i: (0, 0)),
                     pl.BlockSpec((None, None, tq, LANES), lambda b, g, i: (b, g, i, 0))]
        args += [onehot, selneg]
    in_specs += [pl.BlockSpec((tq, LANES), lambda b, g, i: (b * nq + i, 0)),
                 pl.BlockSpec((1, LANES), lambda b, g, i: (0, 0)),
                 pl.BlockSpec((None, LANES, 2 * LANES), lambda b, g, i: (g, 0, 0))]
    args += [gl, bg, ex]
    nh = NSA_HEADS_PER_GROUP
    return pl.pallas_call(
        functools.partial(_nsa_kv_kernel, mode=mode, tq=tq, tk=tk),
        grid=(B, G, nq), in_specs=in_specs,
        out_specs=pl.BlockSpec((tq, 2 * LANES), lambda b, g, i: (b * nq + i, g)),
        out_shape=jax.ShapeDtypeStruct((T, D_MODEL), BF16),
        scratch_shapes=[pltpu.VMEM((nh, 1, tq), F32), pltpu.VMEM((nh, 1, tq), F32),
                        pltpu.VMEM((nh, HEAD_DIM, tq), F32)],
        compiler_params=_cparams(("parallel", "parallel", "arbitrary")), name=f"nsa_{mode}")(*args)


def _kmean_kernel(k_ref, o_ref, *, nblk):
    k = k_ref[...].astype(F32)
    o_ref[...] = jnp.mean(k.reshape(nblk, MOBA_BLOCK, k.shape[1]), axis=1)


def _moba_kmean(k_rot):
    T, D = k_rot.shape
    nblk = 8
    rows = nblk * MOBA_BLOCK
    assert T % rows == 0
    return pl.pallas_call(
        functools.partial(_kmean_kernel, nblk=nblk), grid=(T // rows,),
        in_specs=[pl.BlockSpec((rows, D), lambda i: (i, 0))],
        out_specs=pl.BlockSpec((nblk, D), lambda i: (i, 0)),
        out_shape=jax.ShapeDtypeStruct((T // MOBA_BLOCK, D), F32),
        compiler_params=_cparams(("parallel",)), name="moba_kmean")(k_rot)


def _moba_select_kernel(q_ref, r_ref, sn_ref, *, tq, nblk):
    i = pl.program_id(2)
    gsT = _nt_dot(r_ref[...], q_ref[...].astype(F32), HIGHEST)
    jblk = lax.broadcasted_iota(jnp.int32, (nblk, tq), 0)
    cb = (i * tq + lax.broadcasted_iota(jnp.int32, (nblk, tq), 1)) >> _log2(MOBA_BLOCK)
    past = jblk < cb
    parts = []
    for h in range(2):
        gs = jnp.where(past, gsT[h * nblk:(h + 1) * nblk, :], NEG)
        chosen = _topk_rows(gs, min(MOBA_TOPK, nblk))
        parts.append(jnp.where(((chosen > 0.0) & past) | (jblk == cb), 0.0, NEG))
    parts.append(jnp.zeros((LANES - 2 * nblk, tq), F32))
    sn_ref[...] = jnp.concatenate(parts, axis=0).T.astype(sn_ref.dtype)


def _moba_select(q_rot, rmat_t, B, S):
    tq = min(ATTN_TILE, S)
    nq = S // tq
    npair = N_HEADS // 2
    nblk = S // MOBA_BLOCK
    return pl.pallas_call(
        functools.partial(_moba_select_kernel, tq=tq, nblk=nblk), grid=(B, npair, nq),
        in_specs=[pl.BlockSpec((tq, LANES), lambda b, hp, i: (b * nq + i, hp)),
                  pl.BlockSpec((None, None, LANES, LANES), lambda b, hp, i: (b, hp, 0, 0))],
        out_specs=pl.BlockSpec((None, None, tq, LANES), lambda b, hp, i: (b, hp, i, 0)),
        out_shape=jax.ShapeDtypeStruct((B, npair, S, LANES), BF16),
        compiler_params=_cparams(("parallel", "parallel", "parallel")), name="moba_select")(q_rot, rmat_t)


def _moba_kernel(q_ref, k_ref, vt_ref, oh_ref, sn_ref, o_ref, m_ref, l_ref, acc_ref, *, tq, tk, nblk):
    i = pl.program_id(2)
    _flash_init(m_ref, l_ref, acc_ref)
    q = q_ref[...] * jnp.asarray(ATTN_SCALE, q_ref.dtype)
    sn = sn_ref[...]
    lane = lax.broadcasted_iota(jnp.int32, sn.shape, 1)
    qa = []
    for h in range(2):
        mine = (lane >= h * nblk) & (lane < (h + 1) * nblk)
        qa.append(jnp.concatenate([_head_of_pair(q, h), jnp.where(mine, sn, jnp.zeros_like(sn))], axis=1))

    def step(j, masked):
        off = pl.multiple_of(j * tk, tk)
        k = jnp.concatenate([k_ref[pl.ds(off, tk), :], oh_ref[pl.ds(off, tk), :]], axis=1)
        vt = vt_ref[:, pl.ds(off, tk)]
        for h in range(2):
            sT = _nt_dot(k, qa[h])
            if masked:
                sT = jnp.where(_causal_t(tk, tq), sT, NEG)
            _flash_step(sT, vt, m_ref, l_ref, acc_ref, h)

    def body(j, carry):
        step(j, False)
        return carry

    lax.fori_loop(0, i, body, 0)
    step(i, True)
    oT = _pair_rows(acc_ref[0] / l_ref[0], acc_ref[1] / l_ref[1])
    o_ref[...] = oT.T.astype(o_ref.dtype)


def _moba_attention(qk_rot, vt, onehot2, selneg, B, S):
    T = qk_rot.shape[0]
    tq = tk = min(ATTN_TILE, S)
    nq = S // tq
    npair = N_HEADS // 2
    ncol = D_MODEL // LANES
    nblk = S // MOBA_BLOCK
    return pl.pallas_call(
        functools.partial(_moba_kernel, tq=tq, tk=tk, nblk=nblk),
        grid=(B, npair, nq),
        in_specs=[pl.BlockSpec((tq, LANES), lambda b, hp, i: (b * nq + i, hp)),
                  pl.BlockSpec((S, LANES), lambda b, hp, i: (b, ncol + hp)),
                  pl.BlockSpec((LANES, S), lambda b, hp, i: (hp, b)),
                  pl.BlockSpec((S, LANES), lambda b, hp, i: (0, 0)),
                  pl.BlockSpec((None, None, tq, LANES), lambda b, hp, i: (b, hp, i, 0))],
        out_specs=pl.BlockSpec((tq, LANES), lambda b, hp, i: (b * nq + i, hp)),
        out_shape=jax.ShapeDtypeStruct((T, D_MODEL), BF16),
        scratch_shapes=[pltpu.VMEM((2, 1, tq), F32), pltpu.VMEM((2, 1, tq), F32),
                        pltpu.VMEM((2, LANES, tq), F32)],
        compiler_params=_cparams(("parallel", "parallel", "arbitrary")), name="moba_attn")(
            qk_rot, qk_rot, vt, onehot2, selneg)


def _router_kernel(wt_ref, x_ref, rb_ref, e_ref, g_ref):
    tm = x_ref.shape[0]
    gsz = N_EXPERTS // N_GROUPS
    scores = jax.nn.sigmoid(_nt_dot(wt_ref[...], x_ref[...], HIGHEST))
    biased = scores + rb_ref[...]
    member = lax.broadcasted_iota(jnp.int32, (gsz, tm), 0).astype(F32)
    gscore = []
    for g in range(N_GROUPS):
        v = biased[g * gsz:(g + 1) * gsz, :]
        m1 = jnp.max(v, axis=0, keepdims=True)
        i1 = jnp.min(jnp.where(v == m1, member, float(gsz)), axis=0, keepdims=True)
        m2 = jnp.max(jnp.where(member == i1, -jnp.inf, v), axis=0, keepdims=True)
        gscore.append(m1 + m2)
    gsel = _topk_rows(jnp.concatenate(gscore, axis=0), TOPK_GROUPS)
    emask = jnp.concatenate([jnp.broadcast_to(gsel[g:g + 1, :], (gsz, tm)) for g in range(N_GROUPS)], axis=0)
    work = jnp.where(emask > 0.0, biased, NEG)
    erow = lax.broadcasted_iota(jnp.int32, (N_EXPERTS, tm), 0).astype(F32)
    idxs, vals = [], []
    for _ in range(TOP_K):
        m = jnp.max(work, axis=0, keepdims=True)
        idx = jnp.min(jnp.where(work == m, erow, float(N_EXPERTS)), axis=0, keepdims=True)
        pick = erow == idx
        idxs.append(idx)
        vals.append(jnp.sum(jnp.where(pick, scores, 0.0), axis=0, keepdims=True))
        work = jnp.where(pick, -jnp.inf, work)
    gw = jnp.concatenate(vals, axis=0)
    e_ref[...] = jnp.concatenate(idxs, axis=0).astype(jnp.int32)
    g_ref[...] = gw / jnp.sum(gw, axis=0, keepdims=True) * ROUTED_SCALE


def _router(x1, router_w, router_b):
    T, D = x1.shape
    tm = min(ROW_TILE, T)
    return pl.pallas_call(
        _router_kernel, grid=(T // tm,),
        in_specs=[pl.BlockSpec((N_EXPERTS, D), lambda i: (0, 0)), pl.BlockSpec((tm, D), lambda i: (i, 0)),
                  pl.BlockSpec((N_EXPERTS, 1), lambda i: (0, 0))],
        out_specs=[pl.BlockSpec((TOP_K, tm), lambda i: (0, i))] * 2,
        out_shape=[jax.ShapeDtypeStruct((TOP_K, T), jnp.int32), jax.ShapeDtypeStruct((TOP_K, T), F32)],
        compiler_params=_cparams(("parallel",)), name="moe_router")(
            router_w.T, x1, router_b.reshape(N_EXPERTS, 1))


def _rank_kernel(e_ref, tri_ref, rank_ref, cnt_ref, carry_ref):
    @pl.when(pl.program_id(0) == 0)
    def _():
        carry_ref[...] = jnp.zeros(carry_ref.shape, F32)

    tm = e_ref.shape[1]
    erow = lax.broadcasted_iota(jnp.int32, (N_EXPERTS, tm), 0)
    tri = tri_ref[...]
    base = carry_ref[...]
    ranks = []
    for k in range(TOP_K):
        oh = erow == e_ref[k:k + 1, :]
        ohb = jnp.where(oh, 1.0, 0.0).astype(BF16)
        incl = jnp.dot(ohb, tri, preferred_element_type=F32)
        ranks.append(jnp.sum(jnp.where(oh, base + incl - 1.0, 0.0), axis=0, keepdims=True))
        base = base + incl[:, tm - 1:tm]
    carry_ref[...] = base
    rank_ref[...] = jnp.concatenate(ranks, axis=0).astype(jnp.int32)
    cnt_ref[...] = jnp.broadcast_to(base, cnt_ref.shape)


def _expert_ranks(eidx_t):
    K, T = eidx_t.shape
    tm = min(ROW_TILE, T)
    tri = jnp.asarray(np.triu(np.ones((tm, tm), np.float32)), BF16)
    return pl.pallas_call(
        _rank_kernel, grid=(T // tm,),
        in_specs=[pl.BlockSpec((K, tm), lambda i: (0, i)), pl.BlockSpec((tm, tm), lambda i: (0, 0))],
        out_specs=[pl.BlockSpec((K, tm), lambda i: (0, i)), pl.BlockSpec((N_EXPERTS, LANES), lambda i: (0, 0))],
        out_shape=[jax.ShapeDtypeStruct((K, T), jnp.int32), jax.ShapeDtypeStruct((N_EXPERTS, LANES), F32)],
        scratch_shapes=[pltpu.VMEM((N_EXPERTS, 1), F32)],
        compiler_params=_cparams(("arbitrary",)), name="moe_rank")(eidx_t, tri)


def _dest_kernel(e_ref, rank_ref, ps_ref, d_ref):
    tm = e_ref.shape[1]
    erow = lax.broadcasted_iota(jnp.int32, (N_EXPERTS, tm), 0)
    ps = ps_ref[...]
    rows = []
    for k in range(TOP_K):
        oh = erow == e_ref[k:k + 1, :]
        rows.append(jnp.sum(jnp.where(oh, ps, 0.0), axis=0, keepdims=True))
    d_ref[...] = jnp.concatenate(rows, axis=0).astype(jnp.int32) + rank_ref[...]


def _expert_dest(eidx_t, rank_t, pstarts):
    K, T = eidx_t.shape
    tm = min(ROW_TILE, T)
    blk = pl.BlockSpec((K, tm), lambda i: (0, i))
    return pl.pallas_call(
        _dest_kernel, grid=(T // tm,),
        in_specs=[blk, blk, pl.BlockSpec((N_EXPERTS, 1), lambda i: (0, 0))],
        out_specs=blk, out_shape=jax.ShapeDtypeStruct((K, T), jnp.int32),
        compiler_params=_cparams(("parallel",)), name="moe_dest")(
            eidx_t, rank_t, pstarts.astype(F32).reshape(N_EXPERTS, 1))


def _expert_kernel(be_ref, nb_ref, x_ref, wg_ref, wu_ref, wd_ref, o_ref):
    b = pl.program_id(0)

    @pl.when(b < nb_ref[0])
    def _():
        x = x_ref[...]
        h = jax.nn.silu(jnp.dot(x, wg_ref[...], preferred_element_type=F32)) * jnp.dot(
            x, wu_ref[...], preferred_element_type=F32)
        o_ref[...] = jnp.dot(h.astype(BF16), wd_ref[...], preferred_element_type=F32).astype(o_ref.dtype)

    @pl.when(b >= nb_ref[0])
    def _():
        o_ref[...] = jnp.zeros(o_ref.shape, o_ref.dtype)


def _expert_ffn(blk_e, nb_used, xs, wg, wu, wd):
    P, D = xs.shape
    NB = P // MOE_BLOCK
    E = EXPERT_DIM
    grid_spec = pltpu.PrefetchScalarGridSpec(
        num_scalar_prefetch=2, grid=(NB,),
        in_specs=[pl.BlockSpec((MOE_BLOCK, D), lambda b, be, nb: (b, 0)),
                  pl.BlockSpec((None, D, E), lambda b, be, nb: (be[b], 0, 0)),
                  pl.BlockSpec((None, D, E), lambda b, be, nb: (be[b], 0, 0)),
                  pl.BlockSpec((None, E, D), lambda b, be, nb: (be[b], 0, 0))],
        out_specs=pl.BlockSpec((MOE_BLOCK, D), lambda b, be, nb: (b, 0)))
    return pl.pallas_call(
        _expert_kernel, grid_spec=grid_spec, out_shape=jax.ShapeDtypeStruct((P, D), BF16),
        compiler_params=_cparams(("arbitrary",)), name="moe_experts")(blk_e, nb_used, xs, wg, wu, wd)


def _post_moe_kernel(x_ref, y_ref, gw_ref, p_ref, sg_ref, su_ref, sd_ref, g_ref, b_ref, wg_ref, wp_ref, o_ref):
    x = x_ref[...]
    D = x.shape[1]
    xb = x.astype(BF16)
    h = jax.nn.silu(jnp.dot(xb, sg_ref[...], preferred_element_type=F32)) * jnp.dot(
        xb, su_ref[...], preferred_element_type=F32)
    ffn = jnp.dot(h.astype(BF16), sd_ref[...], preferred_element_type=F32)
    gw = gw_ref[...]
    for k in range(TOP_K):
        ffn = ffn + gw[:, k:k + 1] * y_ref[:, k * D:(k + 1) * D].astype(F32)
    z = DN_ALPHA * x + ffn
    x2 = _layer_norm(z, g_ref[...], b_ref[...])
    gate = jax.nn.sigmoid(jnp.dot(x2.astype(BF16), wg_ref[...], preferred_element_type=F32))
    proj = jnp.dot(p_ref[...].astype(BF16), wp_ref[...], preferred_element_type=F32)
    o_ref[...] = x2 + gate * proj


def _post_moe(x1, yg, gw, p, sg, su, sd, g, b, wgate, wproj):
    T, D = x1.shape
    tm = min(ROW_TILE // 2, T)
    PD = p.shape[1]
    SD = sg.shape[1]
    row = pl.BlockSpec((tm, D), lambda i: (i, 0))
    vec = pl.BlockSpec((1, D), lambda i: (0, 0))
    full = lambda r, c: pl.BlockSpec((r, c), lambda i: (0, 0))
    return pl.pallas_call(
        _post_moe_kernel, grid=(T // tm,),
        in_specs=[row, pl.BlockSpec((tm, TOP_K * D), lambda i: (i, 0)), pl.BlockSpec((tm, TOP_K), lambda i: (i, 0)),
                  pl.BlockSpec((tm, PD), lambda i: (i, 0)), full(D, SD), full(D, SD),
                  full(SD, D), vec, vec, full(D, D), full(PD, D)],
        out_specs=row, out_shape=jax.ShapeDtypeStruct((T, D), F32),
        compiler_params=_cparams(("parallel",)), name="post_moe")(
            x1, yg, gw, p, sg, su, sd, g.reshape(1, D), b.reshape(1, D), wgate, wproj)


def _rope_tables(positions):
    inv = 1.0 / (ROPE_THETA ** (jnp.arange(0, HEAD_DIM, 2, dtype=F32) / HEAD_DIM))
    ang = positions.astype(F32).reshape(-1)[:, None] * inv
    cos, sin = jnp.cos(ang), jnp.sin(ang)
    cosf = jnp.concatenate([cos] * (LANES // (HEAD_DIM // 2)), axis=1)
    sinf = jnp.concatenate([-sin, sin] * (LANES // HEAD_DIM), axis=1)
    return cosf, sinf


def _fox_mixer(x, B, S, w_in, b_f):
    D = D_MODEL
    wb = w_in[:, :3 * D].astype(BF16)
    qk = _proj(x, wb[:, :2 * D])
    vt = _proj_t(wb[:, 2 * D:].T, x)
    fl = _proj(x, _pad_cols(w_in[:, 3 * D:], LANES), out_dtype=F32, tn=LANES, precision=HIGHEST)
    cp = _fox_gate(fl, b_f, B, S)
    return [_fox_attention(qk, cp, vt, B, S)]


def _nsa_mixer(x, B, S, cosf, sinf, w_in, b_gate, pe_k, pe_v, ck_w1, ck_w2, cv_w1, cv_w2):
    D, G, HD = D_MODEL, NSA_KV_GROUPS, HEAD_DIM
    HG = NSA_HEADS_PER_GROUP
    kvw = G * HD
    wb = w_in[:, :D + 6 * kvw].astype(BF16)
    q_c, q_r = _proj(x, wb[:, :D], mode="both", cos=cosf, sin=sinf)
    w_rot = jnp.concatenate([wb[:, D + 2 * kvw:D + 3 * kvw], wb[:, D + 4 * kvw:D + 5 * kvw]], axis=1)
    k_rot = _proj(x, w_rot, mode="rope", cos=cosf, sin=sinf)
    kvc = _proj(x, wb[:, D:D + 2 * kvw])
    w_v = jnp.concatenate([wb[:, D + 3 * kvw:D + 4 * kvw], wb[:, D + 5 * kvw:D + 6 * kvw]], axis=1)
    vt = _proj_t(w_v.T, x)
    gl = _proj(x, _pad_cols(w_in[:, D + 6 * kvw:], LANES), out_dtype=F32, tn=LANES, precision=HIGHEST)
    bg = _pad_cols(b_gate.reshape(1, -1), LANES)

    def grouped(t2d):
        t = t2d.reshape(B, S, G, HD).transpose(0, 2, 1, 3)
        return jnp.concatenate([t, t], axis=-1)

    n_chunks = S // NSA_CMP_STRIDE
    n_cmp = n_chunks - NSA_CMP_LEN // NSA_CMP_STRIDE + 1
    ncp = n_chunks

    def compress(t2d, pe, w1, w2):
        ch = t2d.reshape(B, n_chunks, NSA_CMP_STRIDE, G, HD).transpose(0, 1, 3, 2, 4)
        ch = ch.reshape(B, n_chunks, G, NSA_CMP_STRIDE * HD)
        flat = jnp.concatenate([ch[:, :n_cmp], ch[:, 1:n_cmp + 1]], axis=-1)
        flat = jnp.pad(flat, ((0, 0), (0, ncp - n_cmp), (0, 0), (0, 0))).reshape(B * ncp * G, -1)
        out = _nsa_compress(flat, pe.reshape(1, -1), w1.astype(BF16), _pad_cols(w2, LANES).astype(BF16))
        return out[:, :HD].reshape(B, ncp, G, HD).astype(BF16)

    kc = compress(kvc[:, :kvw], pe_k, ck_w1, ck_w2).transpose(0, 2, 1, 3)
    kk_c = jnp.concatenate([kc, kc], axis=-1)
    vt_c = compress(kvc[:, kvw:], pe_v, cv_w1, cv_w2).transpose(0, 2, 3, 1)

    n_sel = S // NSA_SEL_LEN
    assert n_sel <= LANES
    cmp_start = np.arange(ncp) * NSA_CMP_STRIDE
    sel_start = np.arange(LANES) * NSA_SEL_LEN
    overlap = ((cmp_start[:, None] < sel_start[None, :] + NSA_SEL_LEN)
               & (cmp_start[:, None] + NSA_CMP_LEN > sel_start[None, :])
               & (np.arange(ncp)[:, None] < n_cmp) & (np.arange(LANES)[None, :] < n_sel))
    overlap_t = jnp.asarray(overlap.T, BF16)
    ex = np.zeros((3, G, LANES, HG * HD), np.float32)
    for br in range(3):
        for g in range(G):
            for hg in range(HG):
                ex[br, g, (g * HG + hg) * 3 + br, hg * HD:(hg + 1) * HD] = 1.0
    ex = jnp.asarray(ex, BF16)

    o_c, selneg = _nsa_cmp_branch(q_c, kk_c, vt_c, overlap_t, gl, bg, ex[0], B, S)
    onehot = jnp.asarray((np.arange(S)[:, None] // NSA_SEL_LEN) == np.arange(LANES)[None, :], BF16)
    kk_s = grouped(k_rot[:, :kvw])
    kk_w = grouped(k_rot[:, kvw:])
    o_s = _nsa_kv_branch("sel", q_r, kk_s, vt[:kvw], gl, bg, ex[1], B, S, onehot=onehot, selneg=selneg)
    o_w = _nsa_kv_branch("win", q_r, kk_w, vt[kvw:], gl, bg, ex[2], B, S)
    return [o_c, o_s, o_w]


def _moba_mixer(x, B, S, cosf, sinf, w_in):
    D, H, HD = D_MODEL, N_HEADS, HEAD_DIM
    wb = w_in.astype(BF16)
    qk_rot = _proj(x, wb[:, :2 * D], mode="rope", cos=cosf, sin=sinf)
    vt = _proj_t(wb[:, 2 * D:].T, x)
    nblk = S // MOBA_BLOCK
    assert S % MOBA_BLOCK == 0 and 2 * nblk <= LANES and nblk % SUBLANES == 0
    kmean = _moba_kmean(qk_rot[:, D:]).reshape(B, nblk, H // 2, 2, HD)
    km = kmean.transpose(0, 2, 3, 1, 4)
    rmat_t = jnp.zeros((B, H // 2, LANES, 2, HD), F32)
    rmat_t = rmat_t.at[:, :, :nblk, 0].set(km[:, :, 0]).at[:, :, nblk:2 * nblk, 1].set(km[:, :, 1])
    selneg = _moba_select(qk_rot, rmat_t.reshape(B, H // 2, LANES, LANES), B, S)
    blk_of = np.arange(S) // MOBA_BLOCK
    oh = np.zeros((S, LANES), np.float32)
    oh[np.arange(S), blk_of] = 1.0
    oh[np.arange(S), nblk + blk_of] = 1.0
    return [_moba_attention(qk_rot, vt, jnp.asarray(oh, BF16), selneg, B, S)]


def _moe_dispatch(x1, router_w, router_b, wg, wu, wd):
    T, D = x1.shape
    eidx_t, gw_t = _router(x1, router_w, router_b)
    rank_t, cnt = _expert_ranks(eidx_t)
    counts = cnt[:, 0].astype(jnp.int32)
    padded = (counts + MOE_BLOCK - 1) // MOE_BLOCK * MOE_BLOCK
    pends = jnp.cumsum(padded)
    pstarts = pends - padded
    R = T * TOP_K
    P = -(-(R + N_EXPERTS * (MOE_BLOCK - 1)) // MOE_BLOCK) * MOE_BLOCK
    NB = P // MOE_BLOCK
    blk_start = jnp.arange(NB, dtype=jnp.int32) * MOE_BLOCK
    blk_e = jnp.minimum(jnp.sum(pends[None, :] <= blk_start[:, None], axis=1), N_EXPERTS - 1).astype(jnp.int32)
    nb_used = (pends[-1] // MOE_BLOCK).astype(jnp.int32).reshape(1)
    dest_t = _expert_dest(eidx_t, rank_t, pstarts)

    tok = jnp.broadcast_to(jnp.arange(T, dtype=jnp.int32)[None, :], (TOP_K, T))
    buf_tok = jnp.full((P,), T, jnp.int32).at[dest_t.reshape(-1)].set(tok.reshape(-1))
    x_pad = jnp.concatenate([x1.astype(BF16), jnp.zeros((1, D), BF16)], axis=0)
    xs = x_pad[buf_tok]
    yb = _expert_ffn(blk_e, nb_used, xs, wg, wu, wd)
    yg = yb[dest_t.T.reshape(-1)].reshape(T, TOP_K * D)
    return yg, gw_t.T


def kernel(x, p, positions, fox_w_in, fox_b_f, fox_w_out, nsa_w_in, nsa_b_gate, nsa_pe_k, nsa_pe_v,
           nsa_cmp_k_w1, nsa_cmp_k_w2, nsa_cmp_v_w1, nsa_cmp_v_w2, nsa_w_out, moba_w_in, moba_w_out,
           ln1_g, ln1_b, router_w, router_b, exp_w_gate, exp_w_up, exp_w_down,
           sh_w_gate, sh_w_up, sh_w_down, ln2_g, ln2_b, ple_w_gate, ple_w_proj):
    B, S, D = x.shape
    T = B * S
    depth = p.shape[0]
    cosf, sinf = _rope_tables(positions)
    xt = x.reshape(T, D)
    for i in range(depth):
        kind, j = i % N_MIXERS, i // N_MIXERS
        if kind == 0:
            o_list = _fox_mixer(xt, B, S, fox_w_in[j], fox_b_f[j])
            w_out = fox_w_out[j]
        elif kind == 1:
            o_list = _nsa_mixer(xt, B, S, cosf, sinf, nsa_w_in[j], nsa_b_gate[j], nsa_pe_k[j], nsa_pe_v[j],
                                nsa_cmp_k_w1[j], nsa_cmp_k_w2[j], nsa_cmp_v_w1[j], nsa_cmp_v_w2[j])
            w_out = nsa_w_out[j]
        else:
            o_list = _moba_mixer(xt, B, S, cosf, sinf, moba_w_in[j])
            w_out = moba_w_out[j]
        x1 = _outproj_ln(o_list, w_out.astype(BF16), xt, ln1_g[i], ln1_b[i])
        yg, gw = _moe_dispatch(x1, router_w[i], router_b[i], exp_w_gate[i].astype(BF16),
                               exp_w_up[i].astype(BF16), exp_w_down[i].astype(BF16))
        xt = _post_moe(x1, yg, gw, p[i].reshape(T, -1), sh_w_gate[i].astype(BF16), sh_w_up[i].astype(BF16),
                       sh_w_down[i].astype(BF16), ln2_g[i], ln2_b[i],
                       ple_w_gate[i].astype(BF16), ple_w_proj[i].astype(BF16))
    return xt.reshape(B, S, D)
```

```python
import functools

import jax
import jax.numpy as jnp
import numpy as np
from jax import lax
from jax.experimental import pallas as pl
from jax.experimental.pallas import tpu as pltpu

D_MODEL = 1024
DEPTH = 4
N_HEADS = 16
HEAD_DIM = 64
ATTN_SCALE = HEAD_DIM ** -0.5
LOG2E = 1.4426950408889634
Q_SCALE = ATTN_SCALE * LOG2E
ROPE_THETA = 10000.0
N_MIXERS = 3

NSA_KV_GROUPS = 4
NSA_HEADS_PER_GROUP = N_HEADS // NSA_KV_GROUPS
NSA_CMP_LEN = 32
NSA_CMP_STRIDE = 16
NSA_SEL_LEN = 64
NSA_SEL_TOPN = 16
NSA_WINDOW = 512
NSA_FORCE_BONUS = 1e4

MOBA_BLOCK = 256
MOBA_TOPK = 3

N_EXPERTS = 64
EXPERT_DIM = 256
TOP_K = 8
N_GROUPS = 8
TOPK_GROUPS = 4
ROUTED_SCALE = 2.5
MOE_BLOCK = 256

DN_ALPHA = (2 * DEPTH) ** 0.25
LN_EPS = 1e-5
NEG = -1e30

LANES = 128
SUBLANES = 8
ATTN_TILE = 512
KV_TILE = 1024
ROW_TILE = 512
VMEM_LIMIT = 48 * 1024 * 1024
C_SPLIT = 3
ONES_ROWS = 16

F32 = jnp.float32
BF16 = jnp.bfloat16
HIGHEST = lax.Precision.HIGHEST


def _cparams(sem):
    return pltpu.CompilerParams(dimension_semantics=sem, vmem_limit_bytes=VMEM_LIMIT)


def _log2(n):
    assert n & (n - 1) == 0
    return n.bit_length() - 1


def _nt_dot(a, b, precision=None):
    return lax.dot_general(a, b, (((1,), (1,)), ((), ())), preferred_element_type=F32,
                           precision=precision)


def _topk_rows(work, n):
    rows = lax.broadcasted_iota(jnp.int32, work.shape, 0).astype(F32)
    chosen = jnp.zeros(work.shape, F32)
    for _ in range(n):
        m = jnp.max(work, axis=0, keepdims=True)
        idx = jnp.min(jnp.where(work == m, rows, float(work.shape[0])), axis=0, keepdims=True)
        pick = rows == idx
        chosen = jnp.where(pick, 1.0, chosen)
        work = jnp.where(pick, -jnp.inf, work)
    return chosen


def _proj_kernel(*refs, mode, precision, scaled):
    refs = list(refs)
    x_ref, w_ref = refs[:2]
    del refs[:2]
    scale_ref = refs.pop(0) if scaled else None
    if mode == "none":
        (o_ref,) = refs
    elif mode == "rope":
        cos_ref, sin_ref, r_ref = refs
    else:
        cos_ref, sin_ref, o_ref, r_ref = refs
    x = x_ref[...]
    w = w_ref[...]
    if x.dtype != w.dtype:
        x = x.astype(w.dtype)
    acc = jnp.dot(x, w, preferred_element_type=F32, precision=precision)
    if scaled:
        acc = acc * scale_ref[...]
    if mode in ("none", "both"):
        o_ref[...] = acc.astype(o_ref.dtype)
    if mode in ("rope", "both"):
        tn = acc.shape[1]
        rep = tn // LANES
        cosf = jnp.concatenate([cos_ref[...]] * rep, axis=1)
        sinf = jnp.concatenate([sin_ref[...]] * rep, axis=1)
        lane = lax.broadcasted_iota(jnp.int32, acc.shape, 1)
        first_half = (lane & (HEAD_DIM // 2)) == 0
        swapped = jnp.where(first_half, pltpu.roll(acc, tn - HEAD_DIM // 2, 1),
                            pltpu.roll(acc, HEAD_DIM // 2, 1))
        r_ref[...] = (acc * cosf + swapped * sinf).astype(r_ref.dtype)


def _proj(x, w, *, mode="none", cos=None, sin=None, out_dtype=BF16, tn=512, precision=None, col_scale=None):
    T, K = x.shape
    N = w.shape[1]
    tm = min(ROW_TILE, T)
    tn = min(tn, N)
    assert T % tm == 0 and N % tn == 0
    grid = (T // tm, N // tn)
    in_specs = [pl.BlockSpec((tm, K), lambda i, j: (i, 0)),
                pl.BlockSpec((K, tn), lambda i, j: (0, j))]
    args = [x, w]
    if col_scale is not None:
        in_specs.append(pl.BlockSpec((1, tn), lambda i, j: (0, j)))
        args.append(col_scale.reshape(1, N).astype(F32))
    if mode != "none":
        in_specs += [pl.BlockSpec((tm, LANES), lambda i, j: (i, 0))] * 2
        args += [cos, sin]
    o_spec = pl.BlockSpec((tm, tn), lambda i, j: (i, j))
    o_shape = jax.ShapeDtypeStruct((T, N), out_dtype)
    if mode == "both":
        out_specs, out_shape = [o_spec, o_spec], [o_shape, o_shape]
    else:
        out_specs, out_shape = o_spec, o_shape
    return pl.pallas_call(
        functools.partial(_proj_kernel, mode=mode, precision=precision, scaled=col_scale is not None),
        grid=grid, in_specs=in_specs, out_specs=out_specs, out_shape=out_shape,
        compiler_params=_cparams(("parallel", "parallel")), name=f"proj_{mode}")(*args)


def _proj_t_kernel(wt_ref, x_ref, o_ref, *, precision):
    wt = wt_ref[...]
    x = x_ref[...]
    if x.dtype != wt.dtype:
        x = x.astype(wt.dtype)
    o_ref[...] = _nt_dot(wt, x, precision).astype(o_ref.dtype)


def _proj_t(wt, x, *, out_dtype=BF16, precision=None):
    N, K = wt.shape
    T = x.shape[0]
    tm = min(ROW_TILE, T)
    tn = min(512, N)
    assert T % tm == 0 and N % tn == 0
    return pl.pallas_call(
        functools.partial(_proj_t_kernel, precision=precision),
        grid=(T // tm, N // tn),
        in_specs=[pl.BlockSpec((tn, K), lambda i, j: (j, 0)), pl.BlockSpec((tm, K), lambda i, j: (i, 0))],
        out_specs=pl.BlockSpec((tn, tm), lambda i, j: (j, i)),
        out_shape=jax.ShapeDtypeStruct((N, T), out_dtype),
        compiler_params=_cparams(("parallel", "parallel")), name="proj_t")(wt, x)


def _pad_cols(w, n):
    return jnp.pad(w, ((0, 0), (0, n - w.shape[1])))


def _layer_norm(z, g, b):
    mu = jnp.mean(z, axis=-1, keepdims=True)
    zc = z - mu
    var = jnp.mean(zc * zc, axis=-1, keepdims=True)
    return zc * lax.rsqrt(var + LN_EPS) * g + b


def _outproj_ln_kernel(*refs, n_o):
    o_refs = refs[:n_o]
    w_ref, x_ref, g_ref, b_ref, y_ref, yb_ref = refs[n_o:]
    if n_o == 1:
        o = o_refs[0][...]
    else:
        o = o_refs[0][...].astype(F32)
        for r in o_refs[1:]:
            o = o + r[...].astype(F32)
        o = o.astype(BF16)
    mix = jnp.dot(o, w_ref[...], preferred_element_type=F32)
    z = DN_ALPHA * x_ref[...] + mix
    y = _layer_norm(z, g_ref[...], b_ref[...])
    y_ref[...] = y
    yb_ref[...] = y.astype(BF16)


def _outproj_ln(o_list, w, x, g, b):
    T, D = x.shape
    tm = min(ROW_TILE, T)
    row = pl.BlockSpec((tm, D), lambda i: (i, 0))
    vec = pl.BlockSpec((1, D), lambda i: (0, 0))
    return pl.pallas_call(
        functools.partial(_outproj_ln_kernel, n_o=len(o_list)),
        grid=(T // tm,),
        in_specs=[row] * len(o_list) + [pl.BlockSpec((D, D), lambda i: (0, 0)), row, vec, vec],
        out_specs=[row, row],
        out_shape=[jax.ShapeDtypeStruct((T, D), F32), jax.ShapeDtypeStruct((T, D), BF16)],
        compiler_params=_cparams(("parallel",)), name="outproj_ln")(
            *o_list, w, x, g.reshape(1, D), b.reshape(1, D))


def _flash_init(m_ref, acc_ref):
    m_ref[...] = jnp.full(m_ref.shape, NEG, F32)
    acc_ref[...] = jnp.zeros(acc_ref.shape, F32)


def _flash_step(sT, lhs, m_ref, acc_ref, h):
    m_prev = m_ref[h]
    m_new = jnp.maximum(m_prev, jnp.max(sT, axis=0, keepdims=True))
    p = jnp.exp2(sT - m_new)
    alpha = jnp.exp2(m_prev - m_new)
    acc_ref[h] = alpha * acc_ref[h] + jnp.dot(lhs, p.astype(BF16), preferred_element_type=F32)
    m_ref[h] = m_new


def _causal_t(tk, tq, shift=0, strict_lower=False):
    key = lax.broadcasted_iota(jnp.int32, (tk, tq), 0)
    qry = lax.broadcasted_iota(jnp.int32, (tk, tq), 1)
    return (key > qry) if strict_lower else (key <= qry + shift)


def _causal_sweep(i, tq, tk, step):
    r = tk // tq
    assert tk == r * tq
    n_full = i >> _log2(r)

    def body(j, carry):
        step(pl.multiple_of(j * tk, tk), tk, None)
        return carry

    lax.fori_loop(0, n_full, body, 0)
    if r == 1:
        step(pl.multiple_of(i * tq, tq), tq, _causal_t(tq, tq))
    else:
        for rem in range(r):
            @pl.when(i - n_full * r == rem)
            def _(rem=rem):
                size = (rem + 1) * tq
                step(pl.multiple_of(n_full * tk, tk), size, _causal_t(size, tq, shift=rem * tq))


def _head_of_pair(q_pair, which):
    lane = lax.broadcasted_iota(jnp.int32, q_pair.shape, 1)
    keep = (lane < HEAD_DIM) if which == 0 else (lane >= HEAD_DIM)
    return jnp.where(keep, q_pair, jnp.zeros_like(q_pair))


def _pair_rows(a_top, a_bot):
    row = lax.broadcasted_iota(jnp.int32, a_top.shape, 0)
    return jnp.where(row < HEAD_DIM, a_top, a_bot)


def _pair_lhs(vt, which):
    row = lax.broadcasted_iota(jnp.int32, vt.shape, 0)
    keep = (row < HEAD_DIM) if which == 0 else (row >= HEAD_DIM)
    return jnp.where(keep, vt, jnp.ones_like(vt))


def _pair_finish(acc_ref):
    a0, a1 = acc_ref[0], acc_ref[1]
    return _pair_rows(a0 / a0[HEAD_DIM:HEAD_DIM + 1, :], a1 / a1[0:1, :])


def _fox_gate_kernel(fl_ref, bf_ref, tri_ref, ex_ref, cp_ref, carry_ref):
    @pl.when(pl.program_id(1) == 0)
    def _():
        carry_ref[...] = jnp.zeros(carry_ref.shape, F32)

    z = fl_ref[...] + bf_ref[...]
    log_f = jnp.minimum(z, 0.0) - jnp.log1p(jnp.exp(-jnp.abs(z)))
    c = jnp.dot(tri_ref[...], log_f, preferred_element_type=F32, precision=HIGHEST) + carry_ref[...]
    carry_ref[...] = c[-1:, :]
    out = jnp.zeros(c.shape, F32)
    rem = c * LOG2E
    for piece in range(C_SPLIT):
        part = rem.astype(BF16)
        rem = rem - part.astype(F32)
        out = out + jnp.dot(part, ex_ref[piece], preferred_element_type=F32)
    cp_ref[...] = out.astype(BF16)


def _fox_gate(fl, b_f, B, S):
    T = fl.shape[0]
    tm = min(ROW_TILE, S)
    ns = S // tm
    tri = jnp.asarray(np.tril(np.ones((tm, tm), np.float32)))
    ex = np.zeros((C_SPLIT, LANES, LANES), np.float32)
    for h in range(N_HEADS):
        for j in range(C_SPLIT):
            ex[j, h, C_SPLIT * h + j] = 1.0
    bf = _pad_cols(b_f.reshape(1, -1), LANES)
    return pl.pallas_call(
        _fox_gate_kernel, grid=(B, ns),
        in_specs=[pl.BlockSpec((tm, LANES), lambda b, s: (b * ns + s, 0)),
                  pl.BlockSpec((1, LANES), lambda b, s: (0, 0)),
                  pl.BlockSpec((tm, tm), lambda b, s: (0, 0)),
                  pl.BlockSpec((C_SPLIT, LANES, LANES), lambda b, s: (0, 0, 0))],
        out_specs=pl.BlockSpec((tm, LANES), lambda b, s: (b * ns + s, 0)),
        out_shape=jax.ShapeDtypeStruct((T, LANES), BF16),
        scratch_shapes=[pltpu.VMEM((1, LANES), F32)],
        compiler_params=_cparams(("parallel", "arbitrary")), name="fox_gate")(
            fl, bf, tri, jnp.asarray(ex, BF16))


def _fox_kernel(q_ref, k_ref, cp_ref, vt_ref, o_ref, m_ref, acc_ref, *, tq, tk):
    hp = pl.program_id(1)
    i = pl.program_id(2)
    _flash_init(m_ref, acc_ref)
    q = q_ref[...]
    lane = lax.broadcasted_iota(jnp.int32, q.shape, 1)
    qa = []
    for h in range(2):
        lo = C_SPLIT * (2 * hp + h)
        minus_one = jnp.where((lane >= lo) & (lane < lo + C_SPLIT), -1.0, 0.0).astype(q.dtype)
        qa.append(jnp.concatenate([_head_of_pair(q, h), minus_one], axis=1))

    def step(off, size, mask):
        k = jnp.concatenate([k_ref[pl.ds(off, size), :], cp_ref[pl.ds(off, size), :]], axis=1)
        vt = vt_ref[:, pl.ds(off, size)]
        for h in range(2):
            sT = _nt_dot(k, qa[h])
            if mask is not None:
                sT = jnp.where(mask, sT, NEG)
            _flash_step(sT, _pair_lhs(vt, h), m_ref, acc_ref, h)

    _causal_sweep(i, tq, tk, step)
    o_ref[...] = _pair_finish(acc_ref).T.astype(o_ref.dtype)


def _fox_attention(qk, cp, vt, B, S):
    T = qk.shape[0]
    tq = min(ATTN_TILE, S)
    tk = min(KV_TILE, S)
    nq = S // tq
    npair = N_HEADS // 2
    ncol = D_MODEL // LANES
    return pl.pallas_call(
        functools.partial(_fox_kernel, tq=tq, tk=tk),
        grid=(B, npair, nq),
        in_specs=[pl.BlockSpec((tq, LANES), lambda b, hp, i: (b * nq + i, hp)),
                  pl.BlockSpec((S, LANES), lambda b, hp, i: (b, ncol + hp)),
                  pl.BlockSpec((S, LANES), lambda b, hp, i: (b, 0)),
                  pl.BlockSpec((LANES, S), lambda b, hp, i: (hp, b))],
        out_specs=pl.BlockSpec((tq, LANES), lambda b, hp, i: (b * nq + i, hp)),
        out_shape=jax.ShapeDtypeStruct((T, D_MODEL), BF16),
        scratch_shapes=[pltpu.VMEM((2, 1, tq), F32), pltpu.VMEM((2, LANES, tq), F32)],
        compiler_params=_cparams(("parallel", "parallel", "arbitrary")), name="fox_attn")(
            qk, qk, cp, vt)


def _gate_expand(gl_ref, bg_ref, ex_ref):
    sig = jax.nn.sigmoid(gl_ref[...] + bg_ref[...])
    hi = sig.astype(BF16)
    lo = (sig - hi.astype(F32)).astype(BF16)
    ex = ex_ref[...]
    return jnp.dot(hi, ex, preferred_element_type=F32) + jnp.dot(lo, ex, preferred_element_type=F32)


def _nsa_compress_kernel(f_ref, pe_ref, w1_ref, w2_ref, o_ref):
    blk = (f_ref[...].astype(F32) + pe_ref[...]).astype(BF16)
    h = jax.nn.gelu(jnp.dot(blk, w1_ref[...], preferred_element_type=F32))
    o_ref[...] = jnp.dot(h.astype(BF16), w2_ref[...], preferred_element_type=F32)


def _nsa_compress(flat, pe_flat, w1, w2p):
    M, K = flat.shape
    tm = min(ROW_TILE, M)
    Hc = w1.shape[1]
    return pl.pallas_call(
        _nsa_compress_kernel, grid=(M // tm,),
        in_specs=[pl.BlockSpec((tm, K), lambda i: (i, 0)), pl.BlockSpec((1, K), lambda i: (0, 0)),
                  pl.BlockSpec((K, Hc), lambda i: (0, 0)), pl.BlockSpec((Hc, LANES), lambda i: (0, 0))],
        out_specs=pl.BlockSpec((tm, LANES), lambda i: (i, 0)),
        out_shape=jax.ShapeDtypeStruct((M, LANES), F32),
        compiler_params=_cparams(("parallel",)), name="nsa_compress")(flat, pe_flat, w1, w2p)


def _nsa_cmp_kernel(q_ref, kk_ref, vt_ref, ovt_ref, gl_ref, bg_ref, ex_ref, o_ref, sn_ref, *, tq, ncp, nselp):
    i = pl.program_id(2)
    q = q_ref[...]
    kk = kk_ref[...]
    vt = vt_ref[...]
    t = i * tq + lax.broadcasted_iota(jnp.int32, (ncp, tq), 1)
    n = lax.broadcasted_iota(jnp.int32, (ncp, tq), 0)
    valid = n * NSA_CMP_STRIDE + (NSA_CMP_LEN - 1) <= t
    psum = jnp.zeros((ncp, tq), F32)
    outs = []
    for hg in range(NSA_HEADS_PER_GROUP):
        qh = _head_of_pair(q[:, LANES * (hg // 2):LANES * (hg // 2 + 1)], hg % 2)
        sT = jnp.where(valid, _nt_dot(kk, qh), NEG)
        m = jnp.max(sT, axis=0, keepdims=True)
        e = jnp.where(valid, jnp.exp2(sT - m), 0.0)
        p = e / jnp.maximum(jnp.sum(e, axis=0, keepdims=True), 1e-30)
        psum = psum + p
        outs.append(jnp.dot(vt, p.astype(BF16), preferred_element_type=F32))
    o = jnp.concatenate(outs, axis=0).T
    o_ref[...] = (o * _gate_expand(gl_ref, bg_ref, ex_ref)).astype(o_ref.dtype)
    hi = psum.astype(BF16)
    lo = (psum - hi.astype(F32)).astype(BF16)
    ovt = ovt_ref[...]
    imp = jnp.dot(ovt, hi, preferred_element_type=F32) + jnp.dot(ovt, lo, preferred_element_type=F32)
    jblk = lax.broadcasted_iota(jnp.int32, (nselp, tq), 0)
    cur = (i * tq + lax.broadcasted_iota(jnp.int32, (nselp, tq), 1)) >> _log2(NSA_SEL_LEN)
    forced = (jblk == 0) | (jblk == cur) | (jblk == cur - 1)
    pri = jnp.where(forced, imp + NSA_FORCE_BONUS, imp)
    past = jblk <= cur
    pri = jnp.where(past, pri, NEG)
    chosen = _topk_rows(pri, NSA_SEL_TOPN)
    sn = jnp.where((chosen > 0.0) & past, 0.0, NEG)
    sn_ref[...] = sn.T.astype(sn_ref.dtype)


def _nsa_cmp_branch(q_c, kk_c, vt_c, overlap_t, gl, bg, ex, B, S):
    T = q_c.shape[0]
    G = NSA_KV_GROUPS
    tq = min(ATTN_TILE, S)
    nq = S // tq
    ncp = kk_c.shape[2]
    nselp = overlap_t.shape[0]
    return pl.pallas_call(
        functools.partial(_nsa_cmp_kernel, tq=tq, ncp=ncp, nselp=nselp),
        grid=(B, G, nq),
        in_specs=[pl.BlockSpec((tq, 2 * LANES), lambda b, g, i: (b * nq + i, g)),
                  pl.BlockSpec((None, None, ncp, LANES), lambda b, g, i: (b, g, 0, 0)),
                  pl.BlockSpec((None, None, HEAD_DIM, ncp), lambda b, g, i: (b, g, 0, 0)),
                  pl.BlockSpec((nselp, ncp), lambda b, g, i: (0, 0)),
                  pl.BlockSpec((tq, LANES), lambda b, g, i: (b * nq + i, 0)),
                  pl.BlockSpec((1, LANES), lambda b, g, i: (0, 0)),
                  pl.BlockSpec((None, LANES, 2 * LANES), lambda b, g, i: (g, 0, 0))],
        out_specs=[pl.BlockSpec((tq, 2 * LANES), lambda b, g, i: (b * nq + i, g)),
                   pl.BlockSpec((None, None, tq, nselp), lambda b, g, i: (b, g, i, 0))],
        out_shape=[jax.ShapeDtypeStruct((T, D_MODEL), BF16),
                   jax.ShapeDtypeStruct((B, G, S, nselp), BF16)],
        compiler_params=_cparams(("parallel", "parallel", "parallel")), name="nsa_cmp")(
            q_c, kk_c, vt_c, overlap_t, gl, bg, ex)


def _nsa_kv_kernel(*refs, mode, tq, tk):
    if mode == "sel":
        q_ref, kk_ref, vt_ref, oh_ref, sn_ref, gl_ref, bg_ref, ex_ref, o_ref, m_ref, acc_ref = refs
    else:
        q_ref, kk_ref, vt_ref, gl_ref, bg_ref, ex_ref, o_ref, m_ref, acc_ref = refs
    i = pl.program_id(2)
    _flash_init(m_ref, acc_ref)
    q = q_ref[...]
    qh = []
    for hg in range(NSA_HEADS_PER_GROUP):
        qq = _head_of_pair(q[:, LANES * (hg // 2):LANES * (hg // 2 + 1)], hg % 2)
        if mode == "sel":
            qq = jnp.concatenate([qq, sn_ref[...]], axis=1)
        qh.append(qq)

    def step(off, size, mask):
        k = kk_ref[pl.ds(off, size), :]
        vt = vt_ref[:, pl.ds(off, size)]
        lhs = jnp.concatenate([vt, jnp.ones((ONES_ROWS, size), vt.dtype)], axis=0)
        if mode == "sel":
            k = jnp.concatenate([k, oh_ref[pl.ds(off, size), :]], axis=1)
        for hg in range(NSA_HEADS_PER_GROUP):
            sT = _nt_dot(k, qh[hg])
            if mask is not None:
                sT = jnp.where(mask, sT, NEG)
            _flash_step(sT, lhs, m_ref, acc_ref, hg)

    if mode == "sel":
        _causal_sweep(i, tq, tk, step)
    else:
        @pl.when(i > 0)
        def _():
            step(pl.multiple_of((i - 1) * tq, tq), tq, _causal_t(tq, tq, strict_lower=True))
        step(pl.multiple_of(i * tq, tq), tq, _causal_t(tq, tq))
    outs = []
    for hg in range(NSA_HEADS_PER_GROUP):
        a = acc_ref[hg]
        outs.append(a[:HEAD_DIM] / a[HEAD_DIM:HEAD_DIM + 1, :])
    o_ref[...] = (jnp.concatenate(outs, axis=0).T * _gate_expand(gl_ref, bg_ref, ex_ref)).astype(o_ref.dtype)


def _nsa_kv_branch(mode, q_r, kk, vt, gl, bg, ex, B, S, onehot=None, selneg=None):
    T = q_r.shape[0]
    G = NSA_KV_GROUPS
    tq = min(ATTN_TILE, S)
    tk = min(KV_TILE, S)
    assert NSA_WINDOW == tq or mode == "sel"
    nq = S // tq
    in_specs = [pl.BlockSpec((tq, 2 * LANES), lambda b, g, i: (b * nq + i, g)),
                pl.BlockSpec((None, None, S, LANES), lambda b, g, i: (b, g, 0, 0)),
                pl.BlockSpec((HEAD_DIM, S), lambda b, g, i: (g, b))]
    args = [q_r, kk, vt]
    if mode == "sel":
        in_specs += [pl.BlockSpec((S, LANES), lambda b, g, i: (0, 0)),
                     pl.BlockSpec((None, None, tq, LANES), lambda b, g, i: (b, g, i, 0))]
        args += [onehot, selneg]
    in_specs += [pl.BlockSpec((tq, LANES), lambda b, g, i: (b * nq + i, 0)),
                 pl.BlockSpec((1, LANES), lambda b, g, i: (0, 0)),
                 pl.BlockSpec((None, LANES, 2 * LANES), lambda b, g, i: (g, 0, 0))]
    args += [gl, bg, ex]
    nh = NSA_HEADS_PER_GROUP
    return pl.pallas_call(
        functools.partial(_nsa_kv_kernel, mode=mode, tq=tq, tk=tk),
        grid=(B, G, nq), in_specs=in_specs,
        out_specs=pl.BlockSpec((tq, 2 * LANES), lambda b, g, i: (b * nq + i, g)),
        out_shape=jax.ShapeDtypeStruct((T, D_MODEL), BF16),
        scratch_shapes=[pltpu.VMEM((nh, 1, tq), F32), pltpu.VMEM((nh, HEAD_DIM + ONES_ROWS, tq), F32)],
        compiler_params=_cparams(("parallel", "parallel", "arbitrary")), name=f"nsa_{mode}")(*args)


def _kmean_kernel(k_ref, o_ref, *, nblk):
    k = k_ref[...].astype(F32)
    o_ref[...] = jnp.mean(k.reshape(nblk, MOBA_BLOCK, k.shape[1]), axis=1)


def _moba_kmean(k_rot):
    T, D = k_rot.shape
    nblk = 8
    rows = nblk * MOBA_BLOCK
    assert T % rows == 0
    return pl.pallas_call(
        functools.partial(_kmean_kernel, nblk=nblk), grid=(T // rows,),
        in_specs=[pl.BlockSpec((rows, D), lambda i: (i, 0))],
        out_specs=pl.BlockSpec((nblk, D), lambda i: (i, 0)),
        out_shape=jax.ShapeDtypeStruct((T // MOBA_BLOCK, D), F32),
        compiler_params=_cparams(("parallel",)), name="moba_kmean")(k_rot)


def _moba_select_kernel(q_ref, r_ref, sn_ref, *, tq, nblk):
    i = pl.program_id(2)
    gsT = _nt_dot(r_ref[...], q_ref[...].astype(F32), HIGHEST)
    jblk = lax.broadcasted_iota(jnp.int32, (nblk, tq), 0)
    cb = (i * tq + lax.broadcasted_iota(jnp.int32, (nblk, tq), 1)) >> _log2(MOBA_BLOCK)
    past = jblk < cb
    parts = []
    for h in range(2):
        gs = jnp.where(past, gsT[h * nblk:(h + 1) * nblk, :], NEG)
        chosen = _topk_rows(gs, min(MOBA_TOPK, nblk))
        parts.append(jnp.where(((chosen > 0.0) & past) | (jblk == cb), 0.0, NEG))
    parts.append(jnp.zeros((LANES - 2 * nblk, tq), F32))
    sn_ref[...] = jnp.concatenate(parts, axis=0).T.astype(sn_ref.dtype)


def _moba_select(q_rot, rmat_t, B, S):
    tq = min(ATTN_TILE, S)
    nq = S // tq
    npair = N_HEADS // 2
    nblk = S // MOBA_BLOCK
    return pl.pallas_call(
        functools.partial(_moba_select_kernel, tq=tq, nblk=nblk), grid=(B, npair, nq),
        in_specs=[pl.BlockSpec((tq, LANES), lambda b, hp, i: (b * nq + i, hp)),
                  pl.BlockSpec((None, None, LANES, LANES), lambda b, hp, i: (b, hp, 0, 0))],
        out_specs=pl.BlockSpec((None, None, tq, LANES), lambda b, hp, i: (b, hp, i, 0)),
        out_shape=jax.ShapeDtypeStruct((B, npair, S, LANES), BF16),
        compiler_params=_cparams(("parallel", "parallel", "parallel")), name="moba_select")(q_rot, rmat_t)


def _moba_kernel(q_ref, k_ref, vt_ref, oh_ref, sn_ref, o_ref, m_ref, acc_ref, *, tq, tk, nblk):
    i = pl.program_id(2)
    _flash_init(m_ref, acc_ref)
    q = q_ref[...]
    sn = sn_ref[...]
    lane = lax.broadcasted_iota(jnp.int32, sn.shape, 1)
    qa = []
    for h in range(2):
        mine = (lane >= h * nblk) & (lane < (h + 1) * nblk)
        qa.append(jnp.concatenate([_head_of_pair(q, h), jnp.where(mine, sn, jnp.zeros_like(sn))], axis=1))

    def step(off, size, mask):
        k = jnp.concatenate([k_ref[pl.ds(off, size), :], oh_ref[pl.ds(off, size), :]], axis=1)
        vt = vt_ref[:, pl.ds(off, size)]
        for h in range(2):
            sT = _nt_dot(k, qa[h])
            if mask is not None:
                sT = jnp.where(mask, sT, NEG)
            _flash_step(sT, _pair_lhs(vt, h), m_ref, acc_ref, h)

    _causal_sweep(i, tq, tk, step)
    o_ref[...] = _pair_finish(acc_ref).T.astype(o_ref.dtype)


def _moba_attention(qk_rot, vt, onehot2, selneg, B, S):
    T = qk_rot.shape[0]
    tq = min(ATTN_TILE, S)
    tk = min(KV_TILE, S)
    nq = S // tq
    npair = N_HEADS // 2
    ncol = D_MODEL // LANES
    nblk = S // MOBA_BLOCK
    return pl.pallas_call(
        functools.partial(_moba_kernel, tq=tq, tk=tk, nblk=nblk),
        grid=(B, npair, nq),
        in_specs=[pl.BlockSpec((tq, LANES), lambda b, hp, i: (b * nq + i, hp)),
                  pl.BlockSpec((S, LANES), lambda b, hp, i: (b, ncol + hp)),
                  pl.BlockSpec((LANES, S), lambda b, hp, i: (hp, b)),
                  pl.BlockSpec((S, LANES), lambda b, hp, i: (0, 0)),
                  pl.BlockSpec((None, None, tq, LANES), lambda b, hp, i: (b, hp, i, 0))],
        out_specs=pl.BlockSpec((tq, LANES), lambda b, hp, i: (b * nq + i, hp)),
        out_shape=jax.ShapeDtypeStruct((T, D_MODEL), BF16),
        scratch_shapes=[pltpu.VMEM((2, 1, tq), F32), pltpu.VMEM((2, LANES, tq), F32)],
        compiler_params=_cparams(("parallel", "parallel", "arbitrary")), name="moba_attn")(
            qk_rot, qk_rot, vt, onehot2, selneg)


def _router_kernel(wt_ref, x_ref, rb_ref, e_ref, g_ref):
    tm = x_ref.shape[0]
    gsz = N_EXPERTS // N_GROUPS
    scores = jax.nn.sigmoid(_nt_dot(wt_ref[...], x_ref[...], HIGHEST))
    biased = scores + rb_ref[...]
    member = lax.broadcasted_iota(jnp.int32, (gsz, tm), 0).astype(F32)
    gscore = []
    for g in range(N_GROUPS):
        v = biased[g * gsz:(g + 1) * gsz, :]
        m1 = jnp.max(v, axis=0, keepdims=True)
        i1 = jnp.min(jnp.where(v == m1, member, float(gsz)), axis=0, keepdims=True)
        m2 = jnp.max(jnp.where(member == i1, -jnp.inf, v), axis=0, keepdims=True)
        gscore.append(m1 + m2)
    gsel = _topk_rows(jnp.concatenate(gscore, axis=0), TOPK_GROUPS)
    emask = jnp.concatenate([jnp.broadcast_to(gsel[g:g + 1, :], (gsz, tm)) for g in range(N_GROUPS)], axis=0)
    work = jnp.where(emask > 0.0, biased, NEG)
    erow = lax.broadcasted_iota(jnp.int32, (N_EXPERTS, tm), 0).astype(F32)
    idxs, vals = [], []
    for _ in range(TOP_K):
        m = jnp.max(work, axis=0, keepdims=True)
        idx = jnp.min(jnp.where(work == m, erow, float(N_EXPERTS)), axis=0, keepdims=True)
        pick = erow == idx
        idxs.append(idx)
        vals.append(jnp.sum(jnp.where(pick, scores, 0.0), axis=0, keepdims=True))
        work = jnp.where(pick, -jnp.inf, work)
    gw = jnp.concatenate(vals, axis=0)
    e_ref[...] = jnp.concatenate(idxs, axis=0).astype(jnp.int32)
    g_ref[...] = gw / jnp.sum(gw, axis=0, keepdims=True) * ROUTED_SCALE


def _router(x1, router_w, router_b):
    T, D = x1.shape
    tm = min(ROW_TILE, T)
    return pl.pallas_call(
        _router_kernel, grid=(T // tm,),
        in_specs=[pl.BlockSpec((N_EXPERTS, D), lambda i: (0, 0)), pl.BlockSpec((tm, D), lambda i: (i, 0)),
                  pl.BlockSpec((N_EXPERTS, 1), lambda i: (0, 0))],
        out_specs=[pl.BlockSpec((TOP_K, tm), lambda i: (0, i))] * 2,
        out_shape=[jax.ShapeDtypeStruct((TOP_K, T), jnp.int32), jax.ShapeDtypeStruct((TOP_K, T), F32)],
        compiler_params=_cparams(("parallel",)), name="moe_router")(
            router_w.T, x1, router_b.reshape(N_EXPERTS, 1))


def _rank_kernel(e_ref, tri_ref, rank_ref, cnt_ref, carry_ref):
    @pl.when(pl.program_id(0) == 0)
    def _():
        carry_ref[...] = jnp.zeros(carry_ref.shape, F32)

    tm = e_ref.shape[1]
    erow = lax.broadcasted_iota(jnp.int32, (N_EXPERTS, tm), 0)
    tri = tri_ref[...]
    base = carry_ref[...]
    ranks = []
    for k in range(TOP_K):
        oh = erow == e_ref[k:k + 1, :]
        ohb = jnp.where(oh, 1.0, 0.0).astype(BF16)
        incl = jnp.dot(ohb, tri, preferred_element_type=F32)
        ranks.append(jnp.sum(jnp.where(oh, base + incl - 1.0, 0.0), axis=0, keepdims=True))
        base = base + incl[:, tm - 1:tm]
    carry_ref[...] = base
    rank_ref[...] = jnp.concatenate(ranks, axis=0).astype(jnp.int32)
    cnt_ref[...] = jnp.broadcast_to(base, cnt_ref.shape)


def _expert_ranks(eidx_t):
    K, T = eidx_t.shape
    tm = min(ROW_TILE, T)
    tri = jnp.asarray(np.triu(np.ones((tm, tm), np.float32)), BF16)
    return pl.pallas_call(
        _rank_kernel, grid=(T // tm,),
        in_specs=[pl.BlockSpec((K, tm), lambda i: (0, i)), pl.BlockSpec((tm, tm), lambda i: (0, 0))],
        out_specs=[pl.BlockSpec((K, tm), lambda i: (0, i)), pl.BlockSpec((N_EXPERTS, LANES), lambda i: (0, 0))],
        out_shape=[jax.ShapeDtypeStruct((K, T), jnp.int32), jax.ShapeDtypeStruct((N_EXPERTS, LANES), F32)],
        scratch_shapes=[pltpu.VMEM((N_EXPERTS, 1), F32)],
        compiler_params=_cparams(("arbitrary",)), name="moe_rank")(eidx_t, tri)


def _dest_kernel(e_ref, rank_ref, ps_ref, d_ref):
    tm = e_ref.shape[1]
    erow = lax.broadcasted_iota(jnp.int32, (N_EXPERTS, tm), 0)
    ps = ps_ref[...]
    rows = []
    for k in range(TOP_K):
        oh = erow == e_ref[k:k + 1, :]
        rows.append(jnp.sum(jnp.where(oh, ps, 0.0), axis=0, keepdims=True))
    d_ref[...] = jnp.concatenate(rows, axis=0).astype(jnp.int32) + rank_ref[...]


def _expert_dest(eidx_t, rank_t, pstarts):
    K, T = eidx_t.shape
    tm = min(ROW_TILE, T)
    blk = pl.BlockSpec((K, tm), lambda i: (0, i))
    return pl.pallas_call(
        _dest_kernel, grid=(T // tm,),
        in_specs=[blk, blk, pl.BlockSpec((N_EXPERTS, 1), lambda i: (0, 0))],
        out_specs=blk, out_shape=jax.ShapeDtypeStruct((K, T), jnp.int32),
        compiler_params=_cparams(("parallel",)), name="moe_dest")(
            eidx_t, rank_t, pstarts.astype(F32).reshape(N_EXPERTS, 1))


def _expert_kernel(be_ref, nb_ref, x_ref, wg_ref, wu_ref, wd_ref, o_ref):
    b = pl.program_id(0)

    @pl.when(b < nb_ref[0])
    def _():
        x = x_ref[...].astype(BF16)
        h = jax.nn.silu(jnp.dot(x, wg_ref[...], preferred_element_type=F32)) * jnp.dot(
            x, wu_ref[...], preferred_element_type=F32)
        o_ref[...] = jnp.dot(h.astype(BF16), wd_ref[...], preferred_element_type=F32).astype(o_ref.dtype)

    @pl.when(b >= nb_ref[0])
    def _():
        o_ref[...] = jnp.zeros(o_ref.shape, o_ref.dtype)


def _expert_ffn(blk_e, nb_used, xs, wg, wu, wd):
    P, D = xs.shape
    NB = P // MOE_BLOCK
    E = EXPERT_DIM
    grid_spec = pltpu.PrefetchScalarGridSpec(
        num_scalar_prefetch=2, grid=(NB,),
        in_specs=[pl.BlockSpec((MOE_BLOCK, D), lambda b, be, nb: (b, 0)),
                  pl.BlockSpec((None, D, E), lambda b, be, nb: (be[b], 0, 0)),
                  pl.BlockSpec((None, D, E), lambda b, be, nb: (be[b], 0, 0)),
                  pl.BlockSpec((None, E, D), lambda b, be, nb: (be[b], 0, 0))],
        out_specs=pl.BlockSpec((MOE_BLOCK, D), lambda b, be, nb: (b, 0)))
    return pl.pallas_call(
        _expert_kernel, grid_spec=grid_spec, out_shape=jax.ShapeDtypeStruct((P, D), BF16),
        compiler_params=_cparams(("arbitrary",)), name="moe_experts")(blk_e, nb_used, xs, wg, wu, wd)


def _post_moe_kernel(x_ref, *refs):
    y_refs = refs[:TOP_K]
    gw_ref, p_ref, sg_ref, su_ref, sd_ref, g_ref, b_ref, wg_ref, wp_ref, o_ref = refs[TOP_K:]
    x = x_ref[...]
    xb = x.astype(BF16)
    h = jax.nn.silu(jnp.dot(xb, sg_ref[...], preferred_element_type=F32)) * jnp.dot(
        xb, su_ref[...], preferred_element_type=F32)
    ffn = jnp.dot(h.astype(BF16), sd_ref[...], preferred_element_type=F32)
    gw = gw_ref[...]
    for k in range(TOP_K):
        ffn = ffn + gw[:, k:k + 1] * y_refs[k][...].astype(F32)
    z = DN_ALPHA * x + ffn
    x2 = _layer_norm(z, g_ref[...], b_ref[...])
    gate = jax.nn.sigmoid(jnp.dot(x2.astype(BF16), wg_ref[...], preferred_element_type=F32))
    proj = jnp.dot(p_ref[...].astype(BF16), wp_ref[...], preferred_element_type=F32)
    o_ref[...] = x2 + gate * proj


def _post_moe(x1, yg, gw, p, sg, su, sd, g, b, wgate, wproj):
    T, D = x1.shape
    tm = min(ROW_TILE // 2, T)
    nt = T // tm
    PD = p.shape[1]
    SD = sg.shape[1]
    row = pl.BlockSpec((tm, D), lambda i: (i, 0))
    vec = pl.BlockSpec((1, D), lambda i: (0, 0))
    full = lambda r, c: pl.BlockSpec((r, c), lambda i: (0, 0))
    return pl.pallas_call(
        _post_moe_kernel, grid=(T // tm,),
        in_specs=[row] + [pl.BlockSpec((tm, D), functools.partial(lambda i, k: (k * nt + i, 0), k=k))
                          for k in range(TOP_K)]
                 + [pl.BlockSpec((tm, TOP_K), lambda i: (i, 0)),
                    pl.BlockSpec((tm, PD), lambda i: (i, 0)), full(D, SD), full(D, SD),
                  full(SD, D), vec, vec, full(D, D), full(PD, D)],
        out_specs=row, out_shape=jax.ShapeDtypeStruct((T, D), F32),
        compiler_params=_cparams(("parallel",)), name="post_moe")(
            x1, *([yg] * TOP_K), gw, p, sg, su, sd, g.reshape(1, D), b.reshape(1, D), wgate, wproj)


def _rope_tables(positions):
    inv = 1.0 / (ROPE_THETA ** (jnp.arange(0, HEAD_DIM, 2, dtype=F32) / HEAD_DIM))
    ang = positions.astype(F32).reshape(-1)[:, None] * inv
    cos, sin = jnp.cos(ang), jnp.sin(ang)
    cosf = jnp.concatenate([cos] * (LANES // (HEAD_DIM // 2)), axis=1)
    sinf = jnp.concatenate([-sin, sin] * (LANES // HEAD_DIM), axis=1)
    return cosf, sinf


def _q_col_scale(n_q, n):
    return jnp.concatenate([jnp.full((n_q,), Q_SCALE, F32), jnp.ones((n - n_q,), F32)])


def _fox_mixer(x, B, S, w_in, b_f):
    D = D_MODEL
    wb = w_in[:, :3 * D].astype(BF16)
    qk = _proj(x, wb[:, :2 * D], col_scale=_q_col_scale(D, 2 * D))
    vt = _proj_t(wb[:, 2 * D:].T, x)
    fl = _proj(x, _pad_cols(w_in[:, 3 * D:], LANES), out_dtype=F32, tn=LANES, precision=HIGHEST)
    cp = _fox_gate(fl, b_f, B, S)
    return [_fox_attention(qk, cp, vt, B, S)]


def _nsa_mixer(x, B, S, cosf, sinf, w_in, b_gate, pe_k, pe_v, ck_w1, ck_w2, cv_w1, cv_w2):
    D, G, HD = D_MODEL, NSA_KV_GROUPS, HEAD_DIM
    HG = NSA_HEADS_PER_GROUP
    kvw = G * HD
    wb = w_in[:, :D + 6 * kvw].astype(BF16)
    q_c, q_r = _proj(x, wb[:, :D], mode="both", cos=cosf, sin=sinf, col_scale=_q_col_scale(D, D))
    w_rot = jnp.concatenate([wb[:, D + 2 * kvw:D + 3 * kvw], wb[:, D + 4 * kvw:D + 5 * kvw]], axis=1)
    k_rot = _proj(x, w_rot, mode="rope", cos=cosf, sin=sinf)
    kvc = _proj(x, wb[:, D:D + 2 * kvw])
    w_v = jnp.concatenate([wb[:, D + 3 * kvw:D + 4 * kvw], wb[:, D + 5 * kvw:D + 6 * kvw]], axis=1)
    vt = _proj_t(w_v.T, x)
    gl = _proj(x, _pad_cols(w_in[:, D + 6 * kvw:], LANES), out_dtype=F32, tn=LANES, precision=HIGHEST)
    bg = _pad_cols(b_gate.reshape(1, -1), LANES)

    def grouped(t2d):
        t = t2d.reshape(B, S, G, HD).transpose(0, 2, 1, 3)
        return jnp.concatenate([t, t], axis=-1)

    n_chunks = S // NSA_CMP_STRIDE
    n_cmp = n_chunks - NSA_CMP_LEN // NSA_CMP_STRIDE + 1
    ncp = n_chunks

    def compress(t2d, pe, w1, w2):
        ch = t2d.reshape(B, n_chunks, NSA_CMP_STRIDE, G, HD).transpose(0, 1, 3, 2, 4)
        ch = ch.reshape(B, n_chunks, G, NSA_CMP_STRIDE * HD)
        flat = jnp.concatenate([ch[:, :n_cmp], ch[:, 1:n_cmp + 1]], axis=-1)
        flat = jnp.pad(flat, ((0, 0), (0, ncp - n_cmp), (0, 0), (0, 0))).reshape(B * ncp * G, -1)
        out = _nsa_compress(flat, pe.reshape(1, -1), w1.astype(BF16), _pad_cols(w2, LANES).astype(BF16))
        return out[:, :HD].reshape(B, ncp, G, HD).astype(BF16)

    kc = compress(kvc[:, :kvw], pe_k, ck_w1, ck_w2).transpose(0, 2, 1, 3)
    kk_c = jnp.concatenate([kc, kc], axis=-1)
    vt_c = compress(kvc[:, kvw:], pe_v, cv_w1, cv_w2).transpose(0, 2, 3, 1)

    n_sel = S // NSA_SEL_LEN
    assert n_sel <= LANES
    cmp_start = np.arange(ncp) * NSA_CMP_STRIDE
    sel_start = np.arange(LANES) * NSA_SEL_LEN
    overlap = ((cmp_start[:, None] < sel_start[None, :] + NSA_SEL_LEN)
               & (cmp_start[:, None] + NSA_CMP_LEN > sel_start[None, :])
               & (np.arange(ncp)[:, None] < n_cmp) & (np.arange(LANES)[None, :] < n_sel))
    overlap_t = jnp.asarray(overlap.T, BF16)
    ex = np.zeros((3, G, LANES, HG * HD), np.float32)
    for br in range(3):
        for g in range(G):
            for hg in range(HG):
                ex[br, g, (g * HG + hg) * 3 + br, hg * HD:(hg + 1) * HD] = 1.0
    ex = jnp.asarray(ex, BF16)

    o_c, selneg = _nsa_cmp_branch(q_c, kk_c, vt_c, overlap_t, gl, bg, ex[0], B, S)
    onehot = jnp.asarray((np.arange(S)[:, None] // NSA_SEL_LEN) == np.arange(LANES)[None, :], BF16)
    kk_s = grouped(k_rot[:, :kvw])
    kk_w = grouped(k_rot[:, kvw:])
    o_s = _nsa_kv_branch("sel", q_r, kk_s, vt[:kvw], gl, bg, ex[1], B, S, onehot=onehot, selneg=selneg)
    o_w = _nsa_kv_branch("win", q_r, kk_w, vt[kvw:], gl, bg, ex[2], B, S)
    return [o_c, o_s, o_w]


def _moba_mixer(x, B, S, cosf, sinf, w_in):
    D, H, HD = D_MODEL, N_HEADS, HEAD_DIM
    wb = w_in.astype(BF16)
    qk_rot = _proj(x, wb[:, :2 * D], mode="rope", cos=cosf, sin=sinf, col_scale=_q_col_scale(D, 2 * D))
    vt = _proj_t(wb[:, 2 * D:].T, x)
    nblk = S // MOBA_BLOCK
    assert S % MOBA_BLOCK == 0 and 2 * nblk <= LANES and nblk % SUBLANES == 0
    kmean = _moba_kmean(qk_rot[:, D:]).reshape(B, nblk, H // 2, 2, HD)
    km = kmean.transpose(0, 2, 3, 1, 4)
    rmat_t = jnp.zeros((B, H // 2, LANES, 2, HD), F32)
    rmat_t = rmat_t.at[:, :, :nblk, 0].set(km[:, :, 0]).at[:, :, nblk:2 * nblk, 1].set(km[:, :, 1])
    selneg = _moba_select(qk_rot, rmat_t.reshape(B, H // 2, LANES, LANES), B, S)
    blk_of = np.arange(S) // MOBA_BLOCK
    oh = np.zeros((S, LANES), np.float32)
    oh[np.arange(S), blk_of] = 1.0
    oh[np.arange(S), nblk + blk_of] = 1.0
    return [_moba_attention(qk_rot, vt, jnp.asarray(oh, BF16), selneg, B, S)]


def _moe_dispatch(x1, x1b, router_w, router_b, wg, wu, wd):
    T, D = x1.shape
    eidx_t, gw_t = _router(x1, router_w, router_b)
    rank_t, cnt = _expert_ranks(eidx_t)
    counts = cnt[:, 0].astype(jnp.int32)
    padded = (counts + MOE_BLOCK - 1) // MOE_BLOCK * MOE_BLOCK
    pends = jnp.cumsum(padded)
    pstarts = pends - padded
    R = T * TOP_K
    P = -(-(R + N_EXPERTS * (MOE_BLOCK - 1)) // MOE_BLOCK) * MOE_BLOCK
    NB = P // MOE_BLOCK
    blk_start = jnp.arange(NB, dtype=jnp.int32) * MOE_BLOCK
    blk_e = jnp.minimum(jnp.sum(pends[None, :] <= blk_start[:, None], axis=1), N_EXPERTS - 1).astype(jnp.int32)
    nb_used = (pends[-1] // MOE_BLOCK).astype(jnp.int32).reshape(1)
    dest_t = _expert_dest(eidx_t, rank_t, pstarts)

    tok = jnp.broadcast_to(jnp.arange(T, dtype=jnp.int32)[None, :], (TOP_K, T))
    buf_tok = jnp.zeros((P,), jnp.int32).at[dest_t.reshape(-1)].set(tok.reshape(-1))
    xs = x1[buf_tok]
    yb = _expert_ffn(blk_e, nb_used, xs, wg, wu, wd)
    yg = yb[dest_t.reshape(-1)]
    return yg, gw_t.T


def kernel(x, p, positions, fox_w_in, fox_b_f, fox_w_out, nsa_w_in, nsa_b_gate, nsa_pe_k, nsa_pe_v,
           nsa_cmp_k_w1, nsa_cmp_k_w2, nsa_cmp_v_w1, nsa_cmp_v_w2, nsa_w_out, moba_w_in, moba_w_out,
           ln1_g, ln1_b, router_w, router_b, exp_w_gate, exp_w_up, exp_w_down,
           sh_w_gate, sh_w_up, sh_w_down, ln2_g, ln2_b, ple_w_gate, ple_w_proj):
    B, S, D = x.shape
    T = B * S
    depth = p.shape[0]
    cosf, sinf = _rope_tables(positions)
    xt = x.reshape(T, D)
    for i in range(depth):
        kind, j = i % N_MIXERS, i // N_MIXERS
        if kind == 0:
            o_list = _fox_mixer(xt, B, S, fox_w_in[j], fox_b_f[j])
            w_out = fox_w_out[j]
        elif kind == 1:
            o_list = _nsa_mixer(xt, B, S, cosf, sinf, nsa_w_in[j], nsa_b_gate[j], nsa_pe_k[j], nsa_pe_v[j],
                                nsa_cmp_k_w1[j], nsa_cmp_k_w2[j], nsa_cmp_v_w1[j], nsa_cmp_v_w2[j])
            w_out = nsa_w_out[j]
        else:
            o_list = _moba_mixer(xt, B, S, cosf, sinf, moba_w_in[j])
            w_out = moba_w_out[j]
        x1, x1b = _outproj_ln(o_list, w_out.astype(BF16), xt, ln1_g[i], ln1_b[i])
        yg, gw = _moe_dispatch(x1, x1b, router_w[i], router_b[i], exp_w_gate[i].astype(BF16),
                               exp_w_up[i].astype(BF16), exp_w_down[i].astype(BF16))
        xt = _post_moe(x1, yg, gw, p[i].reshape(T, -1), sh_w_gate[i].astype(BF16), sh_w_up[i].astype(BF16),
                       sh_w_down[i].astype(BF16), ln2_g[i], ln2_b[i],
                       ple_w_gate[i].astype(BF16), ple_w_proj[i].astype(BF16))
    return xt.reshape(B, S, D)
```

```python
import functools

import jax
import jax.numpy as jnp
import numpy as np
from jax import lax
from jax.experimental import pallas as pl
from jax.experimental.pallas import tpu as pltpu

D_MODEL = 1024
DEPTH = 4
N_HEADS = 16
HEAD_DIM = 64
ATTN_SCALE = HEAD_DIM ** -0.5
LOG2E = 1.4426950408889634
Q_SCALE = ATTN_SCALE * LOG2E
ROPE_THETA = 10000.0
N_MIXERS = 3

NSA_KV_GROUPS = 4
NSA_HEADS_PER_GROUP = N_HEADS // NSA_KV_GROUPS
NSA_CMP_LEN = 32
NSA_CMP_STRIDE = 16
NSA_SEL_LEN = 64
NSA_SEL_TOPN = 16
NSA_WINDOW = 512
NSA_FORCE_BONUS = 1e4

MOBA_BLOCK = 256
MOBA_TOPK = 3

N_EXPERTS = 64
EXPERT_DIM = 256
TOP_K = 8
N_GROUPS = 8
TOPK_GROUPS = 4
ROUTED_SCALE = 2.5
MOE_BLOCK = 256

DN_ALPHA = (2 * DEPTH) ** 0.25
LN_EPS = 1e-5
NEG = -1e30

LANES = 128
SUBLANES = 8
ATTN_TILE = 512
KV_TILE = 1024
ROW_TILE = 512
VMEM_LIMIT = 48 * 1024 * 1024
C_SPLIT = 3
ONES_ROWS = 16

F32 = jnp.float32
BF16 = jnp.bfloat16
HIGHEST = lax.Precision.HIGHEST


def _cparams(sem):
    return pltpu.CompilerParams(dimension_semantics=sem, vmem_limit_bytes=VMEM_LIMIT)


def _log2(n):
    assert n & (n - 1) == 0
    return n.bit_length() - 1


def _nt_dot(a, b, precision=None):
    return lax.dot_general(a, b, (((1,), (1,)), ((), ())), preferred_element_type=F32,
                           precision=precision)


def _topk_rows(work, n):
    rows = lax.broadcasted_iota(jnp.int32, work.shape, 0).astype(F32)
    chosen = jnp.zeros(work.shape, F32)
    for _ in range(n):
        m = jnp.max(work, axis=0, keepdims=True)
        idx = jnp.min(jnp.where(work == m, rows, float(work.shape[0])), axis=0, keepdims=True)
        pick = rows == idx
        chosen = jnp.where(pick, 1.0, chosen)
        work = jnp.where(pick, -jnp.inf, work)
    return chosen


def _proj_kernel(*refs, mode, precision, scaled):
    refs = list(refs)
    x_ref, w_ref = refs[:2]
    del refs[:2]
    scale_ref = refs.pop(0) if scaled else None
    if mode == "none":
        (o_ref,) = refs
    elif mode == "rope":
        cos_ref, sin_ref, r_ref = refs
    else:
        cos_ref, sin_ref, o_ref, r_ref = refs
    x = x_ref[...]
    w = w_ref[...]
    if x.dtype != w.dtype:
        x = x.astype(w.dtype)
    acc = jnp.dot(x, w, preferred_element_type=F32, precision=precision)
    if scaled:
        acc = acc * scale_ref[...]
    if mode in ("none", "both"):
        o_ref[...] = acc.astype(o_ref.dtype)
    if mode in ("rope", "both"):
        tn = acc.shape[1]
        rep = tn // LANES
        cosf = jnp.concatenate([cos_ref[...]] * rep, axis=1)
        sinf = jnp.concatenate([sin_ref[...]] * rep, axis=1)
        lane = lax.broadcasted_iota(jnp.int32, acc.shape, 1)
        first_half = (lane & (HEAD_DIM // 2)) == 0
        swapped = jnp.where(first_half, pltpu.roll(acc, tn - HEAD_DIM // 2, 1),
                            pltpu.roll(acc, HEAD_DIM // 2, 1))
        r_ref[...] = (acc * cosf + swapped * sinf).astype(r_ref.dtype)


def _proj(x, w, *, mode="none", cos=None, sin=None, out_dtype=BF16, tn=512, precision=None, col_scale=None):
    T, K = x.shape
    N = w.shape[1]
    tm = min(ROW_TILE, T)
    tn = min(tn, N)
    assert T % tm == 0 and N % tn == 0
    grid = (T // tm, N // tn)
    in_specs = [pl.BlockSpec((tm, K), lambda i, j: (i, 0)),
                pl.BlockSpec((K, tn), lambda i, j: (0, j))]
    args = [x, w]
    if col_scale is not None:
        in_specs.append(pl.BlockSpec((1, tn), lambda i, j: (0, j)))
        args.append(col_scale.reshape(1, N).astype(F32))
    if mode != "none":
        in_specs += [pl.BlockSpec((tm, LANES), lambda i, j: (i, 0))] * 2
        args += [cos, sin]
    o_spec = pl.BlockSpec((tm, tn), lambda i, j: (i, j))
    o_shape = jax.ShapeDtypeStruct((T, N), out_dtype)
    if mode == "both":
        out_specs, out_shape = [o_spec, o_spec], [o_shape, o_shape]
    else:
        out_specs, out_shape = o_spec, o_shape
    return pl.pallas_call(
        functools.partial(_proj_kernel, mode=mode, precision=precision, scaled=col_scale is not None),
        grid=grid, in_specs=in_specs, out_specs=out_specs, out_shape=out_shape,
        compiler_params=_cparams(("parallel", "parallel")), name=f"proj_{mode}")(*args)


def _proj_t_kernel(wt_ref, x_ref, o_ref, *, precision):
    wt = wt_ref[...]
    x = x_ref[...]
    if x.dtype != wt.dtype:
        x = x.astype(wt.dtype)
    o_ref[...] = _nt_dot(wt, x, precision).astype(o_ref.dtype)


def _proj_t(wt, x, *, out_dtype=BF16, precision=None):
    N, K = wt.shape
    T = x.shape[0]
    tm = min(ROW_TILE, T)
    tn = min(512, N)
    assert T % tm == 0 and N % tn == 0
    return pl.pallas_call(
        functools.partial(_proj_t_kernel, precision=precision),
        grid=(T // tm, N // tn),
        in_specs=[pl.BlockSpec((tn, K), lambda i, j: (j, 0)), pl.BlockSpec((tm, K), lambda i, j: (i, 0))],
        out_specs=pl.BlockSpec((tn, tm), lambda i, j: (j, i)),
        out_shape=jax.ShapeDtypeStruct((N, T), out_dtype),
        compiler_params=_cparams(("parallel", "parallel")), name="proj_t")(wt, x)


def _pad_cols(w, n):
    return jnp.pad(w, ((0, 0), (0, n - w.shape[1])))


def _layer_norm(z, g, b):
    mu = jnp.mean(z, axis=-1, keepdims=True)
    zc = z - mu
    var = jnp.mean(zc * zc, axis=-1, keepdims=True)
    return zc * lax.rsqrt(var + LN_EPS) * g + b


def _outproj_ln_kernel(*refs, n_o):
    o_refs = refs[:n_o]
    w_ref, x_ref, g_ref, b_ref, y_ref, yb_ref = refs[n_o:]
    if n_o == 1:
        o = o_refs[0][...]
    else:
        o = o_refs[0][...].astype(F32)
        for r in o_refs[1:]:
            o = o + r[...].astype(F32)
        o = o.astype(BF16)
    mix = jnp.dot(o, w_ref[...], preferred_element_type=F32)
    z = DN_ALPHA * x_ref[...] + mix
    y = _layer_norm(z, g_ref[...], b_ref[...])
    y_ref[...] = y
    yb_ref[...] = y.astype(BF16)


def _outproj_ln(o_list, w, x, g, b):
    T, D = x.shape
    tm = min(ROW_TILE, T)
    row = pl.BlockSpec((tm, D), lambda i: (i, 0))
    vec = pl.BlockSpec((1, D), lambda i: (0, 0))
    return pl.pallas_call(
        functools.partial(_outproj_ln_kernel, n_o=len(o_list)),
        grid=(T // tm,),
        in_specs=[row] * len(o_list) + [pl.BlockSpec((D, D), lambda i: (0, 0)), row, vec, vec],
        out_specs=[row, row],
        out_shape=[jax.ShapeDtypeStruct((T, D), F32), jax.ShapeDtypeStruct((T, D), BF16)],
        compiler_params=_cparams(("parallel",)), name="outproj_ln")(
            *o_list, w, x, g.reshape(1, D), b.reshape(1, D))


def _flash_init(m_ref, acc_ref):
    m_ref[...] = jnp.full(m_ref.shape, NEG, F32)
    acc_ref[...] = jnp.zeros(acc_ref.shape, F32)


def _flash_step(sT, lhs, m_ref, acc_ref, h):
    m_prev = m_ref[h]
    m_new = jnp.maximum(m_prev, jnp.max(sT, axis=0, keepdims=True))
    p = jnp.exp2(sT - m_new)
    alpha = jnp.exp2(m_prev - m_new)
    acc_ref[h] = alpha * acc_ref[h] + jnp.dot(lhs, p.astype(BF16), preferred_element_type=F32)
    m_ref[h] = m_new


def _causal_t(tk, tq, shift=0, strict_lower=False):
    key = lax.broadcasted_iota(jnp.int32, (tk, tq), 0)
    qry = lax.broadcasted_iota(jnp.int32, (tk, tq), 1)
    return (key > qry) if strict_lower else (key <= qry + shift)


def _causal_sweep(i, tq, tk, step):
    r = tk // tq
    assert tk == r * tq
    n_full = i >> _log2(r)

    def body(j, carry):
        step(pl.multiple_of(j * tk, tk), tk, None)
        return carry

    lax.fori_loop(0, n_full, body, 0)
    if r == 1:
        step(pl.multiple_of(i * tq, tq), tq, _causal_t(tq, tq))
    else:
        for rem in range(r):
            @pl.when(i - n_full * r == rem)
            def _(rem=rem):
                size = (rem + 1) * tq
                step(pl.multiple_of(n_full * tk, tk), size, _causal_t(size, tq, shift=rem * tq))


def _staggered(units, scores, update):
    s = [scores(units[0])]
    for n, u in enumerate(units):
        if n + 1 < len(units):
            s.append(scores(units[n + 1]))
        update(u, s[n])


def _head_of_pair(q_pair, which):
    lane = lax.broadcasted_iota(jnp.int32, q_pair.shape, 1)
    keep = (lane < HEAD_DIM) if which == 0 else (lane >= HEAD_DIM)
    return jnp.where(keep, q_pair, jnp.zeros_like(q_pair))


def _pair_rows(a_top, a_bot):
    row = lax.broadcasted_iota(jnp.int32, a_top.shape, 0)
    return jnp.where(row < HEAD_DIM, a_top, a_bot)


def _pair_lhs(vt, which):
    row = lax.broadcasted_iota(jnp.int32, vt.shape, 0)
    keep = (row < HEAD_DIM) if which == 0 else (row >= HEAD_DIM)
    return jnp.where(keep, vt, jnp.ones_like(vt))


def _pair_finish(acc_ref):
    a0, a1 = acc_ref[0], acc_ref[1]
    return _pair_rows(a0 / a0[HEAD_DIM:HEAD_DIM + 1, :], a1 / a1[0:1, :])


def _fox_gate_kernel(fl_ref, bf_ref, tri_ref, ex_ref, cp_ref, carry_ref):
    @pl.when(pl.program_id(1) == 0)
    def _():
        carry_ref[...] = jnp.zeros(carry_ref.shape, F32)

    z = fl_ref[...] + bf_ref[...]
    log_f = jnp.minimum(z, 0.0) - jnp.log1p(jnp.exp(-jnp.abs(z)))
    c = jnp.dot(tri_ref[...], log_f, preferred_element_type=F32, precision=HIGHEST) + carry_ref[...]
    carry_ref[...] = c[-1:, :]
    out = jnp.zeros(c.shape, F32)
    rem = c * LOG2E
    for piece in range(C_SPLIT):
        part = rem.astype(BF16)
        rem = rem - part.astype(F32)
        out = out + jnp.dot(part, ex_ref[piece], preferred_element_type=F32)
    cp_ref[...] = out.astype(BF16)


def _fox_gate(fl, b_f, B, S):
    T = fl.shape[0]
    tm = min(ROW_TILE, S)
    ns = S // tm
    tri = jnp.asarray(np.tril(np.ones((tm, tm), np.float32)))
    ex = np.zeros((C_SPLIT, LANES, LANES), np.float32)
    for h in range(N_HEADS):
        for j in range(C_SPLIT):
            ex[j, h, C_SPLIT * h + j] = 1.0
    bf = _pad_cols(b_f.reshape(1, -1), LANES)
    return pl.pallas_call(
        _fox_gate_kernel, grid=(B, ns),
        in_specs=[pl.BlockSpec((tm, LANES), lambda b, s: (b * ns + s, 0)),
                  pl.BlockSpec((1, LANES), lambda b, s: (0, 0)),
                  pl.BlockSpec((tm, tm), lambda b, s: (0, 0)),
                  pl.BlockSpec((C_SPLIT, LANES, LANES), lambda b, s: (0, 0, 0))],
        out_specs=pl.BlockSpec((tm, LANES), lambda b, s: (b * ns + s, 0)),
        out_shape=jax.ShapeDtypeStruct((T, LANES), BF16),
        scratch_shapes=[pltpu.VMEM((1, LANES), F32)],
        compiler_params=_cparams(("parallel", "arbitrary")), name="fox_gate")(
            fl, bf, tri, jnp.asarray(ex, BF16))


def _fox_kernel(q_ref, k_ref, cp_ref, vt_ref, o_ref, m_ref, acc_ref, *, tq, tk):
    hp = pl.program_id(1)
    i = pl.program_id(2)
    _flash_init(m_ref, acc_ref)
    q = q_ref[...]
    lane = lax.broadcasted_iota(jnp.int32, q.shape, 1)
    qa = []
    for h in range(2):
        lo = C_SPLIT * (2 * hp + h)
        minus_one = jnp.where((lane >= lo) & (lane < lo + C_SPLIT), -1.0, 0.0).astype(q.dtype)
        qa.append(jnp.concatenate([_head_of_pair(q, h), minus_one], axis=1))

    def step(off, size, mask):
        k = jnp.concatenate([k_ref[pl.ds(off, size), :], cp_ref[pl.ds(off, size), :]], axis=1)
        vt = vt_ref[:, pl.ds(off, size)]

        lhs = [_pair_lhs(vt, h) for h in range(2)]

        def scores(h):
            sT = _nt_dot(k, qa[h])
            return sT if mask is None else jnp.where(mask, sT, NEG)

        _staggered([0, 1], scores, lambda h, sT: _flash_step(sT, lhs[h], m_ref, acc_ref, h))

    _causal_sweep(i, tq, tk, step)
    o_ref[...] = _pair_finish(acc_ref).T.astype(o_ref.dtype)


def _fox_attention(qk, cp, vt, B, S):
    T = qk.shape[0]
    tq = min(ATTN_TILE, S)
    tk = min(KV_TILE, S)
    nq = S // tq
    npair = N_HEADS // 2
    ncol = D_MODEL // LANES
    return pl.pallas_call(
        functools.partial(_fox_kernel, tq=tq, tk=tk),
        grid=(B, npair, nq),
        in_specs=[pl.BlockSpec((tq, LANES), lambda b, hp, i: (b * nq + i, hp)),
                  pl.BlockSpec((S, LANES), lambda b, hp, i: (b, ncol + hp)),
                  pl.BlockSpec((S, LANES), lambda b, hp, i: (b, 0)),
                  pl.BlockSpec((LANES, S), lambda b, hp, i: (hp, b))],
        out_specs=pl.BlockSpec((tq, LANES), lambda b, hp, i: (b * nq + i, hp)),
        out_shape=jax.ShapeDtypeStruct((T, D_MODEL), BF16),
        scratch_shapes=[pltpu.VMEM((2, 1, tq), F32), pltpu.VMEM((2, LANES, tq), F32)],
        compiler_params=_cparams(("parallel", "parallel", "arbitrary")), name="fox_attn")(
            qk, qk, cp, vt)


def _gate_expand(gl_ref, bg_ref, ex_ref):
    sig = jax.nn.sigmoid(gl_ref[...] + bg_ref[...])
    hi = sig.astype(BF16)
    lo = (sig - hi.astype(F32)).astype(BF16)
    ex = ex_ref[...]
    return jnp.dot(hi, ex, preferred_element_type=F32) + jnp.dot(lo, ex, preferred_element_type=F32)


def _nsa_compress_kernel(f_ref, pe_ref, w1_ref, w2_ref, o_ref):
    blk = (f_ref[...].astype(F32) + pe_ref[...]).astype(BF16)
    h = jax.nn.gelu(jnp.dot(blk, w1_ref[...], preferred_element_type=F32))
    o_ref[...] = jnp.dot(h.astype(BF16), w2_ref[...], preferred_element_type=F32)


def _nsa_compress(flat, pe_flat, w1, w2p):
    M, K = flat.shape
    tm = min(ROW_TILE, M)
    Hc = w1.shape[1]
    return pl.pallas_call(
        _nsa_compress_kernel, grid=(M // tm,),
        in_specs=[pl.BlockSpec((tm, K), lambda i: (i, 0)), pl.BlockSpec((1, K), lambda i: (0, 0)),
                  pl.BlockSpec((K, Hc), lambda i: (0, 0)), pl.BlockSpec((Hc, LANES), lambda i: (0, 0))],
        out_specs=pl.BlockSpec((tm, LANES), lambda i: (i, 0)),
        out_shape=jax.ShapeDtypeStruct((M, LANES), F32),
        compiler_params=_cparams(("parallel",)), name="nsa_compress")(flat, pe_flat, w1, w2p)


def _nsa_cmp_kernel(q_ref, kk_ref, vt_ref, ovt_ref, gl_ref, bg_ref, ex_ref, o_ref, sn_ref, *, tq, ncp, nselp):
    i = pl.program_id(2)
    q = q_ref[...]
    kk = kk_ref[...]
    vt = vt_ref[...]
    t = i * tq + lax.broadcasted_iota(jnp.int32, (ncp, tq), 1)
    n = lax.broadcasted_iota(jnp.int32, (ncp, tq), 0)
    valid = n * NSA_CMP_STRIDE + (NSA_CMP_LEN - 1) <= t
    psum = jnp.zeros((ncp, tq), F32)
    outs = []
    for hg in range(NSA_HEADS_PER_GROUP):
        qh = _head_of_pair(q[:, LANES * (hg // 2):LANES * (hg // 2 + 1)], hg % 2)
        sT = jnp.where(valid, _nt_dot(kk, qh), NEG)
        m = jnp.max(sT, axis=0, keepdims=True)
        e = jnp.where(valid, jnp.exp2(sT - m), 0.0)
        p = e / jnp.maximum(jnp.sum(e, axis=0, keepdims=True), 1e-30)
        psum = psum + p
        outs.append(jnp.dot(vt, p.astype(BF16), preferred_element_type=F32))
    o = jnp.concatenate(outs, axis=0).T
    o_ref[...] = (o * _gate_expand(gl_ref, bg_ref, ex_ref)).astype(o_ref.dtype)
    hi = psum.astype(BF16)
    lo = (psum - hi.astype(F32)).astype(BF16)
    ovt = ovt_ref[...]
    imp = jnp.dot(ovt, hi, preferred_element_type=F32) + jnp.dot(ovt, lo, preferred_element_type=F32)
    jblk = lax.broadcasted_iota(jnp.int32, (nselp, tq), 0)
    cur = (i * tq + lax.broadcasted_iota(jnp.int32, (nselp, tq), 1)) >> _log2(NSA_SEL_LEN)
    forced = (jblk == 0) | (jblk == cur) | (jblk == cur - 1)
    pri = jnp.where(forced, imp + NSA_FORCE_BONUS, imp)
    past = jblk <= cur
    pri = jnp.where(past, pri, NEG)
    chosen = _topk_rows(pri, NSA_SEL_TOPN)
    sn = jnp.where((chosen > 0.0) & past, 0.0, NEG)
    sn_ref[...] = sn.T.astype(sn_ref.dtype)


def _nsa_cmp_branch(q_c, kk_c, vt_c, overlap_t, gl, bg, ex, B, S):
    T = q_c.shape[0]
    G = NSA_KV_GROUPS
    tq = min(ATTN_TILE, S)
    nq = S // tq
    ncp = kk_c.shape[2]
    nselp = overlap_t.shape[0]
    return pl.pallas_call(
        functools.partial(_nsa_cmp_kernel, tq=tq, ncp=ncp, nselp=nselp),
        grid=(B, G, nq),
        in_specs=[pl.BlockSpec((tq, 2 * LANES), lambda b, g, i: (b * nq + i, g)),
                  pl.BlockSpec((None, None, ncp, LANES), lambda b, g, i: (b, g, 0, 0)),
                  pl.BlockSpec((None, None, HEAD_DIM, ncp), lambda b, g, i: (b, g, 0, 0)),
                  pl.BlockSpec((nselp, ncp), lambda b, g, i: (0, 0)),
                  pl.BlockSpec((tq, LANES), lambda b, g, i: (b * nq + i, 0)),
                  pl.BlockSpec((1, LANES), lambda b, g, i: (0, 0)),
                  pl.BlockSpec((None, LANES, 2 * LANES), lambda b, g, i: (g, 0, 0))],
        out_specs=[pl.BlockSpec((tq, 2 * LANES), lambda b, g, i: (b * nq + i, g)),
                   pl.BlockSpec((None, None, tq, nselp), lambda b, g, i: (b, g, i, 0))],
        out_shape=[jax.ShapeDtypeStruct((T, D_MODEL), BF16),
                   jax.ShapeDtypeStruct((B, G, S, nselp), BF16)],
        compiler_params=_cparams(("parallel", "parallel", "parallel")), name="nsa_cmp")(
            q_c, kk_c, vt_c, overlap_t, gl, bg, ex)


def _nsa_kv_kernel(*refs, mode, tq, tk):
    if mode == "sel":
        q_ref, kk_ref, vt_ref, oh_ref, sn_ref, gl_ref, bg_ref, ex_ref, o_ref, m_ref, acc_ref = refs
    else:
        q_ref, kk_ref, vt_ref, gl_ref, bg_ref, ex_ref, o_ref, m_ref, acc_ref = refs
    i = pl.program_id(2)
    _flash_init(m_ref, acc_ref)
    q = q_ref[...]
    qh = []
    for hg in range(NSA_HEADS_PER_GROUP):
        qq = _head_of_pair(q[:, LANES * (hg // 2):LANES * (hg // 2 + 1)], hg % 2)
        if mode == "sel":
            qq = jnp.concatenate([qq, sn_ref[...]], axis=1)
        qh.append(qq)

    def step(off, size, mask):
        k = kk_ref[pl.ds(off, size), :]
        vt = vt_ref[:, pl.ds(off, size)]
        lhs = jnp.concatenate([vt, jnp.ones((ONES_ROWS, size), vt.dtype)], axis=0)
        if mode == "sel":
            k = jnp.concatenate([k, oh_ref[pl.ds(off, size), :]], axis=1)

        def scores(hg):
            sT = _nt_dot(k, qh[hg])
            return sT if mask is None else jnp.where(mask, sT, NEG)

        _staggered(list(range(NSA_HEADS_PER_GROUP)), scores,
                   lambda hg, sT: _flash_step(sT, lhs, m_ref, acc_ref, hg))

    if mode == "sel":
        _causal_sweep(i, tq, tk, step)
    else:
        @pl.when(i > 0)
        def _():
            step(pl.multiple_of((i - 1) * tq, tq), tq, _causal_t(tq, tq, strict_lower=True))
        step(pl.multiple_of(i * tq, tq), tq, _causal_t(tq, tq))
    outs = []
    for hg in range(NSA_HEADS_PER_GROUP):
        a = acc_ref[hg]
        outs.append(a[:HEAD_DIM] / a[HEAD_DIM:HEAD_DIM + 1, :])
    o_ref[...] = (jnp.concatenate(outs, axis=0).T * _gate_expand(gl_ref, bg_ref, ex_ref)).astype(o_ref.dtype)


def _nsa_kv_branch(mode, q_r, kk, vt, gl, bg, ex, B, S, onehot=None, selneg=None):
    T = q_r.shape[0]
    G = NSA_KV_GROUPS
    tq = min(ATTN_TILE, S)
    tk = min(KV_TILE, S)
    assert NSA_WINDOW == tq or mode == "sel"
    nq = S // tq
    in_specs = [pl.BlockSpec((tq, 2 * LANES), lambda b, g, i: (b * nq + i, g)),
                pl.BlockSpec((None, None, S, LANES), lambda b, g, i: (b, g, 0, 0)),
                pl.BlockSpec((HEAD_DIM, S), lambda b, g, i: (g, b))]
    args = [q_r, kk, vt]
    if mode == "sel":
        in_specs += [pl.BlockSpec((S, LANES), lambda b, g, i: (0, 0)),
                     pl.BlockSpec((None, None, tq, LANES), lambda b, g, i: (b, g, i, 0))]
        args += [onehot, selneg]
    in_specs += [pl.BlockSpec((tq, LANES), lambda b, g, i: (b * nq + i, 0)),
                 pl.BlockSpec((1, LANES), lambda b, g, i: (0, 0)),
                 pl.BlockSpec((None, LANES, 2 * LANES), lambda b, g, i: (g, 0, 0))]
    args += [gl, bg, ex]
    nh = NSA_HEADS_PER_GROUP
    return pl.pallas_call(
        functools.partial(_nsa_kv_kernel, mode=mode, tq=tq, tk=tk),
        grid=(B, G, nq), in_specs=in_specs,
        out_specs=pl.BlockSpec((tq, 2 * LANES), lambda b, g, i: (b * nq + i, g)),
        out_shape=jax.ShapeDtypeStruct((T, D_MODEL), BF16),
        scratch_shapes=[pltpu.VMEM((nh, 1, tq), F32), pltpu.VMEM((nh, HEAD_DIM + ONES_ROWS, tq), F32)],
        compiler_params=_cparams(("parallel", "parallel", "arbitrary")), name=f"nsa_{mode}")(*args)


def _kmean_kernel(k_ref, o_ref, *, nblk):
    k = k_ref[...].astype(F32)
    o_ref[...] = jnp.mean(k.reshape(nblk, MOBA_BLOCK, k.shape[1]), axis=1)


def _moba_kmean(k_rot):
    T, D = k_rot.shape
    nblk = 8
    rows = nblk * MOBA_BLOCK
    assert T % rows == 0
    return pl.pallas_call(
        functools.partial(_kmean_kernel, nblk=nblk), grid=(T // rows,),
        in_specs=[pl.BlockSpec((rows, D), lambda i: (i, 0))],
        out_specs=pl.BlockSpec((nblk, D), lambda i: (i, 0)),
        out_shape=jax.ShapeDtypeStruct((T // MOBA_BLOCK, D), F32),
        compiler_params=_cparams(("parallel",)), name="moba_kmean")(k_rot)


def _moba_select_kernel(q_ref, r_ref, sn_ref, *, tq, nblk):
    i = pl.program_id(2)
    gsT = _nt_dot(r_ref[...], q_ref[...].astype(F32), HIGHEST)
    jblk = lax.broadcasted_iota(jnp.int32, (nblk, tq), 0)
    cb = (i * tq + lax.broadcasted_iota(jnp.int32, (nblk, tq), 1)) >> _log2(MOBA_BLOCK)
    past = jblk < cb
    parts = []
    for h in range(2):
        gs = jnp.where(past, gsT[h * nblk:(h + 1) * nblk, :], NEG)
        chosen = _topk_rows(gs, min(MOBA_TOPK, nblk))
        parts.append(jnp.where(((chosen > 0.0) & past) | (jblk == cb), 0.0, NEG))
    parts.append(jnp.zeros((LANES - 2 * nblk, tq), F32))
    sn_ref[...] = jnp.concatenate(parts, axis=0).T.astype(sn_ref.dtype)


def _moba_select(q_rot, rmat_t, B, S):
    tq = min(ATTN_TILE, S)
    nq = S // tq
    npair = N_HEADS // 2
    nblk = S // MOBA_BLOCK
    return pl.pallas_call(
        functools.partial(_moba_select_kernel, tq=tq, nblk=nblk), grid=(B, npair, nq),
        in_specs=[pl.BlockSpec((tq, LANES), lambda b, hp, i: (b * nq + i, hp)),
                  pl.BlockSpec((None, None, LANES, LANES), lambda b, hp, i: (b, hp, 0, 0))],
        out_specs=pl.BlockSpec((None, None, tq, LANES), lambda b, hp, i: (b, hp, i, 0)),
        out_shape=jax.ShapeDtypeStruct((B, npair, S, LANES), BF16),
        compiler_params=_cparams(("parallel", "parallel", "parallel")), name="moba_select")(q_rot, rmat_t)


def _moba_kernel(q_ref, k_ref, vt_ref, oh_ref, sn_ref, o_ref, m_ref, acc_ref, *, tq, tk, nblk):
    i = pl.program_id(2)
    _flash_init(m_ref, acc_ref)
    q = q_ref[...]
    sn = sn_ref[...]
    lane = lax.broadcasted_iota(jnp.int32, sn.shape, 1)
    qa = []
    for h in range(2):
        mine = (lane >= h * nblk) & (lane < (h + 1) * nblk)
        qa.append(jnp.concatenate([_head_of_pair(q, h), jnp.where(mine, sn, jnp.zeros_like(sn))], axis=1))

    def step(off, size, mask):
        k = jnp.concatenate([k_ref[pl.ds(off, size), :], oh_ref[pl.ds(off, size), :]], axis=1)
        vt = vt_ref[:, pl.ds(off, size)]

        lhs = [_pair_lhs(vt, h) for h in range(2)]

        def scores(h):
            sT = _nt_dot(k, qa[h])
            return sT if mask is None else jnp.where(mask, sT, NEG)

        _staggered([0, 1], scores, lambda h, sT: _flash_step(sT, lhs[h], m_ref, acc_ref, h))

    _causal_sweep(i, tq, tk, step)
    o_ref[...] = _pair_finish(acc_ref).T.astype(o_ref.dtype)


def _moba_attention(qk_rot, vt, onehot2, selneg, B, S):
    T = qk_rot.shape[0]
    tq = min(ATTN_TILE, S)
    tk = min(KV_TILE, S)
    nq = S // tq
    npair = N_HEADS // 2
    ncol = D_MODEL // LANES
    nblk = S // MOBA_BLOCK
    return pl.pallas_call(
        functools.partial(_moba_kernel, tq=tq, tk=tk, nblk=nblk),
        grid=(B, npair, nq),
        in_specs=[pl.BlockSpec((tq, LANES), lambda b, hp, i: (b * nq + i, hp)),
                  pl.BlockSpec((S, LANES), lambda b, hp, i: (b, ncol + hp)),
                  pl.BlockSpec((LANES, S), lambda b, hp, i: (hp, b)),
                  pl.BlockSpec((S, LANES), lambda b, hp, i: (0, 0)),
                  pl.BlockSpec((None, None, tq, LANES), lambda b, hp, i: (b, hp, i, 0))],
        out_specs=pl.BlockSpec((tq, LANES), lambda b, hp, i: (b * nq + i, hp)),
        out_shape=jax.ShapeDtypeStruct((T, D_MODEL), BF16),
        scratch_shapes=[pltpu.VMEM((2, 1, tq), F32), pltpu.VMEM((2, LANES, tq), F32)],
        compiler_params=_cparams(("parallel", "parallel", "arbitrary")), name="moba_attn")(
            qk_rot, qk_rot, vt, onehot2, selneg)


def _router_kernel(wt_ref, x_ref, rb_ref, e_ref, g_ref):
    tm = x_ref.shape[0]
    gsz = N_EXPERTS // N_GROUPS
    scores = jax.nn.sigmoid(_nt_dot(wt_ref[...], x_ref[...], HIGHEST))
    biased = scores + rb_ref[...]
    member = lax.broadcasted_iota(jnp.int32, (gsz, tm), 0).astype(F32)
    gscore = []
    for g in range(N_GROUPS):
        v = biased[g * gsz:(g + 1) * gsz, :]
        m1 = jnp.max(v, axis=0, keepdims=True)
        i1 = jnp.min(jnp.where(v == m1, member, float(gsz)), axis=0, keepdims=True)
        m2 = jnp.max(jnp.where(member == i1, -jnp.inf, v), axis=0, keepdims=True)
        gscore.append(m1 + m2)
    gsel = _topk_rows(jnp.concatenate(gscore, axis=0), TOPK_GROUPS)
    emask = jnp.concatenate([jnp.broadcast_to(gsel[g:g + 1, :], (gsz, tm)) for g in range(N_GROUPS)], axis=0)
    work = jnp.where(emask > 0.0, biased, NEG)
    erow = lax.broadcasted_iota(jnp.int32, (N_EXPERTS, tm), 0).astype(F32)
    idxs, vals = [], []
    for _ in range(TOP_K):
        m = jnp.max(work, axis=0, keepdims=True)
        idx = jnp.min(jnp.where(work == m, erow, float(N_EXPERTS)), axis=0, keepdims=True)
        pick = erow == idx
        idxs.append(idx)
        vals.append(jnp.sum(jnp.where(pick, scores, 0.0), axis=0, keepdims=True))
        work = jnp.where(pick, -jnp.inf, work)
    gw = jnp.concatenate(vals, axis=0)
    e_ref[...] = jnp.concatenate(idxs, axis=0).astype(jnp.int32)
    g_ref[...] = gw / jnp.sum(gw, axis=0, keepdims=True) * ROUTED_SCALE


def _router(x1, router_w, router_b):
    T, D = x1.shape
    tm = min(ROW_TILE, T)
    return pl.pallas_call(
        _router_kernel, grid=(T // tm,),
        in_specs=[pl.BlockSpec((N_EXPERTS, D), lambda i: (0, 0)), pl.BlockSpec((tm, D), lambda i: (i, 0)),
                  pl.BlockSpec((N_EXPERTS, 1), lambda i: (0, 0))],
        out_specs=[pl.BlockSpec((TOP_K, tm), lambda i: (0, i))] * 2,
        out_shape=[jax.ShapeDtypeStruct((TOP_K, T), jnp.int32), jax.ShapeDtypeStruct((TOP_K, T), F32)],
        compiler_params=_cparams(("parallel",)), name="moe_router")(
            router_w.T, x1, router_b.reshape(N_EXPERTS, 1))


def _rank_kernel(e_ref, tri_ref, rank_ref, cnt_ref, carry_ref):
    @pl.when(pl.program_id(0) == 0)
    def _():
        carry_ref[...] = jnp.zeros(carry_ref.shape, F32)

    tm = e_ref.shape[1]
    erow = lax.broadcasted_iota(jnp.int32, (N_EXPERTS, tm), 0)
    tri = tri_ref[...]
    base = carry_ref[...]
    ranks = []
    for k in range(TOP_K):
        oh = erow == e_ref[k:k + 1, :]
        ohb = jnp.where(oh, 1.0, 0.0).astype(BF16)
        incl = jnp.dot(ohb, tri, preferred_element_type=F32)
        ranks.append(jnp.sum(jnp.where(oh, base + incl - 1.0, 0.0), axis=0, keepdims=True))
        base = base + incl[:, tm - 1:tm]
    carry_ref[...] = base
    rank_ref[...] = jnp.concatenate(ranks, axis=0).astype(jnp.int32)
    cnt_ref[...] = jnp.broadcast_to(base, cnt_ref.shape)


def _expert_ranks(eidx_t):
    K, T = eidx_t.shape
    tm = min(ROW_TILE, T)
    tri = jnp.asarray(np.triu(np.ones((tm, tm), np.float32)), BF16)
    return pl.pallas_call(
        _rank_kernel, grid=(T // tm,),
        in_specs=[pl.BlockSpec((K, tm), lambda i: (0, i)), pl.BlockSpec((tm, tm), lambda i: (0, 0))],
        out_specs=[pl.BlockSpec((K, tm), lambda i: (0, i)), pl.BlockSpec((N_EXPERTS, LANES), lambda i: (0, 0))],
        out_shape=[jax.ShapeDtypeStruct((K, T), jnp.int32), jax.ShapeDtypeStruct((N_EXPERTS, LANES), F32)],
        scratch_shapes=[pltpu.VMEM((N_EXPERTS, 1), F32)],
        compiler_params=_cparams(("arbitrary",)), name="moe_rank")(eidx_t, tri)


def _dest_kernel(e_ref, rank_ref, ps_ref, d_ref):
    tm = e_ref.shape[1]
    erow = lax.broadcasted_iota(jnp.int32, (N_EXPERTS, tm), 0)
    ps = ps_ref[...]
    rows = []
    for k in range(TOP_K):
        oh = erow == e_ref[k:k + 1, :]
        rows.append(jnp.sum(jnp.where(oh, ps, 0.0), axis=0, keepdims=True))
    d_ref[...] = jnp.concatenate(rows, axis=0).astype(jnp.int32) + rank_ref[...]


def _expert_dest(eidx_t, rank_t, pstarts):
    K, T = eidx_t.shape
    tm = min(ROW_TILE, T)
    blk = pl.BlockSpec((K, tm), lambda i: (0, i))
    return pl.pallas_call(
        _dest_kernel, grid=(T // tm,),
        in_specs=[blk, blk, pl.BlockSpec((N_EXPERTS, 1), lambda i: (0, 0))],
        out_specs=blk, out_shape=jax.ShapeDtypeStruct((K, T), jnp.int32),
        compiler_params=_cparams(("parallel",)), name="moe_dest")(
            eidx_t, rank_t, pstarts.astype(F32).reshape(N_EXPERTS, 1))


def _expert_kernel(be_ref, nb_ref, x_ref, wg_ref, wu_ref, wd_ref, o_ref, wgb_ref, wub_ref, wdb_ref):
    b = pl.program_id(0)

    @pl.when((b == 0) | (be_ref[b] != be_ref[jnp.maximum(b - 1, 0)]))
    def _():
        wgb_ref[...] = wg_ref[...].astype(BF16)
        wub_ref[...] = wu_ref[...].astype(BF16)
        wdb_ref[...] = wd_ref[...].astype(BF16)

    @pl.when(b < nb_ref[0])
    def _():
        x = x_ref[...].astype(BF16)
        h = jax.nn.silu(jnp.dot(x, wgb_ref[...], preferred_element_type=F32)) * jnp.dot(
            x, wub_ref[...], preferred_element_type=F32)
        o_ref[...] = jnp.dot(h.astype(BF16), wdb_ref[...], preferred_element_type=F32).astype(o_ref.dtype)

    @pl.when(b >= nb_ref[0])
    def _():
        o_ref[...] = jnp.zeros(o_ref.shape, o_ref.dtype)


def _expert_ffn(blk_e, nb_used, xs, wg, wu, wd):
    P, D = xs.shape
    NB = P // MOE_BLOCK
    E = EXPERT_DIM
    grid_spec = pltpu.PrefetchScalarGridSpec(
        num_scalar_prefetch=2, grid=(NB,),
        in_specs=[pl.BlockSpec((MOE_BLOCK, D), lambda b, be, nb: (b, 0)),
                  pl.BlockSpec((None, D, E), lambda b, be, nb: (be[b], 0, 0)),
                  pl.BlockSpec((None, D, E), lambda b, be, nb: (be[b], 0, 0)),
                  pl.BlockSpec((None, E, D), lambda b, be, nb: (be[b], 0, 0))],
        out_specs=pl.BlockSpec((MOE_BLOCK, D), lambda b, be, nb: (b, 0)),
        scratch_shapes=[pltpu.VMEM((D, E), BF16), pltpu.VMEM((D, E), BF16), pltpu.VMEM((E, D), BF16)])
    return pl.pallas_call(
        _expert_kernel, grid_spec=grid_spec, out_shape=jax.ShapeDtypeStruct((P, D), BF16),
        compiler_params=_cparams(("arbitrary",)), name="moe_experts")(blk_e, nb_used, xs, wg, wu, wd)


def _post_moe_kernel(x_ref, *refs):
    y_refs = refs[:TOP_K]
    gw_ref, p_ref, sg_ref, su_ref, sd_ref, g_ref, b_ref, wg_ref, wp_ref, o_ref = refs[TOP_K:]
    x = x_ref[...]
    xb = x.astype(BF16)
    h = jax.nn.silu(jnp.dot(xb, sg_ref[...], preferred_element_type=F32)) * jnp.dot(
        xb, su_ref[...], preferred_element_type=F32)
    ffn = jnp.dot(h.astype(BF16), sd_ref[...], preferred_element_type=F32)
    gw = gw_ref[...]
    for k in range(TOP_K):
        ffn = ffn + gw[:, k:k + 1] * y_refs[k][...].astype(F32)
    z = DN_ALPHA * x + ffn
    x2 = _layer_norm(z, g_ref[...], b_ref[...])
    gate = jax.nn.sigmoid(jnp.dot(x2.astype(BF16), wg_ref[...], preferred_element_type=F32))
    proj = jnp.dot(p_ref[...].astype(BF16), wp_ref[...], preferred_element_type=F32)
    o_ref[...] = x2 + gate * proj


def _post_moe(x1, yg, gw, p, sg, su, sd, g, b, wgate, wproj):
    T, D = x1.shape
    tm = min(ROW_TILE // 2, T)
    nt = T // tm
    PD = p.shape[1]
    SD = sg.shape[1]
    row = pl.BlockSpec((tm, D), lambda i: (i, 0))
    vec = pl.BlockSpec((1, D), lambda i: (0, 0))
    full = lambda r, c: pl.BlockSpec((r, c), lambda i: (0, 0))
    return pl.pallas_call(
        _post_moe_kernel, grid=(T // tm,),
        in_specs=[row] + [pl.BlockSpec((tm, D), functools.partial(lambda i, k: (k * nt + i, 0), k=k))
                          for k in range(TOP_K)]
                 + [pl.BlockSpec((tm, TOP_K), lambda i: (i, 0)),
                    pl.BlockSpec((tm, PD), lambda i: (i, 0)), full(D, SD), full(D, SD),
                  full(SD, D), vec, vec, full(D, D), full(PD, D)],
        out_specs=row, out_shape=jax.ShapeDtypeStruct((T, D), F32),
        compiler_params=_cparams(("parallel",)), name="post_moe")(
            x1, *([yg] * TOP_K), gw, p, sg, su, sd, g.reshape(1, D), b.reshape(1, D), wgate, wproj)


def _rope_tables(positions):
    inv = 1.0 / (ROPE_THETA ** (jnp.arange(0, HEAD_DIM, 2, dtype=F32) / HEAD_DIM))
    ang = positions.astype(F32).reshape(-1)[:, None] * inv
    cos, sin = jnp.cos(ang), jnp.sin(ang)
    cosf = jnp.concatenate([cos] * (LANES // (HEAD_DIM // 2)), axis=1)
    sinf = jnp.concatenate([-sin, sin] * (LANES // HEAD_DIM), axis=1)
    return cosf, sinf


def _q_col_scale(n_q, n):
    return jnp.concatenate([jnp.full((n_q,), Q_SCALE, F32), jnp.ones((n - n_q,), F32)])


def _fox_mixer(x, B, S, w_in, b_f):
    D = D_MODEL
    wb = w_in[:, :3 * D].astype(BF16)
    qk = _proj(x, wb[:, :2 * D], col_scale=_q_col_scale(D, 2 * D))
    vt = _proj_t(wb[:, 2 * D:].T, x)
    fl = _proj(x, _pad_cols(w_in[:, 3 * D:], LANES), out_dtype=F32, tn=LANES, precision=HIGHEST)
    cp = _fox_gate(fl, b_f, B, S)
    return [_fox_attention(qk, cp, vt, B, S)]


def _nsa_mixer(x, B, S, cosf, sinf, w_in, b_gate, pe_k, pe_v, ck_w1, ck_w2, cv_w1, cv_w2):
    D, G, HD = D_MODEL, NSA_KV_GROUPS, HEAD_DIM
    HG = NSA_HEADS_PER_GROUP
    kvw = G * HD
    wb = w_in[:, :D + 6 * kvw].astype(BF16)
    q_c, q_r = _proj(x, wb[:, :D], mode="both", cos=cosf, sin=sinf, col_scale=_q_col_scale(D, D))
    w_rot = jnp.concatenate([wb[:, D + 2 * kvw:D + 3 * kvw], wb[:, D + 4 * kvw:D + 5 * kvw]], axis=1)
    k_rot = _proj(x, w_rot, mode="rope", cos=cosf, sin=sinf)
    kvc = _proj(x, wb[:, D:D + 2 * kvw])
    w_v = jnp.concatenate([wb[:, D + 3 * kvw:D + 4 * kvw], wb[:, D + 5 * kvw:D + 6 * kvw]], axis=1)
    vt = _proj_t(w_v.T, x)
    gl = _proj(x, _pad_cols(w_in[:, D + 6 * kvw:], LANES), out_dtype=F32, tn=LANES, precision=HIGHEST)
    bg = _pad_cols(b_gate.reshape(1, -1), LANES)

    def grouped(t2d):
        t = t2d.reshape(B, S, G, HD).transpose(0, 2, 1, 3)
        return jnp.concatenate([t, t], axis=-1)

    n_chunks = S // NSA_CMP_STRIDE
    n_cmp = n_chunks - NSA_CMP_LEN // NSA_CMP_STRIDE + 1
    ncp = n_chunks

    def compress(t2d, pe, w1, w2):
        ch = t2d.reshape(B, n_chunks, NSA_CMP_STRIDE, G, HD).transpose(0, 1, 3, 2, 4)
        ch = ch.reshape(B, n_chunks, G, NSA_CMP_STRIDE * HD)
        flat = jnp.concatenate([ch[:, :n_cmp], ch[:, 1:n_cmp + 1]], axis=-1)
        flat = jnp.pad(flat, ((0, 0), (0, ncp - n_cmp), (0, 0), (0, 0))).reshape(B * ncp * G, -1)
        out = _nsa_compress(flat, pe.reshape(1, -1), w1.astype(BF16), _pad_cols(w2, LANES).astype(BF16))
        return out[:, :HD].reshape(B, ncp, G, HD).astype(BF16)

    kc = compress(kvc[:, :kvw], pe_k, ck_w1, ck_w2).transpose(0, 2, 1, 3)
    kk_c = jnp.concatenate([kc, kc], axis=-1)
    vt_c = compress(kvc[:, kvw:], pe_v, cv_w1, cv_w2).transpose(0, 2, 3, 1)

    n_sel = S // NSA_SEL_LEN
    assert n_sel <= LANES
    cmp_start = np.arange(ncp) * NSA_CMP_STRIDE
    sel_start = np.arange(LANES) * NSA_SEL_LEN
    overlap = ((cmp_start[:, None] < sel_start[None, :] + NSA_SEL_LEN)
               & (cmp_start[:, None] + NSA_CMP_LEN > sel_start[None, :])
               & (np.arange(ncp)[:, None] < n_cmp) & (np.arange(LANES)[None, :] < n_sel))
    overlap_t = jnp.asarray(overlap.T, BF16)
    ex = np.zeros((3, G, LANES, HG * HD), np.float32)
    for br in range(3):
        for g in range(G):
            for hg in range(HG):
                ex[br, g, (g * HG + hg) * 3 + br, hg * HD:(hg + 1) * HD] = 1.0
    ex = jnp.asarray(ex, BF16)

    o_c, selneg = _nsa_cmp_branch(q_c, kk_c, vt_c, overlap_t, gl, bg, ex[0], B, S)
    onehot = jnp.asarray((np.arange(S)[:, None] // NSA_SEL_LEN) == np.arange(LANES)[None, :], BF16)
    kk_s = grouped(k_rot[:, :kvw])
    kk_w = grouped(k_rot[:, kvw:])
    o_s = _nsa_kv_branch("sel", q_r, kk_s, vt[:kvw], gl, bg, ex[1], B, S, onehot=onehot, selneg=selneg)
    o_w = _nsa_kv_branch("win", q_r, kk_w, vt[kvw:], gl, bg, ex[2], B, S)
    return [o_c, o_s, o_w]


def _moba_mixer(x, B, S, cosf, sinf, w_in):
    D, H, HD = D_MODEL, N_HEADS, HEAD_DIM
    wb = w_in.astype(BF16)
    qk_rot = _proj(x, wb[:, :2 * D], mode="rope", cos=cosf, sin=sinf, col_scale=_q_col_scale(D, 2 * D))
    vt = _proj_t(wb[:, 2 * D:].T, x)
    nblk = S // MOBA_BLOCK
    assert S % MOBA_BLOCK == 0 and 2 * nblk <= LANES and nblk % SUBLANES == 0
    kmean = _moba_kmean(qk_rot[:, D:]).reshape(B, nblk, H // 2, 2, HD)
    km = kmean.transpose(0, 2, 3, 1, 4)
    rmat_t = jnp.zeros((B, H // 2, LANES, 2, HD), F32)
    rmat_t = rmat_t.at[:, :, :nblk, 0].set(km[:, :, 0]).at[:, :, nblk:2 * nblk, 1].set(km[:, :, 1])
    selneg = _moba_select(qk_rot, rmat_t.reshape(B, H // 2, LANES, LANES), B, S)
    blk_of = np.arange(S) // MOBA_BLOCK
    oh = np.zeros((S, LANES), np.float32)
    oh[np.arange(S), blk_of] = 1.0
    oh[np.arange(S), nblk + blk_of] = 1.0
    return [_moba_attention(qk_rot, vt, jnp.asarray(oh, BF16), selneg, B, S)]


def _moe_dispatch(x1, x1b, router_w, router_b, wg, wu, wd):
    T, D = x1.shape
    eidx_t, gw_t = _router(x1, router_w, router_b)
    rank_t, cnt = _expert_ranks(eidx_t)
    counts = cnt[:, 0].astype(jnp.int32)
    padded = (counts + MOE_BLOCK - 1) // MOE_BLOCK * MOE_BLOCK
    pends = jnp.cumsum(padded)
    pstarts = pends - padded
    R = T * TOP_K
    P = -(-(R + N_EXPERTS * (MOE_BLOCK - 1)) // MOE_BLOCK) * MOE_BLOCK
    NB = P // MOE_BLOCK
    blk_start = jnp.arange(NB, dtype=jnp.int32) * MOE_BLOCK
    blk_e = jnp.minimum(jnp.sum(pends[None, :] <= blk_start[:, None], axis=1), N_EXPERTS - 1).astype(jnp.int32)
    nb_used = (pends[-1] // MOE_BLOCK).astype(jnp.int32).reshape(1)
    dest_t = _expert_dest(eidx_t, rank_t, pstarts)

    tok = jnp.broadcast_to(jnp.arange(T, dtype=jnp.int32)[None, :], (TOP_K, T))
    buf_tok = jnp.zeros((P,), jnp.int32).at[dest_t.reshape(-1)].set(
        tok.reshape(-1), unique_indices=True, mode="promise_in_bounds")
    xs = x1[buf_tok]
    yb = _expert_ffn(blk_e, nb_used, xs, wg, wu, wd)
    yg = yb[dest_t.reshape(-1)]
    return yg, gw_t.T


def kernel(x, p, positions, fox_w_in, fox_b_f, fox_w_out, nsa_w_in, nsa_b_gate, nsa_pe_k, nsa_pe_v,
           nsa_cmp_k_w1, nsa_cmp_k_w2, nsa_cmp_v_w1, nsa_cmp_v_w2, nsa_w_out, moba_w_in, moba_w_out,
           ln1_g, ln1_b, router_w, router_b, exp_w_gate, exp_w_up, exp_w_down,
           sh_w_gate, sh_w_up, sh_w_down, ln2_g, ln2_b, ple_w_gate, ple_w_proj):
    B, S, D = x.shape
    T = B * S
    depth = p.shape[0]
    cosf, sinf = _rope_tables(positions)
    xt = x.reshape(T, D)
    for i in range(depth):
        kind, j = i % N_MIXERS, i // N_MIXERS
        if kind == 0:
            o_list = _fox_mixer(xt, B, S, fox_w_in[j], fox_b_f[j])
            w_out = fox_w_out[j]
        elif kind == 1:
            o_list = _nsa_mixer(xt, B, S, cosf, sinf, nsa_w_in[j], nsa_b_gate[j], nsa_pe_k[j], nsa_pe_v[j],
                                nsa_cmp_k_w1[j], nsa_cmp_k_w2[j], nsa_cmp_v_w1[j], nsa_cmp_v_w2[j])
            w_out = nsa_w_out[j]
        else:
            o_list = _moba_mixer(xt, B, S, cosf, sinf, moba_w_in[j])
            w_out = moba_w_out[j]
        x1, x1b = _outproj_ln(o_list, w_out.astype(BF16), xt, ln1_g[i], ln1_b[i])
        yg, gw = _moe_dispatch(x1, x1b, router_w[i], router_b[i], exp_w_gate[i], exp_w_up[i], exp_w_down[i])
        xt = _post_moe(x1, yg, gw, p[i].reshape(T, -1), sh_w_gate[i].astype(BF16), sh_w_up[i].astype(BF16),
                       sh_w_down[i].astype(BF16), ln2_g[i], ln2_b[i],
                       ple_w_gate[i].astype(BF16), ple_w_proj[i].astype(BF16))
    return xt.reshape(B, S, D)
```

```python
import functools

import jax
import jax.numpy as jnp
import numpy as np
from jax import lax
from jax.experimental import pallas as pl
from jax.experimental.pallas import tpu as pltpu
from jax.experimental.pallas import tpu_sc as plsc

D_MODEL = 1024
DEPTH = 4
N_HEADS = 16
HEAD_DIM = 64
ATTN_SCALE = HEAD_DIM ** -0.5
LOG2E = 1.4426950408889634
Q_SCALE = ATTN_SCALE * LOG2E
ROPE_THETA = 10000.0
N_MIXERS = 3

NSA_KV_GROUPS = 4
NSA_HEADS_PER_GROUP = N_HEADS // NSA_KV_GROUPS
NSA_CMP_LEN = 32
NSA_CMP_STRIDE = 16
NSA_SEL_LEN = 64
NSA_SEL_TOPN = 16
NSA_WINDOW = 512
NSA_FORCE_BONUS = 1e4

MOBA_BLOCK = 256
MOBA_TOPK = 3

N_EXPERTS = 64
EXPERT_DIM = 256
TOP_K = 8
N_GROUPS = 8
TOPK_GROUPS = 4
ROUTED_SCALE = 2.5
MOE_BLOCK = 256

DN_ALPHA = (2 * DEPTH) ** 0.25
LN_EPS = 1e-5
NEG = -1e30

SC_CORES = 2
SC_SUBCORES = 16
SC_LANES = 16
LANES = 128
SUBLANES = 8
ATTN_TILE = 512
KV_TILE = 1024
ROW_TILE = 512
VMEM_LIMIT = 48 * 1024 * 1024
C_SPLIT = 3
ONES_ROWS = 16

F32 = jnp.float32
BF16 = jnp.bfloat16
HIGHEST = lax.Precision.HIGHEST


def _cparams(sem):
    return pltpu.CompilerParams(dimension_semantics=sem, vmem_limit_bytes=VMEM_LIMIT)


def _log2(n):
    assert n & (n - 1) == 0
    return n.bit_length() - 1


def _nt_dot(a, b, precision=None):
    return lax.dot_general(a, b, (((1,), (1,)), ((), ())), preferred_element_type=F32,
                           precision=precision)


def _topk_rows(work, n):
    rows = lax.broadcasted_iota(jnp.int32, work.shape, 0).astype(F32)
    chosen = jnp.zeros(work.shape, F32)
    for _ in range(n):
        m = jnp.max(work, axis=0, keepdims=True)
        idx = jnp.min(jnp.where(work == m, rows, float(work.shape[0])), axis=0, keepdims=True)
        pick = rows == idx
        chosen = jnp.where(pick, 1.0, chosen)
        work = jnp.where(pick, -jnp.inf, work)
    return chosen


def _proj_kernel(*refs, mode, precision, scaled):
    refs = list(refs)
    x_ref, w_ref = refs[:2]
    del refs[:2]
    scale_ref = refs.pop(0) if scaled else None
    if mode == "none":
        (o_ref,) = refs
    elif mode == "rope":
        cos_ref, sin_ref, r_ref = refs
    else:
        cos_ref, sin_ref, o_ref, r_ref = refs
    x = x_ref[...]
    w = w_ref[...]
    if x.dtype != w.dtype:
        x = x.astype(w.dtype)
    acc = jnp.dot(x, w, preferred_element_type=F32, precision=precision)
    if scaled:
        acc = acc * scale_ref[...]
    if mode in ("none", "both"):
        o_ref[...] = acc.astype(o_ref.dtype)
    if mode in ("rope", "both"):
        tn = acc.shape[1]
        rep = tn // LANES
        cosf = jnp.concatenate([cos_ref[...]] * rep, axis=1)
        sinf = jnp.concatenate([sin_ref[...]] * rep, axis=1)
        lane = lax.broadcasted_iota(jnp.int32, acc.shape, 1)
        first_half = (lane & (HEAD_DIM // 2)) == 0
        swapped = jnp.where(first_half, pltpu.roll(acc, tn - HEAD_DIM // 2, 1),
                            pltpu.roll(acc, HEAD_DIM // 2, 1))
        r_ref[...] = (acc * cosf + swapped * sinf).astype(r_ref.dtype)


def _proj(x, w, *, mode="none", cos=None, sin=None, out_dtype=BF16, tn=512, precision=None, col_scale=None):
    T, K = x.shape
    N = w.shape[1]
    tm = min(ROW_TILE, T)
    tn = min(tn, N)
    assert T % tm == 0 and N % tn == 0
    grid = (T // tm, N // tn)
    in_specs = [pl.BlockSpec((tm, K), lambda i, j: (i, 0)),
                pl.BlockSpec((K, tn), lambda i, j: (0, j))]
    args = [x, w]
    if col_scale is not None:
        in_specs.append(pl.BlockSpec((1, tn), lambda i, j: (0, j)))
        args.append(col_scale.reshape(1, N).astype(F32))
    if mode != "none":
        in_specs += [pl.BlockSpec((tm, LANES), lambda i, j: (i, 0))] * 2
        args += [cos, sin]
    o_spec = pl.BlockSpec((tm, tn), lambda i, j: (i, j))
    o_shape = jax.ShapeDtypeStruct((T, N), out_dtype)
    if mode == "both":
        out_specs, out_shape = [o_spec, o_spec], [o_shape, o_shape]
    else:
        out_specs, out_shape = o_spec, o_shape
    return pl.pallas_call(
        functools.partial(_proj_kernel, mode=mode, precision=precision, scaled=col_scale is not None),
        grid=grid, in_specs=in_specs, out_specs=out_specs, out_shape=out_shape,
        compiler_params=_cparams(("parallel", "parallel")), name=f"proj_{mode}")(*args)


def _proj_t_kernel(wt_ref, x_ref, o_ref, *, precision):
    wt = wt_ref[...]
    x = x_ref[...]
    if x.dtype != wt.dtype:
        x = x.astype(wt.dtype)
    o_ref[...] = _nt_dot(wt, x, precision).astype(o_ref.dtype)


def _proj_t(wt, x, *, out_dtype=BF16, precision=None):
    N, K = wt.shape
    T = x.shape[0]
    tm = min(ROW_TILE, T)
    tn = min(512, N)
    assert T % tm == 0 and N % tn == 0
    return pl.pallas_call(
        functools.partial(_proj_t_kernel, precision=precision),
        grid=(T // tm, N // tn),
        in_specs=[pl.BlockSpec((tn, K), lambda i, j: (j, 0)), pl.BlockSpec((tm, K), lambda i, j: (i, 0))],
        out_specs=pl.BlockSpec((tn, tm), lambda i, j: (j, i)),
        out_shape=jax.ShapeDtypeStruct((N, T), out_dtype),
        compiler_params=_cparams(("parallel", "parallel")), name="proj_t")(wt, x)


def _pad_cols(w, n):
    return jnp.pad(w, ((0, 0), (0, n - w.shape[1])))


def _layer_norm(z, g, b):
    mu = jnp.mean(z, axis=-1, keepdims=True)
    zc = z - mu
    var = jnp.mean(zc * zc, axis=-1, keepdims=True)
    return zc * lax.rsqrt(var + LN_EPS) * g + b


def _outproj_ln_kernel(*refs, n_o):
    o_refs = refs[:n_o]
    w_ref, x_ref, g_ref, b_ref, y_ref, yb_ref = refs[n_o:]
    if n_o == 1:
        o = o_refs[0][...]
    else:
        o = o_refs[0][...].astype(F32)
        for r in o_refs[1:]:
            o = o + r[...].astype(F32)
        o = o.astype(BF16)
    mix = jnp.dot(o, w_ref[...], preferred_element_type=F32)
    z = DN_ALPHA * x_ref[...] + mix
    y = _layer_norm(z, g_ref[...], b_ref[...])
    y_ref[...] = y
    yb_ref[...] = y.astype(BF16)


def _outproj_ln(o_list, w, x, g, b):
    T, D = x.shape
    tm = min(ROW_TILE, T)
    row = pl.BlockSpec((tm, D), lambda i: (i, 0))
    vec = pl.BlockSpec((1, D), lambda i: (0, 0))
    return pl.pallas_call(
        functools.partial(_outproj_ln_kernel, n_o=len(o_list)),
        grid=(T // tm,),
        in_specs=[row] * len(o_list) + [pl.BlockSpec((D, D), lambda i: (0, 0)), row, vec, vec],
        out_specs=[row, row],
        out_shape=[jax.ShapeDtypeStruct((T, D), F32), jax.ShapeDtypeStruct((T, D), BF16)],
        compiler_params=_cparams(("parallel",)), name="outproj_ln")(
            *o_list, w, x, g.reshape(1, D), b.reshape(1, D))


def _flash_init(m_ref, acc_ref):
    m_ref[...] = jnp.full(m_ref.shape, NEG, F32)
    acc_ref[...] = jnp.zeros(acc_ref.shape, F32)


def _flash_step(sT, lhs, m_ref, acc_ref, h):
    m_prev = m_ref[h]
    m_new = jnp.maximum(m_prev, jnp.max(sT, axis=0, keepdims=True))
    p = jnp.exp2(sT - m_new)
    alpha = jnp.exp2(m_prev - m_new)
    acc_ref[h] = alpha * acc_ref[h] + jnp.dot(lhs, p.astype(BF16), preferred_element_type=F32)
    m_ref[h] = m_new


def _causal_t(tk, tq, shift=0, strict_lower=False):
    key = lax.broadcasted_iota(jnp.int32, (tk, tq), 0)
    qry = lax.broadcasted_iota(jnp.int32, (tk, tq), 1)
    return (key > qry) if strict_lower else (key <= qry + shift)


def _causal_sweep(i, tq, tk, step):
    r = tk // tq
    assert tk == r * tq
    n_full = i >> _log2(r)

    def body(j, carry):
        step(pl.multiple_of(j * tk, tk), tk, None)
        return carry

    lax.fori_loop(0, n_full, body, 0)
    if r == 1:
        step(pl.multiple_of(i * tq, tq), tq, _causal_t(tq, tq))
    else:
        for rem in range(r):
            @pl.when(i - n_full * r == rem)
            def _(rem=rem):
                size = (rem + 1) * tq
                step(pl.multiple_of(n_full * tk, tk), size, _causal_t(size, tq, shift=rem * tq))


def _staggered(units, scores, update):
    s = [scores(units[0])]
    for n, u in enumerate(units):
        if n + 1 < len(units):
            s.append(scores(units[n + 1]))
        update(u, s[n])


def _head_of_pair(q_pair, which):
    lane = lax.broadcasted_iota(jnp.int32, q_pair.shape, 1)
    keep = (lane < HEAD_DIM) if which == 0 else (lane >= HEAD_DIM)
    return jnp.where(keep, q_pair, jnp.zeros_like(q_pair))


def _pair_rows(a_top, a_bot):
    row = lax.broadcasted_iota(jnp.int32, a_top.shape, 0)
    return jnp.where(row < HEAD_DIM, a_top, a_bot)


def _pair_lhs(vt, which):
    row = lax.broadcasted_iota(jnp.int32, vt.shape, 0)
    keep = (row < HEAD_DIM) if which == 0 else (row >= HEAD_DIM)
    return jnp.where(keep, vt, jnp.ones_like(vt))


def _pair_finish(acc_ref):
    a0, a1 = acc_ref[0], acc_ref[1]
    return _pair_rows(a0 / a0[HEAD_DIM:HEAD_DIM + 1, :], a1 / a1[0:1, :])


def _fox_gate_kernel(fl_ref, bf_ref, tri_ref, ex_ref, cp_ref, carry_ref):
    @pl.when(pl.program_id(1) == 0)
    def _():
        carry_ref[...] = jnp.zeros(carry_ref.shape, F32)

    z = fl_ref[...] + bf_ref[...]
    log_f = jnp.minimum(z, 0.0) - jnp.log1p(jnp.exp(-jnp.abs(z)))
    c = jnp.dot(tri_ref[...], log_f, preferred_element_type=F32, precision=HIGHEST) + carry_ref[...]
    carry_ref[...] = c[-1:, :]
    out = jnp.zeros(c.shape, F32)
    rem = c * LOG2E
    for piece in range(C_SPLIT):
        part = rem.astype(BF16)
        rem = rem - part.astype(F32)
        out = out + jnp.dot(part, ex_ref[piece], preferred_element_type=F32)
    cp_ref[...] = out.astype(BF16)


def _fox_gate(fl, b_f, B, S):
    T = fl.shape[0]
    tm = min(ROW_TILE, S)
    ns = S // tm
    tri = jnp.asarray(np.tril(np.ones((tm, tm), np.float32)))
    ex = np.zeros((C_SPLIT, LANES, LANES), np.float32)
    for h in range(N_HEADS):
        for j in range(C_SPLIT):
            ex[j, h, C_SPLIT * h + j] = 1.0
    bf = _pad_cols(b_f.reshape(1, -1), LANES)
    return pl.pallas_call(
        _fox_gate_kernel, grid=(B, ns),
        in_specs=[pl.BlockSpec((tm, LANES), lambda b, s: (b * ns + s, 0)),
                  pl.BlockSpec((1, LANES), lambda b, s: (0, 0)),
                  pl.BlockSpec((tm, tm), lambda b, s: (0, 0)),
                  pl.BlockSpec((C_SPLIT, LANES, LANES), lambda b, s: (0, 0, 0))],
        out_specs=pl.BlockSpec((tm, LANES), lambda b, s: (b * ns + s, 0)),
        out_shape=jax.ShapeDtypeStruct((T, LANES), BF16),
        scratch_shapes=[pltpu.VMEM((1, LANES), F32)],
        compiler_params=_cparams(("parallel", "arbitrary")), name="fox_gate")(
            fl, bf, tri, jnp.asarray(ex, BF16))


def _fox_kernel(q_ref, k_ref, cp_ref, vt_ref, o_ref, m_ref, acc_ref, *, tq, tk):
    hp = pl.program_id(1)
    i = pl.program_id(2)
    _flash_init(m_ref, acc_ref)
    q = q_ref[...]
    lane = lax.broadcasted_iota(jnp.int32, q.shape, 1)
    qa = []
    for h in range(2):
        lo = C_SPLIT * (2 * hp + h)
        minus_one = jnp.where((lane >= lo) & (lane < lo + C_SPLIT), -1.0, 0.0).astype(q.dtype)
        qa.append(jnp.concatenate([_head_of_pair(q, h), minus_one], axis=1))

    def step(off, size, mask):
        k = jnp.concatenate([k_ref[pl.ds(off, size), :], cp_ref[pl.ds(off, size), :]], axis=1)
        vt = vt_ref[:, pl.ds(off, size)]

        lhs = [_pair_lhs(vt, h) for h in range(2)]

        def scores(h):
            sT = _nt_dot(k, qa[h])
            return sT if mask is None else jnp.where(mask, sT, NEG)

        _staggered([0, 1], scores, lambda h, sT: _flash_step(sT, lhs[h], m_ref, acc_ref, h))

    _causal_sweep(i, tq, tk, step)
    o_ref[...] = _pair_finish(acc_ref).T.astype(o_ref.dtype)


def _fox_attention(qk, cp, vt, B, S):
    T = qk.shape[0]
    tq = min(ATTN_TILE, S)
    tk = min(KV_TILE, S)
    nq = S // tq
    npair = N_HEADS // 2
    ncol = D_MODEL // LANES
    return pl.pallas_call(
        functools.partial(_fox_kernel, tq=tq, tk=tk),
        grid=(B, npair, nq),
        in_specs=[pl.BlockSpec((tq, LANES), lambda b, hp, i: (b * nq + i, hp)),
                  pl.BlockSpec((S, LANES), lambda b, hp, i: (b, ncol + hp)),
                  pl.BlockSpec((S, LANES), lambda b, hp, i: (b, 0)),
                  pl.BlockSpec((LANES, S), lambda b, hp, i: (hp, b))],
        out_specs=pl.BlockSpec((tq, LANES), lambda b, hp, i: (b * nq + i, hp)),
        out_shape=jax.ShapeDtypeStruct((T, D_MODEL), BF16),
        scratch_shapes=[pltpu.VMEM((2, 1, tq), F32), pltpu.VMEM((2, LANES, tq), F32)],
        compiler_params=_cparams(("parallel", "parallel", "arbitrary")), name="fox_attn")(
            qk, qk, cp, vt)


def _gate_expand(gl_ref, bg_ref, ex_ref):
    sig = jax.nn.sigmoid(gl_ref[...] + bg_ref[...])
    hi = sig.astype(BF16)
    lo = (sig - hi.astype(F32)).astype(BF16)
    ex = ex_ref[...]
    return jnp.dot(hi, ex, preferred_element_type=F32) + jnp.dot(lo, ex, preferred_element_type=F32)


def _nsa_compress_kernel(f_ref, pe_ref, w1_ref, w2_ref, o_ref):
    blk = (f_ref[...].astype(F32) + pe_ref[...]).astype(BF16)
    h = jax.nn.gelu(jnp.dot(blk, w1_ref[...], preferred_element_type=F32))
    o_ref[...] = jnp.dot(h.astype(BF16), w2_ref[...], preferred_element_type=F32)


def _nsa_compress(flat, pe_flat, w1, w2p):
    M, K = flat.shape
    tm = min(ROW_TILE, M)
    Hc = w1.shape[1]
    return pl.pallas_call(
        _nsa_compress_kernel, grid=(M // tm,),
        in_specs=[pl.BlockSpec((tm, K), lambda i: (i, 0)), pl.BlockSpec((1, K), lambda i: (0, 0)),
                  pl.BlockSpec((K, Hc), lambda i: (0, 0)), pl.BlockSpec((Hc, LANES), lambda i: (0, 0))],
        out_specs=pl.BlockSpec((tm, LANES), lambda i: (i, 0)),
        out_shape=jax.ShapeDtypeStruct((M, LANES), F32),
        compiler_params=_cparams(("parallel",)), name="nsa_compress")(flat, pe_flat, w1, w2p)


def _nsa_cmp_kernel(q_ref, kk_ref, vt_ref, ovt_ref, gl_ref, bg_ref, ex_ref, o_ref, sn_ref, *, tq, ncp, nselp):
    i = pl.program_id(2)
    q = q_ref[...]
    kk = kk_ref[...]
    vt = vt_ref[...]
    t = i * tq + lax.broadcasted_iota(jnp.int32, (ncp, tq), 1)
    n = lax.broadcasted_iota(jnp.int32, (ncp, tq), 0)
    valid = n * NSA_CMP_STRIDE + (NSA_CMP_LEN - 1) <= t
    psum = jnp.zeros((ncp, tq), F32)
    outs = []
    for hg in range(NSA_HEADS_PER_GROUP):
        qh = _head_of_pair(q[:, LANES * (hg // 2):LANES * (hg // 2 + 1)], hg % 2)
        sT = jnp.where(valid, _nt_dot(kk, qh), NEG)
        m = jnp.max(sT, axis=0, keepdims=True)
        e = jnp.where(valid, jnp.exp2(sT - m), 0.0)
        p = e / jnp.maximum(jnp.sum(e, axis=0, keepdims=True), 1e-30)
        psum = psum + p
        outs.append(jnp.dot(vt, p.astype(BF16), preferred_element_type=F32))
    o = jnp.concatenate(outs, axis=0).T
    o_ref[...] = (o * _gate_expand(gl_ref, bg_ref, ex_ref)).astype(o_ref.dtype)
    hi = psum.astype(BF16)
    lo = (psum - hi.astype(F32)).astype(BF16)
    ovt = ovt_ref[...]
    imp = jnp.dot(ovt, hi, preferred_element_type=F32) + jnp.dot(ovt, lo, preferred_element_type=F32)
    jblk = lax.broadcasted_iota(jnp.int32, (nselp, tq), 0)
    cur = (i * tq + lax.broadcasted_iota(jnp.int32, (nselp, tq), 1)) >> _log2(NSA_SEL_LEN)
    forced = (jblk == 0) | (jblk == cur) | (jblk == cur - 1)
    pri = jnp.where(forced, imp + NSA_FORCE_BONUS, imp)
    past = jblk <= cur
    pri = jnp.where(past, pri, NEG)
    chosen = _topk_rows(pri, NSA_SEL_TOPN)
    sn = jnp.where((chosen > 0.0) & past, 0.0, NEG)
    sn_ref[...] = sn.T.astype(sn_ref.dtype)


def _nsa_cmp_branch(q_c, kk_c, vt_c, overlap_t, gl, bg, ex, B, S):
    T = q_c.shape[0]
    G = NSA_KV_GROUPS
    tq = min(ATTN_TILE, S)
    nq = S // tq
    ncp = kk_c.shape[2]
    nselp = overlap_t.shape[0]
    return pl.pallas_call(
        functools.partial(_nsa_cmp_kernel, tq=tq, ncp=ncp, nselp=nselp),
        grid=(B, G, nq),
        in_specs=[pl.BlockSpec((tq, 2 * LANES), lambda b, g, i: (b * nq + i, g)),
                  pl.BlockSpec((None, None, ncp, LANES), lambda b, g, i: (b, g, 0, 0)),
                  pl.BlockSpec((None, None, HEAD_DIM, ncp), lambda b, g, i: (b, g, 0, 0)),
                  pl.BlockSpec((nselp, ncp), lambda b, g, i: (0, 0)),
                  pl.BlockSpec((tq, LANES), lambda b, g, i: (b * nq + i, 0)),
                  pl.BlockSpec((1, LANES), lambda b, g, i: (0, 0)),
                  pl.BlockSpec((None, LANES, 2 * LANES), lambda b, g, i: (g, 0, 0))],
        out_specs=[pl.BlockSpec((tq, 2 * LANES), lambda b, g, i: (b * nq + i, g)),
                   pl.BlockSpec((None, None, tq, nselp), lambda b, g, i: (b, g, i, 0))],
        out_shape=[jax.ShapeDtypeStruct((T, D_MODEL), BF16),
                   jax.ShapeDtypeStruct((B, G, S, nselp), BF16)],
        compiler_params=_cparams(("parallel", "parallel", "parallel")), name="nsa_cmp")(
            q_c, kk_c, vt_c, overlap_t, gl, bg, ex)


def _nsa_kv_kernel(*refs, mode, tq, tk):
    if mode == "sel":
        q_ref, kk_ref, vt_ref, oh_ref, sn_ref, gl_ref, bg_ref, ex_ref, o_ref, m_ref, acc_ref = refs
    else:
        q_ref, kk_ref, vt_ref, gl_ref, bg_ref, ex_ref, o_ref, m_ref, acc_ref = refs
    i = pl.program_id(2)
    _flash_init(m_ref, acc_ref)
    q = q_ref[...]
    qh = []
    for hg in range(NSA_HEADS_PER_GROUP):
        qq = _head_of_pair(q[:, LANES * (hg // 2):LANES * (hg // 2 + 1)], hg % 2)
        if mode == "sel":
            qq = jnp.concatenate([qq, sn_ref[...]], axis=1)
        qh.append(qq)

    def step(off, size, mask):
        k = kk_ref[pl.ds(off, size), :]
        vt = vt_ref[:, pl.ds(off, size)]
        lhs = jnp.concatenate([vt, jnp.ones((ONES_ROWS, size), vt.dtype)], axis=0)
        if mode == "sel":
            k = jnp.concatenate([k, oh_ref[pl.ds(off, size), :]], axis=1)

        def scores(hg):
            sT = _nt_dot(k, qh[hg])
            return sT if mask is None else jnp.where(mask, sT, NEG)

        _staggered(list(range(NSA_HEADS_PER_GROUP)), scores,
                   lambda hg, sT: _flash_step(sT, lhs, m_ref, acc_ref, hg))

    if mode == "sel":
        _causal_sweep(i, tq, tk, step)
    else:
        @pl.when(i > 0)
        def _():
            step(pl.multiple_of((i - 1) * tq, tq), tq, _causal_t(tq, tq, strict_lower=True))
        step(pl.multiple_of(i * tq, tq), tq, _causal_t(tq, tq))
    outs = []
    for hg in range(NSA_HEADS_PER_GROUP):
        a = acc_ref[hg]
        outs.append(a[:HEAD_DIM] / a[HEAD_DIM:HEAD_DIM + 1, :])
    o_ref[...] = (jnp.concatenate(outs, axis=0).T * _gate_expand(gl_ref, bg_ref, ex_ref)).astype(o_ref.dtype)


def _nsa_kv_branch(mode, q_r, kk, vt, gl, bg, ex, B, S, onehot=None, selneg=None):
    T = q_r.shape[0]
    G = NSA_KV_GROUPS
    tq = min(ATTN_TILE, S)
    tk = min(KV_TILE, S)
    assert NSA_WINDOW == tq or mode == "sel"
    nq = S // tq
    in_specs = [pl.BlockSpec((tq, 2 * LANES), lambda b, g, i: (b * nq + i, g)),
                pl.BlockSpec((None, None, S, LANES), lambda b, g, i: (b, g, 0, 0)),
                pl.BlockSpec((HEAD_DIM, S), lambda b, g, i: (g, b))]
    args = [q_r, kk, vt]
    if mode == "sel":
        in_specs += [pl.BlockSpec((S, LANES), lambda b, g, i: (0, 0)),
                     pl.BlockSpec((None, None, tq, LANES), lambda b, g, i: (b, g, i, 0))]
        args += [onehot, selneg]
    in_specs += [pl.BlockSpec((tq, LANES), lambda b, g, i: (b * nq + i, 0)),
                 pl.BlockSpec((1, LANES), lambda b, g, i: (0, 0)),
                 pl.BlockSpec((None, LANES, 2 * LANES), lambda b, g, i: (g, 0, 0))]
    args += [gl, bg, ex]
    nh = NSA_HEADS_PER_GROUP
    return pl.pallas_call(
        functools.partial(_nsa_kv_kernel, mode=mode, tq=tq, tk=tk),
        grid=(B, G, nq), in_specs=in_specs,
        out_specs=pl.BlockSpec((tq, 2 * LANES), lambda b, g, i: (b * nq + i, g)),
        out_shape=jax.ShapeDtypeStruct((T, D_MODEL), BF16),
        scratch_shapes=[pltpu.VMEM((nh, 1, tq), F32), pltpu.VMEM((nh, HEAD_DIM + ONES_ROWS, tq), F32)],
        compiler_params=_cparams(("parallel", "parallel", "arbitrary")), name=f"nsa_{mode}")(*args)


def _kmean_kernel(k_ref, o_ref, *, nblk):
    k = k_ref[...].astype(F32)
    o_ref[...] = jnp.mean(k.reshape(nblk, MOBA_BLOCK, k.shape[1]), axis=1)


def _moba_kmean(k_rot):
    T, D = k_rot.shape
    nblk = 8
    rows = nblk * MOBA_BLOCK
    assert T % rows == 0
    return pl.pallas_call(
        functools.partial(_kmean_kernel, nblk=nblk), grid=(T // rows,),
        in_specs=[pl.BlockSpec((rows, D), lambda i: (i, 0))],
        out_specs=pl.BlockSpec((nblk, D), lambda i: (i, 0)),
        out_shape=jax.ShapeDtypeStruct((T // MOBA_BLOCK, D), F32),
        compiler_params=_cparams(("parallel",)), name="moba_kmean")(k_rot)


def _moba_select_kernel(q_ref, r_ref, sn_ref, *, tq, nblk):
    i = pl.program_id(2)
    gsT = _nt_dot(r_ref[...], q_ref[...].astype(F32), HIGHEST)
    jblk = lax.broadcasted_iota(jnp.int32, (nblk, tq), 0)
    cb = (i * tq + lax.broadcasted_iota(jnp.int32, (nblk, tq), 1)) >> _log2(MOBA_BLOCK)
    past = jblk < cb
    parts = []
    for h in range(2):
        gs = jnp.where(past, gsT[h * nblk:(h + 1) * nblk, :], NEG)
        chosen = _topk_rows(gs, min(MOBA_TOPK, nblk))
        parts.append(jnp.where(((chosen > 0.0) & past) | (jblk == cb), 0.0, NEG))
    parts.append(jnp.zeros((LANES - 2 * nblk, tq), F32))
    sn_ref[...] = jnp.concatenate(parts, axis=0).T.astype(sn_ref.dtype)


def _moba_select(q_rot, rmat_t, B, S):
    tq = min(ATTN_TILE, S)
    nq = S // tq
    npair = N_HEADS // 2
    nblk = S // MOBA_BLOCK
    return pl.pallas_call(
        functools.partial(_moba_select_kernel, tq=tq, nblk=nblk), grid=(B, npair, nq),
        in_specs=[pl.BlockSpec((tq, LANES), lambda b, hp, i: (b * nq + i, hp)),
                  pl.BlockSpec((None, None, LANES, LANES), lambda b, hp, i: (b, hp, 0, 0))],
        out_specs=pl.BlockSpec((None, None, tq, LANES), lambda b, hp, i: (b, hp, i, 0)),
        out_shape=jax.ShapeDtypeStruct((B, npair, S, LANES), BF16),
        compiler_params=_cparams(("parallel", "parallel", "parallel")), name="moba_select")(q_rot, rmat_t)


def _moba_kernel(q_ref, k_ref, vt_ref, oh_ref, sn_ref, o_ref, m_ref, acc_ref, *, tq, tk, nblk):
    i = pl.program_id(2)
    _flash_init(m_ref, acc_ref)
    q = q_ref[...]
    sn = sn_ref[...]
    lane = lax.broadcasted_iota(jnp.int32, sn.shape, 1)
    qa = []
    for h in range(2):
        mine = (lane >= h * nblk) & (lane < (h + 1) * nblk)
        qa.append(jnp.concatenate([_head_of_pair(q, h), jnp.where(mine, sn, jnp.zeros_like(sn))], axis=1))

    def step(off, size, mask):
        k = jnp.concatenate([k_ref[pl.ds(off, size), :], oh_ref[pl.ds(off, size), :]], axis=1)
        vt = vt_ref[:, pl.ds(off, size)]

        lhs = [_pair_lhs(vt, h) for h in range(2)]

        def scores(h):
            sT = _nt_dot(k, qa[h])
            return sT if mask is None else jnp.where(mask, sT, NEG)

        _staggered([0, 1], scores, lambda h, sT: _flash_step(sT, lhs[h], m_ref, acc_ref, h))

    _causal_sweep(i, tq, tk, step)
    o_ref[...] = _pair_finish(acc_ref).T.astype(o_ref.dtype)


def _moba_attention(qk_rot, vt, onehot2, selneg, B, S):
    T = qk_rot.shape[0]
    tq = min(ATTN_TILE, S)
    tk = min(KV_TILE, S)
    nq = S // tq
    npair = N_HEADS // 2
    ncol = D_MODEL // LANES
    nblk = S // MOBA_BLOCK
    return pl.pallas_call(
        functools.partial(_moba_kernel, tq=tq, tk=tk, nblk=nblk),
        grid=(B, npair, nq),
        in_specs=[pl.BlockSpec((tq, LANES), lambda b, hp, i: (b * nq + i, hp)),
                  pl.BlockSpec((S, LANES), lambda b, hp, i: (b, ncol + hp)),
                  pl.BlockSpec((LANES, S), lambda b, hp, i: (hp, b)),
                  pl.BlockSpec((S, LANES), lambda b, hp, i: (0, 0)),
                  pl.BlockSpec((None, None, tq, LANES), lambda b, hp, i: (b, hp, i, 0))],
        out_specs=pl.BlockSpec((tq, LANES), lambda b, hp, i: (b * nq + i, hp)),
        out_shape=jax.ShapeDtypeStruct((T, D_MODEL), BF16),
        scratch_shapes=[pltpu.VMEM((2, 1, tq), F32), pltpu.VMEM((2, LANES, tq), F32)],
        compiler_params=_cparams(("parallel", "parallel", "arbitrary")), name="moba_attn")(
            qk_rot, qk_rot, vt, onehot2, selneg)


def _router_kernel(wt_ref, x_ref, rb_ref, e_ref, g_ref):
    tm = x_ref.shape[0]
    gsz = N_EXPERTS // N_GROUPS
    scores = jax.nn.sigmoid(_nt_dot(wt_ref[...], x_ref[...], HIGHEST))
    biased = scores + rb_ref[...]
    member = lax.broadcasted_iota(jnp.int32, (gsz, tm), 0).astype(F32)
    gscore = []
    for g in range(N_GROUPS):
        v = biased[g * gsz:(g + 1) * gsz, :]
        m1 = jnp.max(v, axis=0, keepdims=True)
        i1 = jnp.min(jnp.where(v == m1, member, float(gsz)), axis=0, keepdims=True)
        m2 = jnp.max(jnp.where(member == i1, -jnp.inf, v), axis=0, keepdims=True)
        gscore.append(m1 + m2)
    gsel = _topk_rows(jnp.concatenate(gscore, axis=0), TOPK_GROUPS)
    emask = jnp.concatenate([jnp.broadcast_to(gsel[g:g + 1, :], (gsz, tm)) for g in range(N_GROUPS)], axis=0)
    work = jnp.where(emask > 0.0, biased, NEG)
    erow = lax.broadcasted_iota(jnp.int32, (N_EXPERTS, tm), 0).astype(F32)
    idxs, vals = [], []
    for _ in range(TOP_K):
        m = jnp.max(work, axis=0, keepdims=True)
        idx = jnp.min(jnp.where(work == m, erow, float(N_EXPERTS)), axis=0, keepdims=True)
        pick = erow == idx
        idxs.append(idx)
        vals.append(jnp.sum(jnp.where(pick, scores, 0.0), axis=0, keepdims=True))
        work = jnp.where(pick, -jnp.inf, work)
    gw = jnp.concatenate(vals, axis=0)
    e_ref[...] = jnp.concatenate(idxs, axis=0).astype(jnp.int32)
    g_ref[...] = gw / jnp.sum(gw, axis=0, keepdims=True) * ROUTED_SCALE


def _router(x1, router_w, router_b):
    T, D = x1.shape
    tm = min(ROW_TILE, T)
    return pl.pallas_call(
        _router_kernel, grid=(T // tm,),
        in_specs=[pl.BlockSpec((N_EXPERTS, D), lambda i: (0, 0)), pl.BlockSpec((tm, D), lambda i: (i, 0)),
                  pl.BlockSpec((N_EXPERTS, 1), lambda i: (0, 0))],
        out_specs=[pl.BlockSpec((TOP_K, tm), lambda i: (0, i))] * 2,
        out_shape=[jax.ShapeDtypeStruct((TOP_K, T), jnp.int32), jax.ShapeDtypeStruct((TOP_K, T), F32)],
        compiler_params=_cparams(("parallel",)), name="moe_router")(
            router_w.T, x1, router_b.reshape(N_EXPERTS, 1))


def _rank_kernel(e_ref, tri_ref, rank_ref, cnt_ref, carry_ref):
    @pl.when(pl.program_id(0) == 0)
    def _():
        carry_ref[...] = jnp.zeros(carry_ref.shape, F32)

    tm = e_ref.shape[1]
    erow = lax.broadcasted_iota(jnp.int32, (N_EXPERTS, tm), 0)
    tri = tri_ref[...]
    base = carry_ref[...]
    ranks = []
    for k in range(TOP_K):
        oh = erow == e_ref[k:k + 1, :]
        ohb = jnp.where(oh, 1.0, 0.0).astype(BF16)
        incl = jnp.dot(ohb, tri, preferred_element_type=F32)
        ranks.append(jnp.sum(jnp.where(oh, base + incl - 1.0, 0.0), axis=0, keepdims=True))
        base = base + incl[:, tm - 1:tm]
    carry_ref[...] = base
    rank_ref[...] = jnp.concatenate(ranks, axis=0).astype(jnp.int32)
    cnt_ref[...] = jnp.broadcast_to(base, cnt_ref.shape)


def _expert_ranks(eidx_t):
    K, T = eidx_t.shape
    tm = min(ROW_TILE, T)
    tri = jnp.asarray(np.triu(np.ones((tm, tm), np.float32)), BF16)
    return pl.pallas_call(
        _rank_kernel, grid=(T // tm,),
        in_specs=[pl.BlockSpec((K, tm), lambda i: (0, i)), pl.BlockSpec((tm, tm), lambda i: (0, 0))],
        out_specs=[pl.BlockSpec((K, tm), lambda i: (0, i)), pl.BlockSpec((N_EXPERTS, LANES), lambda i: (0, 0))],
        out_shape=[jax.ShapeDtypeStruct((K, T), jnp.int32), jax.ShapeDtypeStruct((N_EXPERTS, LANES), F32)],
        scratch_shapes=[pltpu.VMEM((N_EXPERTS, 1), F32)],
        compiler_params=_cparams(("arbitrary",)), name="moe_rank")(eidx_t, tri)


def _dest_kernel(e_ref, rank_ref, ps_ref, d_ref):
    tm = e_ref.shape[1]
    erow = lax.broadcasted_iota(jnp.int32, (N_EXPERTS, tm), 0)
    ps = ps_ref[...]
    rows = []
    for k in range(TOP_K):
        oh = erow == e_ref[k:k + 1, :]
        rows.append(jnp.sum(jnp.where(oh, ps, 0.0), axis=0, keepdims=True))
    d_ref[...] = jnp.concatenate(rows, axis=0).astype(jnp.int32) + rank_ref[...]


def _expert_dest(eidx_t, rank_t, pstarts):
    K, T = eidx_t.shape
    tm = min(ROW_TILE, T)
    blk = pl.BlockSpec((K, tm), lambda i: (0, i))
    return pl.pallas_call(
        _dest_kernel, grid=(T // tm,),
        in_specs=[blk, blk, pl.BlockSpec((N_EXPERTS, 1), lambda i: (0, 0))],
        out_specs=blk, out_shape=jax.ShapeDtypeStruct((K, T), jnp.int32),
        compiler_params=_cparams(("parallel",)), name="moe_dest")(
            eidx_t, rank_t, pstarts.astype(F32).reshape(N_EXPERTS, 1))


def _expert_kernel(be_ref, nb_ref, x_ref, wg_ref, wu_ref, wd_ref, o_ref, wgb_ref, wub_ref, wdb_ref):
    b = pl.program_id(0)

    @pl.when((b == 0) | (be_ref[b] != be_ref[jnp.maximum(b - 1, 0)]))
    def _():
        wgb_ref[...] = wg_ref[...].astype(BF16)
        wub_ref[...] = wu_ref[...].astype(BF16)
        wdb_ref[...] = wd_ref[...].astype(BF16)

    @pl.when(b < nb_ref[0])
    def _():
        x = x_ref[...].astype(BF16)
        h = jax.nn.silu(jnp.dot(x, wgb_ref[...], preferred_element_type=F32)) * jnp.dot(
            x, wub_ref[...], preferred_element_type=F32)
        o_ref[...] = jnp.dot(h.astype(BF16), wdb_ref[...], preferred_element_type=F32).astype(o_ref.dtype)

    @pl.when(b >= nb_ref[0])
    def _():
        o_ref[...] = jnp.zeros(o_ref.shape, o_ref.dtype)


def _expert_ffn(blk_e, nb_used, xs, wg, wu, wd, layer):
    P, D = xs.shape
    NB = P // MOE_BLOCK
    E = EXPERT_DIM
    grid_spec = pltpu.PrefetchScalarGridSpec(
        num_scalar_prefetch=2, grid=(NB,),
        in_specs=[pl.BlockSpec((MOE_BLOCK, D), lambda b, be, nb: (b, 0)),
                  pl.BlockSpec((None, None, D, E), lambda b, be, nb: (layer, be[b], 0, 0)),
                  pl.BlockSpec((None, None, D, E), lambda b, be, nb: (layer, be[b], 0, 0)),
                  pl.BlockSpec((None, None, E, D), lambda b, be, nb: (layer, be[b], 0, 0))],
        out_specs=pl.BlockSpec((MOE_BLOCK, D), lambda b, be, nb: (b, 0)),
        scratch_shapes=[pltpu.VMEM((D, E), BF16), pltpu.VMEM((D, E), BF16), pltpu.VMEM((E, D), BF16)])
    return pl.pallas_call(
        _expert_kernel, grid_spec=grid_spec, out_shape=jax.ShapeDtypeStruct((P, D), BF16),
        compiler_params=_cparams(("arbitrary",)), name="moe_experts")(blk_e, nb_used, xs, wg, wu, wd)


def _post_moe_kernel(x_ref, *refs):
    y_refs = refs[:TOP_K]
    gw_ref, p_ref, sg_ref, su_ref, sd_ref, g_ref, b_ref, wg_ref, wp_ref, o_ref = refs[TOP_K:]
    x = x_ref[...]
    xb = x.astype(BF16)
    h = jax.nn.silu(jnp.dot(xb, sg_ref[...], preferred_element_type=F32)) * jnp.dot(
        xb, su_ref[...], preferred_element_type=F32)
    ffn = jnp.dot(h.astype(BF16), sd_ref[...], preferred_element_type=F32)
    gw = gw_ref[...]
    for k in range(TOP_K):
        ffn = ffn + gw[:, k:k + 1] * y_refs[k][...].astype(F32)
    z = DN_ALPHA * x + ffn
    x2 = _layer_norm(z, g_ref[...], b_ref[...])
    gate = jax.nn.sigmoid(jnp.dot(x2.astype(BF16), wg_ref[...], preferred_element_type=F32))
    proj = jnp.dot(p_ref[...].astype(BF16), wp_ref[...], preferred_element_type=F32)
    o_ref[...] = x2 + gate * proj


def _post_moe(x1, yg, gw, p, sg, su, sd, g, b, wgate, wproj):
    T, D = x1.shape
    tm = min(ROW_TILE // 2, T)
    nt = T // tm
    PD = p.shape[1]
    SD = sg.shape[1]
    row = pl.BlockSpec((tm, D), lambda i: (i, 0))
    vec = pl.BlockSpec((1, D), lambda i: (0, 0))
    full = lambda r, c: pl.BlockSpec((r, c), lambda i: (0, 0))
    return pl.pallas_call(
        _post_moe_kernel, grid=(T // tm,),
        in_specs=[row] + [pl.BlockSpec((tm, D), functools.partial(lambda i, k: (k * nt + i, 0), k=k))
                          for k in range(TOP_K)]
                 + [pl.BlockSpec((tm, TOP_K), lambda i: (i, 0)),
                    pl.BlockSpec((tm, PD), lambda i: (i, 0)), full(D, SD), full(D, SD),
                  full(SD, D), vec, vec, full(D, D), full(PD, D)],
        out_specs=row, out_shape=jax.ShapeDtypeStruct((T, D), F32),
        compiler_params=_cparams(("parallel",)), name="post_moe")(
            x1, *([yg] * TOP_K), gw, p, sg, su, sd, g.reshape(1, D), b.reshape(1, D), wgate, wproj)


def _rope_tables(positions):
    inv = 1.0 / (ROPE_THETA ** (jnp.arange(0, HEAD_DIM, 2, dtype=F32) / HEAD_DIM))
    ang = positions.astype(F32).reshape(-1)[:, None] * inv
    cos, sin = jnp.cos(ang), jnp.sin(ang)
    cosf = jnp.concatenate([cos] * (LANES // (HEAD_DIM // 2)), axis=1)
    sinf = jnp.concatenate([-sin, sin] * (LANES // HEAD_DIM), axis=1)
    return cosf, sinf


def _q_col_scale(n_q, n):
    return jnp.concatenate([jnp.full((n_q,), Q_SCALE, F32), jnp.ones((n - n_q,), F32)])


def _fox_mixer(x, B, S, w_in, b_f):
    D = D_MODEL
    wb = w_in[:, :3 * D].astype(BF16)
    qk = _proj(x, wb[:, :2 * D], col_scale=_q_col_scale(D, 2 * D))
    vt = _proj_t(wb[:, 2 * D:].T, x)
    fl = _proj(x, _pad_cols(w_in[:, 3 * D:], LANES), out_dtype=F32, tn=LANES, precision=HIGHEST)
    cp = _fox_gate(fl, b_f, B, S)
    return [_fox_attention(qk, cp, vt, B, S)]


def _nsa_mixer(x, B, S, cosf, sinf, w_in, b_gate, pe_k, pe_v, ck_w1, ck_w2, cv_w1, cv_w2):
    D, G, HD = D_MODEL, NSA_KV_GROUPS, HEAD_DIM
    HG = NSA_HEADS_PER_GROUP
    kvw = G * HD
    wb = w_in[:, :D + 6 * kvw].astype(BF16)
    q_c, q_r = _proj(x, wb[:, :D], mode="both", cos=cosf, sin=sinf, col_scale=_q_col_scale(D, D))
    w_rot = jnp.concatenate([wb[:, D + 2 * kvw:D + 3 * kvw], wb[:, D + 4 * kvw:D + 5 * kvw]], axis=1)
    k_rot = _proj(x, w_rot, mode="rope", cos=cosf, sin=sinf)
    kvc = _proj(x, wb[:, D:D + 2 * kvw])
    w_v = jnp.concatenate([wb[:, D + 3 * kvw:D + 4 * kvw], wb[:, D + 5 * kvw:D + 6 * kvw]], axis=1)
    vt = _proj_t(w_v.T, x)
    gl = _proj(x, _pad_cols(w_in[:, D + 6 * kvw:], LANES), out_dtype=F32, tn=LANES, precision=HIGHEST)
    bg = _pad_cols(b_gate.reshape(1, -1), LANES)

    def grouped(t2d):
        t = t2d.reshape(B, S, G, HD).transpose(0, 2, 1, 3)
        return jnp.concatenate([t, t], axis=-1)

    n_chunks = S // NSA_CMP_STRIDE
    n_cmp = n_chunks - NSA_CMP_LEN // NSA_CMP_STRIDE + 1
    ncp = n_chunks

    def compress(t2d, pe, w1, w2):
        ch = t2d.reshape(B, n_chunks, NSA_CMP_STRIDE, G, HD).transpose(0, 1, 3, 2, 4)
        ch = ch.reshape(B, n_chunks, G, NSA_CMP_STRIDE * HD)
        flat = jnp.concatenate([ch[:, :n_cmp], ch[:, 1:n_cmp + 1]], axis=-1)
        flat = jnp.pad(flat, ((0, 0), (0, ncp - n_cmp), (0, 0), (0, 0))).reshape(B * ncp * G, -1)
        out = _nsa_compress(flat, pe.reshape(1, -1), w1.astype(BF16), _pad_cols(w2, LANES).astype(BF16))
        return out[:, :HD].reshape(B, ncp, G, HD).astype(BF16)

    kc = compress(kvc[:, :kvw], pe_k, ck_w1, ck_w2).transpose(0, 2, 1, 3)
    kk_c = jnp.concatenate([kc, kc], axis=-1)
    vt_c = compress(kvc[:, kvw:], pe_v, cv_w1, cv_w2).transpose(0, 2, 3, 1)

    n_sel = S // NSA_SEL_LEN
    assert n_sel <= LANES
    cmp_start = np.arange(ncp) * NSA_CMP_STRIDE
    sel_start = np.arange(LANES) * NSA_SEL_LEN
    overlap = ((cmp_start[:, None] < sel_start[None, :] + NSA_SEL_LEN)
               & (cmp_start[:, None] + NSA_CMP_LEN > sel_start[None, :])
               & (np.arange(ncp)[:, None] < n_cmp) & (np.arange(LANES)[None, :] < n_sel))
    overlap_t = jnp.asarray(overlap.T, BF16)
    ex = np.zeros((3, G, LANES, HG * HD), np.float32)
    for br in range(3):
        for g in range(G):
            for hg in range(HG):
                ex[br, g, (g * HG + hg) * 3 + br, hg * HD:(hg + 1) * HD] = 1.0
    ex = jnp.asarray(ex, BF16)

    o_c, selneg = _nsa_cmp_branch(q_c, kk_c, vt_c, overlap_t, gl, bg, ex[0], B, S)
    onehot = jnp.asarray((np.arange(S)[:, None] // NSA_SEL_LEN) == np.arange(LANES)[None, :], BF16)
    kk_s = grouped(k_rot[:, :kvw])
    kk_w = grouped(k_rot[:, kvw:])
    o_s = _nsa_kv_branch("sel", q_r, kk_s, vt[:kvw], gl, bg, ex[1], B, S, onehot=onehot, selneg=selneg)
    o_w = _nsa_kv_branch("win", q_r, kk_w, vt[kvw:], gl, bg, ex[2], B, S)
    return [o_c, o_s, o_w]


def _moba_mixer(x, B, S, cosf, sinf, w_in):
    D, H, HD = D_MODEL, N_HEADS, HEAD_DIM
    wb = w_in.astype(BF16)
    qk_rot = _proj(x, wb[:, :2 * D], mode="rope", cos=cosf, sin=sinf, col_scale=_q_col_scale(D, 2 * D))
    vt = _proj_t(wb[:, 2 * D:].T, x)
    nblk = S // MOBA_BLOCK
    assert S % MOBA_BLOCK == 0 and 2 * nblk <= LANES and nblk % SUBLANES == 0
    kmean = _moba_kmean(qk_rot[:, D:]).reshape(B, nblk, H // 2, 2, HD)
    km = kmean.transpose(0, 2, 3, 1, 4)
    rmat_t = jnp.zeros((B, H // 2, LANES, 2, HD), F32)
    rmat_t = rmat_t.at[:, :, :nblk, 0].set(km[:, :, 0]).at[:, :, nblk:2 * nblk, 1].set(km[:, :, 1])
    selneg = _moba_select(qk_rot, rmat_t.reshape(B, H // 2, LANES, LANES), B, S)
    blk_of = np.arange(S) // MOBA_BLOCK
    oh = np.zeros((S, LANES), np.float32)
    oh[np.arange(S), blk_of] = 1.0
    oh[np.arange(S), nblk + blk_of] = 1.0
    return [_moba_attention(qk_rot, vt, jnp.asarray(oh, BF16), selneg, B, S)]


def _sc_invert_kernel(dest_hbm, tok_hbm, out_hbm, d_v, t_v, buf_v, *, seg, chunk, n_chunks):
    lo = pl.multiple_of((lax.axis_index("s") * SC_CORES + lax.axis_index("c")) * seg, SC_LANES)

    @pl.loop(0, seg // SC_LANES)
    def _(j):
        buf_v[pl.ds(j * SC_LANES, SC_LANES)] = jnp.zeros((SC_LANES,), jnp.int32)

    @pl.loop(0, n_chunks)
    def _(c):
        pltpu.sync_copy(dest_hbm.at[pl.ds(c * chunk, chunk)], d_v)
        pltpu.sync_copy(tok_hbm.at[pl.ds(c * chunk, chunk)], t_v)

        @pl.loop(0, chunk // SC_LANES)
        def _(j):
            d = d_v[pl.ds(j * SC_LANES, SC_LANES)] - lo
            mine = (d >= 0) & (d < seg)
            plsc.store_scatter(buf_v, [jnp.where(mine, d, 0)], t_v[pl.ds(j * SC_LANES, SC_LANES)], mask=mine)

    pltpu.sync_copy(buf_v, out_hbm.at[pl.ds(lo, seg)])


def _invert_dest(dest, tok, P):
    R = dest.shape[0]
    n_workers = SC_CORES * SC_SUBCORES
    seg = P // n_workers
    chunk = min(4096, R)
    assert P == seg * n_workers and seg % SC_LANES == 0 and R % chunk == 0
    mesh = plsc.VectorSubcoreMesh(core_axis_name="c", subcore_axis_name="s",
                                  num_cores=SC_CORES, num_subcores=SC_SUBCORES)
    return pl.kernel(
        functools.partial(_sc_invert_kernel, seg=seg, chunk=chunk, n_chunks=R // chunk),
        out_type=jax.ShapeDtypeStruct((P,), jnp.int32), mesh=mesh,
        scratch_types=[pltpu.VMEM((chunk,), jnp.int32), pltpu.VMEM((chunk,), jnp.int32),
                       pltpu.VMEM((seg,), jnp.int32)],
        compiler_params=pltpu.CompilerParams(needs_layout_passes=False),
        name="moe_invert")(dest, tok)


def _moe_dispatch(x1, x1b, router_w, router_b, wg, wu, wd, layer):
    T, D = x1.shape
    eidx_t, gw_t = _router(x1, router_w, router_b)
    rank_t, cnt = _expert_ranks(eidx_t)
    counts = cnt[:, 0].astype(jnp.int32)
    padded = (counts + MOE_BLOCK - 1) // MOE_BLOCK * MOE_BLOCK
    pends = jnp.cumsum(padded)
    pstarts = pends - padded
    R = T * TOP_K
    P = -(-(R + N_EXPERTS * (MOE_BLOCK - 1)) // MOE_BLOCK) * MOE_BLOCK
    NB = P // MOE_BLOCK
    blk_start = jnp.arange(NB, dtype=jnp.int32) * MOE_BLOCK
    blk_e = jnp.minimum(jnp.sum(pends[None, :] <= blk_start[:, None], axis=1), N_EXPERTS - 1).astype(jnp.int32)
    nb_used = (pends[-1] // MOE_BLOCK).astype(jnp.int32).reshape(1)
    dest_t = _expert_dest(eidx_t, rank_t, pstarts)

    tok = jnp.broadcast_to(jnp.arange(T, dtype=jnp.int32)[None, :], (TOP_K, T))
    buf_tok = _invert_dest(dest_t.reshape(-1), tok.reshape(-1), P)
    xs = x1[buf_tok]
    yb = _expert_ffn(blk_e, nb_used, xs, wg, wu, wd, layer)
    yg = yb[dest_t.reshape(-1)]
    return yg, gw_t.T


def kernel(x, p, positions, fox_w_in, fox_b_f, fox_w_out, nsa_w_in, nsa_b_gate, nsa_pe_k, nsa_pe_v,
           nsa_cmp_k_w1, nsa_cmp_k_w2, nsa_cmp_v_w1, nsa_cmp_v_w2, nsa_w_out, moba_w_in, moba_w_out,
           ln1_g, ln1_b, router_w, router_b, exp_w_gate, exp_w_up, exp_w_down,
           sh_w_gate, sh_w_up, sh_w_down, ln2_g, ln2_b, ple_w_gate, ple_w_proj):
    B, S, D = x.shape
    T = B * S
    depth = p.shape[0]
    cosf, sinf = _rope_tables(positions)
    xt = x.reshape(T, D)
    for i in range(depth):
        kind, j = i % N_MIXERS, i // N_MIXERS
        if kind == 0:
            o_list = _fox_mixer(xt, B, S, fox_w_in[j], fox_b_f[j])
            w_out = fox_w_out[j]
        elif kind == 1:
            o_list = _nsa_mixer(xt, B, S, cosf, sinf, nsa_w_in[j], nsa_b_gate[j], nsa_pe_k[j], nsa_pe_v[j],
                                nsa_cmp_k_w1[j], nsa_cmp_k_w2[j], nsa_cmp_v_w1[j], nsa_cmp_v_w2[j])
            w_out = nsa_w_out[j]
        else:
            o_list = _moba_mixer(xt, B, S, cosf, sinf, moba_w_in[j])
            w_out = moba_w_out[j]
        x1, x1b = _outproj_ln(o_list, w_out.astype(BF16), xt, ln1_g[i], ln1_b[i])
        yg, gw = _moe_dispatch(x1, x1b, router_w[i], router_b[i], exp_w_gate, exp_w_up, exp_w_down, i)
        xt = _post_moe(x1, yg, gw, p[i].reshape(T, -1), sh_w_gate[i].astype(BF16), sh_w_up[i].astype(BF16),
                       sh_w_down[i].astype(BF16), ln2_g[i], ln2_b[i],
                       ple_w_gate[i].astype(BF16), ple_w_proj[i].astype(BF16))
    return xt.reshape(B, S, D)
```

```python
import functools

import jax
import jax.numpy as jnp
import numpy as np
from jax import lax
from jax.experimental import pallas as pl
from jax.experimental.pallas import tpu as pltpu
from jax.experimental.pallas import tpu_sc as plsc

D_MODEL = 1024
DEPTH = 4
N_HEADS = 16
HEAD_DIM = 64
ATTN_SCALE = HEAD_DIM ** -0.5
LOG2E = 1.4426950408889634
Q_SCALE = ATTN_SCALE * LOG2E
ROPE_THETA = 10000.0
N_MIXERS = 3

NSA_KV_GROUPS = 4
NSA_HEADS_PER_GROUP = N_HEADS // NSA_KV_GROUPS
NSA_CMP_LEN = 32
NSA_CMP_STRIDE = 16
NSA_SEL_LEN = 64
NSA_SEL_TOPN = 16
NSA_WINDOW = 512
NSA_FORCE_BONUS = 1e4

MOBA_BLOCK = 256
MOBA_TOPK = 3

N_EXPERTS = 64
EXPERT_DIM = 256
TOP_K = 8
N_GROUPS = 8
TOPK_GROUPS = 4
ROUTED_SCALE = 2.5
MOE_BLOCK = 256

DN_ALPHA = (2 * DEPTH) ** 0.25
LN_EPS = 1e-5
NEG = -1e30

SC_CORES = 2
SC_SUBCORES = 16
SC_LANES = 16
LANES = 128
SUBLANES = 8
ATTN_TILE = 512
PAIR_TILE = 1024
KV_TILE = 1024
ROW_TILE = 512
VMEM_LIMIT = 48 * 1024 * 1024
C_SPLIT = 3
ONES_ROWS = 16

F32 = jnp.float32
BF16 = jnp.bfloat16
HIGHEST = lax.Precision.HIGHEST


def _cparams(sem):
    return pltpu.CompilerParams(dimension_semantics=sem, vmem_limit_bytes=VMEM_LIMIT)


def _log2(n):
    assert n & (n - 1) == 0
    return n.bit_length() - 1


def _nt_dot(a, b, precision=None):
    return lax.dot_general(a, b, (((1,), (1,)), ((), ())), preferred_element_type=F32,
                           precision=precision)


def _topk_rows(work, n):
    rows = lax.broadcasted_iota(jnp.int32, work.shape, 0).astype(F32)
    chosen = jnp.zeros(work.shape, F32)
    for _ in range(n):
        m = jnp.max(work, axis=0, keepdims=True)
        idx = jnp.min(jnp.where(work == m, rows, float(work.shape[0])), axis=0, keepdims=True)
        pick = rows == idx
        chosen = jnp.where(pick, 1.0, chosen)
        work = jnp.where(pick, -jnp.inf, work)
    return chosen


def _proj_kernel(*refs, mode, precision, scaled):
    refs = list(refs)
    x_ref, w_ref = refs[:2]
    del refs[:2]
    scale_ref = refs.pop(0) if scaled else None
    if mode == "none":
        (o_ref,) = refs
    elif mode == "rope":
        cos_ref, sin_ref, r_ref = refs
    else:
        cos_ref, sin_ref, o_ref, r_ref = refs
    x = x_ref[...]
    w = w_ref[...]
    if x.dtype != w.dtype:
        x = x.astype(w.dtype)
    acc = jnp.dot(x, w, preferred_element_type=F32, precision=precision)
    if scaled:
        acc = acc * scale_ref[...]
    if mode in ("none", "both"):
        o_ref[...] = acc.astype(o_ref.dtype)
    if mode in ("rope", "both"):
        tn = acc.shape[1]
        rep = tn // LANES
        cosf = jnp.concatenate([cos_ref[...]] * rep, axis=1)
        sinf = jnp.concatenate([sin_ref[...]] * rep, axis=1)
        lane = lax.broadcasted_iota(jnp.int32, acc.shape, 1)
        first_half = (lane & (HEAD_DIM // 2)) == 0
        swapped = jnp.where(first_half, pltpu.roll(acc, tn - HEAD_DIM // 2, 1),
                            pltpu.roll(acc, HEAD_DIM // 2, 1))
        r_ref[...] = (acc * cosf + swapped * sinf).astype(r_ref.dtype)


def _proj(x, w, *, mode="none", cos=None, sin=None, out_dtype=BF16, tn=512, precision=None, col_scale=None):
    T, K = x.shape
    N = w.shape[1]
    tm = min(ROW_TILE, T)
    tn = min(tn, N)
    assert T % tm == 0 and N % tn == 0
    grid = (T // tm, N // tn)
    in_specs = [pl.BlockSpec((tm, K), lambda i, j: (i, 0)),
                pl.BlockSpec((K, tn), lambda i, j: (0, j))]
    args = [x, w]
    if col_scale is not None:
        in_specs.append(pl.BlockSpec((1, tn), lambda i, j: (0, j)))
        args.append(col_scale.reshape(1, N).astype(F32))
    if mode != "none":
        in_specs += [pl.BlockSpec((tm, LANES), lambda i, j: (i, 0))] * 2
        args += [cos, sin]
    o_spec = pl.BlockSpec((tm, tn), lambda i, j: (i, j))
    o_shape = jax.ShapeDtypeStruct((T, N), out_dtype)
    if mode == "both":
        out_specs, out_shape = [o_spec, o_spec], [o_shape, o_shape]
    else:
        out_specs, out_shape = o_spec, o_shape
    return pl.pallas_call(
        functools.partial(_proj_kernel, mode=mode, precision=precision, scaled=col_scale is not None),
        grid=grid, in_specs=in_specs, out_specs=out_specs, out_shape=out_shape,
        compiler_params=_cparams(("parallel", "parallel")), name=f"proj_{mode}")(*args)


def _proj_t_kernel(wt_ref, x_ref, o_ref, *, precision):
    wt = wt_ref[...]
    x = x_ref[...]
    if x.dtype != wt.dtype:
        x = x.astype(wt.dtype)
    o_ref[...] = _nt_dot(wt, x, precision).astype(o_ref.dtype)


def _proj_t(wt, x, *, out_dtype=BF16, precision=None):
    N, K = wt.shape
    T = x.shape[0]
    tm = min(ROW_TILE, T)
    tn = min(512, N)
    assert T % tm == 0 and N % tn == 0
    return pl.pallas_call(
        functools.partial(_proj_t_kernel, precision=precision),
        grid=(T // tm, N // tn),
        in_specs=[pl.BlockSpec((tn, K), lambda i, j: (j, 0)), pl.BlockSpec((tm, K), lambda i, j: (i, 0))],
        out_specs=pl.BlockSpec((tn, tm), lambda i, j: (j, i)),
        out_shape=jax.ShapeDtypeStruct((N, T), out_dtype),
        compiler_params=_cparams(("parallel", "parallel")), name="proj_t")(wt, x)


def _pad_cols(w, n):
    return jnp.pad(w, ((0, 0), (0, n - w.shape[1])))


def _layer_norm(z, g, b):
    mu = jnp.mean(z, axis=-1, keepdims=True)
    zc = z - mu
    var = jnp.mean(zc * zc, axis=-1, keepdims=True)
    return zc * lax.rsqrt(var + LN_EPS) * g + b


def _outproj_ln_kernel(*refs, n_o):
    o_refs = refs[:n_o]
    w_ref, x_ref, g_ref, b_ref, y_ref, yb_ref = refs[n_o:]
    if n_o == 1:
        o = o_refs[0][...]
    else:
        o = o_refs[0][...].astype(F32)
        for r in o_refs[1:]:
            o = o + r[...].astype(F32)
        o = o.astype(BF16)
    mix = jnp.dot(o, w_ref[...], preferred_element_type=F32)
    z = DN_ALPHA * x_ref[...] + mix
    y = _layer_norm(z, g_ref[...], b_ref[...])
    y_ref[...] = y
    yb_ref[...] = y.astype(BF16)


def _outproj_ln(o_list, w, x, g, b):
    T, D = x.shape
    tm = min(ROW_TILE, T)
    row = pl.BlockSpec((tm, D), lambda i: (i, 0))
    vec = pl.BlockSpec((1, D), lambda i: (0, 0))
    return pl.pallas_call(
        functools.partial(_outproj_ln_kernel, n_o=len(o_list)),
        grid=(T // tm,),
        in_specs=[row] * len(o_list) + [pl.BlockSpec((D, D), lambda i: (0, 0)), row, vec, vec],
        out_specs=[row, row],
        out_shape=[jax.ShapeDtypeStruct((T, D), F32), jax.ShapeDtypeStruct((T, D), BF16)],
        compiler_params=_cparams(("parallel",)), name="outproj_ln")(
            *o_list, w, x, g.reshape(1, D), b.reshape(1, D))


def _flash_init(m_ref, acc_ref):
    m_ref[...] = jnp.full(m_ref.shape, NEG, F32)
    acc_ref[...] = jnp.zeros(acc_ref.shape, F32)


def _flash_step(sT, lhs, m_ref, acc_ref, h, cols=slice(None)):
    m_prev = m_ref[h, :, cols]
    m_new = jnp.maximum(m_prev, jnp.max(sT, axis=0, keepdims=True))
    p = jnp.exp2(sT - m_new)
    alpha = jnp.exp2(m_prev - m_new)
    acc_ref[h, :, cols] = alpha * acc_ref[h, :, cols] + jnp.dot(lhs, p.astype(BF16), preferred_element_type=F32)
    m_ref[h, :, cols] = m_new


def _causal_t(tk, tq, shift=0, strict_lower=False):
    key = lax.broadcasted_iota(jnp.int32, (tk, tq), 0)
    qry = lax.broadcasted_iota(jnp.int32, (tk, tq), 1)
    return (key > qry) if strict_lower else (key <= qry + shift)


def _causal_sweep(i, tq, tk, step):
    r = tk // tq
    assert tk == r * tq
    n_full = i >> _log2(r)

    def body(j, carry):
        step(pl.multiple_of(j * tk, tk), tk, None)
        return carry

    lax.fori_loop(0, n_full, body, 0)
    if r == 1:
        step(pl.multiple_of(i * tq, tq), tq, _causal_t(tq, tq))
    else:
        for rem in range(r):
            @pl.when(i - n_full * r == rem)
            def _(rem=rem):
                size = (rem + 1) * tq
                step(pl.multiple_of(n_full * tk, tk), size, _causal_t(size, tq, shift=rem * tq))


def _staggered(units, scores, update):
    s = [scores(units[0])]
    for n, u in enumerate(units):
        if n + 1 < len(units):
            s.append(scores(units[n + 1]))
        update(u, s[n])


def _pair_units(tq):
    w = min(ATTN_TILE, tq)
    return [(h, slice(c * w, (c + 1) * w)) for h in range(2) for c in range(tq // w)]


def _head_of_pair(q_pair, which):
    lane = lax.broadcasted_iota(jnp.int32, q_pair.shape, 1)
    keep = (lane < HEAD_DIM) if which == 0 else (lane >= HEAD_DIM)
    return jnp.where(keep, q_pair, jnp.zeros_like(q_pair))


def _pair_rows(a_top, a_bot):
    row = lax.broadcasted_iota(jnp.int32, a_top.shape, 0)
    return jnp.where(row < HEAD_DIM, a_top, a_bot)


def _pair_lhs(vt, which):
    row = lax.broadcasted_iota(jnp.int32, vt.shape, 0)
    keep = (row < HEAD_DIM) if which == 0 else (row >= HEAD_DIM)
    return jnp.where(keep, vt, jnp.ones_like(vt))


def _pair_finish(acc_ref):
    a0, a1 = acc_ref[0], acc_ref[1]
    return _pair_rows(a0 / a0[HEAD_DIM:HEAD_DIM + 1, :], a1 / a1[0:1, :])


def _fox_gate_kernel(fl_ref, bf_ref, tri_ref, ex_ref, cp_ref, carry_ref):
    @pl.when(pl.program_id(1) == 0)
    def _():
        carry_ref[...] = jnp.zeros(carry_ref.shape, F32)

    z = fl_ref[...] + bf_ref[...]
    log_f = jnp.minimum(z, 0.0) - jnp.log1p(jnp.exp(-jnp.abs(z)))
    c = jnp.dot(tri_ref[...], log_f, preferred_element_type=F32, precision=HIGHEST) + carry_ref[...]
    carry_ref[...] = c[-1:, :]
    out = jnp.zeros(c.shape, F32)
    rem = c * LOG2E
    for piece in range(C_SPLIT):
        part = rem.astype(BF16)
        rem = rem - part.astype(F32)
        out = out + jnp.dot(part, ex_ref[piece], preferred_element_type=F32)
    cp_ref[...] = out.astype(BF16)


def _fox_gate(fl, b_f, B, S):
    T = fl.shape[0]
    tm = min(ROW_TILE, S)
    ns = S // tm
    tri = jnp.asarray(np.tril(np.ones((tm, tm), np.float32)))
    ex = np.zeros((C_SPLIT, LANES, LANES), np.float32)
    for h in range(N_HEADS):
        for j in range(C_SPLIT):
            ex[j, h, C_SPLIT * h + j] = 1.0
    bf = _pad_cols(b_f.reshape(1, -1), LANES)
    return pl.pallas_call(
        _fox_gate_kernel, grid=(B, ns),
        in_specs=[pl.BlockSpec((tm, LANES), lambda b, s: (b * ns + s, 0)),
                  pl.BlockSpec((1, LANES), lambda b, s: (0, 0)),
                  pl.BlockSpec((tm, tm), lambda b, s: (0, 0)),
                  pl.BlockSpec((C_SPLIT, LANES, LANES), lambda b, s: (0, 0, 0))],
        out_specs=pl.BlockSpec((tm, LANES), lambda b, s: (b * ns + s, 0)),
        out_shape=jax.ShapeDtypeStruct((T, LANES), BF16),
        scratch_shapes=[pltpu.VMEM((1, LANES), F32)],
        compiler_params=_cparams(("parallel", "arbitrary")), name="fox_gate")(
            fl, bf, tri, jnp.asarray(ex, BF16))


def _fox_kernel(q_ref, k_ref, cp_ref, vt_ref, o_ref, m_ref, acc_ref, *, tq, tk):
    hp = pl.program_id(1)
    i = pl.program_id(2)
    _flash_init(m_ref, acc_ref)
    q = q_ref[...]
    lane = lax.broadcasted_iota(jnp.int32, q.shape, 1)
    qa = []
    for h in range(2):
        lo = C_SPLIT * (2 * hp + h)
        minus_one = jnp.where((lane >= lo) & (lane < lo + C_SPLIT), -1.0, 0.0).astype(q.dtype)
        qa.append(jnp.concatenate([_head_of_pair(q, h), minus_one], axis=1))

    def step(off, size, mask):
        k = jnp.concatenate([k_ref[pl.ds(off, size), :], cp_ref[pl.ds(off, size), :]], axis=1)
        vt = vt_ref[:, pl.ds(off, size)]

        lhs = [_pair_lhs(vt, h) for h in range(2)]

        def scores(u):
            sT = _nt_dot(k, qa[u[0]][u[1], :])
            return sT if mask is None else jnp.where(mask[:, u[1]], sT, NEG)

        _staggered(_pair_units(tq), scores,
                   lambda u, sT: _flash_step(sT, lhs[u[0]], m_ref, acc_ref, u[0], u[1]))

    _causal_sweep(i, tq, tk, step)
    o_ref[...] = _pair_finish(acc_ref).T.astype(o_ref.dtype)


def _fox_attention(qk, cp, vt, B, S):
    T = qk.shape[0]
    tq = min(PAIR_TILE, S)
    tk = min(KV_TILE, S)
    nq = S // tq
    npair = N_HEADS // 2
    ncol = D_MODEL // LANES
    return pl.pallas_call(
        functools.partial(_fox_kernel, tq=tq, tk=tk),
        grid=(B, npair, nq),
        in_specs=[pl.BlockSpec((tq, LANES), lambda b, hp, i: (b * nq + i, hp)),
                  pl.BlockSpec((S, LANES), lambda b, hp, i: (b, ncol + hp)),
                  pl.BlockSpec((S, LANES), lambda b, hp, i: (b, 0)),
                  pl.BlockSpec((LANES, S), lambda b, hp, i: (hp, b))],
        out_specs=pl.BlockSpec((tq, LANES), lambda b, hp, i: (b * nq + i, hp)),
        out_shape=jax.ShapeDtypeStruct((T, D_MODEL), BF16),
        scratch_shapes=[pltpu.VMEM((2, 1, tq), F32), pltpu.VMEM((2, LANES, tq), F32)],
        compiler_params=_cparams(("parallel", "parallel", "arbitrary")), name="fox_attn")(
            qk, qk, cp, vt)


def _gate_expand(gl_ref, bg_ref, ex_ref):
    sig = jax.nn.sigmoid(gl_ref[...] + bg_ref[...])
    hi = sig.astype(BF16)
    lo = (sig - hi.astype(F32)).astype(BF16)
    ex = ex_ref[...]
    return jnp.dot(hi, ex, preferred_element_type=F32) + jnp.dot(lo, ex, preferred_element_type=F32)


def _nsa_compress_kernel(f_ref, pe_ref, w1_ref, w2_ref, o_ref):
    blk = (f_ref[...].astype(F32) + pe_ref[...]).astype(BF16)
    h = jax.nn.gelu(jnp.dot(blk, w1_ref[...], preferred_element_type=F32))
    o_ref[...] = jnp.dot(h.astype(BF16), w2_ref[...], preferred_element_type=F32)


def _nsa_compress(flat, pe_flat, w1, w2p):
    M, K = flat.shape
    tm = min(ROW_TILE, M)
    Hc = w1.shape[1]
    return pl.pallas_call(
        _nsa_compress_kernel, grid=(M // tm,),
        in_specs=[pl.BlockSpec((tm, K), lambda i: (i, 0)), pl.BlockSpec((1, K), lambda i: (0, 0)),
                  pl.BlockSpec((K, Hc), lambda i: (0, 0)), pl.BlockSpec((Hc, LANES), lambda i: (0, 0))],
        out_specs=pl.BlockSpec((tm, LANES), lambda i: (i, 0)),
        out_shape=jax.ShapeDtypeStruct((M, LANES), F32),
        compiler_params=_cparams(("parallel",)), name="nsa_compress")(flat, pe_flat, w1, w2p)


def _nsa_cmp_kernel(q_ref, kk_ref, vt_ref, ovt_ref, gl_ref, bg_ref, ex_ref, o_ref, sn_ref, *, tq, ncp, nselp):
    i = pl.program_id(2)
    q = q_ref[...]
    kk = kk_ref[...]
    vt = vt_ref[...]
    t = i * tq + lax.broadcasted_iota(jnp.int32, (ncp, tq), 1)
    n = lax.broadcasted_iota(jnp.int32, (ncp, tq), 0)
    valid = n * NSA_CMP_STRIDE + (NSA_CMP_LEN - 1) <= t
    psum = jnp.zeros((ncp, tq), F32)
    outs = []
    for hg in range(NSA_HEADS_PER_GROUP):
        qh = _head_of_pair(q[:, LANES * (hg // 2):LANES * (hg // 2 + 1)], hg % 2)
        sT = jnp.where(valid, _nt_dot(kk, qh), NEG)
        m = jnp.max(sT, axis=0, keepdims=True)
        e = jnp.where(valid, jnp.exp2(sT - m), 0.0)
        p = e / jnp.maximum(jnp.sum(e, axis=0, keepdims=True), 1e-30)
        psum = psum + p
        outs.append(jnp.dot(vt, p.astype(BF16), preferred_element_type=F32))
    o = jnp.concatenate(outs, axis=0).T
    o_ref[...] = (o * _gate_expand(gl_ref, bg_ref, ex_ref)).astype(o_ref.dtype)
    hi = psum.astype(BF16)
    lo = (psum - hi.astype(F32)).astype(BF16)
    ovt = ovt_ref[...]
    imp = jnp.dot(ovt, hi, preferred_element_type=F32) + jnp.dot(ovt, lo, preferred_element_type=F32)
    jblk = lax.broadcasted_iota(jnp.int32, (nselp, tq), 0)
    cur = (i * tq + lax.broadcasted_iota(jnp.int32, (nselp, tq), 1)) >> _log2(NSA_SEL_LEN)
    forced = (jblk == 0) | (jblk == cur) | (jblk == cur - 1)
    pri = jnp.where(forced, imp + NSA_FORCE_BONUS, imp)
    past = jblk <= cur
    pri = jnp.where(past, pri, NEG)
    chosen = _topk_rows(pri, NSA_SEL_TOPN)
    sn = jnp.where((chosen > 0.0) & past, 0.0, NEG)
    sn_ref[...] = sn.T.astype(sn_ref.dtype)


def _nsa_cmp_branch(q_c, kk_c, vt_c, overlap_t, gl, bg, ex, B, S):
    T = q_c.shape[0]
    G = NSA_KV_GROUPS
    tq = min(ATTN_TILE, S)
    nq = S // tq
    ncp = kk_c.shape[2]
    nselp = overlap_t.shape[0]
    return pl.pallas_call(
        functools.partial(_nsa_cmp_kernel, tq=tq, ncp=ncp, nselp=nselp),
        grid=(B, G, nq),
        in_specs=[pl.BlockSpec((tq, 2 * LANES), lambda b, g, i: (b * nq + i, g)),
                  pl.BlockSpec((None, None, ncp, LANES), lambda b, g, i: (b, g, 0, 0)),
                  pl.BlockSpec((None, None, HEAD_DIM, ncp), lambda b, g, i: (b, g, 0, 0)),
                  pl.BlockSpec((nselp, ncp), lambda b, g, i: (0, 0)),
                  pl.BlockSpec((tq, LANES), lambda b, g, i: (b * nq + i, 0)),
                  pl.BlockSpec((1, LANES), lambda b, g, i: (0, 0)),
                  pl.BlockSpec((None, LANES, 2 * LANES), lambda b, g, i: (g, 0, 0))],
        out_specs=[pl.BlockSpec((tq, 2 * LANES), lambda b, g, i: (b * nq + i, g)),
                   pl.BlockSpec((None, None, tq, nselp), lambda b, g, i: (b, g, i, 0))],
        out_shape=[jax.ShapeDtypeStruct((T, D_MODEL), BF16),
                   jax.ShapeDtypeStruct((B, G, S, nselp), BF16)],
        compiler_params=_cparams(("parallel", "parallel", "parallel")), name="nsa_cmp")(
            q_c, kk_c, vt_c, overlap_t, gl, bg, ex)


def _nsa_kv_kernel(*refs, mode, tq, tk):
    if mode == "sel":
        q_ref, kk_ref, vt_ref, oh_ref, sn_ref, gl_ref, bg_ref, ex_ref, o_ref, m_ref, acc_ref = refs
    else:
        q_ref, kk_ref, vt_ref, gl_ref, bg_ref, ex_ref, o_ref, m_ref, acc_ref = refs
    i = pl.program_id(2)
    _flash_init(m_ref, acc_ref)
    q = q_ref[...]
    qh = []
    for hg in range(NSA_HEADS_PER_GROUP):
        qq = _head_of_pair(q[:, LANES * (hg // 2):LANES * (hg // 2 + 1)], hg % 2)
        if mode == "sel":
            qq = jnp.concatenate([qq, sn_ref[...]], axis=1)
        qh.append(qq)

    def step(off, size, mask):
        k = kk_ref[pl.ds(off, size), :]
        vt = vt_ref[:, pl.ds(off, size)]
        lhs = jnp.concatenate([vt, jnp.ones((ONES_ROWS, size), vt.dtype)], axis=0)
        if mode == "sel":
            k = jnp.concatenate([k, oh_ref[pl.ds(off, size), :]], axis=1)

        def scores(hg):
            sT = _nt_dot(k, qh[hg])
            return sT if mask is None else jnp.where(mask, sT, NEG)

        _staggered(list(range(NSA_HEADS_PER_GROUP)), scores,
                   lambda hg, sT: _flash_step(sT, lhs, m_ref, acc_ref, hg))

    if mode == "sel":
        _causal_sweep(i, tq, tk, step)
    else:
        @pl.when(i > 0)
        def _():
            step(pl.multiple_of((i - 1) * tq, tq), tq, _causal_t(tq, tq, strict_lower=True))
        step(pl.multiple_of(i * tq, tq), tq, _causal_t(tq, tq))
    outs = []
    for hg in range(NSA_HEADS_PER_GROUP):
        a = acc_ref[hg]
        outs.append(a[:HEAD_DIM] / a[HEAD_DIM:HEAD_DIM + 1, :])
    o_ref[...] = (jnp.concatenate(outs, axis=0).T * _gate_expand(gl_ref, bg_ref, ex_ref)).astype(o_ref.dtype)


def _nsa_kv_branch(mode, q_r, kk, vt, gl, bg, ex, B, S, onehot=None, selneg=None):
    T = q_r.shape[0]
    G = NSA_KV_GROUPS
    tq = min(ATTN_TILE, S)
    tk = min(KV_TILE, S)
    assert NSA_WINDOW == tq or mode == "sel"
    nq = S // tq
    in_specs = [pl.BlockSpec((tq, 2 * LANES), lambda b, g, i: (b * nq + i, g)),
                pl.BlockSpec((None, None, S, LANES), lambda b, g, i: (b, g, 0, 0)),
                pl.BlockSpec((HEAD_DIM, S), lambda b, g, i: (g, b))]
    args = [q_r, kk, vt]
    if mode == "sel":
        in_specs += [pl.BlockSpec((S, LANES), lambda b, g, i: (0, 0)),
                     pl.BlockSpec((None, None, tq, LANES), lambda b, g, i: (b, g, i, 0))]
        args += [onehot, selneg]
    in_specs += [pl.BlockSpec((tq, LANES), lambda b, g, i: (b * nq + i, 0)),
                 pl.BlockSpec((1, LANES), lambda b, g, i: (0, 0)),
                 pl.BlockSpec((None, LANES, 2 * LANES), lambda b, g, i: (g, 0, 0))]
    args += [gl, bg, ex]
    nh = NSA_HEADS_PER_GROUP
    return pl.pallas_call(
        functools.partial(_nsa_kv_kernel, mode=mode, tq=tq, tk=tk),
        grid=(B, G, nq), in_specs=in_specs,
        out_specs=pl.BlockSpec((tq, 2 * LANES), lambda b, g, i: (b * nq + i, g)),
        out_shape=jax.ShapeDtypeStruct((T, D_MODEL), BF16),
        scratch_shapes=[pltpu.VMEM((nh, 1, tq), F32), pltpu.VMEM((nh, HEAD_DIM + ONES_ROWS, tq), F32)],
        compiler_params=_cparams(("parallel", "parallel", "arbitrary")), name=f"nsa_{mode}")(*args)


def _kmean_kernel(k_ref, o_ref, *, nblk):
    k = k_ref[...].astype(F32)
    o_ref[...] = jnp.mean(k.reshape(nblk, MOBA_BLOCK, k.shape[1]), axis=1)


def _moba_kmean(k_rot):
    T, D = k_rot.shape
    nblk = 8
    rows = nblk * MOBA_BLOCK
    assert T % rows == 0
    return pl.pallas_call(
        functools.partial(_kmean_kernel, nblk=nblk), grid=(T // rows,),
        in_specs=[pl.BlockSpec((rows, D), lambda i: (i, 0))],
        out_specs=pl.BlockSpec((nblk, D), lambda i: (i, 0)),
        out_shape=jax.ShapeDtypeStruct((T // MOBA_BLOCK, D), F32),
        compiler_params=_cparams(("parallel",)), name="moba_kmean")(k_rot)


def _moba_select_kernel(q_ref, r_ref, sn_ref, *, tq, nblk):
    i = pl.program_id(2)
    gsT = _nt_dot(r_ref[...], q_ref[...].astype(F32), HIGHEST)
    jblk = lax.broadcasted_iota(jnp.int32, (nblk, tq), 0)
    cb = (i * tq + lax.broadcasted_iota(jnp.int32, (nblk, tq), 1)) >> _log2(MOBA_BLOCK)
    past = jblk < cb
    parts = []
    for h in range(2):
        gs = jnp.where(past, gsT[h * nblk:(h + 1) * nblk, :], NEG)
        chosen = _topk_rows(gs, min(MOBA_TOPK, nblk))
        parts.append(jnp.where(((chosen > 0.0) & past) | (jblk == cb), 0.0, NEG))
    parts.append(jnp.zeros((LANES - 2 * nblk, tq), F32))
    sn_ref[...] = jnp.concatenate(parts, axis=0).T.astype(sn_ref.dtype)


def _moba_select(q_rot, rmat_t, B, S):
    tq = min(ATTN_TILE, S)
    nq = S // tq
    npair = N_HEADS // 2
    nblk = S // MOBA_BLOCK
    return pl.pallas_call(
        functools.partial(_moba_select_kernel, tq=tq, nblk=nblk), grid=(B, npair, nq),
        in_specs=[pl.BlockSpec((tq, LANES), lambda b, hp, i: (b * nq + i, hp)),
                  pl.BlockSpec((None, None, LANES, LANES), lambda b, hp, i: (b, hp, 0, 0))],
        out_specs=pl.BlockSpec((None, None, tq, LANES), lambda b, hp, i: (b, hp, i, 0)),
        out_shape=jax.ShapeDtypeStruct((B, npair, S, LANES), BF16),
        compiler_params=_cparams(("parallel", "parallel", "parallel")), name="moba_select")(q_rot, rmat_t)


def _moba_kernel(q_ref, k_ref, vt_ref, oh_ref, sn_ref, o_ref, m_ref, acc_ref, *, tq, tk, nblk):
    i = pl.program_id(2)
    _flash_init(m_ref, acc_ref)
    q = q_ref[...]
    sn = sn_ref[...]
    lane = lax.broadcasted_iota(jnp.int32, sn.shape, 1)
    qa = []
    for h in range(2):
        mine = (lane >= h * nblk) & (lane < (h + 1) * nblk)
        qa.append(jnp.concatenate([_head_of_pair(q, h), jnp.where(mine, sn, jnp.zeros_like(sn))], axis=1))

    def step(off, size, mask):
        k = jnp.concatenate([k_ref[pl.ds(off, size), :], oh_ref[pl.ds(off, size), :]], axis=1)
        vt = vt_ref[:, pl.ds(off, size)]

        lhs = [_pair_lhs(vt, h) for h in range(2)]

        def scores(u):
            sT = _nt_dot(k, qa[u[0]][u[1], :])
            return sT if mask is None else jnp.where(mask[:, u[1]], sT, NEG)

        _staggered(_pair_units(tq), scores,
                   lambda u, sT: _flash_step(sT, lhs[u[0]], m_ref, acc_ref, u[0], u[1]))

    _causal_sweep(i, tq, tk, step)
    o_ref[...] = _pair_finish(acc_ref).T.astype(o_ref.dtype)


def _moba_attention(qk_rot, vt, onehot2, selneg, B, S):
    T = qk_rot.shape[0]
    tq = min(PAIR_TILE, S)
    tk = min(KV_TILE, S)
    nq = S // tq
    npair = N_HEADS // 2
    ncol = D_MODEL // LANES
    nblk = S // MOBA_BLOCK
    return pl.pallas_call(
        functools.partial(_moba_kernel, tq=tq, tk=tk, nblk=nblk),
        grid=(B, npair, nq),
        in_specs=[pl.BlockSpec((tq, LANES), lambda b, hp, i: (b * nq + i, hp)),
                  pl.BlockSpec((S, LANES), lambda b, hp, i: (b, ncol + hp)),
                  pl.BlockSpec((LANES, S), lambda b, hp, i: (hp, b)),
                  pl.BlockSpec((S, LANES), lambda b, hp, i: (0, 0)),
                  pl.BlockSpec((None, None, tq, LANES), lambda b, hp, i: (b, hp, i, 0))],
        out_specs=pl.BlockSpec((tq, LANES), lambda b, hp, i: (b * nq + i, hp)),
        out_shape=jax.ShapeDtypeStruct((T, D_MODEL), BF16),
        scratch_shapes=[pltpu.VMEM((2, 1, tq), F32), pltpu.VMEM((2, LANES, tq), F32)],
        compiler_params=_cparams(("parallel", "parallel", "arbitrary")), name="moba_attn")(
            qk_rot, qk_rot, vt, onehot2, selneg)


def _router_kernel(wt_ref, x_ref, rb_ref, e_ref, g_ref):
    tm = x_ref.shape[0]
    gsz = N_EXPERTS // N_GROUPS
    scores = jax.nn.sigmoid(_nt_dot(wt_ref[...], x_ref[...], HIGHEST))
    biased = scores + rb_ref[...]
    member = lax.broadcasted_iota(jnp.int32, (gsz, tm), 0).astype(F32)
    gscore = []
    for g in range(N_GROUPS):
        v = biased[g * gsz:(g + 1) * gsz, :]
        m1 = jnp.max(v, axis=0, keepdims=True)
        i1 = jnp.min(jnp.where(v == m1, member, float(gsz)), axis=0, keepdims=True)
        m2 = jnp.max(jnp.where(member == i1, -jnp.inf, v), axis=0, keepdims=True)
        gscore.append(m1 + m2)
    gsel = _topk_rows(jnp.concatenate(gscore, axis=0), TOPK_GROUPS)
    emask = jnp.concatenate([jnp.broadcast_to(gsel[g:g + 1, :], (gsz, tm)) for g in range(N_GROUPS)], axis=0)
    work = jnp.where(emask > 0.0, biased, NEG)
    erow = lax.broadcasted_iota(jnp.int32, (N_EXPERTS, tm), 0).astype(F32)
    idxs, vals = [], []
    for _ in range(TOP_K):
        m = jnp.max(work, axis=0, keepdims=True)
        idx = jnp.min(jnp.where(work == m, erow, float(N_EXPERTS)), axis=0, keepdims=True)
        pick = erow == idx
        idxs.append(idx)
        vals.append(jnp.sum(jnp.where(pick, scores, 0.0), axis=0, keepdims=True))
        work = jnp.where(pick, -jnp.inf, work)
    gw = jnp.concatenate(vals, axis=0)
    e_ref[...] = jnp.concatenate(idxs, axis=0).astype(jnp.int32)
    g_ref[...] = gw / jnp.sum(gw, axis=0, keepdims=True) * ROUTED_SCALE


def _router(x1, router_w, router_b):
    T, D = x1.shape
    tm = min(ROW_TILE, T)
    return pl.pallas_call(
        _router_kernel, grid=(T // tm,),
        in_specs=[pl.BlockSpec((N_EXPERTS, D), lambda i: (0, 0)), pl.BlockSpec((tm, D), lambda i: (i, 0)),
                  pl.BlockSpec((N_EXPERTS, 1), lambda i: (0, 0))],
        out_specs=[pl.BlockSpec((TOP_K, tm), lambda i: (0, i))] * 2,
        out_shape=[jax.ShapeDtypeStruct((TOP_K, T), jnp.int32), jax.ShapeDtypeStruct((TOP_K, T), F32)],
        compiler_params=_cparams(("parallel",)), name="moe_router")(
            router_w.T, x1, router_b.reshape(N_EXPERTS, 1))


def _rank_kernel(e_ref, tri_ref, rank_ref, cnt_ref, carry_ref):
    @pl.when(pl.program_id(0) == 0)
    def _():
        carry_ref[...] = jnp.zeros(carry_ref.shape, F32)

    tm = e_ref.shape[1]
    erow = lax.broadcasted_iota(jnp.int32, (N_EXPERTS, tm), 0)
    tri = tri_ref[...]
    base = carry_ref[...]
    ranks = []
    for k in range(TOP_K):
        oh = erow == e_ref[k:k + 1, :]
        ohb = jnp.where(oh, 1.0, 0.0).astype(BF16)
        incl = jnp.dot(ohb, tri, preferred_element_type=F32)
        ranks.append(jnp.sum(jnp.where(oh, base + incl - 1.0, 0.0), axis=0, keepdims=True))
        base = base + incl[:, tm - 1:tm]
    carry_ref[...] = base
    rank_ref[...] = jnp.concatenate(ranks, axis=0).astype(jnp.int32)
    cnt_ref[...] = jnp.broadcast_to(base, cnt_ref.shape)


def _expert_ranks(eidx_t):
    K, T = eidx_t.shape
    tm = min(ROW_TILE, T)
    tri = jnp.asarray(np.triu(np.ones((tm, tm), np.float32)), BF16)
    return pl.pallas_call(
        _rank_kernel, grid=(T // tm,),
        in_specs=[pl.BlockSpec((K, tm), lambda i: (0, i)), pl.BlockSpec((tm, tm), lambda i: (0, 0))],
        out_specs=[pl.BlockSpec((K, tm), lambda i: (0, i)), pl.BlockSpec((N_EXPERTS, LANES), lambda i: (0, 0))],
        out_shape=[jax.ShapeDtypeStruct((K, T), jnp.int32), jax.ShapeDtypeStruct((N_EXPERTS, LANES), F32)],
        scratch_shapes=[pltpu.VMEM((N_EXPERTS, 1), F32)],
        compiler_params=_cparams(("arbitrary",)), name="moe_rank")(eidx_t, tri)


def _dest_kernel(e_ref, rank_ref, ps_ref, d_ref):
    tm = e_ref.shape[1]
    erow = lax.broadcasted_iota(jnp.int32, (N_EXPERTS, tm), 0)
    ps = ps_ref[...]
    rows = []
    for k in range(TOP_K):
        oh = erow == e_ref[k:k + 1, :]
        rows.append(jnp.sum(jnp.where(oh, ps, 0.0), axis=0, keepdims=True))
    d_ref[...] = jnp.concatenate(rows, axis=0).astype(jnp.int32) + rank_ref[...]


def _expert_dest(eidx_t, rank_t, pstarts):
    K, T = eidx_t.shape
    tm = min(ROW_TILE, T)
    blk = pl.BlockSpec((K, tm), lambda i: (0, i))
    return pl.pallas_call(
        _dest_kernel, grid=(T // tm,),
        in_specs=[blk, blk, pl.BlockSpec((N_EXPERTS, 1), lambda i: (0, 0))],
        out_specs=blk, out_shape=jax.ShapeDtypeStruct((K, T), jnp.int32),
        compiler_params=_cparams(("parallel",)), name="moe_dest")(
            eidx_t, rank_t, pstarts.astype(F32).reshape(N_EXPERTS, 1))


def _expert_kernel(be_ref, nb_ref, x_ref, wg_ref, wu_ref, wd_ref, o_ref, wgu_ref, wdb_ref):
    b = pl.program_id(0)
    E = wd_ref.shape[0]

    @pl.when((b == 0) | (be_ref[b] != be_ref[jnp.maximum(b - 1, 0)]))
    def _():
        wgu_ref[:, :E] = wg_ref[...].astype(BF16)
        wgu_ref[:, E:] = wu_ref[...].astype(BF16)
        wdb_ref[...] = wd_ref[...].astype(BF16)

    @pl.when(b < nb_ref[0])
    def _():
        gu = jnp.dot(x_ref[...].astype(BF16), wgu_ref[...], preferred_element_type=F32)
        h = jax.nn.silu(gu[:, :E]) * gu[:, E:]
        o_ref[...] = jnp.dot(h.astype(BF16), wdb_ref[...], preferred_element_type=F32).astype(o_ref.dtype)

    @pl.when(b >= nb_ref[0])
    def _():
        o_ref[...] = jnp.zeros(o_ref.shape, o_ref.dtype)


def _expert_ffn(blk_e, nb_used, xs, wg, wu, wd, layer):
    P, D = xs.shape
    NB = P // MOE_BLOCK
    E = EXPERT_DIM
    grid_spec = pltpu.PrefetchScalarGridSpec(
        num_scalar_prefetch=2, grid=(NB,),
        in_specs=[pl.BlockSpec((MOE_BLOCK, D), lambda b, be, nb: (b, 0)),
                  pl.BlockSpec((None, None, D, E), lambda b, be, nb: (layer, be[b], 0, 0)),
                  pl.BlockSpec((None, None, D, E), lambda b, be, nb: (layer, be[b], 0, 0)),
                  pl.BlockSpec((None, None, E, D), lambda b, be, nb: (layer, be[b], 0, 0))],
        out_specs=pl.BlockSpec((MOE_BLOCK, D), lambda b, be, nb: (b, 0)),
        scratch_shapes=[pltpu.VMEM((D, 2 * E), BF16), pltpu.VMEM((E, D), BF16)])
    return pl.pallas_call(
        _expert_kernel, grid_spec=grid_spec, out_shape=jax.ShapeDtypeStruct((P, D), BF16),
        compiler_params=_cparams(("arbitrary",)), name="moe_experts")(blk_e, nb_used, xs, wg, wu, wd)


def _post_moe_kernel(x_ref, *refs):
    y_refs = refs[:TOP_K]
    gw_ref, p_ref, sg_ref, su_ref, sd_ref, g_ref, b_ref, wg_ref, wp_ref, o_ref, ob_ref = refs[TOP_K:]
    x = x_ref[...]
    xb = x.astype(BF16)
    h = jax.nn.silu(jnp.dot(xb, sg_ref[...], preferred_element_type=F32)) * jnp.dot(
        xb, su_ref[...], preferred_element_type=F32)
    ffn = jnp.dot(h.astype(BF16), sd_ref[...], preferred_element_type=F32)
    gw = gw_ref[...]
    for k in range(TOP_K):
        ffn = ffn + gw[:, k:k + 1] * y_refs[k][...].astype(F32)
    z = DN_ALPHA * x + ffn
    x2 = _layer_norm(z, g_ref[...], b_ref[...])
    gate = jax.nn.sigmoid(jnp.dot(x2.astype(BF16), wg_ref[...], preferred_element_type=F32))
    proj = jnp.dot(p_ref[...].astype(BF16), wp_ref[...], preferred_element_type=F32)
    out = x2 + gate * proj
    o_ref[...] = out
    ob_ref[...] = out.astype(BF16)


def _post_moe(x1, yg, gw, p, sg, su, sd, g, b, wgate, wproj):
    T, D = x1.shape
    tm = min(ROW_TILE // 2, T)
    nt = T // tm
    PD = p.shape[1]
    SD = sg.shape[1]
    row = pl.BlockSpec((tm, D), lambda i: (i, 0))
    vec = pl.BlockSpec((1, D), lambda i: (0, 0))
    full = lambda r, c: pl.BlockSpec((r, c), lambda i: (0, 0))
    return pl.pallas_call(
        _post_moe_kernel, grid=(T // tm,),
        in_specs=[row] + [pl.BlockSpec((tm, D), functools.partial(lambda i, k: (k * nt + i, 0), k=k))
                          for k in range(TOP_K)]
                 + [pl.BlockSpec((tm, TOP_K), lambda i: (i, 0)),
                    pl.BlockSpec((tm, PD), lambda i: (i, 0)), full(D, SD), full(D, SD),
                  full(SD, D), vec, vec, full(D, D), full(PD, D)],
        out_specs=[row, row],
        out_shape=[jax.ShapeDtypeStruct((T, D), F32), jax.ShapeDtypeStruct((T, D), BF16)],
        compiler_params=_cparams(("parallel",)), name="post_moe")(
            x1, *([yg] * TOP_K), gw, p, sg, su, sd, g.reshape(1, D), b.reshape(1, D), wgate, wproj)


def _rope_tables(positions):
    inv = 1.0 / (ROPE_THETA ** (jnp.arange(0, HEAD_DIM, 2, dtype=F32) / HEAD_DIM))
    ang = positions.astype(F32).reshape(-1)[:, None] * inv
    cos, sin = jnp.cos(ang), jnp.sin(ang)
    cosf = jnp.concatenate([cos] * (LANES // (HEAD_DIM // 2)), axis=1)
    sinf = jnp.concatenate([-sin, sin] * (LANES // HEAD_DIM), axis=1)
    return cosf, sinf


def _q_col_scale(n_q, n):
    return jnp.concatenate([jnp.full((n_q,), Q_SCALE, F32), jnp.ones((n - n_q,), F32)])


def _fox_mixer(x, xb, B, S, w_in, b_f):
    D = D_MODEL
    wb = w_in[:, :3 * D].astype(BF16)
    qk = _proj(xb, wb[:, :2 * D], col_scale=_q_col_scale(D, 2 * D))
    vt = _proj_t(wb[:, 2 * D:].T, xb)
    fl = _proj(x, _pad_cols(w_in[:, 3 * D:], LANES), out_dtype=F32, tn=LANES, precision=HIGHEST)
    cp = _fox_gate(fl, b_f, B, S)
    return [_fox_attention(qk, cp, vt, B, S)]


def _nsa_mixer(x, xb, B, S, cosf, sinf, w_in, b_gate, pe_k, pe_v, ck_w1, ck_w2, cv_w1, cv_w2):
    D, G, HD = D_MODEL, NSA_KV_GROUPS, HEAD_DIM
    HG = NSA_HEADS_PER_GROUP
    kvw = G * HD
    wb = w_in[:, :D + 6 * kvw].astype(BF16)
    q_c, q_r = _proj(xb, wb[:, :D], mode="both", cos=cosf, sin=sinf, col_scale=_q_col_scale(D, D))
    w_rot = jnp.concatenate([wb[:, D + 2 * kvw:D + 3 * kvw], wb[:, D + 4 * kvw:D + 5 * kvw]], axis=1)
    k_rot = _proj(xb, w_rot, mode="rope", cos=cosf, sin=sinf)
    kvc = _proj(xb, wb[:, D:D + 2 * kvw])
    w_v = jnp.concatenate([wb[:, D + 3 * kvw:D + 4 * kvw], wb[:, D + 5 * kvw:D + 6 * kvw]], axis=1)
    vt = _proj_t(w_v.T, xb)
    gl = _proj(x, _pad_cols(w_in[:, D + 6 * kvw:], LANES), out_dtype=F32, tn=LANES, precision=HIGHEST)
    bg = _pad_cols(b_gate.reshape(1, -1), LANES)

    def grouped(t2d):
        t = t2d.reshape(B, S, G, HD).transpose(0, 2, 1, 3)
        return jnp.concatenate([t, t], axis=-1)

    n_chunks = S // NSA_CMP_STRIDE
    n_cmp = n_chunks - NSA_CMP_LEN // NSA_CMP_STRIDE + 1
    ncp = n_chunks

    def compress(t2d, pe, w1, w2):
        ch = t2d.reshape(B, n_chunks, NSA_CMP_STRIDE, G, HD).transpose(0, 1, 3, 2, 4)
        ch = ch.reshape(B, n_chunks, G, NSA_CMP_STRIDE * HD)
        flat = jnp.concatenate([ch[:, :n_cmp], ch[:, 1:n_cmp + 1]], axis=-1)
        flat = jnp.pad(flat, ((0, 0), (0, ncp - n_cmp), (0, 0), (0, 0))).reshape(B * ncp * G, -1)
        out = _nsa_compress(flat, pe.reshape(1, -1), w1.astype(BF16), _pad_cols(w2, LANES).astype(BF16))
        return out[:, :HD].reshape(B, ncp, G, HD).astype(BF16)

    kc = compress(kvc[:, :kvw], pe_k, ck_w1, ck_w2).transpose(0, 2, 1, 3)
    kk_c = jnp.concatenate([kc, kc], axis=-1)
    vt_c = compress(kvc[:, kvw:], pe_v, cv_w1, cv_w2).transpose(0, 2, 3, 1)

    n_sel = S // NSA_SEL_LEN
    assert n_sel <= LANES
    cmp_start = np.arange(ncp) * NSA_CMP_STRIDE
    sel_start = np.arange(LANES) * NSA_SEL_LEN
    overlap = ((cmp_start[:, None] < sel_start[None, :] + NSA_SEL_LEN)
               & (cmp_start[:, None] + NSA_CMP_LEN > sel_start[None, :])
               & (np.arange(ncp)[:, None] < n_cmp) & (np.arange(LANES)[None, :] < n_sel))
    overlap_t = jnp.asarray(overlap.T, BF16)
    ex = np.zeros((3, G, LANES, HG * HD), np.float32)
    for br in range(3):
        for g in range(G):
            for hg in range(HG):
                ex[br, g, (g * HG + hg) * 3 + br, hg * HD:(hg + 1) * HD] = 1.0
    ex = jnp.asarray(ex, BF16)

    o_c, selneg = _nsa_cmp_branch(q_c, kk_c, vt_c, overlap_t, gl, bg, ex[0], B, S)
    onehot = jnp.asarray((np.arange(S)[:, None] // NSA_SEL_LEN) == np.arange(LANES)[None, :], BF16)
    kk_s = grouped(k_rot[:, :kvw])
    kk_w = grouped(k_rot[:, kvw:])
    o_s = _nsa_kv_branch("sel", q_r, kk_s, vt[:kvw], gl, bg, ex[1], B, S, onehot=onehot, selneg=selneg)
    o_w = _nsa_kv_branch("win", q_r, kk_w, vt[kvw:], gl, bg, ex[2], B, S)
    return [o_c, o_s, o_w]


def _moba_mixer(xb, B, S, cosf, sinf, w_in):
    D, H, HD = D_MODEL, N_HEADS, HEAD_DIM
    wb = w_in.astype(BF16)
    qk_rot = _proj(xb, wb[:, :2 * D], mode="rope", cos=cosf, sin=sinf, col_scale=_q_col_scale(D, 2 * D))
    vt = _proj_t(wb[:, 2 * D:].T, xb)
    nblk = S // MOBA_BLOCK
    assert S % MOBA_BLOCK == 0 and 2 * nblk <= LANES and nblk % SUBLANES == 0
    kmean = _moba_kmean(qk_rot[:, D:]).reshape(B, nblk, H // 2, 2, HD)
    km = kmean.transpose(0, 2, 3, 1, 4)
    rmat_t = jnp.zeros((B, H // 2, LANES, 2, HD), F32)
    rmat_t = rmat_t.at[:, :, :nblk, 0].set(km[:, :, 0]).at[:, :, nblk:2 * nblk, 1].set(km[:, :, 1])
    selneg = _moba_select(qk_rot, rmat_t.reshape(B, H // 2, LANES, LANES), B, S)
    blk_of = np.arange(S) // MOBA_BLOCK
    oh = np.zeros((S, LANES), np.float32)
    oh[np.arange(S), blk_of] = 1.0
    oh[np.arange(S), nblk + blk_of] = 1.0
    return [_moba_attention(qk_rot, vt, jnp.asarray(oh, BF16), selneg, B, S)]


def _sc_invert_kernel(dest_hbm, tok_hbm, out_hbm, d_v, t_v, buf_v, *, seg, chunk, n_chunks):
    lo = pl.multiple_of((lax.axis_index("s") * SC_CORES + lax.axis_index("c")) * seg, SC_LANES)

    @pl.loop(0, seg // SC_LANES)
    def _(j):
        buf_v[pl.ds(j * SC_LANES, SC_LANES)] = jnp.zeros((SC_LANES,), jnp.int32)

    @pl.loop(0, n_chunks)
    def _(c):
        pltpu.sync_copy(dest_hbm.at[pl.ds(c * chunk, chunk)], d_v)
        pltpu.sync_copy(tok_hbm.at[pl.ds(c * chunk, chunk)], t_v)

        @pl.loop(0, chunk // SC_LANES)
        def _(j):
            d = d_v[pl.ds(j * SC_LANES, SC_LANES)] - lo
            mine = (d >= 0) & (d < seg)
            plsc.store_scatter(buf_v, [jnp.where(mine, d, 0)], t_v[pl.ds(j * SC_LANES, SC_LANES)], mask=mine)

    pltpu.sync_copy(buf_v, out_hbm.at[pl.ds(lo, seg)])


def _invert_dest(dest, tok, P):
    R = dest.shape[0]
    n_workers = SC_CORES * SC_SUBCORES
    seg = P // n_workers
    chunk = min(4096, R)
    assert P == seg * n_workers and seg % SC_LANES == 0 and R % chunk == 0
    mesh = plsc.VectorSubcoreMesh(core_axis_name="c", subcore_axis_name="s",
                                  num_cores=SC_CORES, num_subcores=SC_SUBCORES)
    return pl.kernel(
        functools.partial(_sc_invert_kernel, seg=seg, chunk=chunk, n_chunks=R // chunk),
        out_type=jax.ShapeDtypeStruct((P,), jnp.int32), mesh=mesh,
        scratch_types=[pltpu.VMEM((chunk,), jnp.int32), pltpu.VMEM((chunk,), jnp.int32),
                       pltpu.VMEM((seg,), jnp.int32)],
        compiler_params=pltpu.CompilerParams(needs_layout_passes=False),
        name="moe_invert")(dest, tok)


def _moe_dispatch(x1, x1b, router_w, router_b, wg, wu, wd, layer):
    T, D = x1.shape
    eidx_t, gw_t = _router(x1, router_w, router_b)
    rank_t, cnt = _expert_ranks(eidx_t)
    counts = cnt[:, 0].astype(jnp.int32)
    padded = (counts + MOE_BLOCK - 1) // MOE_BLOCK * MOE_BLOCK
    pends = jnp.cumsum(padded)
    pstarts = pends - padded
    R = T * TOP_K
    P = -(-(R + N_EXPERTS * (MOE_BLOCK - 1)) // MOE_BLOCK) * MOE_BLOCK
    NB = P // MOE_BLOCK
    blk_start = jnp.arange(NB, dtype=jnp.int32) * MOE_BLOCK
    blk_e = jnp.minimum(jnp.sum(pends[None, :] <= blk_start[:, None], axis=1), N_EXPERTS - 1).astype(jnp.int32)
    nb_used = (pends[-1] // MOE_BLOCK).astype(jnp.int32).reshape(1)
    dest_t = _expert_dest(eidx_t, rank_t, pstarts)

    tok = jnp.broadcast_to(jnp.arange(T, dtype=jnp.int32)[None, :], (TOP_K, T))
    buf_tok = _invert_dest(dest_t.reshape(-1), tok.reshape(-1), P)
    xs = x1[buf_tok]
    yb = _expert_ffn(blk_e, nb_used, xs, wg, wu, wd, layer)
    yg = yb[dest_t.reshape(-1)]
    return yg, gw_t.T


def kernel(x, p, positions, fox_w_in, fox_b_f, fox_w_out, nsa_w_in, nsa_b_gate, nsa_pe_k, nsa_pe_v,
           nsa_cmp_k_w1, nsa_cmp_k_w2, nsa_cmp_v_w1, nsa_cmp_v_w2, nsa_w_out, moba_w_in, moba_w_out,
           ln1_g, ln1_b, router_w, router_b, exp_w_gate, exp_w_up, exp_w_down,
           sh_w_gate, sh_w_up, sh_w_down, ln2_g, ln2_b, ple_w_gate, ple_w_proj):
    B, S, D = x.shape
    T = B * S
    depth = p.shape[0]
    cosf, sinf = _rope_tables(positions)
    xt = x.reshape(T, D)
    xtb = xt.astype(BF16)
    for i in range(depth):
        kind, j = i % N_MIXERS, i // N_MIXERS
        if kind == 0:
            o_list = _fox_mixer(xt, xtb, B, S, fox_w_in[j], fox_b_f[j])
            w_out = fox_w_out[j]
        elif kind == 1:
            o_list = _nsa_mixer(xt, xtb, B, S, cosf, sinf, nsa_w_in[j], nsa_b_gate[j], nsa_pe_k[j], nsa_pe_v[j],
                                nsa_cmp_k_w1[j], nsa_cmp_k_w2[j], nsa_cmp_v_w1[j], nsa_cmp_v_w2[j])
            w_out = nsa_w_out[j]
        else:
            o_list = _moba_mixer(xtb, B, S, cosf, sinf, moba_w_in[j])
            w_out = moba_w_out[j]
        x1, x1b = _outproj_ln(o_list, w_out.astype(BF16), xt, ln1_g[i], ln1_b[i])
        yg, gw = _moe_dispatch(x1, x1b, router_w[i], router_b[i], exp_w_gate, exp_w_up, exp_w_down, i)
        xt, xtb = _post_moe(x1, yg, gw, p[i].reshape(T, -1), sh_w_gate[i].astype(BF16), sh_w_up[i].astype(BF16),
                            sh_w_down[i].astype(BF16), ln2_g[i], ln2_b[i],
                            ple_w_gate[i].astype(BF16), ple_w_proj[i].astype(BF16))
    return xt.reshape(B, S, D)
```

```python
import functools

import jax
import jax.numpy as jnp
import numpy as np
from jax import lax
from jax.experimental import pallas as pl
from jax.experimental.pallas import tpu as pltpu
from jax.experimental.pallas import tpu_sc as plsc

D_MODEL = 1024
DEPTH = 4
N_HEADS = 16
HEAD_DIM = 64
ATTN_SCALE = HEAD_DIM ** -0.5
LOG2E = 1.4426950408889634
Q_SCALE = ATTN_SCALE * LOG2E
ROPE_THETA = 10000.0
N_MIXERS = 3

NSA_KV_GROUPS = 4
NSA_HEADS_PER_GROUP = N_HEADS // NSA_KV_GROUPS
NSA_CMP_LEN = 32
NSA_CMP_STRIDE = 16
NSA_SEL_LEN = 64
NSA_SEL_TOPN = 16
NSA_WINDOW = 512
NSA_FORCE_BONUS = 1e4

MOBA_BLOCK = 256
MOBA_TOPK = 3

N_EXPERTS = 64
EXPERT_DIM = 256
TOP_K = 8
N_GROUPS = 8
TOPK_GROUPS = 4
ROUTED_SCALE = 2.5
MOE_BLOCK = 256

DN_ALPHA = (2 * DEPTH) ** 0.25
LN_EPS = 1e-5
NEG = -1e30

SC_CORES = 2
SC_SUBCORES = 16
SC_LANES = 16
SC_ROW_WORDS = 256
SC_WINDOW = 128
LANES = 128
SUBLANES = 8
ATTN_TILE = 512
PAIR_TILE = 1024
KV_TILE = 1024
ROW_TILE = 512
VMEM_LIMIT = 48 * 1024 * 1024
C_SPLIT = 3
ONES_ROWS = 16

F32 = jnp.float32
BF16 = jnp.bfloat16
HIGHEST = lax.Precision.HIGHEST


def _cparams(sem):
    return pltpu.CompilerParams(dimension_semantics=sem, vmem_limit_bytes=VMEM_LIMIT)


def _log2(n):
    assert n & (n - 1) == 0
    return n.bit_length() - 1


def _nt_dot(a, b, precision=None):
    return lax.dot_general(a, b, (((1,), (1,)), ((), ())), preferred_element_type=F32,
                           precision=precision)


def _topk_rows(work, n):
    rows = lax.broadcasted_iota(jnp.int32, work.shape, 0).astype(F32)
    chosen = jnp.zeros(work.shape, F32)
    for _ in range(n):
        m = jnp.max(work, axis=0, keepdims=True)
        idx = jnp.min(jnp.where(work == m, rows, float(work.shape[0])), axis=0, keepdims=True)
        pick = rows == idx
        chosen = jnp.where(pick, 1.0, chosen)
        work = jnp.where(pick, -jnp.inf, work)
    return chosen


def _proj_kernel(*refs, mode, precision, scaled):
    refs = list(refs)
    x_ref, w_ref = refs[:2]
    del refs[:2]
    scale_ref = refs.pop(0) if scaled else None
    if mode == "none":
        (o_ref,) = refs
    elif mode == "rope":
        cos_ref, sin_ref, r_ref = refs
    else:
        cos_ref, sin_ref, o_ref, r_ref = refs
    x = x_ref[...]
    w = w_ref[...]
    if x.dtype != w.dtype:
        x = x.astype(w.dtype)
    acc = jnp.dot(x, w, preferred_element_type=F32, precision=precision)
    if scaled:
        acc = acc * scale_ref[...]
    if mode in ("none", "both"):
        o_ref[...] = acc.astype(o_ref.dtype)
    if mode in ("rope", "both"):
        tn = acc.shape[1]
        rep = tn // LANES
        cosf = jnp.concatenate([cos_ref[...]] * rep, axis=1)
        sinf = jnp.concatenate([sin_ref[...]] * rep, axis=1)
        lane = lax.broadcasted_iota(jnp.int32, acc.shape, 1)
        first_half = (lane & (HEAD_DIM // 2)) == 0
        swapped = jnp.where(first_half, pltpu.roll(acc, tn - HEAD_DIM // 2, 1),
                            pltpu.roll(acc, HEAD_DIM // 2, 1))
        r_ref[...] = (acc * cosf + swapped * sinf).astype(r_ref.dtype)


def _proj(x, w, *, mode="none", cos=None, sin=None, out_dtype=BF16, tn=512, precision=None, col_scale=None):
    T, K = x.shape
    N = w.shape[1]
    tm = min(ROW_TILE, T)
    tn = min(tn, N)
    assert T % tm == 0 and N % tn == 0
    grid = (T // tm, N // tn)
    in_specs = [pl.BlockSpec((tm, K), lambda i, j: (i, 0)),
                pl.BlockSpec((K, tn), lambda i, j: (0, j))]
    args = [x, w]
    if col_scale is not None:
        in_specs.append(pl.BlockSpec((1, tn), lambda i, j: (0, j)))
        args.append(col_scale.reshape(1, N).astype(F32))
    if mode != "none":
        in_specs += [pl.BlockSpec((tm, LANES), lambda i, j: (i, 0))] * 2
        args += [cos, sin]
    o_spec = pl.BlockSpec((tm, tn), lambda i, j: (i, j))
    o_shape = jax.ShapeDtypeStruct((T, N), out_dtype)
    if mode == "both":
        out_specs, out_shape = [o_spec, o_spec], [o_shape, o_shape]
    else:
        out_specs, out_shape = o_spec, o_shape
    return pl.pallas_call(
        functools.partial(_proj_kernel, mode=mode, precision=precision, scaled=col_scale is not None),
        grid=grid, in_specs=in_specs, out_specs=out_specs, out_shape=out_shape,
        compiler_params=_cparams(("parallel", "parallel")), name=f"proj_{mode}")(*args)


def _proj_t_kernel(wt_ref, x_ref, o_ref, *, precision):
    wt = wt_ref[...]
    x = x_ref[...]
    if x.dtype != wt.dtype:
        x = x.astype(wt.dtype)
    o_ref[...] = _nt_dot(wt, x, precision).astype(o_ref.dtype)


def _proj_t(wt, x, *, out_dtype=BF16, precision=None):
    N, K = wt.shape
    T = x.shape[0]
    tm = min(ROW_TILE, T)
    tn = min(512, N)
    assert T % tm == 0 and N % tn == 0
    return pl.pallas_call(
        functools.partial(_proj_t_kernel, precision=precision),
        grid=(T // tm, N // tn),
        in_specs=[pl.BlockSpec((tn, K), lambda i, j: (j, 0)), pl.BlockSpec((tm, K), lambda i, j: (i, 0))],
        out_specs=pl.BlockSpec((tn, tm), lambda i, j: (j, i)),
        out_shape=jax.ShapeDtypeStruct((N, T), out_dtype),
        compiler_params=_cparams(("parallel", "parallel")), name="proj_t")(wt, x)


def _pack_rows(v):
    n = v.shape[1] // 2
    lo = pltpu.bitcast(v[:, :n].astype(BF16).astype(F32), jnp.int32)
    hi = pltpu.bitcast(v[:, n:].astype(BF16).astype(F32), jnp.int32)
    return (hi & jnp.int32(-65536)) | lax.shift_right_logical(lo, jnp.int32(16))


def _unpack_rows(w):
    lo = pltpu.bitcast(lax.shift_left(w, jnp.int32(16)), F32)
    hi = pltpu.bitcast(w & jnp.int32(-65536), F32)
    return jnp.concatenate([lo, hi], axis=1)


def _pad_cols(w, n):
    return jnp.pad(w, ((0, 0), (0, n - w.shape[1])))


def _layer_norm(z, g, b):
    mu = jnp.mean(z, axis=-1, keepdims=True)
    zc = z - mu
    var = jnp.mean(zc * zc, axis=-1, keepdims=True)
    return zc * lax.rsqrt(var + LN_EPS) * g + b


def _outproj_ln_kernel(*refs, n_o):
    o_refs = refs[:n_o]
    w_ref, x_ref, g_ref, b_ref, y_ref, yp_ref = refs[n_o:]
    if n_o == 1:
        o = o_refs[0][...]
    else:
        o = o_refs[0][...].astype(F32)
        for r in o_refs[1:]:
            o = o + r[...].astype(F32)
        o = o.astype(BF16)
    mix = jnp.dot(o, w_ref[...], preferred_element_type=F32)
    z = DN_ALPHA * x_ref[...] + mix
    y = _layer_norm(z, g_ref[...], b_ref[...])
    y_ref[...] = y
    yp_ref[...] = _pack_rows(y)


def _outproj_ln(o_list, w, x, g, b):
    T, D = x.shape
    tm = min(ROW_TILE, T)
    row = pl.BlockSpec((tm, D), lambda i: (i, 0))
    vec = pl.BlockSpec((1, D), lambda i: (0, 0))
    return pl.pallas_call(
        functools.partial(_outproj_ln_kernel, n_o=len(o_list)),
        grid=(T // tm,),
        in_specs=[row] * len(o_list) + [pl.BlockSpec((D, D), lambda i: (0, 0)), row, vec, vec],
        out_specs=[row, pl.BlockSpec((tm, D // 2), lambda i: (i, 0))],
        out_shape=[jax.ShapeDtypeStruct((T, D), F32), jax.ShapeDtypeStruct((T, D // 2), jnp.int32)],
        compiler_params=_cparams(("parallel",)), name="outproj_ln")(
            *o_list, w, x, g.reshape(1, D), b.reshape(1, D))


def _flash_init(m_ref, acc_ref):
    m_ref[...] = jnp.full(m_ref.shape, NEG, F32)
    acc_ref[...] = jnp.zeros(acc_ref.shape, F32)


def _flash_step(sT, lhs, m_ref, acc_ref, h, cols=slice(None)):
    m_prev = m_ref[h, :, cols]
    m_new = jnp.maximum(m_prev, jnp.max(sT, axis=0, keepdims=True))
    p = jnp.exp2(sT - m_new)
    alpha = jnp.exp2(m_prev - m_new)
    acc_ref[h, :, cols] = alpha * acc_ref[h, :, cols] + jnp.dot(lhs, p.astype(BF16), preferred_element_type=F32)
    m_ref[h, :, cols] = m_new


def _causal_t(tk, tq, shift=0, strict_lower=False):
    key = lax.broadcasted_iota(jnp.int32, (tk, tq), 0)
    qry = lax.broadcasted_iota(jnp.int32, (tk, tq), 1)
    return (key > qry) if strict_lower else (key <= qry + shift)


def _causal_sweep(i, tq, tk, step):
    r = tk // tq
    assert tk == r * tq
    n_full = i >> _log2(r)

    def body(j, carry):
        step(pl.multiple_of(j * tk, tk), tk, None)
        return carry

    lax.fori_loop(0, n_full, body, 0)
    if r == 1:
        step(pl.multiple_of(i * tq, tq), tq, _causal_t(tq, tq))
    else:
        for rem in range(r):
            @pl.when(i - n_full * r == rem)
            def _(rem=rem):
                size = (rem + 1) * tq
                step(pl.multiple_of(n_full * tk, tk), size, _causal_t(size, tq, shift=rem * tq))


def _staggered(units, scores, update):
    s = [scores(units[0])]
    for n, u in enumerate(units):
        if n + 1 < len(units):
            s.append(scores(units[n + 1]))
        update(u, s[n])


def _pair_units(tq):
    w = min(ATTN_TILE, tq)
    return [(h, slice(c * w, (c + 1) * w)) for h in range(2) for c in range(tq // w)]


def _head_of_pair(q_pair, which):
    lane = lax.broadcasted_iota(jnp.int32, q_pair.shape, 1)
    keep = (lane < HEAD_DIM) if which == 0 else (lane >= HEAD_DIM)
    return jnp.where(keep, q_pair, jnp.zeros_like(q_pair))


def _pair_rows(a_top, a_bot):
    row = lax.broadcasted_iota(jnp.int32, a_top.shape, 0)
    return jnp.where(row < HEAD_DIM, a_top, a_bot)


def _pair_lhs(vt, which):
    row = lax.broadcasted_iota(jnp.int32, vt.shape, 0)
    keep = (row < HEAD_DIM) if which == 0 else (row >= HEAD_DIM)
    return jnp.where(keep, vt, jnp.ones_like(vt))


def _pair_finish(acc_ref):
    a0, a1 = acc_ref[0], acc_ref[1]
    return _pair_rows(a0 / a0[HEAD_DIM:HEAD_DIM + 1, :], a1 / a1[0:1, :])


def _fox_gate_kernel(fl_ref, bf_ref, tri_ref, ex_ref, cp_ref, carry_ref):
    @pl.when(pl.program_id(1) == 0)
    def _():
        carry_ref[...] = jnp.zeros(carry_ref.shape, F32)

    z = fl_ref[...] + bf_ref[...]
    log_f = jnp.minimum(z, 0.0) - jnp.log1p(jnp.exp(-jnp.abs(z)))
    c = jnp.dot(tri_ref[...], log_f, preferred_element_type=F32, precision=HIGHEST) + carry_ref[...]
    carry_ref[...] = c[-1:, :]
    out = jnp.zeros(c.shape, F32)
    rem = c * LOG2E
    for piece in range(C_SPLIT):
        part = rem.astype(BF16)
        rem = rem - part.astype(F32)
        out = out + jnp.dot(part, ex_ref[piece], preferred_element_type=F32)
    cp_ref[...] = out.astype(BF16)


def _fox_gate(fl, b_f, B, S):
    T = fl.shape[0]
    tm = min(ROW_TILE, S)
    ns = S // tm
    tri = jnp.asarray(np.tril(np.ones((tm, tm), np.float32)))
    ex = np.zeros((C_SPLIT, LANES, LANES), np.float32)
    for h in range(N_HEADS):
        for j in range(C_SPLIT):
            ex[j, h, C_SPLIT * h + j] = 1.0
    bf = _pad_cols(b_f.reshape(1, -1), LANES)
    return pl.pallas_call(
        _fox_gate_kernel, grid=(B, ns),
        in_specs=[pl.BlockSpec((tm, LANES), lambda b, s: (b * ns + s, 0)),
                  pl.BlockSpec((1, LANES), lambda b, s: (0, 0)),
                  pl.BlockSpec((tm, tm), lambda b, s: (0, 0)),
                  pl.BlockSpec((C_SPLIT, LANES, LANES), lambda b, s: (0, 0, 0))],
        out_specs=pl.BlockSpec((tm, LANES), lambda b, s: (b * ns + s, 0)),
        out_shape=jax.ShapeDtypeStruct((T, LANES), BF16),
        scratch_shapes=[pltpu.VMEM((1, LANES), F32)],
        compiler_params=_cparams(("parallel", "arbitrary")), name="fox_gate")(
            fl, bf, tri, jnp.asarray(ex, BF16))


def _fox_kernel(q_ref, k_ref, cp_ref, vt_ref, o_ref, m_ref, acc_ref, *, tq, tk):
    hp = pl.program_id(1)
    i = pl.program_id(2)
    _flash_init(m_ref, acc_ref)
    q = q_ref[...]
    lane = lax.broadcasted_iota(jnp.int32, q.shape, 1)
    qa = []
    for h in range(2):
        lo = C_SPLIT * (2 * hp + h)
        minus_one = jnp.where((lane >= lo) & (lane < lo + C_SPLIT), -1.0, 0.0).astype(q.dtype)
        qa.append(jnp.concatenate([_head_of_pair(q, h), minus_one], axis=1))

    def step(off, size, mask):
        k = jnp.concatenate([k_ref[pl.ds(off, size), :], cp_ref[pl.ds(off, size), :]], axis=1)
        vt = vt_ref[:, pl.ds(off, size)]

        lhs = [_pair_lhs(vt, h) for h in range(2)]

        def scores(u):
            sT = _nt_dot(k, qa[u[0]][u[1], :])
            return sT if mask is None else jnp.where(mask[:, u[1]], sT, NEG)

        _staggered(_pair_units(tq), scores,
                   lambda u, sT: _flash_step(sT, lhs[u[0]], m_ref, acc_ref, u[0], u[1]))

    _causal_sweep(i, tq, tk, step)
    o_ref[...] = _pair_finish(acc_ref).T.astype(o_ref.dtype)


def _fox_attention(qk, cp, vt, B, S):
    T = qk.shape[0]
    tq = min(PAIR_TILE, S)
    tk = min(KV_TILE, S)
    nq = S // tq
    npair = N_HEADS // 2
    ncol = D_MODEL // LANES
    return pl.pallas_call(
        functools.partial(_fox_kernel, tq=tq, tk=tk),
        grid=(B, npair, nq),
        in_specs=[pl.BlockSpec((tq, LANES), lambda b, hp, i: (b * nq + i, hp)),
                  pl.BlockSpec((S, LANES), lambda b, hp, i: (b, ncol + hp)),
                  pl.BlockSpec((S, LANES), lambda b, hp, i: (b, 0)),
                  pl.BlockSpec((LANES, S), lambda b, hp, i: (hp, b))],
        out_specs=pl.BlockSpec((tq, LANES), lambda b, hp, i: (b * nq + i, hp)),
        out_shape=jax.ShapeDtypeStruct((T, D_MODEL), BF16),
        scratch_shapes=[pltpu.VMEM((2, 1, tq), F32), pltpu.VMEM((2, LANES, tq), F32)],
        compiler_params=_cparams(("parallel", "parallel", "arbitrary")), name="fox_attn")(
            qk, qk, cp, vt)


def _gate_expand(gl_ref, bg_ref, ex_ref):
    sig = jax.nn.sigmoid(gl_ref[...] + bg_ref[...])
    hi = sig.astype(BF16)
    lo = (sig - hi.astype(F32)).astype(BF16)
    ex = ex_ref[...]
    return jnp.dot(hi, ex, preferred_element_type=F32) + jnp.dot(lo, ex, preferred_element_type=F32)


def _nsa_compress_kernel(f_ref, pe_ref, w1_ref, w2_ref, o_ref):
    blk = (f_ref[...].astype(F32) + pe_ref[...]).astype(BF16)
    h = jax.nn.gelu(jnp.dot(blk, w1_ref[...], preferred_element_type=F32))
    o_ref[...] = jnp.dot(h.astype(BF16), w2_ref[...], preferred_element_type=F32)


def _nsa_compress(flat, pe_flat, w1, w2p):
    M, K = flat.shape
    tm = min(ROW_TILE, M)
    Hc = w1.shape[1]
    return pl.pallas_call(
        _nsa_compress_kernel, grid=(M // tm,),
        in_specs=[pl.BlockSpec((tm, K), lambda i: (i, 0)), pl.BlockSpec((1, K), lambda i: (0, 0)),
                  pl.BlockSpec((K, Hc), lambda i: (0, 0)), pl.BlockSpec((Hc, LANES), lambda i: (0, 0))],
        out_specs=pl.BlockSpec((tm, LANES), lambda i: (i, 0)),
        out_shape=jax.ShapeDtypeStruct((M, LANES), F32),
        compiler_params=_cparams(("parallel",)), name="nsa_compress")(flat, pe_flat, w1, w2p)


def _nsa_cmp_kernel(q_ref, kk_ref, vt_ref, ovt_ref, gl_ref, bg_ref, ex_ref, o_ref, sn_ref, *, tq, ncp, nselp):
    i = pl.program_id(2)
    q = q_ref[...]
    kk = kk_ref[...]
    vt = vt_ref[...]
    t = i * tq + lax.broadcasted_iota(jnp.int32, (ncp, tq), 1)
    n = lax.broadcasted_iota(jnp.int32, (ncp, tq), 0)
    valid = n * NSA_CMP_STRIDE + (NSA_CMP_LEN - 1) <= t
    psum = jnp.zeros((ncp, tq), F32)
    outs = []
    for hg in range(NSA_HEADS_PER_GROUP):
        qh = _head_of_pair(q[:, LANES * (hg // 2):LANES * (hg // 2 + 1)], hg % 2)
        sT = jnp.where(valid, _nt_dot(kk, qh), NEG)
        m = jnp.max(sT, axis=0, keepdims=True)
        e = jnp.where(valid, jnp.exp2(sT - m), 0.0)
        p = e / jnp.maximum(jnp.sum(e, axis=0, keepdims=True), 1e-30)
        psum = psum + p
        outs.append(jnp.dot(vt, p.astype(BF16), preferred_element_type=F32))
    o = jnp.concatenate(outs, axis=0).T
    o_ref[...] = (o * _gate_expand(gl_ref, bg_ref, ex_ref)).astype(o_ref.dtype)
    hi = psum.astype(BF16)
    lo = (psum - hi.astype(F32)).astype(BF16)
    ovt = ovt_ref[...]
    imp = jnp.dot(ovt, hi, preferred_element_type=F32) + jnp.dot(ovt, lo, preferred_element_type=F32)
    jblk = lax.broadcasted_iota(jnp.int32, (nselp, tq), 0)
    cur = (i * tq + lax.broadcasted_iota(jnp.int32, (nselp, tq), 1)) >> _log2(NSA_SEL_LEN)
    forced = (jblk == 0) | (jblk == cur) | (jblk == cur - 1)
    pri = jnp.where(forced, imp + NSA_FORCE_BONUS, imp)
    past = jblk <= cur
    pri = jnp.where(past, pri, NEG)
    chosen = _topk_rows(pri, NSA_SEL_TOPN)
    sn = jnp.where((chosen > 0.0) & past, 0.0, NEG)
    sn_ref[...] = sn.T.astype(sn_ref.dtype)


def _nsa_cmp_branch(q_c, kk_c, vt_c, overlap_t, gl, bg, ex, B, S):
    T = q_c.shape[0]
    G = NSA_KV_GROUPS
    tq = min(ATTN_TILE, S)
    nq = S // tq
    ncp = kk_c.shape[2]
    nselp = overlap_t.shape[0]
    return pl.pallas_call(
        functools.partial(_nsa_cmp_kernel, tq=tq, ncp=ncp, nselp=nselp),
        grid=(B, G, nq),
        in_specs=[pl.BlockSpec((tq, 2 * LANES), lambda b, g, i: (b * nq + i, g)),
                  pl.BlockSpec((None, None, ncp, LANES), lambda b, g, i: (b, g, 0, 0)),
                  pl.BlockSpec((None, None, HEAD_DIM, ncp), lambda b, g, i: (b, g, 0, 0)),
                  pl.BlockSpec((nselp, ncp), lambda b, g, i: (0, 0)),
                  pl.BlockSpec((tq, LANES), lambda b, g, i: (b * nq + i, 0)),
                  pl.BlockSpec((1, LANES), lambda b, g, i: (0, 0)),
                  pl.BlockSpec((None, LANES, 2 * LANES), lambda b, g, i: (g, 0, 0))],
        out_specs=[pl.BlockSpec((tq, 2 * LANES), lambda b, g, i: (b * nq + i, g)),
                   pl.BlockSpec((None, None, tq, nselp), lambda b, g, i: (b, g, i, 0))],
        out_shape=[jax.ShapeDtypeStruct((T, D_MODEL), BF16),
                   jax.ShapeDtypeStruct((B, G, S, nselp), BF16)],
        compiler_params=_cparams(("parallel", "parallel", "parallel")), name="nsa_cmp")(
            q_c, kk_c, vt_c, overlap_t, gl, bg, ex)


def _nsa_kv_kernel(*refs, mode, tq, tk):
    if mode == "sel":
        q_ref, kk_ref, vt_ref, oh_ref, sn_ref, gl_ref, bg_ref, ex_ref, o_ref, m_ref, acc_ref = refs
    else:
        q_ref, kk_ref, vt_ref, gl_ref, bg_ref, ex_ref, o_ref, m_ref, acc_ref = refs
    i = pl.program_id(2)
    _flash_init(m_ref, acc_ref)
    q = q_ref[...]
    qh = []
    for hg in range(NSA_HEADS_PER_GROUP):
        qq = _head_of_pair(q[:, LANES * (hg // 2):LANES * (hg // 2 + 1)], hg % 2)
        if mode == "sel":
            qq = jnp.concatenate([qq, sn_ref[...]], axis=1)
        qh.append(qq)

    def step(off, size, mask):
        k = kk_ref[pl.ds(off, size), :]
        vt = vt_ref[:, pl.ds(off, size)]
        lhs = jnp.concatenate([vt, jnp.ones((ONES_ROWS, size), vt.dtype)], axis=0)
        if mode == "sel":
            k = jnp.concatenate([k, oh_ref[pl.ds(off, size), :]], axis=1)

        def scores(hg):
            sT = _nt_dot(k, qh[hg])
            return sT if mask is None else jnp.where(mask, sT, NEG)

        _staggered(list(range(NSA_HEADS_PER_GROUP)), scores,
                   lambda hg, sT: _flash_step(sT, lhs, m_ref, acc_ref, hg))

    if mode == "sel":
        _causal_sweep(i, tq, tk, step)
    else:
        @pl.when(i > 0)
        def _():
            step(pl.multiple_of((i - 1) * tq, tq), tq, _causal_t(tq, tq, strict_lower=True))
        step(pl.multiple_of(i * tq, tq), tq, _causal_t(tq, tq))
    outs = []
    for hg in range(NSA_HEADS_PER_GROUP):
        a = acc_ref[hg]
        outs.append(a[:HEAD_DIM] / a[HEAD_DIM:HEAD_DIM + 1, :])
    o_ref[...] = (jnp.concatenate(outs, axis=0).T * _gate_expand(gl_ref, bg_ref, ex_ref)).astype(o_ref.dtype)


def _nsa_kv_branch(mode, q_r, kk, vt, gl, bg, ex, B, S, onehot=None, selneg=None):
    T = q_r.shape[0]
    G = NSA_KV_GROUPS
    tq = min(ATTN_TILE, S)
    tk = min(KV_TILE, S)
    assert NSA_WINDOW == tq or mode == "sel"
    nq = S // tq
    in_specs = [pl.BlockSpec((tq, 2 * LANES), lambda b, g, i: (b * nq + i, g)),
                pl.BlockSpec((None, None, S, LANES), lambda b, g, i: (b, g, 0, 0)),
                pl.BlockSpec((HEAD_DIM, S), lambda b, g, i: (g, b))]
    args = [q_r, kk, vt]
    if mode == "sel":
        in_specs += [pl.BlockSpec((S, LANES), lambda b, g, i: (0, 0)),
                     pl.BlockSpec((None, None, tq, LANES), lambda b, g, i: (b, g, i, 0))]
        args += [onehot, selneg]
    in_specs += [pl.BlockSpec((tq, LANES), lambda b, g, i: (b * nq + i, 0)),
                 pl.BlockSpec((1, LANES), lambda b, g, i: (0, 0)),
                 pl.BlockSpec((None, LANES, 2 * LANES), lambda b, g, i: (g, 0, 0))]
    args += [gl, bg, ex]
    nh = NSA_HEADS_PER_GROUP
    return pl.pallas_call(
        functools.partial(_nsa_kv_kernel, mode=mode, tq=tq, tk=tk),
        grid=(B, G, nq), in_specs=in_specs,
        out_specs=pl.BlockSpec((tq, 2 * LANES), lambda b, g, i: (b * nq + i, g)),
        out_shape=jax.ShapeDtypeStruct((T, D_MODEL), BF16),
        scratch_shapes=[pltpu.VMEM((nh, 1, tq), F32), pltpu.VMEM((nh, HEAD_DIM + ONES_ROWS, tq), F32)],
        compiler_params=_cparams(("parallel", "parallel", "arbitrary")), name=f"nsa_{mode}")(*args)


def _kmean_kernel(k_ref, o_ref, *, nblk):
    k = k_ref[...].astype(F32)
    o_ref[...] = jnp.mean(k.reshape(nblk, MOBA_BLOCK, k.shape[1]), axis=1)


def _moba_kmean(k_rot):
    T, D = k_rot.shape
    nblk = 8
    rows = nblk * MOBA_BLOCK
    assert T % rows == 0
    return pl.pallas_call(
        functools.partial(_kmean_kernel, nblk=nblk), grid=(T // rows,),
        in_specs=[pl.BlockSpec((rows, D), lambda i: (i, 0))],
        out_specs=pl.BlockSpec((nblk, D), lambda i: (i, 0)),
        out_shape=jax.ShapeDtypeStruct((T // MOBA_BLOCK, D), F32),
        compiler_params=_cparams(("parallel",)), name="moba_kmean")(k_rot)


def _moba_select_kernel(q_ref, r_ref, sn_ref, *, tq, nblk):
    i = pl.program_id(2)
    gsT = _nt_dot(r_ref[...], q_ref[...].astype(F32), HIGHEST)
    jblk = lax.broadcasted_iota(jnp.int32, (nblk, tq), 0)
    cb = (i * tq + lax.broadcasted_iota(jnp.int32, (nblk, tq), 1)) >> _log2(MOBA_BLOCK)
    past = jblk < cb
    parts = []
    for h in range(2):
        gs = jnp.where(past, gsT[h * nblk:(h + 1) * nblk, :], NEG)
        chosen = _topk_rows(gs, min(MOBA_TOPK, nblk))
        parts.append(jnp.where(((chosen > 0.0) & past) | (jblk == cb), 0.0, NEG))
    parts.append(jnp.zeros((LANES - 2 * nblk, tq), F32))
    sn_ref[...] = jnp.concatenate(parts, axis=0).T.astype(sn_ref.dtype)


def _moba_select(q_rot, rmat_t, B, S):
    tq = min(ATTN_TILE, S)
    nq = S // tq
    npair = N_HEADS // 2
    nblk = S // MOBA_BLOCK
    return pl.pallas_call(
        functools.partial(_moba_select_kernel, tq=tq, nblk=nblk), grid=(B, npair, nq),
        in_specs=[pl.BlockSpec((tq, LANES), lambda b, hp, i: (b * nq + i, hp)),
                  pl.BlockSpec((None, None, LANES, LANES), lambda b, hp, i: (b, hp, 0, 0))],
        out_specs=pl.BlockSpec((None, None, tq, LANES), lambda b, hp, i: (b, hp, i, 0)),
        out_shape=jax.ShapeDtypeStruct((B, npair, S, LANES), BF16),
        compiler_params=_cparams(("parallel", "parallel", "parallel")), name="moba_select")(q_rot, rmat_t)


def _moba_kernel(q_ref, k_ref, vt_ref, oh_ref, sn_ref, o_ref, m_ref, acc_ref, *, tq, tk, nblk):
    i = pl.program_id(2)
    _flash_init(m_ref, acc_ref)
    q = q_ref[...]
    sn = sn_ref[...]
    lane = lax.broadcasted_iota(jnp.int32, sn.shape, 1)
    qa = []
    for h in range(2):
        mine = (lane >= h * nblk) & (lane < (h + 1) * nblk)
        qa.append(jnp.concatenate([_head_of_pair(q, h), jnp.where(mine, sn, jnp.zeros_like(sn))], axis=1))

    def step(off, size, mask):
        k = jnp.concatenate([k_ref[pl.ds(off, size), :], oh_ref[pl.ds(off, size), :]], axis=1)
        vt = vt_ref[:, pl.ds(off, size)]

        lhs = [_pair_lhs(vt, h) for h in range(2)]

        def scores(u):
            sT = _nt_dot(k, qa[u[0]][u[1], :])
            return sT if mask is None else jnp.where(mask[:, u[1]], sT, NEG)

        _staggered(_pair_units(tq), scores,
                   lambda u, sT: _flash_step(sT, lhs[u[0]], m_ref, acc_ref, u[0], u[1]))

    _causal_sweep(i, tq, tk, step)
    o_ref[...] = _pair_finish(acc_ref).T.astype(o_ref.dtype)


def _moba_attention(qk_rot, vt, onehot2, selneg, B, S):
    T = qk_rot.shape[0]
    tq = min(PAIR_TILE, S)
    tk = min(KV_TILE, S)
    nq = S // tq
    npair = N_HEADS // 2
    ncol = D_MODEL // LANES
    nblk = S // MOBA_BLOCK
    return pl.pallas_call(
        functools.partial(_moba_kernel, tq=tq, tk=tk, nblk=nblk),
        grid=(B, npair, nq),
        in_specs=[pl.BlockSpec((tq, LANES), lambda b, hp, i: (b * nq + i, hp)),
                  pl.BlockSpec((S, LANES), lambda b, hp, i: (b, ncol + hp)),
                  pl.BlockSpec((LANES, S), lambda b, hp, i: (hp, b)),
                  pl.BlockSpec((S, LANES), lambda b, hp, i: (0, 0)),
                  pl.BlockSpec((None, None, tq, LANES), lambda b, hp, i: (b, hp, i, 0))],
        out_specs=pl.BlockSpec((tq, LANES), lambda b, hp, i: (b * nq + i, hp)),
        out_shape=jax.ShapeDtypeStruct((T, D_MODEL), BF16),
        scratch_shapes=[pltpu.VMEM((2, 1, tq), F32), pltpu.VMEM((2, LANES, tq), F32)],
        compiler_params=_cparams(("parallel", "parallel", "arbitrary")), name="moba_attn")(
            qk_rot, qk_rot, vt, onehot2, selneg)


def _router_kernel(wt_ref, x_ref, rb_ref, e_ref, g_ref):
    tm = x_ref.shape[0]
    gsz = N_EXPERTS // N_GROUPS
    scores = jax.nn.sigmoid(_nt_dot(wt_ref[...], x_ref[...], HIGHEST))
    biased = scores + rb_ref[...]
    member = lax.broadcasted_iota(jnp.int32, (gsz, tm), 0).astype(F32)
    gscore = []
    for g in range(N_GROUPS):
        v = biased[g * gsz:(g + 1) * gsz, :]
        m1 = jnp.max(v, axis=0, keepdims=True)
        i1 = jnp.min(jnp.where(v == m1, member, float(gsz)), axis=0, keepdims=True)
        m2 = jnp.max(jnp.where(member == i1, -jnp.inf, v), axis=0, keepdims=True)
        gscore.append(m1 + m2)
    gsel = _topk_rows(jnp.concatenate(gscore, axis=0), TOPK_GROUPS)
    emask = jnp.concatenate([jnp.broadcast_to(gsel[g:g + 1, :], (gsz, tm)) for g in range(N_GROUPS)], axis=0)
    work = jnp.where(emask > 0.0, biased, NEG)
    erow = lax.broadcasted_iota(jnp.int32, (N_EXPERTS, tm), 0).astype(F32)
    idxs, vals = [], []
    for _ in range(TOP_K):
        m = jnp.max(work, axis=0, keepdims=True)
        idx = jnp.min(jnp.where(work == m, erow, float(N_EXPERTS)), axis=0, keepdims=True)
        pick = erow == idx
        idxs.append(idx)
        vals.append(jnp.sum(jnp.where(pick, scores, 0.0), axis=0, keepdims=True))
        work = jnp.where(pick, -jnp.inf, work)
    gw = jnp.concatenate(vals, axis=0)
    e_ref[...] = jnp.concatenate(idxs, axis=0).astype(jnp.int32)
    g_ref[...] = gw / jnp.sum(gw, axis=0, keepdims=True) * ROUTED_SCALE


def _router(x1, router_w, router_b):
    T, D = x1.shape
    tm = min(ROW_TILE, T)
    return pl.pallas_call(
        _router_kernel, grid=(T // tm,),
        in_specs=[pl.BlockSpec((N_EXPERTS, D), lambda i: (0, 0)), pl.BlockSpec((tm, D), lambda i: (i, 0)),
                  pl.BlockSpec((N_EXPERTS, 1), lambda i: (0, 0))],
        out_specs=[pl.BlockSpec((TOP_K, tm), lambda i: (0, i))] * 2,
        out_shape=[jax.ShapeDtypeStruct((TOP_K, T), jnp.int32), jax.ShapeDtypeStruct((TOP_K, T), F32)],
        compiler_params=_cparams(("parallel",)), name="moe_router")(
            router_w.T, x1, router_b.reshape(N_EXPERTS, 1))


def _rank_kernel(e_ref, tri_ref, rank_ref, cnt_ref, carry_ref):
    @pl.when(pl.program_id(0) == 0)
    def _():
        carry_ref[...] = jnp.zeros(carry_ref.shape, F32)

    tm = e_ref.shape[1]
    erow = lax.broadcasted_iota(jnp.int32, (N_EXPERTS, tm), 0)
    tri = tri_ref[...]
    base = carry_ref[...]
    ranks = []
    for k in range(TOP_K):
        oh = erow == e_ref[k:k + 1, :]
        ohb = jnp.where(oh, 1.0, 0.0).astype(BF16)
        incl = jnp.dot(ohb, tri, preferred_element_type=F32)
        ranks.append(jnp.sum(jnp.where(oh, base + incl - 1.0, 0.0), axis=0, keepdims=True))
        base = base + incl[:, tm - 1:tm]
    carry_ref[...] = base
    rank_ref[...] = jnp.concatenate(ranks, axis=0).astype(jnp.int32)
    cnt_ref[...] = jnp.broadcast_to(base, cnt_ref.shape)


def _expert_ranks(eidx_t):
    K, T = eidx_t.shape
    tm = min(ROW_TILE, T)
    tri = jnp.asarray(np.triu(np.ones((tm, tm), np.float32)), BF16)
    return pl.pallas_call(
        _rank_kernel, grid=(T // tm,),
        in_specs=[pl.BlockSpec((K, tm), lambda i: (0, i)), pl.BlockSpec((tm, tm), lambda i: (0, 0))],
        out_specs=[pl.BlockSpec((K, tm), lambda i: (0, i)), pl.BlockSpec((N_EXPERTS, LANES), lambda i: (0, 0))],
        out_shape=[jax.ShapeDtypeStruct((K, T), jnp.int32), jax.ShapeDtypeStruct((N_EXPERTS, LANES), F32)],
        scratch_shapes=[pltpu.VMEM((N_EXPERTS, 1), F32)],
        compiler_params=_cparams(("arbitrary",)), name="moe_rank")(eidx_t, tri)


def _dest_kernel(e_ref, rank_ref, ps_ref, d_ref):
    tm = e_ref.shape[1]
    erow = lax.broadcasted_iota(jnp.int32, (N_EXPERTS, tm), 0)
    ps = ps_ref[...]
    rows = []
    for k in range(TOP_K):
        oh = erow == e_ref[k:k + 1, :]
        rows.append(jnp.sum(jnp.where(oh, ps, 0.0), axis=0, keepdims=True))
    d_ref[...] = jnp.concatenate(rows, axis=0).astype(jnp.int32) + rank_ref[...]


def _expert_dest(eidx_t, rank_t, pstarts):
    K, T = eidx_t.shape
    tm = min(ROW_TILE, T)
    blk = pl.BlockSpec((K, tm), lambda i: (0, i))
    return pl.pallas_call(
        _dest_kernel, grid=(T // tm,),
        in_specs=[blk, blk, pl.BlockSpec((N_EXPERTS, 1), lambda i: (0, 0))],
        out_specs=blk, out_shape=jax.ShapeDtypeStruct((K, T), jnp.int32),
        compiler_params=_cparams(("parallel",)), name="moe_dest")(
            eidx_t, rank_t, pstarts.astype(F32).reshape(N_EXPERTS, 1))


def _expert_kernel(be_ref, nb_ref, x_ref, wg_ref, wu_ref, wd_ref, o_ref, wgu_ref, wdb_ref):
    b = pl.program_id(0)
    E = wd_ref.shape[0]

    @pl.when((b == 0) | (be_ref[b] != be_ref[jnp.maximum(b - 1, 0)]))
    def _():
        wgu_ref[:, :E] = wg_ref[...].astype(BF16)
        wgu_ref[:, E:] = wu_ref[...].astype(BF16)
        wdb_ref[...] = wd_ref[...].astype(BF16)

    @pl.when(b < nb_ref[0])
    def _():
        gu = jnp.dot(_unpack_rows(x_ref[...]).astype(BF16), wgu_ref[...], preferred_element_type=F32)
        h = jax.nn.silu(gu[:, :E]) * gu[:, E:]
        o_ref[...] = _pack_rows(jnp.dot(h.astype(BF16), wdb_ref[...], preferred_element_type=F32))

    @pl.when(b >= nb_ref[0])
    def _():
        o_ref[...] = jnp.zeros(o_ref.shape, o_ref.dtype)


def _expert_ffn(blk_e, nb_used, xs, wg, wu, wd, layer):
    P = xs.shape[0]
    D = wg.shape[2]
    NB = P // MOE_BLOCK
    E = EXPERT_DIM
    grid_spec = pltpu.PrefetchScalarGridSpec(
        num_scalar_prefetch=2, grid=(NB,),
        in_specs=[pl.BlockSpec((MOE_BLOCK, D // 2), lambda b, be, nb: (b, 0)),
                  pl.BlockSpec((None, None, D, E), lambda b, be, nb: (layer, be[b], 0, 0)),
                  pl.BlockSpec((None, None, D, E), lambda b, be, nb: (layer, be[b], 0, 0)),
                  pl.BlockSpec((None, None, E, D), lambda b, be, nb: (layer, be[b], 0, 0))],
        out_specs=pl.BlockSpec((MOE_BLOCK, D // 2), lambda b, be, nb: (b, 0)),
        scratch_shapes=[pltpu.VMEM((D, 2 * E), BF16), pltpu.VMEM((E, D), BF16)])
    return pl.pallas_call(
        _expert_kernel, grid_spec=grid_spec, out_shape=jax.ShapeDtypeStruct((P, D // 2), jnp.int32),
        compiler_params=_cparams(("arbitrary",)), name="moe_experts")(blk_e, nb_used, xs, wg, wu, wd)


def _post_moe_kernel(x_ref, *refs):
    y_refs = refs[:TOP_K]
    gw_ref, p_ref, sg_ref, su_ref, sd_ref, g_ref, b_ref, wg_ref, wp_ref, o_ref, ob_ref = refs[TOP_K:]
    x = x_ref[...]
    xb = x.astype(BF16)
    h = jax.nn.silu(jnp.dot(xb, sg_ref[...], preferred_element_type=F32)) * jnp.dot(
        xb, su_ref[...], preferred_element_type=F32)
    ffn = jnp.dot(h.astype(BF16), sd_ref[...], preferred_element_type=F32)
    gw = gw_ref[...]
    for k in range(TOP_K):
        ffn = ffn + gw[:, k:k + 1] * _unpack_rows(y_refs[k][...])
    z = DN_ALPHA * x + ffn
    x2 = _layer_norm(z, g_ref[...], b_ref[...])
    gate = jax.nn.sigmoid(jnp.dot(x2.astype(BF16), wg_ref[...], preferred_element_type=F32))
    proj = jnp.dot(p_ref[...].astype(BF16), wp_ref[...], preferred_element_type=F32)
    out = x2 + gate * proj
    o_ref[...] = out
    ob_ref[...] = out.astype(BF16)


def _post_moe(x1, yg, gw, p, sg, su, sd, g, b, wgate, wproj):
    T, D = x1.shape
    tm = min(ROW_TILE // 2, T)
    nt = T // tm
    PD = p.shape[1]
    SD = sg.shape[1]
    row = pl.BlockSpec((tm, D), lambda i: (i, 0))
    vec = pl.BlockSpec((1, D), lambda i: (0, 0))
    full = lambda r, c: pl.BlockSpec((r, c), lambda i: (0, 0))
    return pl.pallas_call(
        _post_moe_kernel, grid=(T // tm,),
        in_specs=[row] + [pl.BlockSpec((tm, D // 2), functools.partial(lambda i, k: (k * nt + i, 0), k=k))
                          for k in range(TOP_K)]
                 + [pl.BlockSpec((tm, TOP_K), lambda i: (i, 0)),
                    pl.BlockSpec((tm, PD), lambda i: (i, 0)), full(D, SD), full(D, SD),
                  full(SD, D), vec, vec, full(D, D), full(PD, D)],
        out_specs=[row, row],
        out_shape=[jax.ShapeDtypeStruct((T, D), F32), jax.ShapeDtypeStruct((T, D), BF16)],
        compiler_params=_cparams(("parallel",)), name="post_moe")(
            x1, *([yg] * TOP_K), gw, p, sg, su, sd, g.reshape(1, D), b.reshape(1, D), wgate, wproj)


def _rope_tables(positions):
    inv = 1.0 / (ROPE_THETA ** (jnp.arange(0, HEAD_DIM, 2, dtype=F32) / HEAD_DIM))
    ang = positions.astype(F32).reshape(-1)[:, None] * inv
    cos, sin = jnp.cos(ang), jnp.sin(ang)
    cosf = jnp.concatenate([cos] * (LANES // (HEAD_DIM // 2)), axis=1)
    sinf = jnp.concatenate([-sin, sin] * (LANES // HEAD_DIM), axis=1)
    return cosf, sinf


def _q_col_scale(n_q, n):
    return jnp.concatenate([jnp.full((n_q,), Q_SCALE, F32), jnp.ones((n - n_q,), F32)])


def _fox_mixer(x, xb, B, S, w_in, b_f):
    D = D_MODEL
    wb = w_in[:, :3 * D].astype(BF16)
    qk = _proj(xb, wb[:, :2 * D], col_scale=_q_col_scale(D, 2 * D))
    vt = _proj_t(wb[:, 2 * D:].T, xb)
    fl = _proj(x, _pad_cols(w_in[:, 3 * D:], LANES), out_dtype=F32, tn=LANES, precision=HIGHEST)
    cp = _fox_gate(fl, b_f, B, S)
    return [_fox_attention(qk, cp, vt, B, S)]


def _nsa_mixer(x, xb, B, S, cosf, sinf, w_in, b_gate, pe_k, pe_v, ck_w1, ck_w2, cv_w1, cv_w2):
    D, G, HD = D_MODEL, NSA_KV_GROUPS, HEAD_DIM
    HG = NSA_HEADS_PER_GROUP
    kvw = G * HD
    wb = w_in[:, :D + 6 * kvw].astype(BF16)
    q_c, q_r = _proj(xb, wb[:, :D], mode="both", cos=cosf, sin=sinf, col_scale=_q_col_scale(D, D))
    w_rot = jnp.concatenate([wb[:, D + 2 * kvw:D + 3 * kvw], wb[:, D + 4 * kvw:D + 5 * kvw]], axis=1)
    k_rot = _proj(xb, w_rot, mode="rope", cos=cosf, sin=sinf)
    kvc = _proj(xb, wb[:, D:D + 2 * kvw])
    w_v = jnp.concatenate([wb[:, D + 3 * kvw:D + 4 * kvw], wb[:, D + 5 * kvw:D + 6 * kvw]], axis=1)
    vt = _proj_t(w_v.T, xb)
    gl = _proj(x, _pad_cols(w_in[:, D + 6 * kvw:], LANES), out_dtype=F32, tn=LANES, precision=HIGHEST)
    bg = _pad_cols(b_gate.reshape(1, -1), LANES)

    def grouped(t2d):
        t = t2d.reshape(B, S, G, HD).transpose(0, 2, 1, 3)
        return jnp.concatenate([t, t], axis=-1)

    n_chunks = S // NSA_CMP_STRIDE
    n_cmp = n_chunks - NSA_CMP_LEN // NSA_CMP_STRIDE + 1
    ncp = n_chunks

    def compress(t2d, pe, w1, w2):
        ch = t2d.reshape(B, n_chunks, NSA_CMP_STRIDE, G, HD).transpose(0, 1, 3, 2, 4)
        ch = ch.reshape(B, n_chunks, G, NSA_CMP_STRIDE * HD)
        flat = jnp.concatenate([ch[:, :n_cmp], ch[:, 1:n_cmp + 1]], axis=-1)
        flat = jnp.pad(flat, ((0, 0), (0, ncp - n_cmp), (0, 0), (0, 0))).reshape(B * ncp * G, -1)
        out = _nsa_compress(flat, pe.reshape(1, -1), w1.astype(BF16), _pad_cols(w2, LANES).astype(BF16))
        return out[:, :HD].reshape(B, ncp, G, HD).astype(BF16)

    kc = compress(kvc[:, :kvw], pe_k, ck_w1, ck_w2).transpose(0, 2, 1, 3)
    kk_c = jnp.concatenate([kc, kc], axis=-1)
    vt_c = compress(kvc[:, kvw:], pe_v, cv_w1, cv_w2).transpose(0, 2, 3, 1)

    n_sel = S // NSA_SEL_LEN
    assert n_sel <= LANES
    cmp_start = np.arange(ncp) * NSA_CMP_STRIDE
    sel_start = np.arange(LANES) * NSA_SEL_LEN
    overlap = ((cmp_start[:, None] < sel_start[None, :] + NSA_SEL_LEN)
               & (cmp_start[:, None] + NSA_CMP_LEN > sel_start[None, :])
               & (np.arange(ncp)[:, None] < n_cmp) & (np.arange(LANES)[None, :] < n_sel))
    overlap_t = jnp.asarray(overlap.T, BF16)
    ex = np.zeros((3, G, LANES, HG * HD), np.float32)
    for br in range(3):
        for g in range(G):
            for hg in range(HG):
                ex[br, g, (g * HG + hg) * 3 + br, hg * HD:(hg + 1) * HD] = 1.0
    ex = jnp.asarray(ex, BF16)

    o_c, selneg = _nsa_cmp_branch(q_c, kk_c, vt_c, overlap_t, gl, bg, ex[0], B, S)
    onehot = jnp.asarray((np.arange(S)[:, None] // NSA_SEL_LEN) == np.arange(LANES)[None, :], BF16)
    kk_s = grouped(k_rot[:, :kvw])
    kk_w = grouped(k_rot[:, kvw:])
    o_s = _nsa_kv_branch("sel", q_r, kk_s, vt[:kvw], gl, bg, ex[1], B, S, onehot=onehot, selneg=selneg)
    o_w = _nsa_kv_branch("win", q_r, kk_w, vt[kvw:], gl, bg, ex[2], B, S)
    return [o_c, o_s, o_w]


def _moba_mixer(xb, B, S, cosf, sinf, w_in):
    D, H, HD = D_MODEL, N_HEADS, HEAD_DIM
    wb = w_in.astype(BF16)
    qk_rot = _proj(xb, wb[:, :2 * D], mode="rope", cos=cosf, sin=sinf, col_scale=_q_col_scale(D, 2 * D))
    vt = _proj_t(wb[:, 2 * D:].T, xb)
    nblk = S // MOBA_BLOCK
    assert S % MOBA_BLOCK == 0 and 2 * nblk <= LANES and nblk % SUBLANES == 0
    kmean = _moba_kmean(qk_rot[:, D:]).reshape(B, nblk, H // 2, 2, HD)
    km = kmean.transpose(0, 2, 3, 1, 4)
    rmat_t = jnp.zeros((B, H // 2, LANES, 2, HD), F32)
    rmat_t = rmat_t.at[:, :, :nblk, 0].set(km[:, :, 0]).at[:, :, nblk:2 * nblk, 1].set(km[:, :, 1])
    selneg = _moba_select(qk_rot, rmat_t.reshape(B, H // 2, LANES, LANES), B, S)
    blk_of = np.arange(S) // MOBA_BLOCK
    oh = np.zeros((S, LANES), np.float32)
    oh[np.arange(S), blk_of] = 1.0
    oh[np.arange(S), nblk + blk_of] = 1.0
    return [_moba_attention(qk_rot, vt, jnp.asarray(oh, BF16), selneg, B, S)]


def _sc_invert_kernel(dest_hbm, tok_hbm, out_hbm, d_v, t_v, buf_v, *, seg, chunk, n_chunks):
    lo = pl.multiple_of((lax.axis_index("s") * SC_CORES + lax.axis_index("c")) * seg, SC_LANES)

    @pl.loop(0, seg // SC_LANES)
    def _(j):
        buf_v[pl.ds(j * SC_LANES, SC_LANES)] = jnp.zeros((SC_LANES,), jnp.int32)

    @pl.loop(0, n_chunks)
    def _(c):
        pltpu.sync_copy(dest_hbm.at[pl.ds(c * chunk, chunk)], d_v)
        pltpu.sync_copy(tok_hbm.at[pl.ds(c * chunk, chunk)], t_v)

        @pl.loop(0, chunk // SC_LANES)
        def _(j):
            d = d_v[pl.ds(j * SC_LANES, SC_LANES)] - lo
            mine = (d >= 0) & (d < seg)
            plsc.store_scatter(buf_v, [jnp.where(mine, d, 0)], t_v[pl.ds(j * SC_LANES, SC_LANES)], mask=mine)

    pltpu.sync_copy(buf_v, out_hbm.at[pl.ds(lo, seg)])


def _invert_dest(dest, tok, P):
    R = dest.shape[0]
    n_workers = SC_CORES * SC_SUBCORES
    seg = P // n_workers
    chunk = min(4096, R)
    assert P == seg * n_workers and seg % SC_LANES == 0 and R % chunk == 0
    mesh = plsc.VectorSubcoreMesh(core_axis_name="c", subcore_axis_name="s",
                                  num_cores=SC_CORES, num_subcores=SC_SUBCORES)
    return pl.kernel(
        functools.partial(_sc_invert_kernel, seg=seg, chunk=chunk, n_chunks=R // chunk),
        out_type=jax.ShapeDtypeStruct((P,), jnp.int32), mesh=mesh,
        scratch_types=[pltpu.VMEM((chunk,), jnp.int32), pltpu.VMEM((chunk,), jnp.int32),
                       pltpu.VMEM((seg,), jnp.int32)],
        compiler_params=pltpu.CompilerParams(needs_layout_passes=False),
        name="moe_invert")(dest, tok)


def _sc_mesh():
    return plsc.VectorSubcoreMesh(core_axis_name="c", subcore_axis_name="s",
                                  num_cores=SC_CORES, num_subcores=SC_SUBCORES)


def _gather_rows(table, idx):
    V, W = table.shape
    N = idx.shape[0]
    parts = W // SC_ROW_WORDS
    assert W == parts * SC_ROW_WORDS
    n_pieces = N * parts
    n_workers = SC_CORES * SC_SUBCORES
    assert n_pieces % (SC_WINDOW * n_workers) == 0
    pieces = (idx[:, None] * parts + jnp.arange(parts, dtype=jnp.int32)[None, :]).reshape(1, n_pieces)

    def kernel_body(x_hbm, i_hbm, o_hbm):
        def body(i_vmem, o_vmem):
            pltpu.sync_copy(x_hbm.at[i_vmem.at[0]], o_vmem)

        pltpu.emit_pipeline(
            body, grid=(n_pieces // SC_WINDOW,),
            in_specs=[pl.BlockSpec((1, SC_WINDOW), lambda i: (0, i))],
            out_specs=[pl.BlockSpec((SC_WINDOW, SC_ROW_WORDS), lambda i: (i, 0))],
            core_axis_name=("c", "s"), dimension_semantics=(pltpu.PARALLEL,))(i_hbm, o_hbm)

    out = pl.kernel(kernel_body, out_type=jax.ShapeDtypeStruct((n_pieces, SC_ROW_WORDS), table.dtype),
                    mesh=_sc_mesh(), scratch_types=[], name="moe_gather")(
                        table.reshape(V * parts, SC_ROW_WORDS), pieces)
    return out.reshape(N, W)


def _moe_dispatch(x1, x1p, router_w, router_b, wg, wu, wd, layer):
    T, D = x1.shape
    eidx_t, gw_t = _router(x1, router_w, router_b)
    rank_t, cnt = _expert_ranks(eidx_t)
    counts = cnt[:, 0].astype(jnp.int32)
    padded = (counts + MOE_BLOCK - 1) // MOE_BLOCK * MOE_BLOCK
    pends = jnp.cumsum(padded)
    pstarts = pends - padded
    R = T * TOP_K
    P = -(-(R + N_EXPERTS * (MOE_BLOCK - 1)) // MOE_BLOCK) * MOE_BLOCK
    NB = P // MOE_BLOCK
    blk_start = jnp.arange(NB, dtype=jnp.int32) * MOE_BLOCK
    blk_e = jnp.minimum(jnp.sum(pends[None, :] <= blk_start[:, None], axis=1), N_EXPERTS - 1).astype(jnp.int32)
    nb_used = (pends[-1] // MOE_BLOCK).astype(jnp.int32).reshape(1)
    dest_t = _expert_dest(eidx_t, rank_t, pstarts)

    tok = jnp.broadcast_to(jnp.arange(T, dtype=jnp.int32)[None, :], (TOP_K, T))
    buf_tok = _invert_dest(dest_t.reshape(-1), tok.reshape(-1), P)
    xs = _gather_rows(x1p, buf_tok)
    yb = _expert_ffn(blk_e, nb_used, xs, wg, wu, wd, layer)
    yg = _gather_rows(yb, dest_t.reshape(-1))
    return yg, gw_t.T


def kernel(x, p, positions, fox_w_in, fox_b_f, fox_w_out, nsa_w_in, nsa_b_gate, nsa_pe_k, nsa_pe_v,
           nsa_cmp_k_w1, nsa_cmp_k_w2, nsa_cmp_v_w1, nsa_cmp_v_w2, nsa_w_out, moba_w_in, moba_w_out,
           ln1_g, ln1_b, router_w, router_b, exp_w_gate, exp_w_up, exp_w_down,
           sh_w_gate, sh_w_up, sh_w_down, ln2_g, ln2_b, ple_w_gate, ple_w_proj):
    B, S, D = x.shape
    T = B * S
    depth = p.shape[0]
    cosf, sinf = _rope_tables(positions)
    xt = x.reshape(T, D)
    xtb = xt.astype(BF16)
    for i in range(depth):
        kind, j = i % N_MIXERS, i // N_MIXERS
        if kind == 0:
            o_list = _fox_mixer(xt, xtb, B, S, fox_w_in[j], fox_b_f[j])
            w_out = fox_w_out[j]
        elif kind == 1:
            o_list = _nsa_mixer(xt, xtb, B, S, cosf, sinf, nsa_w_in[j], nsa_b_gate[j], nsa_pe_k[j], nsa_pe_v[j],
                                nsa_cmp_k_w1[j], nsa_cmp_k_w2[j], nsa_cmp_v_w1[j], nsa_cmp_v_w2[j])
            w_out = nsa_w_out[j]
        else:
            o_list = _moba_mixer(xtb, B, S, cosf, sinf, moba_w_in[j])
            w_out = moba_w_out[j]
        x1, x1p = _outproj_ln(o_list, w_out.astype(BF16), xt, ln1_g[i], ln1_b[i])
        yg, gw = _moe_dispatch(x1, x1p, router_w[i], router_b[i], exp_w_gate, exp_w_up, exp_w_down, i)
        xt, xtb = _post_moe(x1, yg, gw, p[i].reshape(T, -1), sh_w_gate[i].astype(BF16), sh_w_up[i].astype(BF16),
                            sh_w_down[i].astype(BF16), ln2_g[i], ln2_b[i],
                            ple_w_gate[i].astype(BF16), ple_w_proj[i].astype(BF16))
    return xt.reshape(B, S, D)
```

```python
import functools

import jax
import jax.numpy as jnp
import numpy as np
from jax import lax
from jax.experimental import pallas as pl
from jax.experimental.pallas import tpu as pltpu
from jax.experimental.pallas import tpu_sc as plsc

D_MODEL = 1024
DEPTH = 4
N_HEADS = 16
HEAD_DIM = 64
ATTN_SCALE = HEAD_DIM ** -0.5
LOG2E = 1.4426950408889634
Q_SCALE = ATTN_SCALE * LOG2E
ROPE_THETA = 10000.0
N_MIXERS = 3

NSA_KV_GROUPS = 4
NSA_HEADS_PER_GROUP = N_HEADS // NSA_KV_GROUPS
NSA_CMP_LEN = 32
NSA_CMP_STRIDE = 16
NSA_SEL_LEN = 64
NSA_SEL_TOPN = 16
NSA_WINDOW = 512
NSA_FORCE_BONUS = 1e4

MOBA_BLOCK = 256
MOBA_TOPK = 3

N_EXPERTS = 64
EXPERT_DIM = 256
TOP_K = 8
N_GROUPS = 8
TOPK_GROUPS = 4
ROUTED_SCALE = 2.5
MOE_BLOCK = 256

DN_ALPHA = (2 * DEPTH) ** 0.25
LN_EPS = 1e-5
NEG = -1e30

SC_CORES = 2
SC_SUBCORES = 16
SC_LANES = 16
SC_ROW_WORDS = 256
SC_WINDOW = 128
LANES = 128
SUBLANES = 8
ATTN_TILE = 512
PAIR_TILE = 1024
KV_TILE = 1024
ROW_TILE = 512
VMEM_LIMIT = 48 * 1024 * 1024
C_SPLIT = 3
ONES_ROWS = 16

F32 = jnp.float32
BF16 = jnp.bfloat16
HIGHEST = lax.Precision.HIGHEST


def _cparams(sem):
    return pltpu.CompilerParams(dimension_semantics=sem, vmem_limit_bytes=VMEM_LIMIT)


def _log2(n):
    assert n & (n - 1) == 0
    return n.bit_length() - 1


def _nt_dot(a, b, precision=None):
    return lax.dot_general(a, b, (((1,), (1,)), ((), ())), preferred_element_type=F32,
                           precision=precision)


def _topk_rows(work, n):
    rows = lax.broadcasted_iota(jnp.int32, work.shape, 0).astype(F32)
    chosen = jnp.zeros(work.shape, F32)
    for _ in range(n):
        m = jnp.max(work, axis=0, keepdims=True)
        idx = jnp.min(jnp.where(work == m, rows, float(work.shape[0])), axis=0, keepdims=True)
        pick = rows == idx
        chosen = jnp.where(pick, 1.0, chosen)
        work = jnp.where(pick, -jnp.inf, work)
    return chosen


def _proj_kernel(*refs, mode, precision, scaled):
    refs = list(refs)
    x_ref, w_ref = refs[:2]
    del refs[:2]
    scale_ref = refs.pop(0) if scaled else None
    if mode == "none":
        (o_ref,) = refs
    elif mode == "rope":
        cos_ref, sin_ref, r_ref = refs
    else:
        cos_ref, sin_ref, o_ref, r_ref = refs
    x = x_ref[...]
    w = w_ref[...]
    if x.dtype != w.dtype:
        x = x.astype(w.dtype)
    acc = jnp.dot(x, w, preferred_element_type=F32, precision=precision)
    if scaled:
        acc = acc * scale_ref[...]
    if mode in ("none", "both"):
        o_ref[...] = acc.astype(o_ref.dtype)
    if mode in ("rope", "both"):
        tn = acc.shape[1]
        rep = tn // LANES
        cosf = jnp.concatenate([cos_ref[...]] * rep, axis=1)
        sinf = jnp.concatenate([sin_ref[...]] * rep, axis=1)
        lane = lax.broadcasted_iota(jnp.int32, acc.shape, 1)
        first_half = (lane & (HEAD_DIM // 2)) == 0
        swapped = jnp.where(first_half, pltpu.roll(acc, tn - HEAD_DIM // 2, 1),
                            pltpu.roll(acc, HEAD_DIM // 2, 1))
        r_ref[...] = (acc * cosf + swapped * sinf).astype(r_ref.dtype)


def _proj(x, w, *, mode="none", cos=None, sin=None, out_dtype=BF16, tn=512, precision=None, col_scale=None):
    T, K = x.shape
    N = w.shape[1]
    tm = min(ROW_TILE, T)
    tn = min(tn, N)
    assert T % tm == 0 and N % tn == 0
    grid = (T // tm, N // tn)
    in_specs = [pl.BlockSpec((tm, K), lambda i, j: (i, 0)),
                pl.BlockSpec((K, tn), lambda i, j: (0, j))]
    args = [x, w]
    if col_scale is not None:
        in_specs.append(pl.BlockSpec((1, tn), lambda i, j: (0, j)))
        args.append(col_scale.reshape(1, N).astype(F32))
    if mode != "none":
        in_specs += [pl.BlockSpec((tm, LANES), lambda i, j: (i, 0))] * 2
        args += [cos, sin]
    o_spec = pl.BlockSpec((tm, tn), lambda i, j: (i, j))
    o_shape = jax.ShapeDtypeStruct((T, N), out_dtype)
    if mode == "both":
        out_specs, out_shape = [o_spec, o_spec], [o_shape, o_shape]
    else:
        out_specs, out_shape = o_spec, o_shape
    return pl.pallas_call(
        functools.partial(_proj_kernel, mode=mode, precision=precision, scaled=col_scale is not None),
        grid=grid, in_specs=in_specs, out_specs=out_specs, out_shape=out_shape,
        compiler_params=_cparams(("parallel", "parallel")), name=f"proj_{mode}")(*args)


def _proj_t_kernel(wt_ref, x_ref, o_ref, *, precision):
    wt = wt_ref[...]
    x = x_ref[...]
    if x.dtype != wt.dtype:
        x = x.astype(wt.dtype)
    o_ref[...] = _nt_dot(wt, x, precision).astype(o_ref.dtype)


def _proj_t(wt, x, *, out_dtype=BF16, precision=None):
    N, K = wt.shape
    T = x.shape[0]
    tm = min(ROW_TILE, T)
    tn = min(512, N)
    assert T % tm == 0 and N % tn == 0
    return pl.pallas_call(
        functools.partial(_proj_t_kernel, precision=precision),
        grid=(T // tm, N // tn),
        in_specs=[pl.BlockSpec((tn, K), lambda i, j: (j, 0)), pl.BlockSpec((tm, K), lambda i, j: (i, 0))],
        out_specs=pl.BlockSpec((tn, tm), lambda i, j: (j, i)),
        out_shape=jax.ShapeDtypeStruct((N, T), out_dtype),
        compiler_params=_cparams(("parallel", "parallel")), name="proj_t")(wt, x)


def _pack_rows(v):
    n = v.shape[1] // 2
    lo = pltpu.bitcast(v[:, :n].astype(BF16).astype(F32), jnp.int32)
    hi = pltpu.bitcast(v[:, n:].astype(BF16).astype(F32), jnp.int32)
    return (hi & jnp.int32(-65536)) | lax.shift_right_logical(lo, jnp.int32(16))


def _unpack_rows(w):
    lo = pltpu.bitcast(lax.shift_left(w, jnp.int32(16)), F32)
    hi = pltpu.bitcast(w & jnp.int32(-65536), F32)
    return jnp.concatenate([lo, hi], axis=1)


def _store_pieces(ref, v):
    w = _pack_rows(v)
    for j in range(ref.shape[0]):
        ref[j] = w[:, j * SC_ROW_WORDS:(j + 1) * SC_ROW_WORDS]


def _load_pieces(ref):
    return _unpack_rows(jnp.concatenate([ref[j] for j in range(ref.shape[0])], axis=1))


def _piece_spec(d, rows, index_map):
    return pl.BlockSpec((d // 2 // SC_ROW_WORDS, rows, SC_ROW_WORDS), index_map)


def _piece_shape(d, rows):
    return jax.ShapeDtypeStruct((d // 2 // SC_ROW_WORDS, rows, SC_ROW_WORDS), jnp.int32)


def _pad_cols(w, n):
    return jnp.pad(w, ((0, 0), (0, n - w.shape[1])))


def _layer_norm(z, g, b):
    mu = jnp.mean(z, axis=-1, keepdims=True)
    zc = z - mu
    var = jnp.mean(zc * zc, axis=-1, keepdims=True)
    return zc * lax.rsqrt(var + LN_EPS) * g + b


def _outproj_ln_kernel(*refs, n_o):
    o_refs = refs[:n_o]
    w_ref, x_ref, g_ref, b_ref, y_ref, yp_ref = refs[n_o:]
    if n_o == 1:
        o = o_refs[0][...]
    else:
        o = o_refs[0][...].astype(F32)
        for r in o_refs[1:]:
            o = o + r[...].astype(F32)
        o = o.astype(BF16)
    mix = jnp.dot(o, w_ref[...], preferred_element_type=F32)
    z = DN_ALPHA * x_ref[...] + mix
    y = _layer_norm(z, g_ref[...], b_ref[...])
    y_ref[...] = y
    _store_pieces(yp_ref, y)


def _outproj_ln(o_list, w, x, g, b):
    T, D = x.shape
    tm = min(ROW_TILE, T)
    row = pl.BlockSpec((tm, D), lambda i: (i, 0))
    vec = pl.BlockSpec((1, D), lambda i: (0, 0))
    return pl.pallas_call(
        functools.partial(_outproj_ln_kernel, n_o=len(o_list)),
        grid=(T // tm,),
        in_specs=[row] * len(o_list) + [pl.BlockSpec((D, D), lambda i: (0, 0)), row, vec, vec],
        out_specs=[row, _piece_spec(D, tm, lambda i: (0, i, 0))],
        out_shape=[jax.ShapeDtypeStruct((T, D), F32), _piece_shape(D, T)],
        compiler_params=_cparams(("parallel",)), name="outproj_ln")(
            *o_list, w, x, g.reshape(1, D), b.reshape(1, D))


def _flash_init(m_ref, acc_ref):
    m_ref[...] = jnp.full(m_ref.shape, NEG, F32)
    acc_ref[...] = jnp.zeros(acc_ref.shape, F32)


def _flash_step(sT, lhs, m_ref, acc_ref, h, cols=slice(None)):
    m_prev = m_ref[h, :, cols]
    m_new = jnp.maximum(m_prev, jnp.max(sT, axis=0, keepdims=True))
    p = jnp.exp2(sT - m_new)
    alpha = jnp.exp2(m_prev - m_new)
    acc_ref[h, :, cols] = alpha * acc_ref[h, :, cols] + jnp.dot(lhs, p.astype(BF16), preferred_element_type=F32)
    m_ref[h, :, cols] = m_new


def _causal_t(tk, tq, shift=0, strict_lower=False):
    key = lax.broadcasted_iota(jnp.int32, (tk, tq), 0)
    qry = lax.broadcasted_iota(jnp.int32, (tk, tq), 1)
    return (key > qry) if strict_lower else (key <= qry + shift)


def _causal_sweep(i, tq, tk, step):
    r = tk // tq
    assert tk == r * tq
    n_full = i >> _log2(r)

    def body(j, carry):
        step(pl.multiple_of(j * tk, tk), tk, None)
        return carry

    lax.fori_loop(0, n_full, body, 0)
    if r == 1:
        step(pl.multiple_of(i * tq, tq), tq, _causal_t(tq, tq))
    else:
        for rem in range(r):
            @pl.when(i - n_full * r == rem)
            def _(rem=rem):
                size = (rem + 1) * tq
                step(pl.multiple_of(n_full * tk, tk), size, _causal_t(size, tq, shift=rem * tq))


def _staggered(units, scores, update):
    s = [scores(units[0])]
    for n, u in enumerate(units):
        if n + 1 < len(units):
            s.append(scores(units[n + 1]))
        update(u, s[n])


def _pair_units(tq):
    w = min(ATTN_TILE, tq)
    return [(h, slice(c * w, (c + 1) * w)) for h in range(2) for c in range(tq // w)]


def _head_of_pair(q_pair, which):
    lane = lax.broadcasted_iota(jnp.int32, q_pair.shape, 1)
    keep = (lane < HEAD_DIM) if which == 0 else (lane >= HEAD_DIM)
    return jnp.where(keep, q_pair, jnp.zeros_like(q_pair))


def _pair_rows(a_top, a_bot):
    row = lax.broadcasted_iota(jnp.int32, a_top.shape, 0)
    return jnp.where(row < HEAD_DIM, a_top, a_bot)


def _pair_lhs(vt, which):
    row = lax.broadcasted_iota(jnp.int32, vt.shape, 0)
    keep = (row < HEAD_DIM) if which == 0 else (row >= HEAD_DIM)
    return jnp.where(keep, vt, jnp.ones_like(vt))


def _pair_finish(acc_ref):
    a0, a1 = acc_ref[0], acc_ref[1]
    return _pair_rows(a0 / a0[HEAD_DIM:HEAD_DIM + 1, :], a1 / a1[0:1, :])


def _fox_gate_kernel(fl_ref, bf_ref, tri_ref, ex_ref, cp_ref, carry_ref):
    @pl.when(pl.program_id(1) == 0)
    def _():
        carry_ref[...] = jnp.zeros(carry_ref.shape, F32)

    z = fl_ref[...] + bf_ref[...]
    log_f = jnp.minimum(z, 0.0) - jnp.log1p(jnp.exp(-jnp.abs(z)))
    c = jnp.dot(tri_ref[...], log_f, preferred_element_type=F32, precision=HIGHEST) + carry_ref[...]
    carry_ref[...] = c[-1:, :]
    out = jnp.zeros(c.shape, F32)
    rem = c * LOG2E
    for piece in range(C_SPLIT):
        part = rem.astype(BF16)
        rem = rem - part.astype(F32)
        out = out + jnp.dot(part, ex_ref[piece], preferred_element_type=F32)
    cp_ref[...] = out.astype(BF16)


def _fox_gate(fl, b_f, B, S):
    T = fl.shape[0]
    tm = min(ROW_TILE, S)
    ns = S // tm
    tri = jnp.asarray(np.tril(np.ones((tm, tm), np.float32)))
    ex = np.zeros((C_SPLIT, LANES, LANES), np.float32)
    for h in range(N_HEADS):
        for j in range(C_SPLIT):
            ex[j, h, C_SPLIT * h + j] = 1.0
    bf = _pad_cols(b_f.reshape(1, -1), LANES)
    return pl.pallas_call(
        _fox_gate_kernel, grid=(B, ns),
        in_specs=[pl.BlockSpec((tm, LANES), lambda b, s: (b * ns + s, 0)),
                  pl.BlockSpec((1, LANES), lambda b, s: (0, 0)),
                  pl.BlockSpec((tm, tm), lambda b, s: (0, 0)),
                  pl.BlockSpec((C_SPLIT, LANES, LANES), lambda b, s: (0, 0, 0))],
        out_specs=pl.BlockSpec((tm, LANES), lambda b, s: (b * ns + s, 0)),
        out_shape=jax.ShapeDtypeStruct((T, LANES), BF16),
        scratch_shapes=[pltpu.VMEM((1, LANES), F32)],
        compiler_params=_cparams(("parallel", "arbitrary")), name="fox_gate")(
            fl, bf, tri, jnp.asarray(ex, BF16))


def _fox_kernel(q_ref, k_ref, cp_ref, vt_ref, o_ref, m_ref, acc_ref, *, tq, tk):
    hp = pl.program_id(1)
    i = pl.program_id(2)
    _flash_init(m_ref, acc_ref)
    q = q_ref[...]
    lane = lax.broadcasted_iota(jnp.int32, q.shape, 1)
    qa = []
    for h in range(2):
        lo = C_SPLIT * (2 * hp + h)
        minus_one = jnp.where((lane >= lo) & (lane < lo + C_SPLIT), -1.0, 0.0).astype(q.dtype)
        qa.append(jnp.concatenate([_head_of_pair(q, h), minus_one], axis=1))

    def step(off, size, mask):
        k = jnp.concatenate([k_ref[pl.ds(off, size), :], cp_ref[pl.ds(off, size), :]], axis=1)
        vt = vt_ref[:, pl.ds(off, size)]

        lhs = [_pair_lhs(vt, h) for h in range(2)]

        def scores(u):
            sT = _nt_dot(k, qa[u[0]][u[1], :])
            return sT if mask is None else jnp.where(mask[:, u[1]], sT, NEG)

        _staggered(_pair_units(tq), scores,
                   lambda u, sT: _flash_step(sT, lhs[u[0]], m_ref, acc_ref, u[0], u[1]))

    _causal_sweep(i, tq, tk, step)
    o_ref[...] = _pair_finish(acc_ref).T.astype(o_ref.dtype)


def _fox_attention(qk, cp, vt, B, S):
    T = qk.shape[0]
    tq = min(PAIR_TILE, S)
    tk = min(KV_TILE, S)
    nq = S // tq
    npair = N_HEADS // 2
    ncol = D_MODEL // LANES
    return pl.pallas_call(
        functools.partial(_fox_kernel, tq=tq, tk=tk),
        grid=(B, npair, nq),
        in_specs=[pl.BlockSpec((tq, LANES), lambda b, hp, i: (b * nq + i, hp)),
                  pl.BlockSpec((S, LANES), lambda b, hp, i: (b, ncol + hp)),
                  pl.BlockSpec((S, LANES), lambda b, hp, i: (b, 0)),
                  pl.BlockSpec((LANES, S), lambda b, hp, i: (hp, b))],
        out_specs=pl.BlockSpec((tq, LANES), lambda b, hp, i: (b * nq + i, hp)),
        out_shape=jax.ShapeDtypeStruct((T, D_MODEL), BF16),
        scratch_shapes=[pltpu.VMEM((2, 1, tq), F32), pltpu.VMEM((2, LANES, tq), F32)],
        compiler_params=_cparams(("parallel", "parallel", "arbitrary")), name="fox_attn")(
            qk, qk, cp, vt)


def _gate_expand(gl_ref, bg_ref, ex_ref):
    sig = jax.nn.sigmoid(gl_ref[...] + bg_ref[...])
    hi = sig.astype(BF16)
    lo = (sig - hi.astype(F32)).astype(BF16)
    ex = ex_ref[...]
    return jnp.dot(hi, ex, preferred_element_type=F32) + jnp.dot(lo, ex, preferred_element_type=F32)


def _nsa_compress_kernel(f_ref, pe_ref, w1_ref, w2_ref, o_ref):
    blk = (f_ref[...].astype(F32) + pe_ref[...]).astype(BF16)
    h = jax.nn.gelu(jnp.dot(blk, w1_ref[...], preferred_element_type=F32))
    o_ref[...] = jnp.dot(h.astype(BF16), w2_ref[...], preferred_element_type=F32)


def _nsa_compress(flat, pe_flat, w1, w2p):
    M, K = flat.shape
    tm = min(ROW_TILE, M)
    Hc = w1.shape[1]
    return pl.pallas_call(
        _nsa_compress_kernel, grid=(M // tm,),
        in_specs=[pl.BlockSpec((tm, K), lambda i: (i, 0)), pl.BlockSpec((1, K), lambda i: (0, 0)),
                  pl.BlockSpec((K, Hc), lambda i: (0, 0)), pl.BlockSpec((Hc, LANES), lambda i: (0, 0))],
        out_specs=pl.BlockSpec((tm, LANES), lambda i: (i, 0)),
        out_shape=jax.ShapeDtypeStruct((M, LANES), F32),
        compiler_params=_cparams(("parallel",)), name="nsa_compress")(flat, pe_flat, w1, w2p)


def _nsa_cmp_kernel(q_ref, kk_ref, vt_ref, ovt_ref, gl_ref, bg_ref, ex_ref, o_ref, sn_ref, *, tq, ncp, nselp):
    i = pl.program_id(2)
    q = q_ref[...]
    kk = kk_ref[...]
    vt = vt_ref[...]
    t = i * tq + lax.broadcasted_iota(jnp.int32, (ncp, tq), 1)
    n = lax.broadcasted_iota(jnp.int32, (ncp, tq), 0)
    valid = n * NSA_CMP_STRIDE + (NSA_CMP_LEN - 1) <= t
    psum = jnp.zeros((ncp, tq), F32)
    outs = []
    for hg in range(NSA_HEADS_PER_GROUP):
        qh = _head_of_pair(q[:, LANES * (hg // 2):LANES * (hg // 2 + 1)], hg % 2)
        sT = jnp.where(valid, _nt_dot(kk, qh), NEG)
        m = jnp.max(sT, axis=0, keepdims=True)
        e = jnp.where(valid, jnp.exp2(sT - m), 0.0)
        p = e / jnp.maximum(jnp.sum(e, axis=0, keepdims=True), 1e-30)
        psum = psum + p
        outs.append(jnp.dot(vt, p.astype(BF16), preferred_element_type=F32))
    o = jnp.concatenate(outs, axis=0).T
    o_ref[...] = (o * _gate_expand(gl_ref, bg_ref, ex_ref)).astype(o_ref.dtype)
    hi = psum.astype(BF16)
    lo = (psum - hi.astype(F32)).astype(BF16)
    ovt = ovt_ref[...]
    imp = jnp.dot(ovt, hi, preferred_element_type=F32) + jnp.dot(ovt, lo, preferred_element_type=F32)
    jblk = lax.broadcasted_iota(jnp.int32, (nselp, tq), 0)
    cur = (i * tq + lax.broadcasted_iota(jnp.int32, (nselp, tq), 1)) >> _log2(NSA_SEL_LEN)
    forced = (jblk == 0) | (jblk == cur) | (jblk == cur - 1)
    pri = jnp.where(forced, imp + NSA_FORCE_BONUS, imp)
    past = jblk <= cur
    pri = jnp.where(past, pri, NEG)
    chosen = _topk_rows(pri, NSA_SEL_TOPN)
    sn = jnp.where((chosen > 0.0) & past, 0.0, NEG)
    sn_ref[...] = sn.T.astype(sn_ref.dtype)


def _nsa_cmp_branch(q_c, kk_c, vt_c, overlap_t, gl, bg, ex, B, S):
    T = q_c.shape[0]
    G = NSA_KV_GROUPS
    tq = min(ATTN_TILE, S)
    nq = S // tq
    ncp = kk_c.shape[2]
    nselp = overlap_t.shape[0]
    return pl.pallas_call(
        functools.partial(_nsa_cmp_kernel, tq=tq, ncp=ncp, nselp=nselp),
        grid=(B, G, nq),
        in_specs=[pl.BlockSpec((tq, 2 * LANES), lambda b, g, i: (b * nq + i, g)),
                  pl.BlockSpec((None, None, ncp, LANES), lambda b, g, i: (b, g, 0, 0)),
                  pl.BlockSpec((None, None, HEAD_DIM, ncp), lambda b, g, i: (b, g, 0, 0)),
                  pl.BlockSpec((nselp, ncp), lambda b, g, i: (0, 0)),
                  pl.BlockSpec((tq, LANES), lambda b, g, i: (b * nq + i, 0)),
                  pl.BlockSpec((1, LANES), lambda b, g, i: (0, 0)),
                  pl.BlockSpec((None, LANES, 2 * LANES), lambda b, g, i: (g, 0, 0))],
        out_specs=[pl.BlockSpec((tq, 2 * LANES), lambda b, g, i: (b * nq + i, g)),
                   pl.BlockSpec((None, None, tq, nselp), lambda b, g, i: (b, g, i, 0))],
        out_shape=[jax.ShapeDtypeStruct((T, D_MODEL), BF16),
                   jax.ShapeDtypeStruct((B, G, S, nselp), BF16)],
        compiler_params=_cparams(("parallel", "parallel", "parallel")), name="nsa_cmp")(
            q_c, kk_c, vt_c, overlap_t, gl, bg, ex)


def _nsa_kv_kernel(*refs, mode, tq, tk):
    if mode == "sel":
        q_ref, kk_ref, vt_ref, oh_ref, sn_ref, gl_ref, bg_ref, ex_ref, o_ref, m_ref, acc_ref = refs
    else:
        q_ref, kk_ref, vt_ref, gl_ref, bg_ref, ex_ref, o_ref, m_ref, acc_ref = refs
    i = pl.program_id(2)
    _flash_init(m_ref, acc_ref)
    q = q_ref[...]
    qh = []
    for hg in range(NSA_HEADS_PER_GROUP):
        qq = _head_of_pair(q[:, LANES * (hg // 2):LANES * (hg // 2 + 1)], hg % 2)
        if mode == "sel":
            qq = jnp.concatenate([qq, sn_ref[...]], axis=1)
        qh.append(qq)

    def step(off, size, mask):
        k = kk_ref[pl.ds(off, size), :]
        vt = vt_ref[:, pl.ds(off, size)]
        lhs = jnp.concatenate([vt, jnp.ones((ONES_ROWS, size), vt.dtype)], axis=0)
        if mode == "sel":
            k = jnp.concatenate([k, oh_ref[pl.ds(off, size), :]], axis=1)

        def scores(hg):
            sT = _nt_dot(k, qh[hg])
            return sT if mask is None else jnp.where(mask, sT, NEG)

        _staggered(list(range(NSA_HEADS_PER_GROUP)), scores,
                   lambda hg, sT: _flash_step(sT, lhs, m_ref, acc_ref, hg))

    if mode == "sel":
        _causal_sweep(i, tq, tk, step)
    else:
        @pl.when(i > 0)
        def _():
            step(pl.multiple_of((i - 1) * tq, tq), tq, _causal_t(tq, tq, strict_lower=True))
        step(pl.multiple_of(i * tq, tq), tq, _causal_t(tq, tq))
    outs = []
    for hg in range(NSA_HEADS_PER_GROUP):
        a = acc_ref[hg]
        outs.append(a[:HEAD_DIM] / a[HEAD_DIM:HEAD_DIM + 1, :])
    o_ref[...] = (jnp.concatenate(outs, axis=0).T * _gate_expand(gl_ref, bg_ref, ex_ref)).astype(o_ref.dtype)


def _nsa_kv_branch(mode, q_r, kk, vt, gl, bg, ex, B, S, onehot=None, selneg=None):
    T = q_r.shape[0]
    G = NSA_KV_GROUPS
    tq = min(ATTN_TILE, S)
    tk = min(KV_TILE, S)
    assert NSA_WINDOW == tq or mode == "sel"
    nq = S // tq
    in_specs = [pl.BlockSpec((tq, 2 * LANES), lambda b, g, i: (b * nq + i, g)),
                pl.BlockSpec((None, None, S, LANES), lambda b, g, i: (b, g, 0, 0)),
                pl.BlockSpec((HEAD_DIM, S), lambda b, g, i: (g, b))]
    args = [q_r, kk, vt]
    if mode == "sel":
        in_specs += [pl.BlockSpec((S, LANES), lambda b, g, i: (0, 0)),
                     pl.BlockSpec((None, None, tq, LANES), lambda b, g, i: (b, g, i, 0))]
        args += [onehot, selneg]
    in_specs += [pl.BlockSpec((tq, LANES), lambda b, g, i: (b * nq + i, 0)),
                 pl.BlockSpec((1, LANES), lambda b, g, i: (0, 0)),
                 pl.BlockSpec((None, LANES, 2 * LANES), lambda b, g, i: (g, 0, 0))]
    args += [gl, bg, ex]
    nh = NSA_HEADS_PER_GROUP
    return pl.pallas_call(
        functools.partial(_nsa_kv_kernel, mode=mode, tq=tq, tk=tk),
        grid=(B, G, nq), in_specs=in_specs,
        out_specs=pl.BlockSpec((tq, 2 * LANES), lambda b, g, i: (b * nq + i, g)),
        out_shape=jax.ShapeDtypeStruct((T, D_MODEL), BF16),
        scratch_shapes=[pltpu.VMEM((nh, 1, tq), F32), pltpu.VMEM((nh, HEAD_DIM + ONES_ROWS, tq), F32)],
        compiler_params=_cparams(("parallel", "parallel", "arbitrary")), name=f"nsa_{mode}")(*args)


def _kmean_kernel(k_ref, o_ref, *, nblk):
    k = k_ref[...].astype(F32)
    o_ref[...] = jnp.mean(k.reshape(nblk, MOBA_BLOCK, k.shape[1]), axis=1)


def _moba_kmean(k_rot):
    T, D = k_rot.shape
    nblk = 8
    rows = nblk * MOBA_BLOCK
    assert T % rows == 0
    return pl.pallas_call(
        functools.partial(_kmean_kernel, nblk=nblk), grid=(T // rows,),
        in_specs=[pl.BlockSpec((rows, D), lambda i: (i, 0))],
        out_specs=pl.BlockSpec((nblk, D), lambda i: (i, 0)),
        out_shape=jax.ShapeDtypeStruct((T // MOBA_BLOCK, D), F32),
        compiler_params=_cparams(("parallel",)), name="moba_kmean")(k_rot)


def _moba_select_kernel(q_ref, r_ref, sn_ref, *, tq, nblk):
    i = pl.program_id(2)
    gsT = _nt_dot(r_ref[...], q_ref[...].astype(F32), HIGHEST)
    jblk = lax.broadcasted_iota(jnp.int32, (nblk, tq), 0)
    cb = (i * tq + lax.broadcasted_iota(jnp.int32, (nblk, tq), 1)) >> _log2(MOBA_BLOCK)
    past = jblk < cb
    parts = []
    for h in range(2):
        gs = jnp.where(past, gsT[h * nblk:(h + 1) * nblk, :], NEG)
        chosen = _topk_rows(gs, min(MOBA_TOPK, nblk))
        parts.append(jnp.where(((chosen > 0.0) & past) | (jblk == cb), 0.0, NEG))
    parts.append(jnp.zeros((LANES - 2 * nblk, tq), F32))
    sn_ref[...] = jnp.concatenate(parts, axis=0).T.astype(sn_ref.dtype)


def _moba_select(q_rot, rmat_t, B, S):
    tq = min(ATTN_TILE, S)
    nq = S // tq
    npair = N_HEADS // 2
    nblk = S // MOBA_BLOCK
    return pl.pallas_call(
        functools.partial(_moba_select_kernel, tq=tq, nblk=nblk), grid=(B, npair, nq),
        in_specs=[pl.BlockSpec((tq, LANES), lambda b, hp, i: (b * nq + i, hp)),
                  pl.BlockSpec((None, None, LANES, LANES), lambda b, hp, i: (b, hp, 0, 0))],
        out_specs=pl.BlockSpec((None, None, tq, LANES), lambda b, hp, i: (b, hp, i, 0)),
        out_shape=jax.ShapeDtypeStruct((B, npair, S, LANES), BF16),
        compiler_params=_cparams(("parallel", "parallel", "parallel")), name="moba_select")(q_rot, rmat_t)


def _moba_kernel(q_ref, k_ref, vt_ref, oh_ref, sn_ref, o_ref, m_ref, acc_ref, *, tq, tk, nblk):
    i = pl.program_id(2)
    _flash_init(m_ref, acc_ref)
    q = q_ref[...]
    sn = sn_ref[...]
    lane = lax.broadcasted_iota(jnp.int32, sn.shape, 1)
    qa = []
    for h in range(2):
        mine = (lane >= h * nblk) & (lane < (h + 1) * nblk)
        qa.append(jnp.concatenate([_head_of_pair(q, h), jnp.where(mine, sn, jnp.zeros_like(sn))], axis=1))

    def step(off, size, mask):
        k = jnp.concatenate([k_ref[pl.ds(off, size), :], oh_ref[pl.ds(off, size), :]], axis=1)
        vt = vt_ref[:, pl.ds(off, size)]

        lhs = [_pair_lhs(vt, h) for h in range(2)]

        def scores(u):
            sT = _nt_dot(k, qa[u[0]][u[1], :])
            return sT if mask is None else jnp.where(mask[:, u[1]], sT, NEG)

        _staggered(_pair_units(tq), scores,
                   lambda u, sT: _flash_step(sT, lhs[u[0]], m_ref, acc_ref, u[0], u[1]))

    _causal_sweep(i, tq, tk, step)
    o_ref[...] = _pair_finish(acc_ref).T.astype(o_ref.dtype)


def _moba_attention(qk_rot, vt, onehot2, selneg, B, S):
    T = qk_rot.shape[0]
    tq = min(PAIR_TILE, S)
    tk = min(KV_TILE, S)
    nq = S // tq
    npair = N_HEADS // 2
    ncol = D_MODEL // LANES
    nblk = S // MOBA_BLOCK
    return pl.pallas_call(
        functools.partial(_moba_kernel, tq=tq, tk=tk, nblk=nblk),
        grid=(B, npair, nq),
        in_specs=[pl.BlockSpec((tq, LANES), lambda b, hp, i: (b * nq + i, hp)),
                  pl.BlockSpec((S, LANES), lambda b, hp, i: (b, ncol + hp)),
                  pl.BlockSpec((LANES, S), lambda b, hp, i: (hp, b)),
                  pl.BlockSpec((S, LANES), lambda b, hp, i: (0, 0)),
                  pl.BlockSpec((None, None, tq, LANES), lambda b, hp, i: (b, hp, i, 0))],
        out_specs=pl.BlockSpec((tq, LANES), lambda b, hp, i: (b * nq + i, hp)),
        out_shape=jax.ShapeDtypeStruct((T, D_MODEL), BF16),
        scratch_shapes=[pltpu.VMEM((2, 1, tq), F32), pltpu.VMEM((2, LANES, tq), F32)],
        compiler_params=_cparams(("parallel", "parallel", "arbitrary")), name="moba_attn")(
            qk_rot, qk_rot, vt, onehot2, selneg)


def _router_kernel(wt_ref, x_ref, rb_ref, e_ref, g_ref):
    tm = x_ref.shape[0]
    gsz = N_EXPERTS // N_GROUPS
    scores = jax.nn.sigmoid(_nt_dot(wt_ref[...], x_ref[...], HIGHEST))
    biased = scores + rb_ref[...]
    member = lax.broadcasted_iota(jnp.int32, (gsz, tm), 0).astype(F32)
    gscore = []
    for g in range(N_GROUPS):
        v = biased[g * gsz:(g + 1) * gsz, :]
        m1 = jnp.max(v, axis=0, keepdims=True)
        i1 = jnp.min(jnp.where(v == m1, member, float(gsz)), axis=0, keepdims=True)
        m2 = jnp.max(jnp.where(member == i1, -jnp.inf, v), axis=0, keepdims=True)
        gscore.append(m1 + m2)
    gsel = _topk_rows(jnp.concatenate(gscore, axis=0), TOPK_GROUPS)
    emask = jnp.concatenate([jnp.broadcast_to(gsel[g:g + 1, :], (gsz, tm)) for g in range(N_GROUPS)], axis=0)
    work = jnp.where(emask > 0.0, biased, NEG)
    erow = lax.broadcasted_iota(jnp.int32, (N_EXPERTS, tm), 0).astype(F32)
    idxs, vals = [], []
    for _ in range(TOP_K):
        m = jnp.max(work, axis=0, keepdims=True)
        idx = jnp.min(jnp.where(work == m, erow, float(N_EXPERTS)), axis=0, keepdims=True)
        pick = erow == idx
        idxs.append(idx)
        vals.append(jnp.sum(jnp.where(pick, scores, 0.0), axis=0, keepdims=True))
        work = jnp.where(pick, -jnp.inf, work)
    gw = jnp.concatenate(vals, axis=0)
    e_ref[...] = jnp.concatenate(idxs, axis=0).astype(jnp.int32)
    g_ref[...] = gw / jnp.sum(gw, axis=0, keepdims=True) * ROUTED_SCALE


def _router(x1, router_w, router_b):
    T, D = x1.shape
    tm = min(ROW_TILE, T)
    return pl.pallas_call(
        _router_kernel, grid=(T // tm,),
        in_specs=[pl.BlockSpec((N_EXPERTS, D), lambda i: (0, 0)), pl.BlockSpec((tm, D), lambda i: (i, 0)),
                  pl.BlockSpec((N_EXPERTS, 1), lambda i: (0, 0))],
        out_specs=[pl.BlockSpec((TOP_K, tm), lambda i: (0, i))] * 2,
        out_shape=[jax.ShapeDtypeStruct((TOP_K, T), jnp.int32), jax.ShapeDtypeStruct((TOP_K, T), F32)],
        compiler_params=_cparams(("parallel",)), name="moe_router")(
            router_w.T, x1, router_b.reshape(N_EXPERTS, 1))


def _rank_kernel(e_ref, tri_ref, rank_ref, cnt_ref, carry_ref):
    @pl.when(pl.program_id(0) == 0)
    def _():
        carry_ref[...] = jnp.zeros(carry_ref.shape, F32)

    tm = e_ref.shape[1]
    erow = lax.broadcasted_iota(jnp.int32, (N_EXPERTS, tm), 0)
    tri = tri_ref[...]
    base = carry_ref[...]
    ranks = []
    for k in range(TOP_K):
        oh = erow == e_ref[k:k + 1, :]
        ohb = jnp.where(oh, 1.0, 0.0).astype(BF16)
        incl = jnp.dot(ohb, tri, preferred_element_type=F32)
        ranks.append(jnp.sum(jnp.where(oh, base + incl - 1.0, 0.0), axis=0, keepdims=True))
        base = base + incl[:, tm - 1:tm]
    carry_ref[...] = base
    rank_ref[...] = jnp.concatenate(ranks, axis=0).astype(jnp.int32)
    cnt_ref[...] = jnp.broadcast_to(base, cnt_ref.shape)


def _expert_ranks(eidx_t):
    K, T = eidx_t.shape
    tm = min(ROW_TILE, T)
    tri = jnp.asarray(np.triu(np.ones((tm, tm), np.float32)), BF16)
    return pl.pallas_call(
        _rank_kernel, grid=(T // tm,),
        in_specs=[pl.BlockSpec((K, tm), lambda i: (0, i)), pl.BlockSpec((tm, tm), lambda i: (0, 0))],
        out_specs=[pl.BlockSpec((K, tm), lambda i: (0, i)), pl.BlockSpec((N_EXPERTS, LANES), lambda i: (0, 0))],
        out_shape=[jax.ShapeDtypeStruct((K, T), jnp.int32), jax.ShapeDtypeStruct((N_EXPERTS, LANES), F32)],
        scratch_shapes=[pltpu.VMEM((N_EXPERTS, 1), F32)],
        compiler_params=_cparams(("arbitrary",)), name="moe_rank")(eidx_t, tri)


def _dest_kernel(e_ref, rank_ref, ps_ref, d_ref):
    tm = e_ref.shape[1]
    erow = lax.broadcasted_iota(jnp.int32, (N_EXPERTS, tm), 0)
    ps = ps_ref[...]
    rows = []
    for k in range(TOP_K):
        oh = erow == e_ref[k:k + 1, :]
        rows.append(jnp.sum(jnp.where(oh, ps, 0.0), axis=0, keepdims=True))
    d_ref[...] = jnp.concatenate(rows, axis=0).astype(jnp.int32) + rank_ref[...]


def _expert_dest(eidx_t, rank_t, pstarts):
    K, T = eidx_t.shape
    tm = min(ROW_TILE, T)
    blk = pl.BlockSpec((K, tm), lambda i: (0, i))
    return pl.pallas_call(
        _dest_kernel, grid=(T // tm,),
        in_specs=[blk, blk, pl.BlockSpec((N_EXPERTS, 1), lambda i: (0, 0))],
        out_specs=blk, out_shape=jax.ShapeDtypeStruct((K, T), jnp.int32),
        compiler_params=_cparams(("parallel",)), name="moe_dest")(
            eidx_t, rank_t, pstarts.astype(F32).reshape(N_EXPERTS, 1))


def _expert_kernel(be_ref, nb_ref, x_ref, wg_ref, wu_ref, wd_ref, o_ref, wgu_ref, wdb_ref):
    b = pl.program_id(0)
    E = wd_ref.shape[0]

    @pl.when((b == 0) | (be_ref[b] != be_ref[jnp.maximum(b - 1, 0)]))
    def _():
        wgu_ref[:, :E] = wg_ref[...].astype(BF16)
        wgu_ref[:, E:] = wu_ref[...].astype(BF16)
        wdb_ref[...] = wd_ref[...].astype(BF16)

    @pl.when(b < nb_ref[0])
    def _():
        gu = jnp.dot(_load_pieces(x_ref).astype(BF16), wgu_ref[...], preferred_element_type=F32)
        h = jax.nn.silu(gu[:, :E]) * gu[:, E:]
        _store_pieces(o_ref, jnp.dot(h.astype(BF16), wdb_ref[...], preferred_element_type=F32))

    @pl.when(b >= nb_ref[0])
    def _():
        o_ref[...] = jnp.zeros(o_ref.shape, o_ref.dtype)


def _expert_ffn(blk_e, nb_used, xs, wg, wu, wd, layer):
    P = xs.shape[1]
    D = wg.shape[2]
    NB = P // MOE_BLOCK
    E = EXPERT_DIM
    grid_spec = pltpu.PrefetchScalarGridSpec(
        num_scalar_prefetch=2, grid=(NB,),
        in_specs=[_piece_spec(D, MOE_BLOCK, lambda b, be, nb: (0, b, 0)),
                  pl.BlockSpec((None, None, D, E), lambda b, be, nb: (layer, be[b], 0, 0)),
                  pl.BlockSpec((None, None, D, E), lambda b, be, nb: (layer, be[b], 0, 0)),
                  pl.BlockSpec((None, None, E, D), lambda b, be, nb: (layer, be[b], 0, 0))],
        out_specs=_piece_spec(D, MOE_BLOCK, lambda b, be, nb: (0, b, 0)),
        scratch_shapes=[pltpu.VMEM((D, 2 * E), BF16), pltpu.VMEM((E, D), BF16)])
    return pl.pallas_call(
        _expert_kernel, grid_spec=grid_spec, out_shape=_piece_shape(D, P),
        compiler_params=_cparams(("arbitrary",)), name="moe_experts")(blk_e, nb_used, xs, wg, wu, wd)


def _post_moe_kernel(x_ref, *refs):
    y_refs = refs[:TOP_K]
    gw_ref, p_ref, sg_ref, su_ref, sd_ref, g_ref, b_ref, wg_ref, wp_ref, o_ref, ob_ref = refs[TOP_K:]
    x = x_ref[...]
    xb = x.astype(BF16)
    h = jax.nn.silu(jnp.dot(xb, sg_ref[...], preferred_element_type=F32)) * jnp.dot(
        xb, su_ref[...], preferred_element_type=F32)
    ffn = jnp.dot(h.astype(BF16), sd_ref[...], preferred_element_type=F32)
    gw = gw_ref[...]
    for k in range(TOP_K):
        ffn = ffn + gw[:, k:k + 1] * _load_pieces(y_refs[k])
    z = DN_ALPHA * x + ffn
    x2 = _layer_norm(z, g_ref[...], b_ref[...])
    gate = jax.nn.sigmoid(jnp.dot(x2.astype(BF16), wg_ref[...], preferred_element_type=F32))
    proj = jnp.dot(p_ref[...].astype(BF16), wp_ref[...], preferred_element_type=F32)
    out = x2 + gate * proj
    o_ref[...] = out
    ob_ref[...] = out.astype(BF16)


def _post_moe(x1, yg, gw, p, sg, su, sd, g, b, wgate, wproj):
    T, D = x1.shape
    tm = min(ROW_TILE // 2, T)
    nt = T // tm
    PD = p.shape[1]
    SD = sg.shape[1]
    row = pl.BlockSpec((tm, D), lambda i: (i, 0))
    vec = pl.BlockSpec((1, D), lambda i: (0, 0))
    full = lambda r, c: pl.BlockSpec((r, c), lambda i: (0, 0))
    return pl.pallas_call(
        _post_moe_kernel, grid=(T // tm,),
        in_specs=[row] + [_piece_spec(D, tm, functools.partial(lambda i, k: (0, k * nt + i, 0), k=k))
                          for k in range(TOP_K)]
                 + [pl.BlockSpec((tm, TOP_K), lambda i: (i, 0)),
                    pl.BlockSpec((tm, PD), lambda i: (i, 0)), full(D, SD), full(D, SD),
                  full(SD, D), vec, vec, full(D, D), full(PD, D)],
        out_specs=[row, row],
        out_shape=[jax.ShapeDtypeStruct((T, D), F32), jax.ShapeDtypeStruct((T, D), BF16)],
        compiler_params=_cparams(("parallel",)), name="post_moe")(
            x1, *([yg] * TOP_K), gw, p, sg, su, sd, g.reshape(1, D), b.reshape(1, D), wgate, wproj)


def _rope_tables(positions):
    inv = 1.0 / (ROPE_THETA ** (jnp.arange(0, HEAD_DIM, 2, dtype=F32) / HEAD_DIM))
    ang = positions.astype(F32).reshape(-1)[:, None] * inv
    cos, sin = jnp.cos(ang), jnp.sin(ang)
    cosf = jnp.concatenate([cos] * (LANES // (HEAD_DIM // 2)), axis=1)
    sinf = jnp.concatenate([-sin, sin] * (LANES // HEAD_DIM), axis=1)
    return cosf, sinf


def _q_col_scale(n_q, n):
    return jnp.concatenate([jnp.full((n_q,), Q_SCALE, F32), jnp.ones((n - n_q,), F32)])


def _fox_mixer(x, xb, B, S, w_in, b_f):
    D = D_MODEL
    wb = w_in[:, :3 * D].astype(BF16)
    qk = _proj(xb, wb[:, :2 * D], col_scale=_q_col_scale(D, 2 * D))
    vt = _proj_t(wb[:, 2 * D:].T, xb)
    fl = _proj(x, _pad_cols(w_in[:, 3 * D:], LANES), out_dtype=F32, tn=LANES, precision=HIGHEST)
    cp = _fox_gate(fl, b_f, B, S)
    return [_fox_attention(qk, cp, vt, B, S)]


def _nsa_mixer(x, xb, B, S, cosf, sinf, w_in, b_gate, pe_k, pe_v, ck_w1, ck_w2, cv_w1, cv_w2):
    D, G, HD = D_MODEL, NSA_KV_GROUPS, HEAD_DIM
    HG = NSA_HEADS_PER_GROUP
    kvw = G * HD
    wb = w_in[:, :D + 6 * kvw].astype(BF16)
    q_c, q_r = _proj(xb, wb[:, :D], mode="both", cos=cosf, sin=sinf, col_scale=_q_col_scale(D, D))
    w_rot = jnp.concatenate([wb[:, D + 2 * kvw:D + 3 * kvw], wb[:, D + 4 * kvw:D + 5 * kvw]], axis=1)
    k_rot = _proj(xb, w_rot, mode="rope", cos=cosf, sin=sinf)
    kvc = _proj(xb, wb[:, D:D + 2 * kvw])
    w_v = jnp.concatenate([wb[:, D + 3 * kvw:D + 4 * kvw], wb[:, D + 5 * kvw:D + 6 * kvw]], axis=1)
    vt = _proj_t(w_v.T, xb)
    gl = _proj(x, _pad_cols(w_in[:, D + 6 * kvw:], LANES), out_dtype=F32, tn=LANES, precision=HIGHEST)
    bg = _pad_cols(b_gate.reshape(1, -1), LANES)

    def grouped(t2d):
        t = t2d.reshape(B, S, G, HD).transpose(0, 2, 1, 3)
        return jnp.concatenate([t, t], axis=-1)

    n_chunks = S // NSA_CMP_STRIDE
    n_cmp = n_chunks - NSA_CMP_LEN // NSA_CMP_STRIDE + 1
    ncp = n_chunks

    def compress(t2d, pe, w1, w2):
        ch = t2d.reshape(B, n_chunks, NSA_CMP_STRIDE, G, HD).transpose(0, 1, 3, 2, 4)
        ch = ch.reshape(B, n_chunks, G, NSA_CMP_STRIDE * HD)
        flat = jnp.concatenate([ch[:, :n_cmp], ch[:, 1:n_cmp + 1]], axis=-1)
        flat = jnp.pad(flat, ((0, 0), (0, ncp - n_cmp), (0, 0), (0, 0))).reshape(B * ncp * G, -1)
        out = _nsa_compress(flat, pe.reshape(1, -1), w1.astype(BF16), _pad_cols(w2, LANES).astype(BF16))
        return out[:, :HD].reshape(B, ncp, G, HD).astype(BF16)

    kc = compress(kvc[:, :kvw], pe_k, ck_w1, ck_w2).transpose(0, 2, 1, 3)
    kk_c = jnp.concatenate([kc, kc], axis=-1)
    vt_c = compress(kvc[:, kvw:], pe_v, cv_w1, cv_w2).transpose(0, 2, 3, 1)

    n_sel = S // NSA_SEL_LEN
    assert n_sel <= LANES
    cmp_start = np.arange(ncp) * NSA_CMP_STRIDE
    sel_start = np.arange(LANES) * NSA_SEL_LEN
    overlap = ((cmp_start[:, None] < sel_start[None, :] + NSA_SEL_LEN)
               & (cmp_start[:, None] + NSA_CMP_LEN > sel_start[None, :])
               & (np.arange(ncp)[:, None] < n_cmp) & (np.arange(LANES)[None, :] < n_sel))
    overlap_t = jnp.asarray(overlap.T, BF16)
    ex = np.zeros((3, G, LANES, HG * HD), np.float32)
    for br in range(3):
        for g in range(G):
            for hg in range(HG):
                ex[br, g, (g * HG + hg) * 3 + br, hg * HD:(hg + 1) * HD] = 1.0
    ex = jnp.asarray(ex, BF16)

    o_c, selneg = _nsa_cmp_branch(q_c, kk_c, vt_c, overlap_t, gl, bg, ex[0], B, S)
    onehot = jnp.asarray((np.arange(S)[:, None] // NSA_SEL_LEN) == np.arange(LANES)[None, :], BF16)
    kk_s = grouped(k_rot[:, :kvw])
    kk_w = grouped(k_rot[:, kvw:])
    o_s = _nsa_kv_branch("sel", q_r, kk_s, vt[:kvw], gl, bg, ex[1], B, S, onehot=onehot, selneg=selneg)
    o_w = _nsa_kv_branch("win", q_r, kk_w, vt[kvw:], gl, bg, ex[2], B, S)
    return [o_c, o_s, o_w]


def _moba_mixer(xb, B, S, cosf, sinf, w_in):
    D, H, HD = D_MODEL, N_HEADS, HEAD_DIM
    wb = w_in.astype(BF16)
    qk_rot = _proj(xb, wb[:, :2 * D], mode="rope", cos=cosf, sin=sinf, col_scale=_q_col_scale(D, 2 * D))
    vt = _proj_t(wb[:, 2 * D:].T, xb)
    nblk = S // MOBA_BLOCK
    assert S % MOBA_BLOCK == 0 and 2 * nblk <= LANES and nblk % SUBLANES == 0
    kmean = _moba_kmean(qk_rot[:, D:]).reshape(B, nblk, H // 2, 2, HD)
    km = kmean.transpose(0, 2, 3, 1, 4)
    rmat_t = jnp.zeros((B, H // 2, LANES, 2, HD), F32)
    rmat_t = rmat_t.at[:, :, :nblk, 0].set(km[:, :, 0]).at[:, :, nblk:2 * nblk, 1].set(km[:, :, 1])
    selneg = _moba_select(qk_rot, rmat_t.reshape(B, H // 2, LANES, LANES), B, S)
    blk_of = np.arange(S) // MOBA_BLOCK
    oh = np.zeros((S, LANES), np.float32)
    oh[np.arange(S), blk_of] = 1.0
    oh[np.arange(S), nblk + blk_of] = 1.0
    return [_moba_attention(qk_rot, vt, jnp.asarray(oh, BF16), selneg, B, S)]


def _sc_invert_kernel(dest_hbm, tok_hbm, out_hbm, d_v, t_v, buf_v, *, seg, chunk, n_chunks, n_tok):
    lo = pl.multiple_of((lax.axis_index("s") * SC_CORES + lax.axis_index("c")) * seg, SC_LANES)

    @pl.loop(0, seg // SC_LANES)
    def _(j):
        row = lo + j * SC_LANES + lax.iota(jnp.int32, SC_LANES)
        buf_v[pl.ds(j * SC_LANES, SC_LANES)] = lax.rem(row, jnp.int32(n_tok))

    @pl.loop(0, n_chunks)
    def _(c):
        pltpu.sync_copy(dest_hbm.at[pl.ds(c * chunk, chunk)], d_v)
        pltpu.sync_copy(tok_hbm.at[pl.ds(c * chunk, chunk)], t_v)

        @pl.loop(0, chunk // SC_LANES)
        def _(j):
            d = d_v[pl.ds(j * SC_LANES, SC_LANES)] - lo
            mine = (d >= 0) & (d < seg)
            plsc.store_scatter(buf_v, [jnp.where(mine, d, 0)], t_v[pl.ds(j * SC_LANES, SC_LANES)], mask=mine)

    pltpu.sync_copy(buf_v, out_hbm.at[pl.ds(lo, seg)])


def _invert_dest(dest, tok, P, n_tok):
    R = dest.shape[0]
    n_workers = SC_CORES * SC_SUBCORES
    seg = P // n_workers
    chunk = min(4096, R)
    assert P == seg * n_workers and seg % SC_LANES == 0 and R % chunk == 0
    mesh = plsc.VectorSubcoreMesh(core_axis_name="c", subcore_axis_name="s",
                                  num_cores=SC_CORES, num_subcores=SC_SUBCORES)
    return pl.kernel(
        functools.partial(_sc_invert_kernel, seg=seg, chunk=chunk, n_chunks=R // chunk, n_tok=n_tok),
        out_type=jax.ShapeDtypeStruct((P,), jnp.int32), mesh=mesh,
        scratch_types=[pltpu.VMEM((chunk,), jnp.int32), pltpu.VMEM((chunk,), jnp.int32),
                       pltpu.VMEM((seg,), jnp.int32)],
        compiler_params=pltpu.CompilerParams(needs_layout_passes=False),
        name="moe_invert")(dest, tok)


def _sc_mesh():
    return plsc.VectorSubcoreMesh(core_axis_name="c", subcore_axis_name="s",
                                  num_cores=SC_CORES, num_subcores=SC_SUBCORES)


def _gather_rows(table, idx):
    parts, V, _ = table.shape
    N = idx.shape[0]
    n_pieces = N * parts
    n_workers = SC_CORES * SC_SUBCORES
    assert n_pieces % (SC_WINDOW * n_workers) == 0
    pieces = (jnp.arange(parts, dtype=jnp.int32)[:, None] * V + idx[None, :]).reshape(1, n_pieces)

    def kernel_body(x_hbm, i_hbm, o_hbm):
        def body(i_vmem, o_vmem):
            pltpu.sync_copy(x_hbm.at[i_vmem.at[0]], o_vmem)

        pltpu.emit_pipeline(
            body, grid=(n_pieces // SC_WINDOW,),
            in_specs=[pl.BlockSpec((1, SC_WINDOW), lambda i: (0, i))],
            out_specs=[pl.BlockSpec((SC_WINDOW, SC_ROW_WORDS), lambda i: (i, 0))],
            core_axis_name=("c", "s"), dimension_semantics=(pltpu.PARALLEL,))(i_hbm, o_hbm)

    out = pl.kernel(kernel_body, out_type=jax.ShapeDtypeStruct((n_pieces, SC_ROW_WORDS), table.dtype),
                    mesh=_sc_mesh(), scratch_types=[], name="moe_gather")(
                        table.reshape(parts * V, SC_ROW_WORDS), pieces)
    return out.reshape(parts, N, SC_ROW_WORDS)


def _moe_dispatch(x1, x1p, router_w, router_b, wg, wu, wd, layer):
    T, D = x1.shape
    eidx_t, gw_t = _router(x1, router_w, router_b)
    rank_t, cnt = _expert_ranks(eidx_t)
    counts = cnt[:, 0].astype(jnp.int32)
    padded = (counts + MOE_BLOCK - 1) // MOE_BLOCK * MOE_BLOCK
    pends = jnp.cumsum(padded)
    pstarts = pends - padded
    R = T * TOP_K
    P = -(-(R + N_EXPERTS * (MOE_BLOCK - 1)) // MOE_BLOCK) * MOE_BLOCK
    NB = P // MOE_BLOCK
    blk_start = jnp.arange(NB, dtype=jnp.int32) * MOE_BLOCK
    blk_e = jnp.minimum(jnp.sum(pends[None, :] <= blk_start[:, None], axis=1), N_EXPERTS - 1).astype(jnp.int32)
    nb_used = (pends[-1] // MOE_BLOCK).astype(jnp.int32).reshape(1)
    dest_t = _expert_dest(eidx_t, rank_t, pstarts)

    tok = jnp.broadcast_to(jnp.arange(T, dtype=jnp.int32)[None, :], (TOP_K, T))
    buf_tok = _invert_dest(dest_t.reshape(-1), tok.reshape(-1), P, T)
    xs = _gather_rows(x1p, buf_tok)
    yb = _expert_ffn(blk_e, nb_used, xs, wg, wu, wd, layer)
    yg = _gather_rows(yb, dest_t.reshape(-1))
    return yg, gw_t.T


def kernel(x, p, positions, fox_w_in, fox_b_f, fox_w_out, nsa_w_in, nsa_b_gate, nsa_pe_k, nsa_pe_v,
           nsa_cmp_k_w1, nsa_cmp_k_w2, nsa_cmp_v_w1, nsa_cmp_v_w2, nsa_w_out, moba_w_in, moba_w_out,
           ln1_g, ln1_b, router_w, router_b, exp_w_gate, exp_w_up, exp_w_down,
           sh_w_gate, sh_w_up, sh_w_down, ln2_g, ln2_b, ple_w_gate, ple_w_proj):
    B, S, D = x.shape
    T = B * S
    depth = p.shape[0]
    cosf, sinf = _rope_tables(positions)
    xt = x.reshape(T, D)
    xtb = xt.astype(BF16)
    for i in range(depth):
        kind, j = i % N_MIXERS, i // N_MIXERS
        if kind == 0:
            o_list = _fox_mixer(xt, xtb, B, S, fox_w_in[j], fox_b_f[j])
            w_out = fox_w_out[j]
        elif kind == 1:
            o_list = _nsa_mixer(xt, xtb, B, S, cosf, sinf, nsa_w_in[j], nsa_b_gate[j], nsa_pe_k[j], nsa_pe_v[j],
                                nsa_cmp_k_w1[j], nsa_cmp_k_w2[j], nsa_cmp_v_w1[j], nsa_cmp_v_w2[j])
            w_out = nsa_w_out[j]
        else:
            o_list = _moba_mixer(xtb, B, S, cosf, sinf, moba_w_in[j])
            w_out = moba_w_out[j]
        x1, x1p = _outproj_ln(o_list, w_out.astype(BF16), xt, ln1_g[i], ln1_b[i])
        yg, gw = _moe_dispatch(x1, x1p, router_w[i], router_b[i], exp_w_gate, exp_w_up, exp_w_down, i)
        xt, xtb = _post_moe(x1, yg, gw, p[i].reshape(T, -1), sh_w_gate[i].astype(BF16), sh_w_up[i].astype(BF16),
                            sh_w_down[i].astype(BF16), ln2_g[i], ln2_b[i],
                            ple_w_gate[i].astype(BF16), ple_w_proj[i].astype(BF16))
    return xt.reshape(B, S, D)
```

```python
import functools

import jax
import jax.numpy as jnp
import numpy as np
from jax import lax
from jax.experimental import pallas as pl
from jax.experimental.pallas import tpu as pltpu
from jax.experimental.pallas import tpu_sc as plsc

D_MODEL = 1024
DEPTH = 4
N_HEADS = 16
HEAD_DIM = 64
ATTN_SCALE = HEAD_DIM ** -0.5
LOG2E = 1.4426950408889634
Q_SCALE = ATTN_SCALE * LOG2E
ROPE_THETA = 10000.0
N_MIXERS = 3

NSA_KV_GROUPS = 4
NSA_HEADS_PER_GROUP = N_HEADS // NSA_KV_GROUPS
NSA_CMP_LEN = 32
NSA_CMP_STRIDE = 16
NSA_SEL_LEN = 64
NSA_SEL_TOPN = 16
NSA_WINDOW = 512
NSA_FORCE_BONUS = 1e4

MOBA_BLOCK = 256
MOBA_TOPK = 3

N_EXPERTS = 64
EXPERT_DIM = 256
TOP_K = 8
N_GROUPS = 8
TOPK_GROUPS = 4
ROUTED_SCALE = 2.5
MOE_BLOCK = 256
EXPERT_SLOTS = 2

DN_ALPHA = (2 * DEPTH) ** 0.25
LN_EPS = 1e-5
NEG = -1e30

SC_CORES = 2
SC_SUBCORES = 16
SC_LANES = 16
SC_ROW_WORDS = 256
SC_WINDOW = 128
LANES = 128
SUBLANES = 8
ATTN_TILE = 512
PAIR_TILE = 1024
KV_TILE = 1024
ROW_TILE = 512
VMEM_LIMIT = 48 * 1024 * 1024
C_SPLIT = 3
ONES_ROWS = 16

F32 = jnp.float32
BF16 = jnp.bfloat16
HIGHEST = lax.Precision.HIGHEST


def _cparams(sem):
    return pltpu.CompilerParams(dimension_semantics=sem, vmem_limit_bytes=VMEM_LIMIT)


def _log2(n):
    assert n & (n - 1) == 0
    return n.bit_length() - 1


def _nt_dot(a, b, precision=None):
    return lax.dot_general(a, b, (((1,), (1,)), ((), ())), preferred_element_type=F32,
                           precision=precision)


def _split_bf16(a):
    hi = a.astype(BF16)
    return hi, (a - hi.astype(F32)).astype(BF16)


def _dot3(x, w, nt=False):
    dot = _nt_dot if nt else functools.partial(jnp.dot, preferred_element_type=F32)
    xh, xl = _split_bf16(x)
    wh, wl = _split_bf16(w)
    return dot(xh, wh) + dot(xl, wh) + dot(xh, wl)


def _topk_rows(work, n):
    rows = lax.broadcasted_iota(jnp.int32, work.shape, 0).astype(F32)
    chosen = jnp.zeros(work.shape, F32)
    for _ in range(n):
        m = jnp.max(work, axis=0, keepdims=True)
        idx = jnp.min(jnp.where(work == m, rows, float(work.shape[0])), axis=0, keepdims=True)
        pick = rows == idx
        chosen = jnp.where(pick, 1.0, chosen)
        work = jnp.where(pick, -jnp.inf, work)
    return chosen


def _proj_kernel(*refs, mode, precision, scaled):
    refs = list(refs)
    x_ref, w_ref = refs[:2]
    del refs[:2]
    scale_ref = refs.pop(0) if scaled else None
    if mode == "none":
        (o_ref,) = refs
    elif mode == "rope":
        cos_ref, sin_ref, r_ref = refs
    else:
        cos_ref, sin_ref, o_ref, r_ref = refs
    x = x_ref[...]
    w = w_ref[...]
    if precision == "split3":
        acc = _dot3(x, w)
    else:
        if x.dtype != w.dtype:
            x = x.astype(w.dtype)
        acc = jnp.dot(x, w, preferred_element_type=F32, precision=precision)
    if scaled:
        acc = acc * scale_ref[...]
    if mode in ("none", "both"):
        o_ref[...] = acc.astype(o_ref.dtype)
    if mode in ("rope", "both"):
        tn = acc.shape[1]
        rep = tn // LANES
        cosf = jnp.concatenate([cos_ref[...]] * rep, axis=1)
        sinf = jnp.concatenate([sin_ref[...]] * rep, axis=1)
        lane = lax.broadcasted_iota(jnp.int32, acc.shape, 1)
        first_half = (lane & (HEAD_DIM // 2)) == 0
        swapped = jnp.where(first_half, pltpu.roll(acc, tn - HEAD_DIM // 2, 1),
                            pltpu.roll(acc, HEAD_DIM // 2, 1))
        r_ref[...] = (acc * cosf + swapped * sinf).astype(r_ref.dtype)


def _proj(x, w, *, mode="none", cos=None, sin=None, out_dtype=BF16, tn=512, precision=None, col_scale=None):
    T, K = x.shape
    N = w.shape[1]
    tm = min(ROW_TILE, T)
    tn = min(tn, N)
    assert T % tm == 0 and N % tn == 0
    grid = (T // tm, N // tn)
    in_specs = [pl.BlockSpec((tm, K), lambda i, j: (i, 0)),
                pl.BlockSpec((K, tn), lambda i, j: (0, j))]
    args = [x, w]
    if col_scale is not None:
        in_specs.append(pl.BlockSpec((1, tn), lambda i, j: (0, j)))
        args.append(col_scale.reshape(1, N).astype(F32))
    if mode != "none":
        in_specs += [pl.BlockSpec((tm, LANES), lambda i, j: (i, 0))] * 2
        args += [cos, sin]
    o_spec = pl.BlockSpec((tm, tn), lambda i, j: (i, j))
    o_shape = jax.ShapeDtypeStruct((T, N), out_dtype)
    if mode == "both":
        out_specs, out_shape = [o_spec, o_spec], [o_shape, o_shape]
    else:
        out_specs, out_shape = o_spec, o_shape
    return pl.pallas_call(
        functools.partial(_proj_kernel, mode=mode, precision=precision, scaled=col_scale is not None),
        grid=grid, in_specs=in_specs, out_specs=out_specs, out_shape=out_shape,
        compiler_params=_cparams(("parallel", "parallel")), name=f"proj_{mode}")(*args)


def _proj_t_kernel(wt_ref, x_ref, o_ref, *, precision):
    wt = wt_ref[...]
    x = x_ref[...]
    if x.dtype != wt.dtype:
        x = x.astype(wt.dtype)
    o_ref[...] = _nt_dot(wt, x, precision).astype(o_ref.dtype)


def _proj_t(wt, x, *, out_dtype=BF16, precision=None):
    N, K = wt.shape
    T = x.shape[0]
    tm = min(ROW_TILE, T)
    tn = min(512, N)
    assert T % tm == 0 and N % tn == 0
    return pl.pallas_call(
        functools.partial(_proj_t_kernel, precision=precision),
        grid=(T // tm, N // tn),
        in_specs=[pl.BlockSpec((tn, K), lambda i, j: (j, 0)), pl.BlockSpec((tm, K), lambda i, j: (i, 0))],
        out_specs=pl.BlockSpec((tn, tm), lambda i, j: (j, i)),
        out_shape=jax.ShapeDtypeStruct((N, T), out_dtype),
        compiler_params=_cparams(("parallel", "parallel")), name="proj_t")(wt, x)


def _pack_rows(v):
    n = v.shape[1] // 2
    lo = pltpu.bitcast(v[:, :n].astype(BF16).astype(F32), jnp.int32)
    hi = pltpu.bitcast(v[:, n:].astype(BF16).astype(F32), jnp.int32)
    return (hi & jnp.int32(-65536)) | lax.shift_right_logical(lo, jnp.int32(16))


def _unpack_rows(w):
    lo = pltpu.bitcast(lax.shift_left(w, jnp.int32(16)), F32)
    hi = pltpu.bitcast(w & jnp.int32(-65536), F32)
    return jnp.concatenate([lo, hi], axis=1)


def _store_pieces(ref, v):
    w = _pack_rows(v)
    for j in range(ref.shape[0]):
        ref[j] = w[:, j * SC_ROW_WORDS:(j + 1) * SC_ROW_WORDS]


def _load_pieces(ref):
    return _unpack_rows(jnp.concatenate([ref[j] for j in range(ref.shape[0])], axis=1))


def _piece_spec(d, rows, index_map):
    return pl.BlockSpec((d // 2 // SC_ROW_WORDS, rows, SC_ROW_WORDS), index_map)


def _piece_shape(d, rows):
    return jax.ShapeDtypeStruct((d // 2 // SC_ROW_WORDS, rows, SC_ROW_WORDS), jnp.int32)


def _pad_cols(w, n):
    return jnp.pad(w, ((0, 0), (0, n - w.shape[1])))


def _layer_norm(z, g, b):
    mu = jnp.mean(z, axis=-1, keepdims=True)
    zc = z - mu
    var = jnp.mean(zc * zc, axis=-1, keepdims=True)
    return zc * lax.rsqrt(var + LN_EPS) * g + b


def _outproj_ln_kernel(*refs, n_o):
    o_refs = refs[:n_o]
    w_ref, x_ref, g_ref, b_ref, y_ref, yp_ref = refs[n_o:]
    if n_o == 1:
        o = o_refs[0][...]
    else:
        o = o_refs[0][...].astype(F32)
        for r in o_refs[1:]:
            o = o + r[...].astype(F32)
        o = o.astype(BF16)
    mix = jnp.dot(o, w_ref[...], preferred_element_type=F32)
    z = DN_ALPHA * x_ref[...] + mix
    y = _layer_norm(z, g_ref[...], b_ref[...])
    y_ref[...] = y
    _store_pieces(yp_ref, y)


def _outproj_ln(o_list, w, x, g, b):
    T, D = x.shape
    tm = min(ROW_TILE, T)
    row = pl.BlockSpec((tm, D), lambda i: (i, 0))
    vec = pl.BlockSpec((1, D), lambda i: (0, 0))
    return pl.pallas_call(
        functools.partial(_outproj_ln_kernel, n_o=len(o_list)),
        grid=(T // tm,),
        in_specs=[row] * len(o_list) + [pl.BlockSpec((D, D), lambda i: (0, 0)), row, vec, vec],
        out_specs=[row, _piece_spec(D, tm, lambda i: (0, i, 0))],
        out_shape=[jax.ShapeDtypeStruct((T, D), F32), _piece_shape(D, T)],
        compiler_params=_cparams(("parallel",)), name="outproj_ln")(
            *o_list, w, x, g.reshape(1, D), b.reshape(1, D))


def _flash_init(m_ref, acc_ref):
    m_ref[...] = jnp.full(m_ref.shape, NEG, F32)
    acc_ref[...] = jnp.zeros(acc_ref.shape, F32)


def _flash_step(sT, lhs, m_ref, acc_ref, h, cols=slice(None)):
    m_prev = m_ref[h, :, cols]
    m_new = jnp.maximum(m_prev, jnp.max(sT, axis=0, keepdims=True))
    p = jnp.exp2(sT - m_new)
    alpha = jnp.exp2(m_prev - m_new)
    acc_ref[h, :, cols] = alpha * acc_ref[h, :, cols] + jnp.dot(lhs, p.astype(BF16), preferred_element_type=F32)
    m_ref[h, :, cols] = m_new


def _causal_t(tk, tq, shift=0, strict_lower=False):
    key = lax.broadcasted_iota(jnp.int32, (tk, tq), 0)
    qry = lax.broadcasted_iota(jnp.int32, (tk, tq), 1)
    return (key > qry) if strict_lower else (key <= qry + shift)


def _causal_sweep(i, tq, tk, step):
    r = tk // tq
    assert tk == r * tq
    n_full = i >> _log2(r)

    def body(j, carry):
        step(pl.multiple_of(j * tk, tk), tk, None)
        return carry

    lax.fori_loop(0, n_full, body, 0)
    if r == 1:
        step(pl.multiple_of(i * tq, tq), tq, _causal_t(tq, tq))
    else:
        for rem in range(r):
            @pl.when(i - n_full * r == rem)
            def _(rem=rem):
                size = (rem + 1) * tq
                step(pl.multiple_of(n_full * tk, tk), size, _causal_t(size, tq, shift=rem * tq))


def _staggered(units, scores, update):
    s = [scores(units[0])]
    for n, u in enumerate(units):
        if n + 1 < len(units):
            s.append(scores(units[n + 1]))
        update(u, s[n])


def _pair_units(tq):
    w = min(ATTN_TILE, tq)
    return [(h, slice(c * w, (c + 1) * w)) for h in range(2) for c in range(tq // w)]


def _head_of_pair(q_pair, which):
    lane = lax.broadcasted_iota(jnp.int32, q_pair.shape, 1)
    keep = (lane < HEAD_DIM) if which == 0 else (lane >= HEAD_DIM)
    return jnp.where(keep, q_pair, jnp.zeros_like(q_pair))


def _pair_rows(a_top, a_bot):
    row = lax.broadcasted_iota(jnp.int32, a_top.shape, 0)
    return jnp.where(row < HEAD_DIM, a_top, a_bot)


def _pair_lhs(vt, which):
    row = lax.broadcasted_iota(jnp.int32, vt.shape, 0)
    keep = (row < HEAD_DIM) if which == 0 else (row >= HEAD_DIM)
    return jnp.where(keep, vt, jnp.ones_like(vt))


def _pair_finish(acc_ref):
    a0, a1 = acc_ref[0], acc_ref[1]
    return _pair_rows(a0 / a0[HEAD_DIM:HEAD_DIM + 1, :], a1 / a1[0:1, :])


def _fox_gate_kernel(fl_ref, bf_ref, tri_ref, ex_ref, cp_ref, carry_ref):
    @pl.when(pl.program_id(1) == 0)
    def _():
        carry_ref[...] = jnp.zeros(carry_ref.shape, F32)

    z = fl_ref[...] + bf_ref[...]
    log_f = jnp.minimum(z, 0.0) - jnp.log1p(jnp.exp(-jnp.abs(z)))
    c = jnp.dot(tri_ref[...], log_f, preferred_element_type=F32, precision=HIGHEST) + carry_ref[...]
    carry_ref[...] = c[-1:, :]
    out = jnp.zeros(c.shape, F32)
    rem = c * LOG2E
    for piece in range(C_SPLIT):
        part = rem.astype(BF16)
        rem = rem - part.astype(F32)
        out = out + jnp.dot(part, ex_ref[piece], preferred_element_type=F32)
    cp_ref[...] = out.astype(BF16)


def _fox_gate(fl, b_f, B, S):
    T = fl.shape[0]
    tm = min(ROW_TILE, S)
    ns = S // tm
    tri = jnp.asarray(np.tril(np.ones((tm, tm), np.float32)))
    ex = np.zeros((C_SPLIT, LANES, LANES), np.float32)
    for h in range(N_HEADS):
        for j in range(C_SPLIT):
            ex[j, h, C_SPLIT * h + j] = 1.0
    bf = _pad_cols(b_f.reshape(1, -1), LANES)
    return pl.pallas_call(
        _fox_gate_kernel, grid=(B, ns),
        in_specs=[pl.BlockSpec((tm, LANES), lambda b, s: (b * ns + s, 0)),
                  pl.BlockSpec((1, LANES), lambda b, s: (0, 0)),
                  pl.BlockSpec((tm, tm), lambda b, s: (0, 0)),
                  pl.BlockSpec((C_SPLIT, LANES, LANES), lambda b, s: (0, 0, 0))],
        out_specs=pl.BlockSpec((tm, LANES), lambda b, s: (b * ns + s, 0)),
        out_shape=jax.ShapeDtypeStruct((T, LANES), BF16),
        scratch_shapes=[pltpu.VMEM((1, LANES), F32)],
        compiler_params=_cparams(("parallel", "arbitrary")), name="fox_gate")(
            fl, bf, tri, jnp.asarray(ex, BF16))


def _fox_kernel(q_ref, k_ref, cp_ref, vt_ref, o_ref, m_ref, acc_ref, *, tq, tk):
    hp = pl.program_id(1)
    i = pl.program_id(2)
    _flash_init(m_ref, acc_ref)
    q = q_ref[...]
    lane = lax.broadcasted_iota(jnp.int32, q.shape, 1)
    qa = []
    for h in range(2):
        lo = C_SPLIT * (2 * hp + h)
        minus_one = jnp.where((lane >= lo) & (lane < lo + C_SPLIT), -1.0, 0.0).astype(q.dtype)
        qa.append(jnp.concatenate([_head_of_pair(q, h), minus_one], axis=1))

    def step(off, size, mask):
        k = jnp.concatenate([k_ref[pl.ds(off, size), :], cp_ref[pl.ds(off, size), :]], axis=1)
        vt = vt_ref[:, pl.ds(off, size)]

        lhs = [_pair_lhs(vt, h) for h in range(2)]

        def scores(u):
            sT = _nt_dot(k, qa[u[0]][u[1], :])
            return sT if mask is None else jnp.where(mask[:, u[1]], sT, NEG)

        _staggered(_pair_units(tq), scores,
                   lambda u, sT: _flash_step(sT, lhs[u[0]], m_ref, acc_ref, u[0], u[1]))

    _causal_sweep(i, tq, tk, step)
    o_ref[...] = _pair_finish(acc_ref).T.astype(o_ref.dtype)


def _fox_attention(qk, cp, vt, B, S):
    T = qk.shape[0]
    tq = min(PAIR_TILE, S)
    tk = min(KV_TILE, S)
    nq = S // tq
    npair = N_HEADS // 2
    ncol = D_MODEL // LANES
    return pl.pallas_call(
        functools.partial(_fox_kernel, tq=tq, tk=tk),
        grid=(B, npair, nq),
        in_specs=[pl.BlockSpec((tq, LANES), lambda b, hp, i: (b * nq + i, hp)),
                  pl.BlockSpec((S, LANES), lambda b, hp, i: (b, ncol + hp)),
                  pl.BlockSpec((S, LANES), lambda b, hp, i: (b, 0)),
                  pl.BlockSpec((LANES, S), lambda b, hp, i: (hp, b))],
        out_specs=pl.BlockSpec((tq, LANES), lambda b, hp, i: (b * nq + i, hp)),
        out_shape=jax.ShapeDtypeStruct((T, D_MODEL), BF16),
        scratch_shapes=[pltpu.VMEM((2, 1, tq), F32), pltpu.VMEM((2, LANES, tq), F32)],
        compiler_params=_cparams(("parallel", "parallel", "arbitrary")), name="fox_attn")(
            qk, qk, cp, vt)


def _gate_expand(gl_ref, bg_ref, ex_ref):
    sig = jax.nn.sigmoid(gl_ref[...] + bg_ref[...])
    hi = sig.astype(BF16)
    lo = (sig - hi.astype(F32)).astype(BF16)
    ex = ex_ref[...]
    return jnp.dot(hi, ex, preferred_element_type=F32) + jnp.dot(lo, ex, preferred_element_type=F32)


def _nsa_compress_kernel(f_ref, pe_ref, w1_ref, w2_ref, o_ref):
    blk = (f_ref[...].astype(F32) + pe_ref[...]).astype(BF16)
    h = jax.nn.gelu(jnp.dot(blk, w1_ref[...], preferred_element_type=F32))
    o_ref[...] = jnp.dot(h.astype(BF16), w2_ref[...], preferred_element_type=F32)


def _nsa_compress(flat, pe_flat, w1, w2p):
    M, K = flat.shape
    tm = min(ROW_TILE, M)
    Hc = w1.shape[1]
    return pl.pallas_call(
        _nsa_compress_kernel, grid=(M // tm,),
        in_specs=[pl.BlockSpec((tm, K), lambda i: (i, 0)), pl.BlockSpec((1, K), lambda i: (0, 0)),
                  pl.BlockSpec((K, Hc), lambda i: (0, 0)), pl.BlockSpec((Hc, LANES), lambda i: (0, 0))],
        out_specs=pl.BlockSpec((tm, LANES), lambda i: (i, 0)),
        out_shape=jax.ShapeDtypeStruct((M, LANES), F32),
        compiler_params=_cparams(("parallel",)), name="nsa_compress")(flat, pe_flat, w1, w2p)


def _nsa_cmp_kernel(q_ref, kk_ref, vt_ref, ovt_ref, gl_ref, bg_ref, ex_ref, o_ref, sn_ref, *, tq, ncp, nselp):
    i = pl.program_id(2)
    q = q_ref[...]
    kk = kk_ref[...]
    vt = vt_ref[...]
    t = i * tq + lax.broadcasted_iota(jnp.int32, (ncp, tq), 1)
    n = lax.broadcasted_iota(jnp.int32, (ncp, tq), 0)
    valid = n * NSA_CMP_STRIDE + (NSA_CMP_LEN - 1) <= t
    psum = jnp.zeros((ncp, tq), F32)
    outs = []
    for hg in range(NSA_HEADS_PER_GROUP):
        qh = _head_of_pair(q[:, LANES * (hg // 2):LANES * (hg // 2 + 1)], hg % 2)
        sT = jnp.where(valid, _nt_dot(kk, qh), NEG)
        m = jnp.max(sT, axis=0, keepdims=True)
        e = jnp.where(valid, jnp.exp2(sT - m), 0.0)
        p = e / jnp.maximum(jnp.sum(e, axis=0, keepdims=True), 1e-30)
        psum = psum + p
        outs.append(jnp.dot(vt, p.astype(BF16), preferred_element_type=F32))
    o = jnp.concatenate(outs, axis=0).T
    o_ref[...] = (o * _gate_expand(gl_ref, bg_ref, ex_ref)).astype(o_ref.dtype)
    hi = psum.astype(BF16)
    lo = (psum - hi.astype(F32)).astype(BF16)
    ovt = ovt_ref[...]
    imp = jnp.dot(ovt, hi, preferred_element_type=F32) + jnp.dot(ovt, lo, preferred_element_type=F32)
    jblk = lax.broadcasted_iota(jnp.int32, (nselp, tq), 0)
    cur = (i * tq + lax.broadcasted_iota(jnp.int32, (nselp, tq), 1)) >> _log2(NSA_SEL_LEN)
    forced = (jblk == 0) | (jblk == cur) | (jblk == cur - 1)
    pri = jnp.where(forced, imp + NSA_FORCE_BONUS, imp)
    past = jblk <= cur
    pri = jnp.where(past, pri, NEG)
    chosen = _topk_rows(pri, NSA_SEL_TOPN)
    sn = jnp.where((chosen > 0.0) & past, 0.0, NEG)
    sn_ref[...] = sn.T.astype(sn_ref.dtype)


def _nsa_cmp_branch(q_c, kk_c, vt_c, overlap_t, gl, bg, ex, B, S):
    T = q_c.shape[0]
    G = NSA_KV_GROUPS
    tq = min(ATTN_TILE, S)
    nq = S // tq
    ncp = kk_c.shape[2]
    nselp = overlap_t.shape[0]
    return pl.pallas_call(
        functools.partial(_nsa_cmp_kernel, tq=tq, ncp=ncp, nselp=nselp),
        grid=(B, G, nq),
        in_specs=[pl.BlockSpec((tq, 2 * LANES), lambda b, g, i: (b * nq + i, g)),
                  pl.BlockSpec((None, None, ncp, LANES), lambda b, g, i: (b, g, 0, 0)),
                  pl.BlockSpec((None, None, HEAD_DIM, ncp), lambda b, g, i: (b, g, 0, 0)),
                  pl.BlockSpec((nselp, ncp), lambda b, g, i: (0, 0)),
                  pl.BlockSpec((tq, LANES), lambda b, g, i: (b * nq + i, 0)),
                  pl.BlockSpec((1, LANES), lambda b, g, i: (0, 0)),
                  pl.BlockSpec((None, LANES, 2 * LANES), lambda b, g, i: (g, 0, 0))],
        out_specs=[pl.BlockSpec((tq, 2 * LANES), lambda b, g, i: (b * nq + i, g)),
                   pl.BlockSpec((None, None, tq, nselp), lambda b, g, i: (b, g, i, 0))],
        out_shape=[jax.ShapeDtypeStruct((T, D_MODEL), BF16),
                   jax.ShapeDtypeStruct((B, G, S, nselp), BF16)],
        compiler_params=_cparams(("parallel", "parallel", "parallel")), name="nsa_cmp")(
            q_c, kk_c, vt_c, overlap_t, gl, bg, ex)


def _nsa_kv_kernel(*refs, mode, tq, tk):
    if mode == "sel":
        q_ref, kk_ref, vt_ref, oh_ref, sn_ref, gl_ref, bg_ref, ex_ref, o_ref, m_ref, acc_ref = refs
    else:
        q_ref, kk_ref, vt_ref, gl_ref, bg_ref, ex_ref, o_ref, m_ref, acc_ref = refs
    i = pl.program_id(2)
    _flash_init(m_ref, acc_ref)
    q = q_ref[...]
    qh = []
    for hg in range(NSA_HEADS_PER_GROUP):
        qq = _head_of_pair(q[:, LANES * (hg // 2):LANES * (hg // 2 + 1)], hg % 2)
        if mode == "sel":
            qq = jnp.concatenate([qq, sn_ref[...]], axis=1)
        qh.append(qq)

    def step(off, size, mask):
        k = kk_ref[pl.ds(off, size), :]
        vt = vt_ref[:, pl.ds(off, size)]
        lhs = jnp.concatenate([vt, jnp.ones((ONES_ROWS, size), vt.dtype)], axis=0)
        if mode == "sel":
            k = jnp.concatenate([k, oh_ref[pl.ds(off, size), :]], axis=1)

        def scores(hg):
            sT = _nt_dot(k, qh[hg])
            return sT if mask is None else jnp.where(mask, sT, NEG)

        _staggered(list(range(NSA_HEADS_PER_GROUP)), scores,
                   lambda hg, sT: _flash_step(sT, lhs, m_ref, acc_ref, hg))

    if mode == "sel":
        _causal_sweep(i, tq, tk, step)
    else:
        @pl.when(i > 0)
        def _():
            step(pl.multiple_of((i - 1) * tq, tq), tq, _causal_t(tq, tq, strict_lower=True))
        step(pl.multiple_of(i * tq, tq), tq, _causal_t(tq, tq))
    outs = []
    for hg in range(NSA_HEADS_PER_GROUP):
        a = acc_ref[hg]
        outs.append(a[:HEAD_DIM] / a[HEAD_DIM:HEAD_DIM + 1, :])
    o_ref[...] = (jnp.concatenate(outs, axis=0).T * _gate_expand(gl_ref, bg_ref, ex_ref)).astype(o_ref.dtype)


def _nsa_kv_branch(mode, q_r, kk, vt, gl, bg, ex, B, S, onehot=None, selneg=None):
    T = q_r.shape[0]
    G = NSA_KV_GROUPS
    tq = min(ATTN_TILE, S)
    tk = min(KV_TILE, S)
    assert NSA_WINDOW == tq or mode == "sel"
    nq = S // tq
    in_specs = [pl.BlockSpec((tq, 2 * LANES), lambda b, g, i: (b * nq + i, g)),
                pl.BlockSpec((None, None, S, LANES), lambda b, g, i: (b, g, 0, 0)),
                pl.BlockSpec((HEAD_DIM, S), lambda b, g, i: (g, b))]
    args = [q_r, kk, vt]
    if mode == "sel":
        in_specs += [pl.BlockSpec((S, LANES), lambda b, g, i: (0, 0)),
                     pl.BlockSpec((None, None, tq, LANES), lambda b, g, i: (b, g, i, 0))]
        args += [onehot, selneg]
    in_specs += [pl.BlockSpec((tq, LANES), lambda b, g, i: (b * nq + i, 0)),
                 pl.BlockSpec((1, LANES), lambda b, g, i: (0, 0)),
                 pl.BlockSpec((None, LANES, 2 * LANES), lambda b, g, i: (g, 0, 0))]
    args += [gl, bg, ex]
    nh = NSA_HEADS_PER_GROUP
    return pl.pallas_call(
        functools.partial(_nsa_kv_kernel, mode=mode, tq=tq, tk=tk),
        grid=(B, G, nq), in_specs=in_specs,
        out_specs=pl.BlockSpec((tq, 2 * LANES), lambda b, g, i: (b * nq + i, g)),
        out_shape=jax.ShapeDtypeStruct((T, D_MODEL), BF16),
        scratch_shapes=[pltpu.VMEM((nh, 1, tq), F32), pltpu.VMEM((nh, HEAD_DIM + ONES_ROWS, tq), F32)],
        compiler_params=_cparams(("parallel", "parallel", "arbitrary")), name=f"nsa_{mode}")(*args)


def _kmean_kernel(k_ref, o_ref, *, nblk):
    k = k_ref[...].astype(F32)
    o_ref[...] = jnp.mean(k.reshape(nblk, MOBA_BLOCK, k.shape[1]), axis=1)


def _moba_kmean(k_rot):
    T, D = k_rot.shape
    nblk = 8
    rows = nblk * MOBA_BLOCK
    assert T % rows == 0
    return pl.pallas_call(
        functools.partial(_kmean_kernel, nblk=nblk), grid=(T // rows,),
        in_specs=[pl.BlockSpec((rows, D), lambda i: (i, 0))],
        out_specs=pl.BlockSpec((nblk, D), lambda i: (i, 0)),
        out_shape=jax.ShapeDtypeStruct((T // MOBA_BLOCK, D), F32),
        compiler_params=_cparams(("parallel",)), name="moba_kmean")(k_rot)


def _moba_select_kernel(q_ref, r_ref, sn_ref, *, tq, nblk):
    i = pl.program_id(2)
    gsT = _nt_dot(r_ref[...], q_ref[...].astype(F32), HIGHEST)
    jblk = lax.broadcasted_iota(jnp.int32, (nblk, tq), 0)
    cb = (i * tq + lax.broadcasted_iota(jnp.int32, (nblk, tq), 1)) >> _log2(MOBA_BLOCK)
    past = jblk < cb
    parts = []
    for h in range(2):
        gs = jnp.where(past, gsT[h * nblk:(h + 1) * nblk, :], NEG)
        chosen = _topk_rows(gs, min(MOBA_TOPK, nblk))
        parts.append(jnp.where(((chosen > 0.0) & past) | (jblk == cb), 0.0, NEG))
    parts.append(jnp.zeros((LANES - 2 * nblk, tq), F32))
    sn_ref[...] = jnp.concatenate(parts, axis=0).T.astype(sn_ref.dtype)


def _moba_select(q_rot, rmat_t, B, S):
    tq = min(ATTN_TILE, S)
    nq = S // tq
    npair = N_HEADS // 2
    nblk = S // MOBA_BLOCK
    return pl.pallas_call(
        functools.partial(_moba_select_kernel, tq=tq, nblk=nblk), grid=(B, npair, nq),
        in_specs=[pl.BlockSpec((tq, LANES), lambda b, hp, i: (b * nq + i, hp)),
                  pl.BlockSpec((None, None, LANES, LANES), lambda b, hp, i: (b, hp, 0, 0))],
        out_specs=pl.BlockSpec((None, None, tq, LANES), lambda b, hp, i: (b, hp, i, 0)),
        out_shape=jax.ShapeDtypeStruct((B, npair, S, LANES), BF16),
        compiler_params=_cparams(("parallel", "parallel", "parallel")), name="moba_select")(q_rot, rmat_t)


def _moba_kernel(q_ref, k_ref, vt_ref, oh_ref, sn_ref, o_ref, m_ref, acc_ref, *, tq, tk, nblk):
    i = pl.program_id(2)
    _flash_init(m_ref, acc_ref)
    q = q_ref[...]
    sn = sn_ref[...]
    lane = lax.broadcasted_iota(jnp.int32, sn.shape, 1)
    qa = []
    for h in range(2):
        mine = (lane >= h * nblk) & (lane < (h + 1) * nblk)
        qa.append(jnp.concatenate([_head_of_pair(q, h), jnp.where(mine, sn, jnp.zeros_like(sn))], axis=1))

    def step(off, size, mask):
        k = jnp.concatenate([k_ref[pl.ds(off, size), :], oh_ref[pl.ds(off, size), :]], axis=1)
        vt = vt_ref[:, pl.ds(off, size)]

        lhs = [_pair_lhs(vt, h) for h in range(2)]

        def scores(u):
            sT = _nt_dot(k, qa[u[0]][u[1], :])
            return sT if mask is None else jnp.where(mask[:, u[1]], sT, NEG)

        _staggered(_pair_units(tq), scores,
                   lambda u, sT: _flash_step(sT, lhs[u[0]], m_ref, acc_ref, u[0], u[1]))

    _causal_sweep(i, tq, tk, step)
    o_ref[...] = _pair_finish(acc_ref).T.astype(o_ref.dtype)


def _moba_attention(qk_rot, vt, onehot2, selneg, B, S):
    T = qk_rot.shape[0]
    tq = min(PAIR_TILE, S)
    tk = min(KV_TILE, S)
    nq = S // tq
    npair = N_HEADS // 2
    ncol = D_MODEL // LANES
    nblk = S // MOBA_BLOCK
    return pl.pallas_call(
        functools.partial(_moba_kernel, tq=tq, tk=tk, nblk=nblk),
        grid=(B, npair, nq),
        in_specs=[pl.BlockSpec((tq, LANES), lambda b, hp, i: (b * nq + i, hp)),
                  pl.BlockSpec((S, LANES), lambda b, hp, i: (b, ncol + hp)),
                  pl.BlockSpec((LANES, S), lambda b, hp, i: (hp, b)),
                  pl.BlockSpec((S, LANES), lambda b, hp, i: (0, 0)),
                  pl.BlockSpec((None, None, tq, LANES), lambda b, hp, i: (b, hp, i, 0))],
        out_specs=pl.BlockSpec((tq, LANES), lambda b, hp, i: (b * nq + i, hp)),
        out_shape=jax.ShapeDtypeStruct((T, D_MODEL), BF16),
        scratch_shapes=[pltpu.VMEM((2, 1, tq), F32), pltpu.VMEM((2, LANES, tq), F32)],
        compiler_params=_cparams(("parallel", "parallel", "arbitrary")), name="moba_attn")(
            qk_rot, qk_rot, vt, onehot2, selneg)


def _router_kernel(wt_ref, x_ref, rb_ref, e_ref, g_ref):
    tm = x_ref.shape[0]
    gsz = N_EXPERTS // N_GROUPS
    scores = jax.nn.sigmoid(_dot3(wt_ref[...], x_ref[...], nt=True))
    biased = scores + rb_ref[...]
    member = lax.broadcasted_iota(jnp.int32, (gsz, tm), 0).astype(F32)
    gscore = []
    for g in range(N_GROUPS):
        v = biased[g * gsz:(g + 1) * gsz, :]
        m1 = jnp.max(v, axis=0, keepdims=True)
        i1 = jnp.min(jnp.where(v == m1, member, float(gsz)), axis=0, keepdims=True)
        m2 = jnp.max(jnp.where(member == i1, -jnp.inf, v), axis=0, keepdims=True)
        gscore.append(m1 + m2)
    gsel = _topk_rows(jnp.concatenate(gscore, axis=0), TOPK_GROUPS)
    emask = jnp.concatenate([jnp.broadcast_to(gsel[g:g + 1, :], (gsz, tm)) for g in range(N_GROUPS)], axis=0)
    work = jnp.where(emask > 0.0, biased, NEG)
    erow = lax.broadcasted_iota(jnp.int32, (N_EXPERTS, tm), 0).astype(F32)
    idxs, vals = [], []
    for _ in range(TOP_K):
        m = jnp.max(work, axis=0, keepdims=True)
        idx = jnp.min(jnp.where(work == m, erow, float(N_EXPERTS)), axis=0, keepdims=True)
        pick = erow == idx
        idxs.append(idx)
        vals.append(jnp.sum(jnp.where(pick, scores, 0.0), axis=0, keepdims=True))
        work = jnp.where(pick, -jnp.inf, work)
    gw = jnp.concatenate(vals, axis=0)
    e_ref[...] = jnp.concatenate(idxs, axis=0).astype(jnp.int32)
    g_ref[...] = gw / jnp.sum(gw, axis=0, keepdims=True) * ROUTED_SCALE


def _router(x1, router_w, router_b):
    T, D = x1.shape
    tm = min(ROW_TILE, T)
    return pl.pallas_call(
        _router_kernel, grid=(T // tm,),
        in_specs=[pl.BlockSpec((N_EXPERTS, D), lambda i: (0, 0)), pl.BlockSpec((tm, D), lambda i: (i, 0)),
                  pl.BlockSpec((N_EXPERTS, 1), lambda i: (0, 0))],
        out_specs=[pl.BlockSpec((TOP_K, tm), lambda i: (0, i))] * 2,
        out_shape=[jax.ShapeDtypeStruct((TOP_K, T), jnp.int32), jax.ShapeDtypeStruct((TOP_K, T), F32)],
        compiler_params=_cparams(("parallel",)), name="moe_router")(
            router_w.T, x1, router_b.reshape(N_EXPERTS, 1))


def _rank_kernel(e_ref, tri_ref, rank_ref, cnt_ref, carry_ref):
    @pl.when(pl.program_id(0) == 0)
    def _():
        carry_ref[...] = jnp.zeros(carry_ref.shape, F32)

    tm = e_ref.shape[1]
    erow = lax.broadcasted_iota(jnp.int32, (N_EXPERTS, tm), 0)
    tri = tri_ref[...]
    base = carry_ref[...]
    ranks = []
    for k in range(TOP_K):
        oh = erow == e_ref[k:k + 1, :]
        ohb = jnp.where(oh, 1.0, 0.0).astype(BF16)
        incl = jnp.dot(ohb, tri, preferred_element_type=F32)
        ranks.append(jnp.sum(jnp.where(oh, base + incl - 1.0, 0.0), axis=0, keepdims=True))
        base = base + incl[:, tm - 1:tm]
    carry_ref[...] = base
    rank_ref[...] = jnp.concatenate(ranks, axis=0).astype(jnp.int32)
    cnt_ref[...] = jnp.broadcast_to(base, cnt_ref.shape)


def _expert_ranks(eidx_t):
    K, T = eidx_t.shape
    tm = min(ROW_TILE, T)
    tri = jnp.asarray(np.triu(np.ones((tm, tm), np.float32)), BF16)
    return pl.pallas_call(
        _rank_kernel, grid=(T // tm,),
        in_specs=[pl.BlockSpec((K, tm), lambda i: (0, i)), pl.BlockSpec((tm, tm), lambda i: (0, 0))],
        out_specs=[pl.BlockSpec((K, tm), lambda i: (0, i)), pl.BlockSpec((N_EXPERTS, LANES), lambda i: (0, 0))],
        out_shape=[jax.ShapeDtypeStruct((K, T), jnp.int32), jax.ShapeDtypeStruct((N_EXPERTS, LANES), F32)],
        scratch_shapes=[pltpu.VMEM((N_EXPERTS, 1), F32)],
        compiler_params=_cparams(("arbitrary",)), name="moe_rank")(eidx_t, tri)


def _dest_kernel(e_ref, rank_ref, ps_ref, d_ref):
    tm = e_ref.shape[1]
    erow = lax.broadcasted_iota(jnp.int32, (N_EXPERTS, tm), 0)
    ps = ps_ref[...]
    rows = []
    for k in range(TOP_K):
        oh = erow == e_ref[k:k + 1, :]
        rows.append(jnp.sum(jnp.where(oh, ps, 0.0), axis=0, keepdims=True))
    d_ref[...] = jnp.concatenate(rows, axis=0).astype(jnp.int32) + rank_ref[...]


def _expert_dest(eidx_t, rank_t, pstarts):
    K, T = eidx_t.shape
    tm = min(ROW_TILE, T)
    blk = pl.BlockSpec((K, tm), lambda i: (0, i))
    return pl.pallas_call(
        _dest_kernel, grid=(T // tm,),
        in_specs=[blk, blk, pl.BlockSpec((N_EXPERTS, 1), lambda i: (0, 0))],
        out_specs=blk, out_shape=jax.ShapeDtypeStruct((K, T), jnp.int32),
        compiler_params=_cparams(("parallel",)), name="moe_dest")(
            eidx_t, rank_t, pstarts.astype(F32).reshape(N_EXPERTS, 1))


def _expert_kernel(be_ref, nb_ref, x_ref, *refs):
    n = EXPERT_SLOTS
    w_refs, (o_ref, wgu_ref, wdb_ref) = refs[:3 * n], refs[3 * n:]
    s = pl.program_id(0)
    E = wdb_ref.shape[1]
    R = MOE_BLOCK

    for j in range(n):
        b = s * n + j

        @pl.when((s == 0) | (be_ref[b] != be_ref[jnp.maximum(b - n, 0)]))
        def _(j=j):
            wg_ref, wu_ref, wd_ref = w_refs[3 * j:3 * j + 3]
            wgu_ref[j, :, :E] = wg_ref[...].astype(BF16)
            wgu_ref[j, :, E:] = wu_ref[...].astype(BF16)
            wdb_ref[j] = wd_ref[...].astype(BF16)

    @pl.when(s * n < nb_ref[0])
    def _():
        x = _load_pieces(x_ref).astype(BF16)
        ys = []

        def gate_up(j):
            return jnp.dot(x[j * R:(j + 1) * R], wgu_ref[j], preferred_element_type=F32)

        def down(j, gu):
            h = jax.nn.silu(gu[:, :E]) * gu[:, E:]
            ys.append(jnp.dot(h.astype(BF16), wdb_ref[j], preferred_element_type=F32))

        _staggered(list(range(n)), gate_up, down)
        _store_pieces(o_ref, jnp.concatenate(ys, axis=0))

    @pl.when(s * n >= nb_ref[0])
    def _():
        o_ref[...] = jnp.zeros(o_ref.shape, o_ref.dtype)


def _expert_ffn(blk_e, nb_used, xs, wg, wu, wd, layer):
    P = xs.shape[1]
    D = wg.shape[2]
    n = EXPERT_SLOTS
    rows = n * MOE_BLOCK
    assert P % rows == 0
    E = EXPERT_DIM
    w_specs = []
    for j in range(n):
        pick = functools.partial(lambda s, be, nb, j: (layer, be[s * n + j], 0, 0), j=j)
        w_specs += [pl.BlockSpec((None, None, D, E), pick), pl.BlockSpec((None, None, D, E), pick),
                    pl.BlockSpec((None, None, E, D), pick)]
    grid_spec = pltpu.PrefetchScalarGridSpec(
        num_scalar_prefetch=2, grid=(P // rows,),
        in_specs=[_piece_spec(D, rows, lambda s, be, nb: (0, s, 0))] + w_specs,
        out_specs=_piece_spec(D, rows, lambda s, be, nb: (0, s, 0)),
        scratch_shapes=[pltpu.VMEM((n, D, 2 * E), BF16), pltpu.VMEM((n, E, D), BF16)])
    return pl.pallas_call(
        _expert_kernel, grid_spec=grid_spec, out_shape=_piece_shape(D, P),
        compiler_params=_cparams(("arbitrary",)), name="moe_experts")(
            blk_e, nb_used, xs, *([wg, wu, wd] * n))


def _post_moe_kernel(x_ref, *refs):
    y_refs = refs[:TOP_K]
    gw_ref, p_ref, sg_ref, su_ref, sd_ref, g_ref, b_ref, wg_ref, wp_ref, o_ref, ob_ref = refs[TOP_K:]
    x = x_ref[...]
    xb = x.astype(BF16)
    h = jax.nn.silu(jnp.dot(xb, sg_ref[...], preferred_element_type=F32)) * jnp.dot(
        xb, su_ref[...], preferred_element_type=F32)
    ffn = jnp.dot(h.astype(BF16), sd_ref[...], preferred_element_type=F32)
    gw = gw_ref[...]
    for k in range(TOP_K):
        ffn = ffn + gw[:, k:k + 1] * _load_pieces(y_refs[k])
    z = DN_ALPHA * x + ffn
    x2 = _layer_norm(z, g_ref[...], b_ref[...])
    gate = jax.nn.sigmoid(jnp.dot(x2.astype(BF16), wg_ref[...], preferred_element_type=F32))
    proj = jnp.dot(p_ref[...].astype(BF16), wp_ref[...], preferred_element_type=F32)
    out = x2 + gate * proj
    o_ref[...] = out
    ob_ref[...] = out.astype(BF16)


def _post_moe(x1, yg, gw, p, sg, su, sd, g, b, wgate, wproj):
    T, D = x1.shape
    tm = min(ROW_TILE // 2, T)
    nt = T // tm
    PD = p.shape[1]
    SD = sg.shape[1]
    row = pl.BlockSpec((tm, D), lambda i: (i, 0))
    vec = pl.BlockSpec((1, D), lambda i: (0, 0))
    full = lambda r, c: pl.BlockSpec((r, c), lambda i: (0, 0))
    return pl.pallas_call(
        _post_moe_kernel, grid=(T // tm,),
        in_specs=[row] + [_piece_spec(D, tm, functools.partial(lambda i, k: (0, k * nt + i, 0), k=k))
                          for k in range(TOP_K)]
                 + [pl.BlockSpec((tm, TOP_K), lambda i: (i, 0)),
                    pl.BlockSpec((tm, PD), lambda i: (i, 0)), full(D, SD), full(D, SD),
                  full(SD, D), vec, vec, full(D, D), full(PD, D)],
        out_specs=[row, row],
        out_shape=[jax.ShapeDtypeStruct((T, D), F32), jax.ShapeDtypeStruct((T, D), BF16)],
        compiler_params=_cparams(("parallel",)), name="post_moe")(
            x1, *([yg] * TOP_K), gw, p, sg, su, sd, g.reshape(1, D), b.reshape(1, D), wgate, wproj)


def _rope_tables(positions):
    inv = 1.0 / (ROPE_THETA ** (jnp.arange(0, HEAD_DIM, 2, dtype=F32) / HEAD_DIM))
    ang = positions.astype(F32).reshape(-1)[:, None] * inv
    cos, sin = jnp.cos(ang), jnp.sin(ang)
    cosf = jnp.concatenate([cos] * (LANES // (HEAD_DIM // 2)), axis=1)
    sinf = jnp.concatenate([-sin, sin] * (LANES // HEAD_DIM), axis=1)
    return cosf, sinf


def _q_col_scale(n_q, n):
    return jnp.concatenate([jnp.full((n_q,), Q_SCALE, F32), jnp.ones((n - n_q,), F32)])


def _fox_mixer(x, xb, B, S, w_in, b_f):
    D = D_MODEL
    wb = w_in[:, :3 * D].astype(BF16)
    qk = _proj(xb, wb[:, :2 * D], col_scale=_q_col_scale(D, 2 * D))
    vt = _proj_t(wb[:, 2 * D:].T, xb)
    fl = _proj(x, _pad_cols(w_in[:, 3 * D:], LANES), out_dtype=F32, tn=LANES, precision="split3")
    cp = _fox_gate(fl, b_f, B, S)
    return [_fox_attention(qk, cp, vt, B, S)]


def _nsa_mixer(x, xb, B, S, cosf, sinf, w_in, b_gate, pe_k, pe_v, ck_w1, ck_w2, cv_w1, cv_w2):
    D, G, HD = D_MODEL, NSA_KV_GROUPS, HEAD_DIM
    HG = NSA_HEADS_PER_GROUP
    kvw = G * HD
    wb = w_in[:, :D + 6 * kvw].astype(BF16)
    q_c, q_r = _proj(xb, wb[:, :D], mode="both", cos=cosf, sin=sinf, col_scale=_q_col_scale(D, D))
    w_rot = jnp.concatenate([wb[:, D + 2 * kvw:D + 3 * kvw], wb[:, D + 4 * kvw:D + 5 * kvw]], axis=1)
    k_rot = _proj(xb, w_rot, mode="rope", cos=cosf, sin=sinf)
    kvc = _proj(xb, wb[:, D:D + 2 * kvw])
    w_v = jnp.concatenate([wb[:, D + 3 * kvw:D + 4 * kvw], wb[:, D + 5 * kvw:D + 6 * kvw]], axis=1)
    vt = _proj_t(w_v.T, xb)
    gl = _proj(x, _pad_cols(w_in[:, D + 6 * kvw:], LANES), out_dtype=F32, tn=LANES, precision="split3")
    bg = _pad_cols(b_gate.reshape(1, -1), LANES)

    def grouped(t2d):
        t = t2d.reshape(B, S, G, HD).transpose(0, 2, 1, 3)
        return jnp.concatenate([t, t], axis=-1)

    n_chunks = S // NSA_CMP_STRIDE
    n_cmp = n_chunks - NSA_CMP_LEN // NSA_CMP_STRIDE + 1
    ncp = n_chunks

    def compress(t2d, pe, w1, w2):
        ch = t2d.reshape(B, n_chunks, NSA_CMP_STRIDE, G, HD).transpose(0, 1, 3, 2, 4)
        ch = ch.reshape(B, n_chunks, G, NSA_CMP_STRIDE * HD)
        flat = jnp.concatenate([ch[:, :n_cmp], ch[:, 1:n_cmp + 1]], axis=-1)
        flat = jnp.pad(flat, ((0, 0), (0, ncp - n_cmp), (0, 0), (0, 0))).reshape(B * ncp * G, -1)
        out = _nsa_compress(flat, pe.reshape(1, -1), w1.astype(BF16), _pad_cols(w2, LANES).astype(BF16))
        return out[:, :HD].reshape(B, ncp, G, HD).astype(BF16)

    kc = compress(kvc[:, :kvw], pe_k, ck_w1, ck_w2).transpose(0, 2, 1, 3)
    kk_c = jnp.concatenate([kc, kc], axis=-1)
    vt_c = compress(kvc[:, kvw:], pe_v, cv_w1, cv_w2).transpose(0, 2, 3, 1)

    n_sel = S // NSA_SEL_LEN
    assert n_sel <= LANES
    cmp_start = np.arange(ncp) * NSA_CMP_STRIDE
    sel_start = np.arange(LANES) * NSA_SEL_LEN
    overlap = ((cmp_start[:, None] < sel_start[None, :] + NSA_SEL_LEN)
               & (cmp_start[:, None] + NSA_CMP_LEN > sel_start[None, :])
               & (np.arange(ncp)[:, None] < n_cmp) & (np.arange(LANES)[None, :] < n_sel))
    overlap_t = jnp.asarray(overlap.T, BF16)
    ex = np.zeros((3, G, LANES, HG * HD), np.float32)
    for br in range(3):
        for g in range(G):
            for hg in range(HG):
                ex[br, g, (g * HG + hg) * 3 + br, hg * HD:(hg + 1) * HD] = 1.0
    ex = jnp.asarray(ex, BF16)

    o_c, selneg = _nsa_cmp_branch(q_c, kk_c, vt_c, overlap_t, gl, bg, ex[0], B, S)
    onehot = jnp.asarray((np.arange(S)[:, None] // NSA_SEL_LEN) == np.arange(LANES)[None, :], BF16)
    kk_s = grouped(k_rot[:, :kvw])
    kk_w = grouped(k_rot[:, kvw:])
    o_s = _nsa_kv_branch("sel", q_r, kk_s, vt[:kvw], gl, bg, ex[1], B, S, onehot=onehot, selneg=selneg)
    o_w = _nsa_kv_branch("win", q_r, kk_w, vt[kvw:], gl, bg, ex[2], B, S)
    return [o_c, o_s, o_w]


def _moba_mixer(xb, B, S, cosf, sinf, w_in):
    D, H, HD = D_MODEL, N_HEADS, HEAD_DIM
    wb = w_in.astype(BF16)
    qk_rot = _proj(xb, wb[:, :2 * D], mode="rope", cos=cosf, sin=sinf, col_scale=_q_col_scale(D, 2 * D))
    vt = _proj_t(wb[:, 2 * D:].T, xb)
    nblk = S // MOBA_BLOCK
    assert S % MOBA_BLOCK == 0 and 2 * nblk <= LANES and nblk % SUBLANES == 0
    kmean = _moba_kmean(qk_rot[:, D:]).reshape(B, nblk, H // 2, 2, HD)
    km = kmean.transpose(0, 2, 3, 1, 4)
    rmat_t = jnp.zeros((B, H // 2, LANES, 2, HD), F32)
    rmat_t = rmat_t.at[:, :, :nblk, 0].set(km[:, :, 0]).at[:, :, nblk:2 * nblk, 1].set(km[:, :, 1])
    selneg = _moba_select(qk_rot, rmat_t.reshape(B, H // 2, LANES, LANES), B, S)
    blk_of = np.arange(S) // MOBA_BLOCK
    oh = np.zeros((S, LANES), np.float32)
    oh[np.arange(S), blk_of] = 1.0
    oh[np.arange(S), nblk + blk_of] = 1.0
    return [_moba_attention(qk_rot, vt, jnp.asarray(oh, BF16), selneg, B, S)]


def _sc_invert_kernel(dest_hbm, tok_hbm, out_hbm, d_v, t_v, buf_v, *, seg, chunk, n_chunks, n_tok):
    lo = pl.multiple_of((lax.axis_index("s") * SC_CORES + lax.axis_index("c")) * seg, SC_LANES)

    @pl.loop(0, seg // SC_LANES)
    def _(j):
        row = lo + j * SC_LANES + lax.iota(jnp.int32, SC_LANES)
        buf_v[pl.ds(j * SC_LANES, SC_LANES)] = lax.rem(row, jnp.int32(n_tok))

    @pl.loop(0, n_chunks)
    def _(c):
        pltpu.sync_copy(dest_hbm.at[pl.ds(c * chunk, chunk)], d_v)
        pltpu.sync_copy(tok_hbm.at[pl.ds(c * chunk, chunk)], t_v)

        @pl.loop(0, chunk // SC_LANES)
        def _(j):
            d = d_v[pl.ds(j * SC_LANES, SC_LANES)] - lo
            mine = (d >= 0) & (d < seg)
            plsc.store_scatter(buf_v, [jnp.where(mine, d, 0)], t_v[pl.ds(j * SC_LANES, SC_LANES)], mask=mine)

    pltpu.sync_copy(buf_v, out_hbm.at[pl.ds(lo, seg)])


def _invert_dest(dest, tok, P, n_tok):
    R = dest.shape[0]
    n_workers = SC_CORES * SC_SUBCORES
    seg = P // n_workers
    chunk = min(4096, R)
    assert P == seg * n_workers and seg % SC_LANES == 0 and R % chunk == 0
    mesh = plsc.VectorSubcoreMesh(core_axis_name="c", subcore_axis_name="s",
                                  num_cores=SC_CORES, num_subcores=SC_SUBCORES)
    return pl.kernel(
        functools.partial(_sc_invert_kernel, seg=seg, chunk=chunk, n_chunks=R // chunk, n_tok=n_tok),
        out_type=jax.ShapeDtypeStruct((P,), jnp.int32), mesh=mesh,
        scratch_types=[pltpu.VMEM((chunk,), jnp.int32), pltpu.VMEM((chunk,), jnp.int32),
                       pltpu.VMEM((seg,), jnp.int32)],
        compiler_params=pltpu.CompilerParams(needs_layout_passes=False),
        name="moe_invert")(dest, tok)


def _sc_mesh():
    return plsc.VectorSubcoreMesh(core_axis_name="c", subcore_axis_name="s",
                                  num_cores=SC_CORES, num_subcores=SC_SUBCORES)


def _gather_rows(table, idx):
    parts, V, _ = table.shape
    N = idx.shape[0]
    n_pieces = N * parts
    n_workers = SC_CORES * SC_SUBCORES
    assert n_pieces % (SC_WINDOW * n_workers) == 0
    pieces = (jnp.arange(parts, dtype=jnp.int32)[:, None] * V + idx[None, :]).reshape(1, n_pieces)

    def kernel_body(x_hbm, i_hbm, o_hbm):
        def body(i_vmem, o_vmem):
            pltpu.sync_copy(x_hbm.at[i_vmem.at[0]], o_vmem)

        pltpu.emit_pipeline(
            body, grid=(n_pieces // SC_WINDOW,),
            in_specs=[pl.BlockSpec((1, SC_WINDOW), lambda i: (0, i))],
            out_specs=[pl.BlockSpec((SC_WINDOW, SC_ROW_WORDS), lambda i: (i, 0))],
            core_axis_name=("c", "s"), dimension_semantics=(pltpu.PARALLEL,))(i_hbm, o_hbm)

    out = pl.kernel(kernel_body, out_type=jax.ShapeDtypeStruct((n_pieces, SC_ROW_WORDS), table.dtype),
                    mesh=_sc_mesh(), scratch_types=[], name="moe_gather")(
                        table.reshape(parts * V, SC_ROW_WORDS), pieces)
    return out.reshape(parts, N, SC_ROW_WORDS)


def _moe_dispatch(x1, x1p, router_w, router_b, wg, wu, wd, layer):
    T, D = x1.shape
    eidx_t, gw_t = _router(x1, router_w, router_b)
    rank_t, cnt = _expert_ranks(eidx_t)
    counts = cnt[:, 0].astype(jnp.int32)
    padded = (counts + MOE_BLOCK - 1) // MOE_BLOCK * MOE_BLOCK
    pends = jnp.cumsum(padded)
    pstarts = pends - padded
    R = T * TOP_K
    P = -(-(R + N_EXPERTS * (MOE_BLOCK - 1)) // MOE_BLOCK) * MOE_BLOCK
    NB = P // MOE_BLOCK
    blk_start = jnp.arange(NB, dtype=jnp.int32) * MOE_BLOCK
    blk_e = jnp.minimum(jnp.sum(pends[None, :] <= blk_start[:, None], axis=1), N_EXPERTS - 1).astype(jnp.int32)
    nb_used = (pends[-1] // MOE_BLOCK).astype(jnp.int32).reshape(1)
    dest_t = _expert_dest(eidx_t, rank_t, pstarts)

    tok = jnp.broadcast_to(jnp.arange(T, dtype=jnp.int32)[None, :], (TOP_K, T))
    buf_tok = _invert_dest(dest_t.reshape(-1), tok.reshape(-1), P, T)
    xs = _gather_rows(x1p, buf_tok)
    yb = _expert_ffn(blk_e, nb_used, xs, wg, wu, wd, layer)
    yg = _gather_rows(yb, dest_t.reshape(-1))
    return yg, gw_t.T


def kernel(x, p, positions, fox_w_in, fox_b_f, fox_w_out, nsa_w_in, nsa_b_gate, nsa_pe_k, nsa_pe_v,
           nsa_cmp_k_w1, nsa_cmp_k_w2, nsa_cmp_v_w1, nsa_cmp_v_w2, nsa_w_out, moba_w_in, moba_w_out,
           ln1_g, ln1_b, router_w, router_b, exp_w_gate, exp_w_up, exp_w_down,
           sh_w_gate, sh_w_up, sh_w_down, ln2_g, ln2_b, ple_w_gate, ple_w_proj):
    B, S, D = x.shape
    T = B * S
    depth = p.shape[0]
    cosf, sinf = _rope_tables(positions)
    xt = x.reshape(T, D)
    xtb = xt.astype(BF16)
    for i in range(depth):
        kind, j = i % N_MIXERS, i // N_MIXERS
        if kind == 0:
            o_list = _fox_mixer(xt, xtb, B, S, fox_w_in[j], fox_b_f[j])
            w_out = fox_w_out[j]
        elif kind == 1:
            o_list = _nsa_mixer(xt, xtb, B, S, cosf, sinf, nsa_w_in[j], nsa_b_gate[j], nsa_pe_k[j], nsa_pe_v[j],
                                nsa_cmp_k_w1[j], nsa_cmp_k_w2[j], nsa_cmp_v_w1[j], nsa_cmp_v_w2[j])
            w_out = nsa_w_out[j]
        else:
            o_list = _moba_mixer(xtb, B, S, cosf, sinf, moba_w_in[j])
            w_out = moba_w_out[j]
        x1, x1p = _outproj_ln(o_list, w_out.astype(BF16), xt, ln1_g[i], ln1_b[i])
        yg, gw = _moe_dispatch(x1, x1p, router_w[i], router_b[i], exp_w_gate, exp_w_up, exp_w_down, i)
        xt, xtb = _post_moe(x1, yg, gw, p[i].reshape(T, -1), sh_w_gate[i].astype(BF16), sh_w_up[i].astype(BF16),
                            sh_w_down[i].astype(BF16), ln2_g[i], ln2_b[i],
                            ple_w_gate[i].astype(BF16), ple_w_proj[i].astype(BF16))
    return xt.reshape(B, S, D)
```

```python
import functools

import jax
import jax.numpy as jnp
import numpy as np
from jax import lax
from jax.experimental import pallas as pl
from jax.experimental.pallas import tpu as pltpu
from jax.experimental.pallas import tpu_sc as plsc

D_MODEL = 1024
DEPTH = 4
N_HEADS = 16
HEAD_DIM = 64
ATTN_SCALE = HEAD_DIM ** -0.5
LOG2E = 1.4426950408889634
Q_SCALE = ATTN_SCALE * LOG2E
ROPE_THETA = 10000.0
N_MIXERS = 3

NSA_KV_GROUPS = 4
NSA_HEADS_PER_GROUP = N_HEADS // NSA_KV_GROUPS
NSA_CMP_LEN = 32
NSA_CMP_STRIDE = 16
NSA_SEL_LEN = 64
NSA_SEL_TOPN = 16
NSA_WINDOW = 512
NSA_FORCE_BONUS = 1e4

MOBA_BLOCK = 256
MOBA_TOPK = 3

N_EXPERTS = 64
EXPERT_DIM = 256
TOP_K = 8
N_GROUPS = 8
TOPK_GROUPS = 4
ROUTED_SCALE = 2.5
MOE_BLOCK = 256
EXPERT_SLOTS = 2

DN_ALPHA = (2 * DEPTH) ** 0.25
LN_EPS = 1e-5
NEG = -1e30

SC_CORES = 2
SC_SUBCORES = 16
SC_LANES = 16
SC_ROW_WORDS = 256
SC_WINDOW = 128
LANES = 128
SUBLANES = 8
ATTN_TILE = 512
PAIR_TILE = 1024
KV_TILE = 1024
ROW_TILE = 512
VMEM_LIMIT = 48 * 1024 * 1024
C_SPLIT = 3
ONES_ROWS = 16

F32 = jnp.float32
BF16 = jnp.bfloat16
HIGHEST = lax.Precision.HIGHEST


def _cparams(sem):
    return pltpu.CompilerParams(dimension_semantics=sem, vmem_limit_bytes=VMEM_LIMIT)


def _log2(n):
    assert n & (n - 1) == 0
    return n.bit_length() - 1


def _nt_dot(a, b, precision=None):
    return lax.dot_general(a, b, (((1,), (1,)), ((), ())), preferred_element_type=F32,
                           precision=precision)


def _split_bf16(a):
    hi = a.astype(BF16)
    return hi, (a - hi.astype(F32)).astype(BF16)


def _dot3(x, w, nt=False):
    dot = _nt_dot if nt else functools.partial(jnp.dot, preferred_element_type=F32)
    xh, xl = _split_bf16(x)
    wh, wl = _split_bf16(w)
    return dot(xh, wh) + dot(xl, wh) + dot(xh, wl)


def _topk_rows(work, n):
    rows = lax.broadcasted_iota(jnp.int32, work.shape, 0).astype(F32)
    chosen = jnp.zeros(work.shape, F32)
    for _ in range(n):
        m = jnp.max(work, axis=0, keepdims=True)
        idx = jnp.min(jnp.where(work == m, rows, float(work.shape[0])), axis=0, keepdims=True)
        pick = rows == idx
        chosen = jnp.where(pick, 1.0, chosen)
        work = jnp.where(pick, -jnp.inf, work)
    return chosen


def _proj_kernel(*refs, mode, precision, scaled):
    refs = list(refs)
    x_ref, w_ref = refs[:2]
    del refs[:2]
    scale_ref = refs.pop(0) if scaled else None
    if mode == "none":
        (o_ref,) = refs
    elif mode == "rope":
        cos_ref, sin_ref, r_ref = refs
    else:
        cos_ref, sin_ref, o_ref, r_ref = refs
    x = x_ref[...]
    w = w_ref[...]
    if precision == "split3":
        acc = _dot3(x, w)
    else:
        if x.dtype != w.dtype:
            x = x.astype(w.dtype)
        acc = jnp.dot(x, w, preferred_element_type=F32, precision=precision)
    if scaled:
        acc = acc * scale_ref[...]
    if mode in ("none", "both"):
        o_ref[...] = acc.astype(o_ref.dtype)
    if mode in ("rope", "both"):
        tn = acc.shape[1]
        rep = tn // LANES
        cosf = jnp.concatenate([cos_ref[...]] * rep, axis=1)
        sinf = jnp.concatenate([sin_ref[...]] * rep, axis=1)
        lane = lax.broadcasted_iota(jnp.int32, acc.shape, 1)
        first_half = (lane & (HEAD_DIM // 2)) == 0
        swapped = jnp.where(first_half, pltpu.roll(acc, tn - HEAD_DIM // 2, 1),
                            pltpu.roll(acc, HEAD_DIM // 2, 1))
        r_ref[...] = (acc * cosf + swapped * sinf).astype(r_ref.dtype)


def _proj(x, w, *, mode="none", cos=None, sin=None, out_dtype=BF16, tn=512, precision=None, col_scale=None):
    T, K = x.shape
    N = w.shape[1]
    tm = min(ROW_TILE, T)
    tn = min(tn, N)
    assert T % tm == 0 and N % tn == 0
    grid = (T // tm, N // tn)
    in_specs = [pl.BlockSpec((tm, K), lambda i, j: (i, 0)),
                pl.BlockSpec((K, tn), lambda i, j: (0, j))]
    args = [x, w]
    if col_scale is not None:
        in_specs.append(pl.BlockSpec((1, tn), lambda i, j: (0, j)))
        args.append(col_scale.reshape(1, N).astype(F32))
    if mode != "none":
        in_specs += [pl.BlockSpec((tm, LANES), lambda i, j: (i, 0))] * 2
        args += [cos, sin]
    o_spec = pl.BlockSpec((tm, tn), lambda i, j: (i, j))
    o_shape = jax.ShapeDtypeStruct((T, N), out_dtype)
    if mode == "both":
        out_specs, out_shape = [o_spec, o_spec], [o_shape, o_shape]
    else:
        out_specs, out_shape = o_spec, o_shape
    return pl.pallas_call(
        functools.partial(_proj_kernel, mode=mode, precision=precision, scaled=col_scale is not None),
        grid=grid, in_specs=in_specs, out_specs=out_specs, out_shape=out_shape,
        compiler_params=_cparams(("parallel", "parallel")), name=f"proj_{mode}")(*args)


def _proj_t_kernel(wt_ref, x_ref, o_ref, *, precision):
    wt = wt_ref[...]
    x = x_ref[...]
    if x.dtype != wt.dtype:
        x = x.astype(wt.dtype)
    o_ref[...] = _nt_dot(wt, x, precision).astype(o_ref.dtype)


def _proj_t(wt, x, *, out_dtype=BF16, precision=None):
    N, K = wt.shape
    T = x.shape[0]
    tm = min(ROW_TILE, T)
    tn = min(512, N)
    assert T % tm == 0 and N % tn == 0
    return pl.pallas_call(
        functools.partial(_proj_t_kernel, precision=precision),
        grid=(T // tm, N // tn),
        in_specs=[pl.BlockSpec((tn, K), lambda i, j: (j, 0)), pl.BlockSpec((tm, K), lambda i, j: (i, 0))],
        out_specs=pl.BlockSpec((tn, tm), lambda i, j: (j, i)),
        out_shape=jax.ShapeDtypeStruct((N, T), out_dtype),
        compiler_params=_cparams(("parallel", "parallel")), name="proj_t")(wt, x)


def _pack_rows(v):
    n = v.shape[1] // 2
    lo = pltpu.bitcast(v[:, :n].astype(BF16).astype(F32), jnp.int32)
    hi = pltpu.bitcast(v[:, n:].astype(BF16).astype(F32), jnp.int32)
    return (hi & jnp.int32(-65536)) | lax.shift_right_logical(lo, jnp.int32(16))


def _unpack_rows(w):
    lo = pltpu.bitcast(lax.shift_left(w, jnp.int32(16)), F32)
    hi = pltpu.bitcast(w & jnp.int32(-65536), F32)
    return jnp.concatenate([lo, hi], axis=1)


def _store_pieces(ref, v):
    w = _pack_rows(v)
    for j in range(ref.shape[0]):
        ref[j] = w[:, j * SC_ROW_WORDS:(j + 1) * SC_ROW_WORDS]


def _load_pieces(ref):
    return _unpack_rows(jnp.concatenate([ref[j] for j in range(ref.shape[0])], axis=1))


def _piece_spec(d, rows, index_map):
    return pl.BlockSpec((d // 2 // SC_ROW_WORDS, rows, SC_ROW_WORDS), index_map)


def _piece_shape(d, rows):
    return jax.ShapeDtypeStruct((d // 2 // SC_ROW_WORDS, rows, SC_ROW_WORDS), jnp.int32)


def _pad_cols(w, n):
    return jnp.pad(w, ((0, 0), (0, n - w.shape[1])))


def _layer_norm(z, g, b):
    mu = jnp.mean(z, axis=-1, keepdims=True)
    zc = z - mu
    var = jnp.mean(zc * zc, axis=-1, keepdims=True)
    return zc * lax.rsqrt(var + LN_EPS) * g + b


def _outproj_ln_kernel(*refs, n_o):
    o_refs = refs[:n_o]
    w_ref, x_ref, g_ref, b_ref, y_ref, yp_ref = refs[n_o:]
    if n_o == 1:
        o = o_refs[0][...]
    else:
        o = o_refs[0][...].astype(F32)
        for r in o_refs[1:]:
            o = o + r[...].astype(F32)
        o = o.astype(BF16)
    mix = jnp.dot(o, w_ref[...], preferred_element_type=F32)
    z = DN_ALPHA * x_ref[...] + mix
    y = _layer_norm(z, g_ref[...], b_ref[...])
    y_ref[...] = y
    _store_pieces(yp_ref, y)


def _outproj_ln(o_list, w, x, g, b):
    T, D = x.shape
    tm = min(ROW_TILE, T)
    row = pl.BlockSpec((tm, D), lambda i: (i, 0))
    vec = pl.BlockSpec((1, D), lambda i: (0, 0))
    return pl.pallas_call(
        functools.partial(_outproj_ln_kernel, n_o=len(o_list)),
        grid=(T // tm,),
        in_specs=[row] * len(o_list) + [pl.BlockSpec((D, D), lambda i: (0, 0)), row, vec, vec],
        out_specs=[row, _piece_spec(D, tm, lambda i: (0, i, 0))],
        out_shape=[jax.ShapeDtypeStruct((T, D), F32), _piece_shape(D, T)],
        compiler_params=_cparams(("parallel",)), name="outproj_ln")(
            *o_list, w, x, g.reshape(1, D), b.reshape(1, D))


def _flash_init(m_ref, acc_ref):
    m_ref[...] = jnp.full(m_ref.shape, NEG, F32)
    acc_ref[...] = jnp.zeros(acc_ref.shape, F32)


def _flash_step(sT, lhs, m_ref, acc_ref, h, cols=slice(None)):
    m_prev = m_ref[h, :, cols]
    m_new = jnp.maximum(m_prev, jnp.max(sT, axis=0, keepdims=True))
    p = jnp.exp2(sT - m_new)
    alpha = jnp.exp2(m_prev - m_new)
    acc_ref[h, :, cols] = alpha * acc_ref[h, :, cols] + jnp.dot(lhs, p.astype(BF16), preferred_element_type=F32)
    m_ref[h, :, cols] = m_new


def _causal_t(tk, tq, shift=0, strict_lower=False):
    key = lax.broadcasted_iota(jnp.int32, (tk, tq), 0)
    qry = lax.broadcasted_iota(jnp.int32, (tk, tq), 1)
    return (key > qry) if strict_lower else (key <= qry + shift)


def _causal_sweep(i, tq, tk, step):
    r = tk // tq
    assert tk == r * tq
    n_full = i >> _log2(r)

    def body(j, carry):
        step(pl.multiple_of(j * tk, tk), tk, None)
        return carry

    lax.fori_loop(0, n_full, body, 0)
    if r == 1:
        step(pl.multiple_of(i * tq, tq), tq, _causal_t(tq, tq))
    else:
        for rem in range(r):
            @pl.when(i - n_full * r == rem)
            def _(rem=rem):
                size = (rem + 1) * tq
                step(pl.multiple_of(n_full * tk, tk), size, _causal_t(size, tq, shift=rem * tq))


def _staggered(units, scores, update):
    s = [scores(units[0])]
    for n, u in enumerate(units):
        if n + 1 < len(units):
            s.append(scores(units[n + 1]))
        update(u, s[n])


def _head_units(n_heads, tq):
    w = min(ATTN_TILE, tq)
    return [(h, slice(c * w, (c + 1) * w)) for h in range(n_heads) for c in range(tq // w)]


def _pair_units(tq):
    return _head_units(2, tq)


def _head_of_pair(q_pair, which):
    lane = lax.broadcasted_iota(jnp.int32, q_pair.shape, 1)
    keep = (lane < HEAD_DIM) if which == 0 else (lane >= HEAD_DIM)
    return jnp.where(keep, q_pair, jnp.zeros_like(q_pair))


def _pair_rows(a_top, a_bot):
    row = lax.broadcasted_iota(jnp.int32, a_top.shape, 0)
    return jnp.where(row < HEAD_DIM, a_top, a_bot)


def _pair_lhs(vt, which):
    row = lax.broadcasted_iota(jnp.int32, vt.shape, 0)
    keep = (row < HEAD_DIM) if which == 0 else (row >= HEAD_DIM)
    return jnp.where(keep, vt, jnp.ones_like(vt))


def _pair_finish(acc_ref):
    a0, a1 = acc_ref[0], acc_ref[1]
    return _pair_rows(a0 / a0[HEAD_DIM:HEAD_DIM + 1, :], a1 / a1[0:1, :])


def _fox_gate_kernel(fl_ref, bf_ref, tri_ref, ex_ref, cp_ref, carry_ref):
    @pl.when(pl.program_id(1) == 0)
    def _():
        carry_ref[...] = jnp.zeros(carry_ref.shape, F32)

    z = fl_ref[...] + bf_ref[...]
    log_f = jnp.minimum(z, 0.0) - jnp.log1p(jnp.exp(-jnp.abs(z)))
    c = jnp.dot(tri_ref[...], log_f, preferred_element_type=F32, precision=HIGHEST) + carry_ref[...]
    carry_ref[...] = c[-1:, :]
    out = jnp.zeros(c.shape, F32)
    rem = c * LOG2E
    for piece in range(C_SPLIT):
        part = rem.astype(BF16)
        rem = rem - part.astype(F32)
        out = out + jnp.dot(part, ex_ref[piece], preferred_element_type=F32)
    cp_ref[...] = out.astype(BF16)


def _fox_gate(fl, b_f, B, S):
    T = fl.shape[0]
    tm = min(ROW_TILE, S)
    ns = S // tm
    tri = jnp.asarray(np.tril(np.ones((tm, tm), np.float32)))
    ex = np.zeros((C_SPLIT, LANES, LANES), np.float32)
    for h in range(N_HEADS):
        for j in range(C_SPLIT):
            ex[j, h, C_SPLIT * h + j] = 1.0
    bf = _pad_cols(b_f.reshape(1, -1), LANES)
    return pl.pallas_call(
        _fox_gate_kernel, grid=(B, ns),
        in_specs=[pl.BlockSpec((tm, LANES), lambda b, s: (b * ns + s, 0)),
                  pl.BlockSpec((1, LANES), lambda b, s: (0, 0)),
                  pl.BlockSpec((tm, tm), lambda b, s: (0, 0)),
                  pl.BlockSpec((C_SPLIT, LANES, LANES), lambda b, s: (0, 0, 0))],
        out_specs=pl.BlockSpec((tm, LANES), lambda b, s: (b * ns + s, 0)),
        out_shape=jax.ShapeDtypeStruct((T, LANES), BF16),
        scratch_shapes=[pltpu.VMEM((1, LANES), F32)],
        compiler_params=_cparams(("parallel", "arbitrary")), name="fox_gate")(
            fl, bf, tri, jnp.asarray(ex, BF16))


def _fox_kernel(q_ref, k_ref, cp_ref, vt_ref, o_ref, m_ref, acc_ref, *, tq, tk):
    hp = pl.program_id(1)
    i = pl.program_id(2)
    _flash_init(m_ref, acc_ref)
    q = q_ref[...]
    lane = lax.broadcasted_iota(jnp.int32, q.shape, 1)
    qa = []
    for h in range(2):
        lo = C_SPLIT * (2 * hp + h)
        minus_one = jnp.where((lane >= lo) & (lane < lo + C_SPLIT), -1.0, 0.0).astype(q.dtype)
        qa.append(jnp.concatenate([_head_of_pair(q, h), minus_one], axis=1))

    def step(off, size, mask):
        k = jnp.concatenate([k_ref[pl.ds(off, size), :], cp_ref[pl.ds(off, size), :]], axis=1)
        vt = vt_ref[:, pl.ds(off, size)]

        lhs = [_pair_lhs(vt, h) for h in range(2)]

        def scores(u):
            sT = _nt_dot(k, qa[u[0]][u[1], :])
            return sT if mask is None else jnp.where(mask[:, u[1]], sT, NEG)

        _staggered(_pair_units(tq), scores,
                   lambda u, sT: _flash_step(sT, lhs[u[0]], m_ref, acc_ref, u[0], u[1]))

    _causal_sweep(i, tq, tk, step)
    o_ref[...] = _pair_finish(acc_ref).T.astype(o_ref.dtype)


def _fox_attention(qk, cp, vt, B, S):
    T = qk.shape[0]
    tq = min(PAIR_TILE, S)
    tk = min(KV_TILE, S)
    nq = S // tq
    npair = N_HEADS // 2
    ncol = D_MODEL // LANES
    return pl.pallas_call(
        functools.partial(_fox_kernel, tq=tq, tk=tk),
        grid=(B, npair, nq),
        in_specs=[pl.BlockSpec((tq, LANES), lambda b, hp, i: (b * nq + i, hp)),
                  pl.BlockSpec((S, LANES), lambda b, hp, i: (b, ncol + hp)),
                  pl.BlockSpec((S, LANES), lambda b, hp, i: (b, 0)),
                  pl.BlockSpec((LANES, S), lambda b, hp, i: (hp, b))],
        out_specs=pl.BlockSpec((tq, LANES), lambda b, hp, i: (b * nq + i, hp)),
        out_shape=jax.ShapeDtypeStruct((T, D_MODEL), BF16),
        scratch_shapes=[pltpu.VMEM((2, 1, tq), F32), pltpu.VMEM((2, LANES, tq), F32)],
        compiler_params=_cparams(("parallel", "parallel", "arbitrary")), name="fox_attn")(
            qk, qk, cp, vt)


def _gate_expand(gl_ref, bg_ref, ex_ref):
    sig = jax.nn.sigmoid(gl_ref[...] + bg_ref[...])
    hi = sig.astype(BF16)
    lo = (sig - hi.astype(F32)).astype(BF16)
    ex = ex_ref[...]
    return jnp.dot(hi, ex, preferred_element_type=F32) + jnp.dot(lo, ex, preferred_element_type=F32)


def _nsa_compress_kernel(f_ref, pe_ref, w1_ref, w2_ref, o_ref):
    blk = (f_ref[...].astype(F32) + pe_ref[...]).astype(BF16)
    h = jax.nn.gelu(jnp.dot(blk, w1_ref[...], preferred_element_type=F32))
    o_ref[...] = jnp.dot(h.astype(BF16), w2_ref[...], preferred_element_type=F32)


def _nsa_compress(flat, pe_flat, w1, w2p):
    M, K = flat.shape
    tm = min(ROW_TILE, M)
    Hc = w1.shape[1]
    return pl.pallas_call(
        _nsa_compress_kernel, grid=(M // tm,),
        in_specs=[pl.BlockSpec((tm, K), lambda i: (i, 0)), pl.BlockSpec((1, K), lambda i: (0, 0)),
                  pl.BlockSpec((K, Hc), lambda i: (0, 0)), pl.BlockSpec((Hc, LANES), lambda i: (0, 0))],
        out_specs=pl.BlockSpec((tm, LANES), lambda i: (i, 0)),
        out_shape=jax.ShapeDtypeStruct((M, LANES), F32),
        compiler_params=_cparams(("parallel",)), name="nsa_compress")(flat, pe_flat, w1, w2p)


def _nsa_cmp_kernel(q_ref, kk_ref, vt_ref, ovt_ref, gl_ref, bg_ref, ex_ref, o_ref, sn_ref, *, tq, ncp, nselp):
    i = pl.program_id(2)
    q = q_ref[...]
    kk = kk_ref[...]
    vt = vt_ref[...]
    t = i * tq + lax.broadcasted_iota(jnp.int32, (ncp, tq), 1)
    n = lax.broadcasted_iota(jnp.int32, (ncp, tq), 0)
    valid = n * NSA_CMP_STRIDE + (NSA_CMP_LEN - 1) <= t
    psum = jnp.zeros((ncp, tq), F32)
    outs = []
    for hg in range(NSA_HEADS_PER_GROUP):
        qh = _head_of_pair(q[:, LANES * (hg // 2):LANES * (hg // 2 + 1)], hg % 2)
        sT = jnp.where(valid, _nt_dot(kk, qh), NEG)
        m = jnp.max(sT, axis=0, keepdims=True)
        e = jnp.where(valid, jnp.exp2(sT - m), 0.0)
        p = e / jnp.maximum(jnp.sum(e, axis=0, keepdims=True), 1e-30)
        psum = psum + p
        outs.append(jnp.dot(vt, p.astype(BF16), preferred_element_type=F32))
    o = jnp.concatenate(outs, axis=0).T
    o_ref[...] = (o * _gate_expand(gl_ref, bg_ref, ex_ref)).astype(o_ref.dtype)
    hi = psum.astype(BF16)
    lo = (psum - hi.astype(F32)).astype(BF16)
    ovt = ovt_ref[...]
    imp = jnp.dot(ovt, hi, preferred_element_type=F32) + jnp.dot(ovt, lo, preferred_element_type=F32)
    jblk = lax.broadcasted_iota(jnp.int32, (nselp, tq), 0)
    cur = (i * tq + lax.broadcasted_iota(jnp.int32, (nselp, tq), 1)) >> _log2(NSA_SEL_LEN)
    forced = (jblk == 0) | (jblk == cur) | (jblk == cur - 1)
    pri = jnp.where(forced, imp + NSA_FORCE_BONUS, imp)
    past = jblk <= cur
    pri = jnp.where(past, pri, NEG)
    chosen = _topk_rows(pri, NSA_SEL_TOPN)
    sn = jnp.where((chosen > 0.0) & past, 0.0, NEG)
    sn_ref[...] = sn.T.astype(sn_ref.dtype)


def _nsa_cmp_branch(q_c, kk_c, vt_c, overlap_t, gl, bg, ex, B, S):
    T = q_c.shape[0]
    G = NSA_KV_GROUPS
    tq = min(ATTN_TILE, S)
    nq = S // tq
    ncp = kk_c.shape[2]
    nselp = overlap_t.shape[0]
    return pl.pallas_call(
        functools.partial(_nsa_cmp_kernel, tq=tq, ncp=ncp, nselp=nselp),
        grid=(B, G, nq),
        in_specs=[pl.BlockSpec((tq, 2 * LANES), lambda b, g, i: (b * nq + i, g)),
                  pl.BlockSpec((None, None, ncp, LANES), lambda b, g, i: (b, g, 0, 0)),
                  pl.BlockSpec((None, None, HEAD_DIM, ncp), lambda b, g, i: (b, g, 0, 0)),
                  pl.BlockSpec((nselp, ncp), lambda b, g, i: (0, 0)),
                  pl.BlockSpec((tq, LANES), lambda b, g, i: (b * nq + i, 0)),
                  pl.BlockSpec((1, LANES), lambda b, g, i: (0, 0)),
                  pl.BlockSpec((None, LANES, 2 * LANES), lambda b, g, i: (g, 0, 0))],
        out_specs=[pl.BlockSpec((tq, 2 * LANES), lambda b, g, i: (b * nq + i, g)),
                   pl.BlockSpec((None, None, tq, nselp), lambda b, g, i: (b, g, i, 0))],
        out_shape=[jax.ShapeDtypeStruct((T, D_MODEL), BF16),
                   jax.ShapeDtypeStruct((B, G, S, nselp), BF16)],
        compiler_params=_cparams(("parallel", "parallel", "parallel")), name="nsa_cmp")(
            q_c, kk_c, vt_c, overlap_t, gl, bg, ex)


def _nsa_kv_kernel(*refs, mode, tq, tk):
    if mode == "sel":
        q_ref, kk_ref, vt_ref, oh_ref, sn_ref, gl_ref, bg_ref, ex_ref, o_ref, m_ref, acc_ref = refs
    else:
        q_ref, kk_ref, vt_ref, gl_ref, bg_ref, ex_ref, o_ref, m_ref, acc_ref = refs
    i = pl.program_id(2)
    _flash_init(m_ref, acc_ref)
    q = q_ref[...]
    qh = []
    for hg in range(NSA_HEADS_PER_GROUP):
        qq = _head_of_pair(q[:, LANES * (hg // 2):LANES * (hg // 2 + 1)], hg % 2)
        if mode == "sel":
            qq = jnp.concatenate([qq, sn_ref[...]], axis=1)
        qh.append(qq)

    def step(off, size, mask):
        k = kk_ref[pl.ds(off, size), :]
        vt = vt_ref[:, pl.ds(off, size)]
        lhs = jnp.concatenate([vt, jnp.ones((ONES_ROWS, size), vt.dtype)], axis=0)
        if mode == "sel":
            k = jnp.concatenate([k, oh_ref[pl.ds(off, size), :]], axis=1)

        def scores(u):
            sT = _nt_dot(k, qh[u[0]][u[1], :])
            return sT if mask is None else jnp.where(mask[:, u[1]], sT, NEG)

        _staggered(_head_units(NSA_HEADS_PER_GROUP, tq), scores,
                   lambda u, sT: _flash_step(sT, lhs, m_ref, acc_ref, u[0], u[1]))

    if mode == "sel":
        _causal_sweep(i, tq, tk, step)
    else:
        @pl.when(i > 0)
        def _():
            step(pl.multiple_of((i - 1) * tq, tq), tq, _causal_t(tq, tq, strict_lower=True))
        step(pl.multiple_of(i * tq, tq), tq, _causal_t(tq, tq))
    outs = []
    for hg in range(NSA_HEADS_PER_GROUP):
        a = acc_ref[hg]
        outs.append(a[:HEAD_DIM] / a[HEAD_DIM:HEAD_DIM + 1, :])
    o_ref[...] = (jnp.concatenate(outs, axis=0).T * _gate_expand(gl_ref, bg_ref, ex_ref)).astype(o_ref.dtype)


def _nsa_kv_branch(mode, q_r, kk, vt, gl, bg, ex, B, S, onehot=None, selneg=None):
    T = q_r.shape[0]
    G = NSA_KV_GROUPS
    tq = min(PAIR_TILE if mode == "sel" else ATTN_TILE, S)
    tk = min(KV_TILE, S)
    assert NSA_WINDOW == tq or mode == "sel"
    nq = S // tq
    in_specs = [pl.BlockSpec((tq, 2 * LANES), lambda b, g, i: (b * nq + i, g)),
                pl.BlockSpec((None, None, S, LANES), lambda b, g, i: (b, g, 0, 0)),
                pl.BlockSpec((HEAD_DIM, S), lambda b, g, i: (g, b))]
    args = [q_r, kk, vt]
    if mode == "sel":
        in_specs += [pl.BlockSpec((S, LANES), lambda b, g, i: (0, 0)),
                     pl.BlockSpec((None, None, tq, LANES), lambda b, g, i: (b, g, i, 0))]
        args += [onehot, selneg]
    in_specs += [pl.BlockSpec((tq, LANES), lambda b, g, i: (b * nq + i, 0)),
                 pl.BlockSpec((1, LANES), lambda b, g, i: (0, 0)),
                 pl.BlockSpec((None, LANES, 2 * LANES), lambda b, g, i: (g, 0, 0))]
    args += [gl, bg, ex]
    nh = NSA_HEADS_PER_GROUP
    return pl.pallas_call(
        functools.partial(_nsa_kv_kernel, mode=mode, tq=tq, tk=tk),
        grid=(B, G, nq), in_specs=in_specs,
        out_specs=pl.BlockSpec((tq, 2 * LANES), lambda b, g, i: (b * nq + i, g)),
        out_shape=jax.ShapeDtypeStruct((T, D_MODEL), BF16),
        scratch_shapes=[pltpu.VMEM((nh, 1, tq), F32), pltpu.VMEM((nh, HEAD_DIM + ONES_ROWS, tq), F32)],
        compiler_params=_cparams(("parallel", "parallel", "arbitrary")), name=f"nsa_{mode}")(*args)


def _kmean_kernel(k_ref, o_ref, *, nblk):
    k = k_ref[...].astype(F32)
    o_ref[...] = jnp.mean(k.reshape(nblk, MOBA_BLOCK, k.shape[1]), axis=1)


def _moba_kmean(k_rot):
    T, D = k_rot.shape
    nblk = 8
    rows = nblk * MOBA_BLOCK
    assert T % rows == 0
    return pl.pallas_call(
        functools.partial(_kmean_kernel, nblk=nblk), grid=(T // rows,),
        in_specs=[pl.BlockSpec((rows, D), lambda i: (i, 0))],
        out_specs=pl.BlockSpec((nblk, D), lambda i: (i, 0)),
        out_shape=jax.ShapeDtypeStruct((T // MOBA_BLOCK, D), F32),
        compiler_params=_cparams(("parallel",)), name="moba_kmean")(k_rot)


def _moba_select_kernel(q_ref, r_ref, sn_ref, *, tq, nblk):
    i = pl.program_id(2)
    gsT = _nt_dot(r_ref[...], q_ref[...].astype(F32), HIGHEST)
    jblk = lax.broadcasted_iota(jnp.int32, (nblk, tq), 0)
    cb = (i * tq + lax.broadcasted_iota(jnp.int32, (nblk, tq), 1)) >> _log2(MOBA_BLOCK)
    past = jblk < cb
    parts = []
    for h in range(2):
        gs = jnp.where(past, gsT[h * nblk:(h + 1) * nblk, :], NEG)
        chosen = _topk_rows(gs, min(MOBA_TOPK, nblk))
        parts.append(jnp.where(((chosen > 0.0) & past) | (jblk == cb), 0.0, NEG))
    parts.append(jnp.zeros((LANES - 2 * nblk, tq), F32))
    sn_ref[...] = jnp.concatenate(parts, axis=0).T.astype(sn_ref.dtype)


def _moba_select(q_rot, rmat_t, B, S):
    tq = min(ATTN_TILE, S)
    nq = S // tq
    npair = N_HEADS // 2
    nblk = S // MOBA_BLOCK
    return pl.pallas_call(
        functools.partial(_moba_select_kernel, tq=tq, nblk=nblk), grid=(B, npair, nq),
        in_specs=[pl.BlockSpec((tq, LANES), lambda b, hp, i: (b * nq + i, hp)),
                  pl.BlockSpec((None, None, LANES, LANES), lambda b, hp, i: (b, hp, 0, 0))],
        out_specs=pl.BlockSpec((None, None, tq, LANES), lambda b, hp, i: (b, hp, i, 0)),
        out_shape=jax.ShapeDtypeStruct((B, npair, S, LANES), BF16),
        compiler_params=_cparams(("parallel", "parallel", "parallel")), name="moba_select")(q_rot, rmat_t)


def _moba_kernel(q_ref, k_ref, vt_ref, oh_ref, sn_ref, o_ref, m_ref, acc_ref, *, tq, tk, nblk):
    i = pl.program_id(2)
    _flash_init(m_ref, acc_ref)
    q = q_ref[...]
    sn = sn_ref[...]
    lane = lax.broadcasted_iota(jnp.int32, sn.shape, 1)
    qa = []
    for h in range(2):
        mine = (lane >= h * nblk) & (lane < (h + 1) * nblk)
        qa.append(jnp.concatenate([_head_of_pair(q, h), jnp.where(mine, sn, jnp.zeros_like(sn))], axis=1))

    def step(off, size, mask):
        k = jnp.concatenate([k_ref[pl.ds(off, size), :], oh_ref[pl.ds(off, size), :]], axis=1)
        vt = vt_ref[:, pl.ds(off, size)]

        lhs = [_pair_lhs(vt, h) for h in range(2)]

        def scores(u):
            sT = _nt_dot(k, qa[u[0]][u[1], :])
            return sT if mask is None else jnp.where(mask[:, u[1]], sT, NEG)

        _staggered(_pair_units(tq), scores,
                   lambda u, sT: _flash_step(sT, lhs[u[0]], m_ref, acc_ref, u[0], u[1]))

    _causal_sweep(i, tq, tk, step)
    o_ref[...] = _pair_finish(acc_ref).T.astype(o_ref.dtype)


def _moba_attention(qk_rot, vt, onehot2, selneg, B, S):
    T = qk_rot.shape[0]
    tq = min(PAIR_TILE, S)
    tk = min(KV_TILE, S)
    nq = S // tq
    npair = N_HEADS // 2
    ncol = D_MODEL // LANES
    nblk = S // MOBA_BLOCK
    return pl.pallas_call(
        functools.partial(_moba_kernel, tq=tq, tk=tk, nblk=nblk),
        grid=(B, npair, nq),
        in_specs=[pl.BlockSpec((tq, LANES), lambda b, hp, i: (b * nq + i, hp)),
                  pl.BlockSpec((S, LANES), lambda b, hp, i: (b, ncol + hp)),
                  pl.BlockSpec((LANES, S), lambda b, hp, i: (hp, b)),
                  pl.BlockSpec((S, LANES), lambda b, hp, i: (0, 0)),
                  pl.BlockSpec((None, None, tq, LANES), lambda b, hp, i: (b, hp, i, 0))],
        out_specs=pl.BlockSpec((tq, LANES), lambda b, hp, i: (b * nq + i, hp)),
        out_shape=jax.ShapeDtypeStruct((T, D_MODEL), BF16),
        scratch_shapes=[pltpu.VMEM((2, 1, tq), F32), pltpu.VMEM((2, LANES, tq), F32)],
        compiler_params=_cparams(("parallel", "parallel", "arbitrary")), name="moba_attn")(
            qk_rot, qk_rot, vt, onehot2, selneg)


def _router_kernel(wt_ref, x_ref, rb_ref, e_ref, g_ref):
    tm = x_ref.shape[0]
    gsz = N_EXPERTS // N_GROUPS
    scores = jax.nn.sigmoid(_dot3(wt_ref[...], x_ref[...], nt=True))
    biased = scores + rb_ref[...]
    member = lax.broadcasted_iota(jnp.int32, (gsz, tm), 0).astype(F32)
    gscore = []
    for g in range(N_GROUPS):
        v = biased[g * gsz:(g + 1) * gsz, :]
        m1 = jnp.max(v, axis=0, keepdims=True)
        i1 = jnp.min(jnp.where(v == m1, member, float(gsz)), axis=0, keepdims=True)
        m2 = jnp.max(jnp.where(member == i1, -jnp.inf, v), axis=0, keepdims=True)
        gscore.append(m1 + m2)
    gsel = _topk_rows(jnp.concatenate(gscore, axis=0), TOPK_GROUPS)
    emask = jnp.concatenate([jnp.broadcast_to(gsel[g:g + 1, :], (gsz, tm)) for g in range(N_GROUPS)], axis=0)
    work = jnp.where(emask > 0.0, biased, NEG)
    erow = lax.broadcasted_iota(jnp.int32, (N_EXPERTS, tm), 0).astype(F32)
    idxs, vals = [], []
    for _ in range(TOP_K):
        m = jnp.max(work, axis=0, keepdims=True)
        idx = jnp.min(jnp.where(work == m, erow, float(N_EXPERTS)), axis=0, keepdims=True)
        pick = erow == idx
        idxs.append(idx)
        vals.append(jnp.sum(jnp.where(pick, scores, 0.0), axis=0, keepdims=True))
        work = jnp.where(pick, -jnp.inf, work)
    gw = jnp.concatenate(vals, axis=0)
    e_ref[...] = jnp.concatenate(idxs, axis=0).astype(jnp.int32)
    g_ref[...] = gw / jnp.sum(gw, axis=0, keepdims=True) * ROUTED_SCALE


def _router(x1, router_w, router_b):
    T, D = x1.shape
    tm = min(ROW_TILE, T)
    return pl.pallas_call(
        _router_kernel, grid=(T // tm,),
        in_specs=[pl.BlockSpec((N_EXPERTS, D), lambda i: (0, 0)), pl.BlockSpec((tm, D), lambda i: (i, 0)),
                  pl.BlockSpec((N_EXPERTS, 1), lambda i: (0, 0))],
        out_specs=[pl.BlockSpec((TOP_K, tm), lambda i: (0, i))] * 2,
        out_shape=[jax.ShapeDtypeStruct((TOP_K, T), jnp.int32), jax.ShapeDtypeStruct((TOP_K, T), F32)],
        compiler_params=_cparams(("parallel",)), name="moe_router")(
            router_w.T, x1, router_b.reshape(N_EXPERTS, 1))


def _rank_kernel(e_ref, tri_ref, rank_ref, cnt_ref, carry_ref):
    @pl.when(pl.program_id(0) == 0)
    def _():
        carry_ref[...] = jnp.zeros(carry_ref.shape, F32)

    tm = e_ref.shape[1]
    erow = lax.broadcasted_iota(jnp.int32, (N_EXPERTS, tm), 0)
    tri = tri_ref[...]
    base = carry_ref[...]
    ranks = []
    for k in range(TOP_K):
        oh = erow == e_ref[k:k + 1, :]
        ohb = jnp.where(oh, 1.0, 0.0).astype(BF16)
        incl = jnp.dot(ohb, tri, preferred_element_type=F32)
        ranks.append(jnp.sum(jnp.where(oh, base + incl - 1.0, 0.0), axis=0, keepdims=True))
        base = base + incl[:, tm - 1:tm]
    carry_ref[...] = base
    rank_ref[...] = jnp.concatenate(ranks, axis=0).astype(jnp.int32)
    cnt_ref[...] = jnp.broadcast_to(base, cnt_ref.shape)


def _expert_ranks(eidx_t):
    K, T = eidx_t.shape
    tm = min(ROW_TILE, T)
    tri = jnp.asarray(np.triu(np.ones((tm, tm), np.float32)), BF16)
    return pl.pallas_call(
        _rank_kernel, grid=(T // tm,),
        in_specs=[pl.BlockSpec((K, tm), lambda i: (0, i)), pl.BlockSpec((tm, tm), lambda i: (0, 0))],
        out_specs=[pl.BlockSpec((K, tm), lambda i: (0, i)), pl.BlockSpec((N_EXPERTS, LANES), lambda i: (0, 0))],
        out_shape=[jax.ShapeDtypeStruct((K, T), jnp.int32), jax.ShapeDtypeStruct((N_EXPERTS, LANES), F32)],
        scratch_shapes=[pltpu.VMEM((N_EXPERTS, 1), F32)],
        compiler_params=_cparams(("arbitrary",)), name="moe_rank")(eidx_t, tri)


def _dest_kernel(e_ref, rank_ref, ps_ref, d_ref):
    tm = e_ref.shape[1]
    erow = lax.broadcasted_iota(jnp.int32, (N_EXPERTS, tm), 0)
    ps = ps_ref[...]
    rows = []
    for k in range(TOP_K):
        oh = erow == e_ref[k:k + 1, :]
        rows.append(jnp.sum(jnp.where(oh, ps, 0.0), axis=0, keepdims=True))
    d_ref[...] = jnp.concatenate(rows, axis=0).astype(jnp.int32) + rank_ref[...]


def _expert_dest(eidx_t, rank_t, pstarts):
    K, T = eidx_t.shape
    tm = min(ROW_TILE, T)
    blk = pl.BlockSpec((K, tm), lambda i: (0, i))
    return pl.pallas_call(
        _dest_kernel, grid=(T // tm,),
        in_specs=[blk, blk, pl.BlockSpec((N_EXPERTS, 1), lambda i: (0, 0))],
        out_specs=blk, out_shape=jax.ShapeDtypeStruct((K, T), jnp.int32),
        compiler_params=_cparams(("parallel",)), name="moe_dest")(
            eidx_t, rank_t, pstarts.astype(F32).reshape(N_EXPERTS, 1))


def _expert_kernel(be_ref, nb_ref, x_ref, *refs):
    n = EXPERT_SLOTS
    w_refs, (o_ref, wgu_ref, wdb_ref) = refs[:3 * n], refs[3 * n:]
    s = pl.program_id(0)
    E = wdb_ref.shape[1]
    R = MOE_BLOCK

    for j in range(n):
        b = s * n + j

        @pl.when((s == 0) | (be_ref[b] != be_ref[jnp.maximum(b - n, 0)]))
        def _(j=j):
            wg_ref, wu_ref, wd_ref = w_refs[3 * j:3 * j + 3]
            wgu_ref[j, :, :E] = wg_ref[...].astype(BF16)
            wgu_ref[j, :, E:] = wu_ref[...].astype(BF16)
            wdb_ref[j] = wd_ref[...].astype(BF16)

    @pl.when(s * n < nb_ref[0])
    def _():
        x = _load_pieces(x_ref).astype(BF16)
        ys = []

        def gate_up(j):
            return jnp.dot(x[j * R:(j + 1) * R], wgu_ref[j], preferred_element_type=F32)

        def down(j, gu):
            h = jax.nn.silu(gu[:, :E]) * gu[:, E:]
            ys.append(jnp.dot(h.astype(BF16), wdb_ref[j], preferred_element_type=F32))

        _staggered(list(range(n)), gate_up, down)
        _store_pieces(o_ref, jnp.concatenate(ys, axis=0))

    @pl.when(s * n >= nb_ref[0])
    def _():
        o_ref[...] = jnp.zeros(o_ref.shape, o_ref.dtype)


def _expert_ffn(blk_e, nb_used, xs, wg, wu, wd, layer):
    P = xs.shape[1]
    D = wg.shape[2]
    n = EXPERT_SLOTS
    rows = n * MOE_BLOCK
    assert P % rows == 0
    E = EXPERT_DIM
    w_specs = []
    for j in range(n):
        pick = functools.partial(lambda s, be, nb, j: (layer, be[s * n + j], 0, 0), j=j)
        w_specs += [pl.BlockSpec((None, None, D, E), pick), pl.BlockSpec((None, None, D, E), pick),
                    pl.BlockSpec((None, None, E, D), pick)]
    grid_spec = pltpu.PrefetchScalarGridSpec(
        num_scalar_prefetch=2, grid=(P // rows,),
        in_specs=[_piece_spec(D, rows, lambda s, be, nb: (0, s, 0))] + w_specs,
        out_specs=_piece_spec(D, rows, lambda s, be, nb: (0, s, 0)),
        scratch_shapes=[pltpu.VMEM((n, D, 2 * E), BF16), pltpu.VMEM((n, E, D), BF16)])
    return pl.pallas_call(
        _expert_kernel, grid_spec=grid_spec, out_shape=_piece_shape(D, P),
        compiler_params=_cparams(("arbitrary",)), name="moe_experts")(
            blk_e, nb_used, xs, *([wg, wu, wd] * n))


def _post_moe_kernel(x_ref, *refs):
    y_refs = refs[:TOP_K]
    gw_ref, p_ref, sg_ref, su_ref, sd_ref, g_ref, b_ref, wg_ref, wp_ref, o_ref, ob_ref = refs[TOP_K:]
    x = x_ref[...]
    xb = x.astype(BF16)
    h = jax.nn.silu(jnp.dot(xb, sg_ref[...], preferred_element_type=F32)) * jnp.dot(
        xb, su_ref[...], preferred_element_type=F32)
    ffn = jnp.dot(h.astype(BF16), sd_ref[...], preferred_element_type=F32)
    gw = gw_ref[...]
    for k in range(TOP_K):
        ffn = ffn + gw[:, k:k + 1] * _load_pieces(y_refs[k])
    z = DN_ALPHA * x + ffn
    x2 = _layer_norm(z, g_ref[...], b_ref[...])
    gate = jax.nn.sigmoid(jnp.dot(x2.astype(BF16), wg_ref[...], preferred_element_type=F32))
    proj = jnp.dot(p_ref[...].astype(BF16), wp_ref[...], preferred_element_type=F32)
    out = x2 + gate * proj
    o_ref[...] = out
    ob_ref[...] = out.astype(BF16)


def _post_moe(x1, yg, gw, p, sg, su, sd, g, b, wgate, wproj):
    T, D = x1.shape
    tm = min(ROW_TILE // 2, T)
    nt = T // tm
    PD = p.shape[1]
    SD = sg.shape[1]
    row = pl.BlockSpec((tm, D), lambda i: (i, 0))
    vec = pl.BlockSpec((1, D), lambda i: (0, 0))
    full = lambda r, c: pl.BlockSpec((r, c), lambda i: (0, 0))
    return pl.pallas_call(
        _post_moe_kernel, grid=(T // tm,),
        in_specs=[row] + [_piece_spec(D, tm, functools.partial(lambda i, k: (0, k * nt + i, 0), k=k))
                          for k in range(TOP_K)]
                 + [pl.BlockSpec((tm, TOP_K), lambda i: (i, 0)),
                    pl.BlockSpec((tm, PD), lambda i: (i, 0)), full(D, SD), full(D, SD),
                  full(SD, D), vec, vec, full(D, D), full(PD, D)],
        out_specs=[row, row],
        out_shape=[jax.ShapeDtypeStruct((T, D), F32), jax.ShapeDtypeStruct((T, D), BF16)],
        compiler_params=_cparams(("parallel",)), name="post_moe")(
            x1, *([yg] * TOP_K), gw, p, sg, su, sd, g.reshape(1, D), b.reshape(1, D), wgate, wproj)


def _rope_tables(positions):
    inv = 1.0 / (ROPE_THETA ** (jnp.arange(0, HEAD_DIM, 2, dtype=F32) / HEAD_DIM))
    ang = positions.astype(F32).reshape(-1)[:, None] * inv
    cos, sin = jnp.cos(ang), jnp.sin(ang)
    cosf = jnp.concatenate([cos] * (LANES // (HEAD_DIM // 2)), axis=1)
    sinf = jnp.concatenate([-sin, sin] * (LANES // HEAD_DIM), axis=1)
    return cosf, sinf


def _q_col_scale(n_q, n):
    return jnp.concatenate([jnp.full((n_q,), Q_SCALE, F32), jnp.ones((n - n_q,), F32)])


def _fox_mixer(x, xb, B, S, w_in, b_f):
    D = D_MODEL
    wb = w_in[:, :3 * D].astype(BF16)
    qk = _proj(xb, wb[:, :2 * D], col_scale=_q_col_scale(D, 2 * D))
    vt = _proj_t(wb[:, 2 * D:].T, xb)
    fl = _proj(x, _pad_cols(w_in[:, 3 * D:], LANES), out_dtype=F32, tn=LANES, precision="split3")
    cp = _fox_gate(fl, b_f, B, S)
    return [_fox_attention(qk, cp, vt, B, S)]


def _nsa_mixer(x, xb, B, S, cosf, sinf, w_in, b_gate, pe_k, pe_v, ck_w1, ck_w2, cv_w1, cv_w2):
    D, G, HD = D_MODEL, NSA_KV_GROUPS, HEAD_DIM
    HG = NSA_HEADS_PER_GROUP
    kvw = G * HD
    wb = w_in[:, :D + 6 * kvw].astype(BF16)
    q_c, q_r = _proj(xb, wb[:, :D], mode="both", cos=cosf, sin=sinf, col_scale=_q_col_scale(D, D))
    w_rot = jnp.concatenate([wb[:, D + 2 * kvw:D + 3 * kvw], wb[:, D + 4 * kvw:D + 5 * kvw]], axis=1)
    k_rot = _proj(xb, w_rot, mode="rope", cos=cosf, sin=sinf)
    kvc = _proj(xb, wb[:, D:D + 2 * kvw])
    w_v = jnp.concatenate([wb[:, D + 3 * kvw:D + 4 * kvw], wb[:, D + 5 * kvw:D + 6 * kvw]], axis=1)
    vt = _proj_t(w_v.T, xb)
    gl = _proj(x, _pad_cols(w_in[:, D + 6 * kvw:], LANES), out_dtype=F32, tn=LANES, precision="split3")
    bg = _pad_cols(b_gate.reshape(1, -1), LANES)

    def grouped(t2d):
        t = t2d.reshape(B, S, G, HD).transpose(0, 2, 1, 3)
        return jnp.concatenate([t, t], axis=-1)

    n_chunks = S // NSA_CMP_STRIDE
    n_cmp = n_chunks - NSA_CMP_LEN // NSA_CMP_STRIDE + 1
    ncp = n_chunks

    def compress(t2d, pe, w1, w2):
        ch = t2d.reshape(B, n_chunks, NSA_CMP_STRIDE, G, HD).transpose(0, 1, 3, 2, 4)
        ch = ch.reshape(B, n_chunks, G, NSA_CMP_STRIDE * HD)
        flat = jnp.concatenate([ch[:, :n_cmp], ch[:, 1:n_cmp + 1]], axis=-1)
        flat = jnp.pad(flat, ((0, 0), (0, ncp - n_cmp), (0, 0), (0, 0))).reshape(B * ncp * G, -1)
        out = _nsa_compress(flat, pe.reshape(1, -1), w1.astype(BF16), _pad_cols(w2, LANES).astype(BF16))
        return out[:, :HD].reshape(B, ncp, G, HD).astype(BF16)

    kc = compress(kvc[:, :kvw], pe_k, ck_w1, ck_w2).transpose(0, 2, 1, 3)
    kk_c = jnp.concatenate([kc, kc], axis=-1)
    vt_c = compress(kvc[:, kvw:], pe_v, cv_w1, cv_w2).transpose(0, 2, 3, 1)

    n_sel = S // NSA_SEL_LEN
    assert n_sel <= LANES
    cmp_start = np.arange(ncp) * NSA_CMP_STRIDE
    sel_start = np.arange(LANES) * NSA_SEL_LEN
    overlap = ((cmp_start[:, None] < sel_start[None, :] + NSA_SEL_LEN)
               & (cmp_start[:, None] + NSA_CMP_LEN > sel_start[None, :])
               & (np.arange(ncp)[:, None] < n_cmp) & (np.arange(LANES)[None, :] < n_sel))
    overlap_t = jnp.asarray(overlap.T, BF16)
    ex = np.zeros((3, G, LANES, HG * HD), np.float32)
    for br in range(3):
        for g in range(G):
            for hg in range(HG):
                ex[br, g, (g * HG + hg) * 3 + br, hg * HD:(hg + 1) * HD] = 1.0
    ex = jnp.asarray(ex, BF16)

    o_c, selneg = _nsa_cmp_branch(q_c, kk_c, vt_c, overlap_t, gl, bg, ex[0], B, S)
    onehot = jnp.asarray((np.arange(S)[:, None] // NSA_SEL_LEN) == np.arange(LANES)[None, :], BF16)
    kk_s = grouped(k_rot[:, :kvw])
    kk_w = grouped(k_rot[:, kvw:])
    o_s = _nsa_kv_branch("sel", q_r, kk_s, vt[:kvw], gl, bg, ex[1], B, S, onehot=onehot, selneg=selneg)
    o_w = _nsa_kv_branch("win", q_r, kk_w, vt[kvw:], gl, bg, ex[2], B, S)
    return [o_c, o_s, o_w]


def _moba_mixer(xb, B, S, cosf, sinf, w_in):
    D, H, HD = D_MODEL, N_HEADS, HEAD_DIM
    wb = w_in.astype(BF16)
    qk_rot = _proj(xb, wb[:, :2 * D], mode="rope", cos=cosf, sin=sinf, col_scale=_q_col_scale(D, 2 * D))
    vt = _proj_t(wb[:, 2 * D:].T, xb)
    nblk = S // MOBA_BLOCK
    assert S % MOBA_BLOCK == 0 and 2 * nblk <= LANES and nblk % SUBLANES == 0
    kmean = _moba_kmean(qk_rot[:, D:]).reshape(B, nblk, H // 2, 2, HD)
    km = kmean.transpose(0, 2, 3, 1, 4)
    rmat_t = jnp.zeros((B, H // 2, LANES, 2, HD), F32)
    rmat_t = rmat_t.at[:, :, :nblk, 0].set(km[:, :, 0]).at[:, :, nblk:2 * nblk, 1].set(km[:, :, 1])
    selneg = _moba_select(qk_rot, rmat_t.reshape(B, H // 2, LANES, LANES), B, S)
    blk_of = np.arange(S) // MOBA_BLOCK
    oh = np.zeros((S, LANES), np.float32)
    oh[np.arange(S), blk_of] = 1.0
    oh[np.arange(S), nblk + blk_of] = 1.0
    return [_moba_attention(qk_rot, vt, jnp.asarray(oh, BF16), selneg, B, S)]


def _sc_invert_kernel(dest_hbm, tok_hbm, out_hbm, d_v, t_v, buf_v, *, seg, chunk, n_chunks, n_tok):
    lo = pl.multiple_of((lax.axis_index("s") * SC_CORES + lax.axis_index("c")) * seg, SC_LANES)

    @pl.loop(0, seg // SC_LANES)
    def _(j):
        row = lo + j * SC_LANES + lax.iota(jnp.int32, SC_LANES)
        buf_v[pl.ds(j * SC_LANES, SC_LANES)] = lax.rem(row, jnp.int32(n_tok))

    @pl.loop(0, n_chunks)
    def _(c):
        pltpu.sync_copy(dest_hbm.at[pl.ds(c * chunk, chunk)], d_v)
        pltpu.sync_copy(tok_hbm.at[pl.ds(c * chunk, chunk)], t_v)

        @pl.loop(0, chunk // SC_LANES)
        def _(j):
            d = d_v[pl.ds(j * SC_LANES, SC_LANES)] - lo
            mine = (d >= 0) & (d < seg)
            plsc.store_scatter(buf_v, [jnp.where(mine, d, 0)], t_v[pl.ds(j * SC_LANES, SC_LANES)], mask=mine)

    pltpu.sync_copy(buf_v, out_hbm.at[pl.ds(lo, seg)])


def _invert_dest(dest, tok, P, n_tok):
    R = dest.shape[0]
    n_workers = SC_CORES * SC_SUBCORES
    seg = P // n_workers
    chunk = min(4096, R)
    assert P == seg * n_workers and seg % SC_LANES == 0 and R % chunk == 0
    mesh = plsc.VectorSubcoreMesh(core_axis_name="c", subcore_axis_name="s",
                                  num_cores=SC_CORES, num_subcores=SC_SUBCORES)
    return pl.kernel(
        functools.partial(_sc_invert_kernel, seg=seg, chunk=chunk, n_chunks=R // chunk, n_tok=n_tok),
        out_type=jax.ShapeDtypeStruct((P,), jnp.int32), mesh=mesh,
        scratch_types=[pltpu.VMEM((chunk,), jnp.int32), pltpu.VMEM((chunk,), jnp.int32),
                       pltpu.VMEM((seg,), jnp.int32)],
        compiler_params=pltpu.CompilerParams(needs_layout_passes=False),
        name="moe_invert")(dest, tok)


def _sc_mesh():
    return plsc.VectorSubcoreMesh(core_axis_name="c", subcore_axis_name="s",
                                  num_cores=SC_CORES, num_subcores=SC_SUBCORES)


def _gather_rows(table, idx):
    parts, V, _ = table.shape
    N = idx.shape[0]
    n_pieces = N * parts
    n_workers = SC_CORES * SC_SUBCORES
    assert n_pieces % (SC_WINDOW * n_workers) == 0
    pieces = (jnp.arange(parts, dtype=jnp.int32)[:, None] * V + idx[None, :]).reshape(1, n_pieces)

    def kernel_body(x_hbm, i_hbm, o_hbm):
        def body(i_vmem, o_vmem):
            pltpu.sync_copy(x_hbm.at[i_vmem.at[0]], o_vmem)

        pltpu.emit_pipeline(
            body, grid=(n_pieces // SC_WINDOW,),
            in_specs=[pl.BlockSpec((1, SC_WINDOW), lambda i: (0, i))],
            out_specs=[pl.BlockSpec((SC_WINDOW, SC_ROW_WORDS), lambda i: (i, 0))],
            core_axis_name=("c", "s"), dimension_semantics=(pltpu.PARALLEL,))(i_hbm, o_hbm)

    out = pl.kernel(kernel_body, out_type=jax.ShapeDtypeStruct((n_pieces, SC_ROW_WORDS), table.dtype),
                    mesh=_sc_mesh(), scratch_types=[], name="moe_gather")(
                        table.reshape(parts * V, SC_ROW_WORDS), pieces)
    return out.reshape(parts, N, SC_ROW_WORDS)


def _moe_dispatch(x1, x1p, router_w, router_b, wg, wu, wd, layer):
    T, D = x1.shape
    eidx_t, gw_t = _router(x1, router_w, router_b)
    rank_t, cnt = _expert_ranks(eidx_t)
    counts = cnt[:, 0].astype(jnp.int32)
    padded = (counts + MOE_BLOCK - 1) // MOE_BLOCK * MOE_BLOCK
    pends = jnp.cumsum(padded)
    pstarts = pends - padded
    R = T * TOP_K
    P = -(-(R + N_EXPERTS * (MOE_BLOCK - 1)) // MOE_BLOCK) * MOE_BLOCK
    NB = P // MOE_BLOCK
    blk_start = jnp.arange(NB, dtype=jnp.int32) * MOE_BLOCK
    blk_e = jnp.minimum(jnp.sum(pends[None, :] <= blk_start[:, None], axis=1), N_EXPERTS - 1).astype(jnp.int32)
    nb_used = (pends[-1] // MOE_BLOCK).astype(jnp.int32).reshape(1)
    dest_t = _expert_dest(eidx_t, rank_t, pstarts)

    tok = jnp.broadcast_to(jnp.arange(T, dtype=jnp.int32)[None, :], (TOP_K, T))
    buf_tok = _invert_dest(dest_t.reshape(-1), tok.reshape(-1), P, T)
    xs = _gather_rows(x1p, buf_tok)
    yb = _expert_ffn(blk_e, nb_used, xs, wg, wu, wd, layer)
    yg = _gather_rows(yb, dest_t.reshape(-1))
    return yg, gw_t.T


def kernel(x, p, positions, fox_w_in, fox_b_f, fox_w_out, nsa_w_in, nsa_b_gate, nsa_pe_k, nsa_pe_v,
           nsa_cmp_k_w1, nsa_cmp_k_w2, nsa_cmp_v_w1, nsa_cmp_v_w2, nsa_w_out, moba_w_in, moba_w_out,
           ln1_g, ln1_b, router_w, router_b, exp_w_gate, exp_w_up, exp_w_down,
           sh_w_gate, sh_w_up, sh_w_down, ln2_g, ln2_b, ple_w_gate, ple_w_proj):
    B, S, D = x.shape
    T = B * S
    depth = p.shape[0]
    cosf, sinf = _rope_tables(positions)
    xt = x.reshape(T, D)
    xtb = xt.astype(BF16)
    for i in range(depth):
        kind, j = i % N_MIXERS, i // N_MIXERS
        if kind == 0:
            o_list = _fox_mixer(xt, xtb, B, S, fox_w_in[j], fox_b_f[j])
            w_out = fox_w_out[j]
        elif kind == 1:
            o_list = _nsa_mixer(xt, xtb, B, S, cosf, sinf, nsa_w_in[j], nsa_b_gate[j], nsa_pe_k[j], nsa_pe_v[j],
                                nsa_cmp_k_w1[j], nsa_cmp_k_w2[j], nsa_cmp_v_w1[j], nsa_cmp_v_w2[j])
            w_out = nsa_w_out[j]
        else:
            o_list = _moba_mixer(xtb, B, S, cosf, sinf, moba_w_in[j])
            w_out = moba_w_out[j]
        x1, x1p = _outproj_ln(o_list, w_out.astype(BF16), xt, ln1_g[i], ln1_b[i])
        yg, gw = _moe_dispatch(x1, x1p, router_w[i], router_b[i], exp_w_gate, exp_w_up, exp_w_down, i)
        xt, xtb = _post_moe(x1, yg, gw, p[i].reshape(T, -1), sh_w_gate[i].astype(BF16), sh_w_up[i].astype(BF16),
                            sh_w_down[i].astype(BF16), ln2_g[i], ln2_b[i],
                            ple_w_gate[i].astype(BF16), ple_w_proj[i].astype(BF16))
    return xt.reshape(B, S, D)
```

```python
import functools
import math

import jax
import jax.numpy as jnp
import numpy as np
from jax import lax
from jax.experimental import pallas as pl
from jax.experimental.pallas import tpu as pltpu
from jax.experimental.pallas import tpu_sc as plsc

D_MODEL = 1024
DEPTH = 4
N_HEADS = 16
HEAD_DIM = 64
ATTN_SCALE = HEAD_DIM ** -0.5
LOG2E = 1.4426950408889634
Q_SCALE = ATTN_SCALE * LOG2E
ROPE_THETA = 10000.0
N_MIXERS = 3

NSA_KV_GROUPS = 4
NSA_HEADS_PER_GROUP = N_HEADS // NSA_KV_GROUPS
NSA_CMP_LEN = 32
NSA_CMP_STRIDE = 16
NSA_SEL_LEN = 64
NSA_SEL_TOPN = 16
NSA_WINDOW = 512
NSA_FORCE_BONUS = 1e4

MOBA_BLOCK = 256
MOBA_TOPK = 3

N_EXPERTS = 64
EXPERT_DIM = 256
TOP_K = 8
N_GROUPS = 8
TOPK_GROUPS = 4
ROUTED_SCALE = 2.5
MOE_BLOCK = 256
EXPERT_SLOTS = 3

DN_ALPHA = (2 * DEPTH) ** 0.25
LN_EPS = 1e-5
NEG = -1e30

SC_CORES = 2
SC_SUBCORES = 16
SC_LANES = 16
SC_ROW_WORDS = 256
SC_WINDOW = 128
SC_INDEX_CHUNK = 4096
LANES = 128
SUBLANES = 8
ATTN_TILE = 512
PAIR_TILE = 1024
KV_TILE = 1024
ROW_TILE = 512
VMEM_LIMIT = 48 * 1024 * 1024
C_SPLIT = 3
ONES_ROWS = 16

F32 = jnp.float32
BF16 = jnp.bfloat16
HIGHEST = lax.Precision.HIGHEST


def _cparams(sem):
    return pltpu.CompilerParams(dimension_semantics=sem, vmem_limit_bytes=VMEM_LIMIT)


def _log2(n):
    assert n & (n - 1) == 0
    return n.bit_length() - 1


def _nt_dot(a, b, precision=None):
    return lax.dot_general(a, b, (((1,), (1,)), ((), ())), preferred_element_type=F32,
                           precision=precision)


def _split_bf16(a):
    hi = a.astype(BF16)
    return hi, (a - hi.astype(F32)).astype(BF16)


def _dot3(x, w, nt=False):
    dot = _nt_dot if nt else functools.partial(jnp.dot, preferred_element_type=F32)
    xh, xl = _split_bf16(x)
    wh, wl = _split_bf16(w)
    return dot(xh, wh) + dot(xl, wh) + dot(xh, wl)


def _topk_rows(work, n):
    rows = lax.broadcasted_iota(jnp.int32, work.shape, 0).astype(F32)
    chosen = jnp.zeros(work.shape, F32)
    for _ in range(n):
        m = jnp.max(work, axis=0, keepdims=True)
        idx = jnp.min(jnp.where(work == m, rows, float(work.shape[0])), axis=0, keepdims=True)
        pick = rows == idx
        chosen = jnp.where(pick, 1.0, chosen)
        work = jnp.where(pick, -jnp.inf, work)
    return chosen


def _proj_kernel(*refs, mode, precision, scaled):
    refs = list(refs)
    x_ref, w_ref = refs[:2]
    del refs[:2]
    scale_ref = refs.pop(0) if scaled else None
    if mode == "none":
        (o_ref,) = refs
    elif mode == "rope":
        cos_ref, sin_ref, r_ref = refs
    else:
        cos_ref, sin_ref, o_ref, r_ref = refs
    x = x_ref[...]
    w = w_ref[...]
    if precision == "split3":
        acc = _dot3(x, w)
    else:
        if x.dtype != w.dtype:
            x = x.astype(w.dtype)
        acc = jnp.dot(x, w, preferred_element_type=F32, precision=precision)
    if scaled:
        acc = acc * scale_ref[...]
    if mode in ("none", "both"):
        o_ref[...] = acc.astype(o_ref.dtype)
    if mode in ("rope", "both"):
        tn = acc.shape[1]
        rep = tn // LANES
        cosf = jnp.concatenate([cos_ref[...]] * rep, axis=1)
        sinf = jnp.concatenate([sin_ref[...]] * rep, axis=1)
        lane = lax.broadcasted_iota(jnp.int32, acc.shape, 1)
        first_half = (lane & (HEAD_DIM // 2)) == 0
        swapped = jnp.where(first_half, pltpu.roll(acc, tn - HEAD_DIM // 2, 1),
                            pltpu.roll(acc, HEAD_DIM // 2, 1))
        r_ref[...] = (acc * cosf + swapped * sinf).astype(r_ref.dtype)


def _proj(x, w, *, mode="none", cos=None, sin=None, out_dtype=BF16, tn=512, precision=None, col_scale=None):
    T, K = x.shape
    N = w.shape[1]
    tm = min(ROW_TILE, T)
    tn = min(tn, N)
    assert T % tm == 0 and N % tn == 0
    grid = (T // tm, N // tn)
    in_specs = [pl.BlockSpec((tm, K), lambda i, j: (i, 0)),
                pl.BlockSpec((K, tn), lambda i, j: (0, j))]
    args = [x, w]
    if col_scale is not None:
        in_specs.append(pl.BlockSpec((1, tn), lambda i, j: (0, j)))
        args.append(col_scale.reshape(1, N).astype(F32))
    if mode != "none":
        in_specs += [pl.BlockSpec((tm, LANES), lambda i, j: (i, 0))] * 2
        args += [cos, sin]
    o_spec = pl.BlockSpec((tm, tn), lambda i, j: (i, j))
    o_shape = jax.ShapeDtypeStruct((T, N), out_dtype)
    if mode == "both":
        out_specs, out_shape = [o_spec, o_spec], [o_shape, o_shape]
    else:
        out_specs, out_shape = o_spec, o_shape
    return pl.pallas_call(
        functools.partial(_proj_kernel, mode=mode, precision=precision, scaled=col_scale is not None),
        grid=grid, in_specs=in_specs, out_specs=out_specs, out_shape=out_shape,
        compiler_params=_cparams(("parallel", "parallel")), name=f"proj_{mode}")(*args)


def _proj_t_kernel(wt_ref, x_ref, o_ref, *, precision):
    wt = wt_ref[...]
    x = x_ref[...]
    if x.dtype != wt.dtype:
        x = x.astype(wt.dtype)
    o_ref[...] = _nt_dot(wt, x, precision).astype(o_ref.dtype)


def _proj_t(wt, x, *, out_dtype=BF16, precision=None):
    N, K = wt.shape
    T = x.shape[0]
    tm = min(ROW_TILE, T)
    tn = min(512, N)
    assert T % tm == 0 and N % tn == 0
    return pl.pallas_call(
        functools.partial(_proj_t_kernel, precision=precision),
        grid=(T // tm, N // tn),
        in_specs=[pl.BlockSpec((tn, K), lambda i, j: (j, 0)), pl.BlockSpec((tm, K), lambda i, j: (i, 0))],
        out_specs=pl.BlockSpec((tn, tm), lambda i, j: (j, i)),
        out_shape=jax.ShapeDtypeStruct((N, T), out_dtype),
        compiler_params=_cparams(("parallel", "parallel")), name="proj_t")(wt, x)


def _pack_rows(v):
    n = v.shape[1] // 2
    lo = pltpu.bitcast(v[:, :n].astype(BF16).astype(F32), jnp.int32)
    hi = pltpu.bitcast(v[:, n:].astype(BF16).astype(F32), jnp.int32)
    return (hi & jnp.int32(-65536)) | lax.shift_right_logical(lo, jnp.int32(16))


def _unpack_rows(w):
    lo = pltpu.bitcast(lax.shift_left(w, jnp.int32(16)), F32)
    hi = pltpu.bitcast(w & jnp.int32(-65536), F32)
    return jnp.concatenate([lo, hi], axis=1)


def _store_pieces(ref, v):
    w = _pack_rows(v)
    for j in range(ref.shape[0]):
        ref[j] = w[:, j * SC_ROW_WORDS:(j + 1) * SC_ROW_WORDS]


def _load_pieces(ref):
    return _unpack_rows(jnp.concatenate([ref[j] for j in range(ref.shape[0])], axis=1))


def _piece_spec(d, rows, index_map):
    return pl.BlockSpec((d // 2 // SC_ROW_WORDS, rows, SC_ROW_WORDS), index_map)


def _piece_shape(d, rows):
    return jax.ShapeDtypeStruct((d // 2 // SC_ROW_WORDS, rows, SC_ROW_WORDS), jnp.int32)


def _pad_cols(w, n):
    return jnp.pad(w, ((0, 0), (0, n - w.shape[1])))


def _layer_norm(z, g, b):
    mu = jnp.mean(z, axis=-1, keepdims=True)
    zc = z - mu
    var = jnp.mean(zc * zc, axis=-1, keepdims=True)
    return zc * lax.rsqrt(var + LN_EPS) * g + b


def _outproj_ln_kernel(*refs, n_o):
    o_refs = refs[:n_o]
    w_ref, x_ref, g_ref, b_ref, y_ref, yp_ref = refs[n_o:]
    if n_o == 1:
        o = o_refs[0][...]
    else:
        o = o_refs[0][...].astype(F32)
        for r in o_refs[1:]:
            o = o + r[...].astype(F32)
        o = o.astype(BF16)
    mix = jnp.dot(o, w_ref[...], preferred_element_type=F32)
    z = DN_ALPHA * x_ref[...] + mix
    y = _layer_norm(z, g_ref[...], b_ref[...])
    y_ref[...] = y
    _store_pieces(yp_ref, y)


def _outproj_ln(o_list, w, x, g, b):
    T, D = x.shape
    tm = min(ROW_TILE, T)
    row = pl.BlockSpec((tm, D), lambda i: (i, 0))
    vec = pl.BlockSpec((1, D), lambda i: (0, 0))
    return pl.pallas_call(
        functools.partial(_outproj_ln_kernel, n_o=len(o_list)),
        grid=(T // tm,),
        in_specs=[row] * len(o_list) + [pl.BlockSpec((D, D), lambda i: (0, 0)), row, vec, vec],
        out_specs=[row, _piece_spec(D, tm, lambda i: (0, i, 0))],
        out_shape=[jax.ShapeDtypeStruct((T, D), F32), _piece_shape(D, T)],
        compiler_params=_cparams(("parallel",)), name="outproj_ln")(
            *o_list, w, x, g.reshape(1, D), b.reshape(1, D))


def _flash_init(m_ref, acc_ref):
    m_ref[...] = jnp.full(m_ref.shape, NEG, F32)
    acc_ref[...] = jnp.zeros(acc_ref.shape, F32)


def _flash_step(sT, lhs, m_ref, acc_ref, h, cols=slice(None)):
    m_prev = m_ref[h, :, cols]
    m_new = jnp.maximum(m_prev, jnp.max(sT, axis=0, keepdims=True))
    p = jnp.exp2(sT - m_new)
    alpha = jnp.exp2(m_prev - m_new)
    acc_ref[h, :, cols] = alpha * acc_ref[h, :, cols] + jnp.dot(lhs, p.astype(BF16), preferred_element_type=F32)
    m_ref[h, :, cols] = m_new


def _causal_t(tk, tq, shift=0, strict_lower=False):
    key = lax.broadcasted_iota(jnp.int32, (tk, tq), 0)
    qry = lax.broadcasted_iota(jnp.int32, (tk, tq), 1)
    return (key > qry) if strict_lower else (key <= qry + shift)


def _causal_sweep(i, tq, tk, step):
    r = tk // tq
    assert tk == r * tq
    n_full = i >> _log2(r)

    def body(j, carry):
        step(pl.multiple_of(j * tk, tk), tk, None)
        return carry

    lax.fori_loop(0, n_full, body, 0)
    if r == 1:
        step(pl.multiple_of(i * tq, tq), tq, _causal_t(tq, tq))
    else:
        for rem in range(r):
            @pl.when(i - n_full * r == rem)
            def _(rem=rem):
                size = (rem + 1) * tq
                step(pl.multiple_of(n_full * tk, tk), size, _causal_t(size, tq, shift=rem * tq))


def _staggered(units, scores, update):
    s = [scores(units[0])]
    for n, u in enumerate(units):
        if n + 1 < len(units):
            s.append(scores(units[n + 1]))
        update(u, s[n])


def _head_units(n_heads, tq):
    w = min(ATTN_TILE, tq)
    return [(h, slice(c * w, (c + 1) * w)) for h in range(n_heads) for c in range(tq // w)]


def _pair_units(tq):
    return _head_units(2, tq)


def _head_of_pair(q_pair, which):
    lane = lax.broadcasted_iota(jnp.int32, q_pair.shape, 1)
    keep = (lane < HEAD_DIM) if which == 0 else (lane >= HEAD_DIM)
    return jnp.where(keep, q_pair, jnp.zeros_like(q_pair))


def _pair_rows(a_top, a_bot):
    row = lax.broadcasted_iota(jnp.int32, a_top.shape, 0)
    return jnp.where(row < HEAD_DIM, a_top, a_bot)


def _pair_lhs(vt, which):
    row = lax.broadcasted_iota(jnp.int32, vt.shape, 0)
    keep = (row < HEAD_DIM) if which == 0 else (row >= HEAD_DIM)
    return jnp.where(keep, vt, jnp.ones_like(vt))


def _pair_finish(acc_ref):
    a0, a1 = acc_ref[0], acc_ref[1]
    return _pair_rows(a0 / a0[HEAD_DIM:HEAD_DIM + 1, :], a1 / a1[0:1, :])


def _fox_gate_kernel(fl_ref, bf_ref, tri_ref, ex_ref, cp_ref, carry_ref):
    @pl.when(pl.program_id(1) == 0)
    def _():
        carry_ref[...] = jnp.zeros(carry_ref.shape, F32)

    z = fl_ref[...] + bf_ref[...]
    log_f = jnp.minimum(z, 0.0) - jnp.log1p(jnp.exp(-jnp.abs(z)))
    c = jnp.dot(tri_ref[...], log_f, preferred_element_type=F32, precision=HIGHEST) + carry_ref[...]
    carry_ref[...] = c[-1:, :]
    out = jnp.zeros(c.shape, F32)
    rem = c * LOG2E
    for piece in range(C_SPLIT):
        part = rem.astype(BF16)
        rem = rem - part.astype(F32)
        out = out + jnp.dot(part, ex_ref[piece], preferred_element_type=F32)
    cp_ref[...] = out.astype(BF16)


def _fox_gate(fl, b_f, B, S):
    T = fl.shape[0]
    tm = min(ROW_TILE, S)
    ns = S // tm
    tri = jnp.asarray(np.tril(np.ones((tm, tm), np.float32)))
    ex = np.zeros((C_SPLIT, LANES, LANES), np.float32)
    for h in range(N_HEADS):
        for j in range(C_SPLIT):
            ex[j, h, C_SPLIT * h + j] = 1.0
    bf = _pad_cols(b_f.reshape(1, -1), LANES)
    return pl.pallas_call(
        _fox_gate_kernel, grid=(B, ns),
        in_specs=[pl.BlockSpec((tm, LANES), lambda b, s: (b * ns + s, 0)),
                  pl.BlockSpec((1, LANES), lambda b, s: (0, 0)),
                  pl.BlockSpec((tm, tm), lambda b, s: (0, 0)),
                  pl.BlockSpec((C_SPLIT, LANES, LANES), lambda b, s: (0, 0, 0))],
        out_specs=pl.BlockSpec((tm, LANES), lambda b, s: (b * ns + s, 0)),
        out_shape=jax.ShapeDtypeStruct((T, LANES), BF16),
        scratch_shapes=[pltpu.VMEM((1, LANES), F32)],
        compiler_params=_cparams(("parallel", "arbitrary")), name="fox_gate")(
            fl, bf, tri, jnp.asarray(ex, BF16))


def _fox_kernel(q_ref, k_ref, cp_ref, vt_ref, o_ref, m_ref, acc_ref, *, tq, tk):
    hp = pl.program_id(1)
    i = pl.program_id(2)
    _flash_init(m_ref, acc_ref)
    q = q_ref[...]
    lane = lax.broadcasted_iota(jnp.int32, q.shape, 1)
    qa = []
    for h in range(2):
        lo = C_SPLIT * (2 * hp + h)
        minus_one = jnp.where((lane >= lo) & (lane < lo + C_SPLIT), -1.0, 0.0).astype(q.dtype)
        qa.append(jnp.concatenate([_head_of_pair(q, h), minus_one], axis=1))

    def step(off, size, mask):
        k = jnp.concatenate([k_ref[pl.ds(off, size), :], cp_ref[pl.ds(off, size), :]], axis=1)
        vt = vt_ref[:, pl.ds(off, size)]

        lhs = [_pair_lhs(vt, h) for h in range(2)]

        def scores(u):
            sT = _nt_dot(k, qa[u[0]][u[1], :])
            return sT if mask is None else jnp.where(mask[:, u[1]], sT, NEG)

        _staggered(_pair_units(tq), scores,
                   lambda u, sT: _flash_step(sT, lhs[u[0]], m_ref, acc_ref, u[0], u[1]))

    _causal_sweep(i, tq, tk, step)
    o_ref[...] = _pair_finish(acc_ref).T.astype(o_ref.dtype)


def _fox_attention(qk, cp, vt, B, S):
    T = qk.shape[0]
    tq = min(PAIR_TILE, S)
    tk = min(KV_TILE, S)
    nq = S // tq
    npair = N_HEADS // 2
    ncol = D_MODEL // LANES
    return pl.pallas_call(
        functools.partial(_fox_kernel, tq=tq, tk=tk),
        grid=(B, npair, nq),
        in_specs=[pl.BlockSpec((tq, LANES), lambda b, hp, i: (b * nq + i, hp)),
                  pl.BlockSpec((S, LANES), lambda b, hp, i: (b, ncol + hp)),
                  pl.BlockSpec((S, LANES), lambda b, hp, i: (b, 0)),
                  pl.BlockSpec((LANES, S), lambda b, hp, i: (hp, b))],
        out_specs=pl.BlockSpec((tq, LANES), lambda b, hp, i: (b * nq + i, hp)),
        out_shape=jax.ShapeDtypeStruct((T, D_MODEL), BF16),
        scratch_shapes=[pltpu.VMEM((2, 1, tq), F32), pltpu.VMEM((2, LANES, tq), F32)],
        compiler_params=_cparams(("parallel", "parallel", "arbitrary")), name="fox_attn")(
            qk, qk, cp, vt)


def _gate_expand(gl_ref, bg_ref, ex_ref):
    sig = jax.nn.sigmoid(gl_ref[...] + bg_ref[...])
    hi = sig.astype(BF16)
    lo = (sig - hi.astype(F32)).astype(BF16)
    ex = ex_ref[...]
    return jnp.dot(hi, ex, preferred_element_type=F32) + jnp.dot(lo, ex, preferred_element_type=F32)


def _nsa_compress_kernel(f_ref, pe_ref, w1_ref, w2_ref, o_ref):
    blk = (f_ref[...].astype(F32) + pe_ref[...]).astype(BF16)
    h = jax.nn.gelu(jnp.dot(blk, w1_ref[...], preferred_element_type=F32))
    o_ref[...] = jnp.dot(h.astype(BF16), w2_ref[...], preferred_element_type=F32)


def _nsa_compress(flat, pe_flat, w1, w2p):
    M, K = flat.shape
    tm = min(ROW_TILE, M)
    Hc = w1.shape[1]
    return pl.pallas_call(
        _nsa_compress_kernel, grid=(M // tm,),
        in_specs=[pl.BlockSpec((tm, K), lambda i: (i, 0)), pl.BlockSpec((1, K), lambda i: (0, 0)),
                  pl.BlockSpec((K, Hc), lambda i: (0, 0)), pl.BlockSpec((Hc, LANES), lambda i: (0, 0))],
        out_specs=pl.BlockSpec((tm, LANES), lambda i: (i, 0)),
        out_shape=jax.ShapeDtypeStruct((M, LANES), F32),
        compiler_params=_cparams(("parallel",)), name="nsa_compress")(flat, pe_flat, w1, w2p)


def _nsa_cmp_kernel(q_ref, kk_ref, vt_ref, ovt_ref, gl_ref, bg_ref, ex_ref, o_ref, sn_ref, *, tq, ncp, nselp):
    i = pl.program_id(2)
    q = q_ref[...]
    kk = kk_ref[...]
    vt = vt_ref[...]
    t = i * tq + lax.broadcasted_iota(jnp.int32, (ncp, tq), 1)
    n = lax.broadcasted_iota(jnp.int32, (ncp, tq), 0)
    valid = n * NSA_CMP_STRIDE + (NSA_CMP_LEN - 1) <= t
    psum = jnp.zeros((ncp, tq), F32)
    outs = []
    for hg in range(NSA_HEADS_PER_GROUP):
        qh = _head_of_pair(q[:, LANES * (hg // 2):LANES * (hg // 2 + 1)], hg % 2)
        sT = jnp.where(valid, _nt_dot(kk, qh), NEG)
        m = jnp.max(sT, axis=0, keepdims=True)
        e = jnp.where(valid, jnp.exp2(sT - m), 0.0)
        p = e / jnp.maximum(jnp.sum(e, axis=0, keepdims=True), 1e-30)
        psum = psum + p
        outs.append(jnp.dot(vt, p.astype(BF16), preferred_element_type=F32))
    o = jnp.concatenate(outs, axis=0).T
    o_ref[...] = (o * _gate_expand(gl_ref, bg_ref, ex_ref)).astype(o_ref.dtype)
    hi = psum.astype(BF16)
    lo = (psum - hi.astype(F32)).astype(BF16)
    ovt = ovt_ref[...]
    imp = jnp.dot(ovt, hi, preferred_element_type=F32) + jnp.dot(ovt, lo, preferred_element_type=F32)
    jblk = lax.broadcasted_iota(jnp.int32, (nselp, tq), 0)
    cur = (i * tq + lax.broadcasted_iota(jnp.int32, (nselp, tq), 1)) >> _log2(NSA_SEL_LEN)
    forced = (jblk == 0) | (jblk == cur) | (jblk == cur - 1)
    pri = jnp.where(forced, imp + NSA_FORCE_BONUS, imp)
    past = jblk <= cur
    pri = jnp.where(past, pri, NEG)
    chosen = _topk_rows(pri, NSA_SEL_TOPN)
    sn = jnp.where((chosen > 0.0) & past, 0.0, NEG)
    sn_ref[...] = sn.T.astype(sn_ref.dtype)


def _nsa_cmp_branch(q_c, kk_c, vt_c, overlap_t, gl, bg, ex, B, S):
    T = q_c.shape[0]
    G = NSA_KV_GROUPS
    tq = min(ATTN_TILE, S)
    nq = S // tq
    ncp = kk_c.shape[2]
    nselp = overlap_t.shape[0]
    return pl.pallas_call(
        functools.partial(_nsa_cmp_kernel, tq=tq, ncp=ncp, nselp=nselp),
        grid=(B, G, nq),
        in_specs=[pl.BlockSpec((tq, 2 * LANES), lambda b, g, i: (b * nq + i, g)),
                  pl.BlockSpec((None, None, ncp, LANES), lambda b, g, i: (b, g, 0, 0)),
                  pl.BlockSpec((None, None, HEAD_DIM, ncp), lambda b, g, i: (b, g, 0, 0)),
                  pl.BlockSpec((nselp, ncp), lambda b, g, i: (0, 0)),
                  pl.BlockSpec((tq, LANES), lambda b, g, i: (b * nq + i, 0)),
                  pl.BlockSpec((1, LANES), lambda b, g, i: (0, 0)),
                  pl.BlockSpec((None, LANES, 2 * LANES), lambda b, g, i: (g, 0, 0))],
        out_specs=[pl.BlockSpec((tq, 2 * LANES), lambda b, g, i: (b * nq + i, g)),
                   pl.BlockSpec((None, None, tq, nselp), lambda b, g, i: (b, g, i, 0))],
        out_shape=[jax.ShapeDtypeStruct((T, D_MODEL), BF16),
                   jax.ShapeDtypeStruct((B, G, S, nselp), BF16)],
        compiler_params=_cparams(("parallel", "parallel", "parallel")), name="nsa_cmp")(
            q_c, kk_c, vt_c, overlap_t, gl, bg, ex)


def _nsa_kv_kernel(*refs, mode, tq, tk):
    if mode == "sel":
        q_ref, kk_ref, vt_ref, oh_ref, sn_ref, gl_ref, bg_ref, ex_ref, o_ref, m_ref, acc_ref = refs
    else:
        q_ref, kk_ref, vt_ref, gl_ref, bg_ref, ex_ref, o_ref, m_ref, acc_ref = refs
    i = pl.program_id(2)
    _flash_init(m_ref, acc_ref)
    q = q_ref[...]
    qh = []
    for hg in range(NSA_HEADS_PER_GROUP):
        qq = _head_of_pair(q[:, LANES * (hg // 2):LANES * (hg // 2 + 1)], hg % 2)
        if mode == "sel":
            qq = jnp.concatenate([qq, sn_ref[...]], axis=1)
        qh.append(qq)

    def step(off, size, mask):
        k = kk_ref[pl.ds(off, size), :]
        vt = vt_ref[:, pl.ds(off, size)]
        lhs = jnp.concatenate([vt, jnp.ones((ONES_ROWS, size), vt.dtype)], axis=0)
        if mode == "sel":
            k = jnp.concatenate([k, oh_ref[pl.ds(off, size), :]], axis=1)

        def scores(u):
            sT = _nt_dot(k, qh[u[0]][u[1], :])
            return sT if mask is None else jnp.where(mask[:, u[1]], sT, NEG)

        _staggered(_head_units(NSA_HEADS_PER_GROUP, tq), scores,
                   lambda u, sT: _flash_step(sT, lhs, m_ref, acc_ref, u[0], u[1]))

    if mode == "sel":
        _causal_sweep(i, tq, tk, step)
    else:
        @pl.when(i > 0)
        def _():
            step(pl.multiple_of((i - 1) * tq, tq), tq, _causal_t(tq, tq, strict_lower=True))
        step(pl.multiple_of(i * tq, tq), tq, _causal_t(tq, tq))
    outs = []
    for hg in range(NSA_HEADS_PER_GROUP):
        a = acc_ref[hg]
        outs.append(a[:HEAD_DIM] / a[HEAD_DIM:HEAD_DIM + 1, :])
    o_ref[...] = (jnp.concatenate(outs, axis=0).T * _gate_expand(gl_ref, bg_ref, ex_ref)).astype(o_ref.dtype)


def _nsa_kv_branch(mode, q_r, kk, vt, gl, bg, ex, B, S, onehot=None, selneg=None):
    T = q_r.shape[0]
    G = NSA_KV_GROUPS
    tq = min(PAIR_TILE if mode == "sel" else ATTN_TILE, S)
    tk = min(KV_TILE, S)
    assert NSA_WINDOW == tq or mode == "sel"
    nq = S // tq
    in_specs = [pl.BlockSpec((tq, 2 * LANES), lambda b, g, i: (b * nq + i, g)),
                pl.BlockSpec((None, None, S, LANES), lambda b, g, i: (b, g, 0, 0)),
                pl.BlockSpec((HEAD_DIM, S), lambda b, g, i: (g, b))]
    args = [q_r, kk, vt]
    if mode == "sel":
        in_specs += [pl.BlockSpec((S, LANES), lambda b, g, i: (0, 0)),
                     pl.BlockSpec((None, None, tq, LANES), lambda b, g, i: (b, g, i, 0))]
        args += [onehot, selneg]
    in_specs += [pl.BlockSpec((tq, LANES), lambda b, g, i: (b * nq + i, 0)),
                 pl.BlockSpec((1, LANES), lambda b, g, i: (0, 0)),
                 pl.BlockSpec((None, LANES, 2 * LANES), lambda b, g, i: (g, 0, 0))]
    args += [gl, bg, ex]
    nh = NSA_HEADS_PER_GROUP
    return pl.pallas_call(
        functools.partial(_nsa_kv_kernel, mode=mode, tq=tq, tk=tk),
        grid=(B, G, nq), in_specs=in_specs,
        out_specs=pl.BlockSpec((tq, 2 * LANES), lambda b, g, i: (b * nq + i, g)),
        out_shape=jax.ShapeDtypeStruct((T, D_MODEL), BF16),
        scratch_shapes=[pltpu.VMEM((nh, 1, tq), F32), pltpu.VMEM((nh, HEAD_DIM + ONES_ROWS, tq), F32)],
        compiler_params=_cparams(("parallel", "parallel", "arbitrary")), name=f"nsa_{mode}")(*args)


def _kmean_kernel(k_ref, o_ref, *, nblk):
    k = k_ref[...].astype(F32)
    o_ref[...] = jnp.mean(k.reshape(nblk, MOBA_BLOCK, k.shape[1]), axis=1)


def _moba_kmean(k_rot):
    T, D = k_rot.shape
    nblk = 8
    rows = nblk * MOBA_BLOCK
    assert T % rows == 0
    return pl.pallas_call(
        functools.partial(_kmean_kernel, nblk=nblk), grid=(T // rows,),
        in_specs=[pl.BlockSpec((rows, D), lambda i: (i, 0))],
        out_specs=pl.BlockSpec((nblk, D), lambda i: (i, 0)),
        out_shape=jax.ShapeDtypeStruct((T // MOBA_BLOCK, D), F32),
        compiler_params=_cparams(("parallel",)), name="moba_kmean")(k_rot)


def _moba_select_kernel(q_ref, r_ref, sn_ref, *, tq, nblk):
    i = pl.program_id(2)
    gsT = _nt_dot(r_ref[...], q_ref[...].astype(F32), HIGHEST)
    jblk = lax.broadcasted_iota(jnp.int32, (nblk, tq), 0)
    cb = (i * tq + lax.broadcasted_iota(jnp.int32, (nblk, tq), 1)) >> _log2(MOBA_BLOCK)
    past = jblk < cb
    parts = []
    for h in range(2):
        gs = jnp.where(past, gsT[h * nblk:(h + 1) * nblk, :], NEG)
        chosen = _topk_rows(gs, min(MOBA_TOPK, nblk))
        parts.append(jnp.where(((chosen > 0.0) & past) | (jblk == cb), 0.0, NEG))
    parts.append(jnp.zeros((LANES - 2 * nblk, tq), F32))
    sn_ref[...] = jnp.concatenate(parts, axis=0).T.astype(sn_ref.dtype)


def _moba_select(q_rot, rmat_t, B, S):
    tq = min(ATTN_TILE, S)
    nq = S // tq
    npair = N_HEADS // 2
    nblk = S // MOBA_BLOCK
    return pl.pallas_call(
        functools.partial(_moba_select_kernel, tq=tq, nblk=nblk), grid=(B, npair, nq),
        in_specs=[pl.BlockSpec((tq, LANES), lambda b, hp, i: (b * nq + i, hp)),
                  pl.BlockSpec((None, None, LANES, LANES), lambda b, hp, i: (b, hp, 0, 0))],
        out_specs=pl.BlockSpec((None, None, tq, LANES), lambda b, hp, i: (b, hp, i, 0)),
        out_shape=jax.ShapeDtypeStruct((B, npair, S, LANES), BF16),
        compiler_params=_cparams(("parallel", "parallel", "parallel")), name="moba_select")(q_rot, rmat_t)


def _moba_kernel(q_ref, k_ref, vt_ref, oh_ref, sn_ref, o_ref, m_ref, acc_ref, *, tq, tk, nblk):
    i = pl.program_id(2)
    _flash_init(m_ref, acc_ref)
    q = q_ref[...]
    sn = sn_ref[...]
    lane = lax.broadcasted_iota(jnp.int32, sn.shape, 1)
    qa = []
    for h in range(2):
        mine = (lane >= h * nblk) & (lane < (h + 1) * nblk)
        qa.append(jnp.concatenate([_head_of_pair(q, h), jnp.where(mine, sn, jnp.zeros_like(sn))], axis=1))

    def step(off, size, mask):
        k = jnp.concatenate([k_ref[pl.ds(off, size), :], oh_ref[pl.ds(off, size), :]], axis=1)
        vt = vt_ref[:, pl.ds(off, size)]

        lhs = [_pair_lhs(vt, h) for h in range(2)]

        def scores(u):
            sT = _nt_dot(k, qa[u[0]][u[1], :])
            return sT if mask is None else jnp.where(mask[:, u[1]], sT, NEG)

        _staggered(_pair_units(tq), scores,
                   lambda u, sT: _flash_step(sT, lhs[u[0]], m_ref, acc_ref, u[0], u[1]))

    _causal_sweep(i, tq, tk, step)
    o_ref[...] = _pair_finish(acc_ref).T.astype(o_ref.dtype)


def _moba_attention(qk_rot, vt, onehot2, selneg, B, S):
    T = qk_rot.shape[0]
    tq = min(PAIR_TILE, S)
    tk = min(KV_TILE, S)
    nq = S // tq
    npair = N_HEADS // 2
    ncol = D_MODEL // LANES
    nblk = S // MOBA_BLOCK
    return pl.pallas_call(
        functools.partial(_moba_kernel, tq=tq, tk=tk, nblk=nblk),
        grid=(B, npair, nq),
        in_specs=[pl.BlockSpec((tq, LANES), lambda b, hp, i: (b * nq + i, hp)),
                  pl.BlockSpec((S, LANES), lambda b, hp, i: (b, ncol + hp)),
                  pl.BlockSpec((LANES, S), lambda b, hp, i: (hp, b)),
                  pl.BlockSpec((S, LANES), lambda b, hp, i: (0, 0)),
                  pl.BlockSpec((None, None, tq, LANES), lambda b, hp, i: (b, hp, i, 0))],
        out_specs=pl.BlockSpec((tq, LANES), lambda b, hp, i: (b * nq + i, hp)),
        out_shape=jax.ShapeDtypeStruct((T, D_MODEL), BF16),
        scratch_shapes=[pltpu.VMEM((2, 1, tq), F32), pltpu.VMEM((2, LANES, tq), F32)],
        compiler_params=_cparams(("parallel", "parallel", "arbitrary")), name="moba_attn")(
            qk_rot, qk_rot, vt, onehot2, selneg)


def _router_kernel(wt_ref, x_ref, rb_ref, e_ref, g_ref):
    tm = x_ref.shape[0]
    gsz = N_EXPERTS // N_GROUPS
    scores = jax.nn.sigmoid(_dot3(wt_ref[...], x_ref[...], nt=True))
    biased = scores + rb_ref[...]
    member = lax.broadcasted_iota(jnp.int32, (gsz, tm), 0).astype(F32)
    gscore = []
    for g in range(N_GROUPS):
        v = biased[g * gsz:(g + 1) * gsz, :]
        m1 = jnp.max(v, axis=0, keepdims=True)
        i1 = jnp.min(jnp.where(v == m1, member, float(gsz)), axis=0, keepdims=True)
        m2 = jnp.max(jnp.where(member == i1, -jnp.inf, v), axis=0, keepdims=True)
        gscore.append(m1 + m2)
    gsel = _topk_rows(jnp.concatenate(gscore, axis=0), TOPK_GROUPS)
    emask = jnp.concatenate([jnp.broadcast_to(gsel[g:g + 1, :], (gsz, tm)) for g in range(N_GROUPS)], axis=0)
    work = jnp.where(emask > 0.0, biased, NEG)
    erow = lax.broadcasted_iota(jnp.int32, (N_EXPERTS, tm), 0).astype(F32)
    idxs, vals = [], []
    for _ in range(TOP_K):
        m = jnp.max(work, axis=0, keepdims=True)
        idx = jnp.min(jnp.where(work == m, erow, float(N_EXPERTS)), axis=0, keepdims=True)
        pick = erow == idx
        idxs.append(idx)
        vals.append(jnp.sum(jnp.where(pick, scores, 0.0), axis=0, keepdims=True))
        work = jnp.where(pick, -jnp.inf, work)
    gw = jnp.concatenate(vals, axis=0)
    e_ref[...] = jnp.concatenate(idxs, axis=0).astype(jnp.int32)
    g_ref[...] = gw / jnp.sum(gw, axis=0, keepdims=True) * ROUTED_SCALE


def _router(x1, router_w, router_b):
    T, D = x1.shape
    tm = min(ROW_TILE, T)
    return pl.pallas_call(
        _router_kernel, grid=(T // tm,),
        in_specs=[pl.BlockSpec((N_EXPERTS, D), lambda i: (0, 0)), pl.BlockSpec((tm, D), lambda i: (i, 0)),
                  pl.BlockSpec((N_EXPERTS, 1), lambda i: (0, 0))],
        out_specs=[pl.BlockSpec((TOP_K, tm), lambda i: (0, i))] * 2,
        out_shape=[jax.ShapeDtypeStruct((TOP_K, T), jnp.int32), jax.ShapeDtypeStruct((TOP_K, T), F32)],
        compiler_params=_cparams(("parallel",)), name="moe_router")(
            router_w.T, x1, router_b.reshape(N_EXPERTS, 1))


def _rank_kernel(e_ref, tri_ref, rank_ref, cnt_ref, carry_ref):
    @pl.when(pl.program_id(0) == 0)
    def _():
        carry_ref[...] = jnp.zeros(carry_ref.shape, F32)

    tm = e_ref.shape[1]
    erow = lax.broadcasted_iota(jnp.int32, (N_EXPERTS, tm), 0)
    tri = tri_ref[...]
    base = carry_ref[...]
    ranks = []
    for k in range(TOP_K):
        oh = erow == e_ref[k:k + 1, :]
        ohb = jnp.where(oh, 1.0, 0.0).astype(BF16)
        incl = jnp.dot(ohb, tri, preferred_element_type=F32)
        ranks.append(jnp.sum(jnp.where(oh, base + incl - 1.0, 0.0), axis=0, keepdims=True))
        base = base + incl[:, tm - 1:tm]
    carry_ref[...] = base
    rank_ref[...] = jnp.concatenate(ranks, axis=0).astype(jnp.int32)
    cnt_ref[...] = jnp.broadcast_to(base, cnt_ref.shape)


def _expert_ranks(eidx_t):
    K, T = eidx_t.shape
    tm = min(ROW_TILE, T)
    tri = jnp.asarray(np.triu(np.ones((tm, tm), np.float32)), BF16)
    return pl.pallas_call(
        _rank_kernel, grid=(T // tm,),
        in_specs=[pl.BlockSpec((K, tm), lambda i: (0, i)), pl.BlockSpec((tm, tm), lambda i: (0, 0))],
        out_specs=[pl.BlockSpec((K, tm), lambda i: (0, i)), pl.BlockSpec((N_EXPERTS, LANES), lambda i: (0, 0))],
        out_shape=[jax.ShapeDtypeStruct((K, T), jnp.int32), jax.ShapeDtypeStruct((N_EXPERTS, LANES), F32)],
        scratch_shapes=[pltpu.VMEM((N_EXPERTS, 1), F32)],
        compiler_params=_cparams(("arbitrary",)), name="moe_rank")(eidx_t, tri)


def _dest_kernel(e_ref, rank_ref, ps_ref, d_ref):
    tm = e_ref.shape[1]
    erow = lax.broadcasted_iota(jnp.int32, (N_EXPERTS, tm), 0)
    ps = ps_ref[...]
    rows = []
    for k in range(TOP_K):
        oh = erow == e_ref[k:k + 1, :]
        rows.append(jnp.sum(jnp.where(oh, ps, 0.0), axis=0, keepdims=True))
    d_ref[...] = jnp.concatenate(rows, axis=0).astype(jnp.int32) + rank_ref[...]


def _expert_dest(eidx_t, rank_t, pstarts):
    K, T = eidx_t.shape
    tm = min(ROW_TILE, T)
    blk = pl.BlockSpec((K, tm), lambda i: (0, i))
    return pl.pallas_call(
        _dest_kernel, grid=(T // tm,),
        in_specs=[blk, blk, pl.BlockSpec((N_EXPERTS, 1), lambda i: (0, 0))],
        out_specs=blk, out_shape=jax.ShapeDtypeStruct((K, T), jnp.int32),
        compiler_params=_cparams(("parallel",)), name="moe_dest")(
            eidx_t, rank_t, pstarts.astype(F32).reshape(N_EXPERTS, 1))


def _expert_kernel(be_ref, nb_ref, x_ref, *refs):
    n = EXPERT_SLOTS
    w_refs, (o_ref, wgu_ref, wdb_ref) = refs[:3 * n], refs[3 * n:]
    s = pl.program_id(0)
    E = wdb_ref.shape[1]
    R = MOE_BLOCK

    for j in range(n):
        b = s * n + j

        @pl.when((s == 0) | (be_ref[b] != be_ref[jnp.maximum(b - n, 0)]))
        def _(j=j):
            wg_ref, wu_ref, wd_ref = w_refs[3 * j:3 * j + 3]
            wgu_ref[j, :, :E] = wg_ref[...].astype(BF16)
            wgu_ref[j, :, E:] = wu_ref[...].astype(BF16)
            wdb_ref[j] = wd_ref[...].astype(BF16)

    @pl.when(s * n < nb_ref[0])
    def _():
        x = _load_pieces(x_ref).astype(BF16)
        ys = []

        def gate_up(j):
            return jnp.dot(x[j * R:(j + 1) * R], wgu_ref[j], preferred_element_type=F32)

        def down(j, gu):
            h = jax.nn.silu(gu[:, :E]) * gu[:, E:]
            ys.append(jnp.dot(h.astype(BF16), wdb_ref[j], preferred_element_type=F32))

        _staggered(list(range(n)), gate_up, down)
        _store_pieces(o_ref, jnp.concatenate(ys, axis=0))

    @pl.when(s * n >= nb_ref[0])
    def _():
        o_ref[...] = jnp.zeros(o_ref.shape, o_ref.dtype)


def _expert_ffn(blk_e, nb_used, xs, wg, wu, wd, layer):
    P = xs.shape[1]
    D = wg.shape[2]
    n = EXPERT_SLOTS
    rows = n * MOE_BLOCK
    assert P % rows == 0
    E = EXPERT_DIM
    w_specs = []
    for j in range(n):
        pick = functools.partial(lambda s, be, nb, j: (layer, be[s * n + j], 0, 0), j=j)
        w_specs += [pl.BlockSpec((None, None, D, E), pick), pl.BlockSpec((None, None, D, E), pick),
                    pl.BlockSpec((None, None, E, D), pick)]
    grid_spec = pltpu.PrefetchScalarGridSpec(
        num_scalar_prefetch=2, grid=(P // rows,),
        in_specs=[_piece_spec(D, rows, lambda s, be, nb: (0, s, 0))] + w_specs,
        out_specs=_piece_spec(D, rows, lambda s, be, nb: (0, s, 0)),
        scratch_shapes=[pltpu.VMEM((n, D, 2 * E), BF16), pltpu.VMEM((n, E, D), BF16)])
    return pl.pallas_call(
        _expert_kernel, grid_spec=grid_spec, out_shape=_piece_shape(D, P),
        compiler_params=_cparams(("arbitrary",)), name="moe_experts")(
            blk_e, nb_used, xs, *([wg, wu, wd] * n))


def _post_moe_kernel(x_ref, *refs):
    y_refs = refs[:TOP_K]
    gw_ref, p_ref, sg_ref, su_ref, sd_ref, g_ref, b_ref, wg_ref, wp_ref, o_ref, ob_ref = refs[TOP_K:]
    x = x_ref[...]
    xb = x.astype(BF16)
    h = jax.nn.silu(jnp.dot(xb, sg_ref[...], preferred_element_type=F32)) * jnp.dot(
        xb, su_ref[...], preferred_element_type=F32)
    ffn = jnp.dot(h.astype(BF16), sd_ref[...], preferred_element_type=F32)
    gw = gw_ref[...]
    for k in range(TOP_K):
        ffn = ffn + gw[:, k:k + 1] * _load_pieces(y_refs[k])
    z = DN_ALPHA * x + ffn
    x2 = _layer_norm(z, g_ref[...], b_ref[...])
    gate = jax.nn.sigmoid(jnp.dot(x2.astype(BF16), wg_ref[...], preferred_element_type=F32))
    proj = jnp.dot(p_ref[...].astype(BF16), wp_ref[...], preferred_element_type=F32)
    out = x2 + gate * proj
    o_ref[...] = out
    ob_ref[...] = out.astype(BF16)


def _post_moe(x1, yg, gw, p, sg, su, sd, g, b, wgate, wproj):
    T, D = x1.shape
    tm = min(ROW_TILE // 2, T)
    nt = T // tm
    PD = p.shape[1]
    SD = sg.shape[1]
    row = pl.BlockSpec((tm, D), lambda i: (i, 0))
    vec = pl.BlockSpec((1, D), lambda i: (0, 0))
    full = lambda r, c: pl.BlockSpec((r, c), lambda i: (0, 0))
    return pl.pallas_call(
        _post_moe_kernel, grid=(T // tm,),
        in_specs=[row] + [_piece_spec(D, tm, functools.partial(lambda i, k: (0, k * nt + i, 0), k=k))
                          for k in range(TOP_K)]
                 + [pl.BlockSpec((tm, TOP_K), lambda i: (i, 0)),
                    pl.BlockSpec((tm, PD), lambda i: (i, 0)), full(D, SD), full(D, SD),
                  full(SD, D), vec, vec, full(D, D), full(PD, D)],
        out_specs=[row, row],
        out_shape=[jax.ShapeDtypeStruct((T, D), F32), jax.ShapeDtypeStruct((T, D), BF16)],
        compiler_params=_cparams(("parallel",)), name="post_moe")(
            x1, *([yg] * TOP_K), gw, p, sg, su, sd, g.reshape(1, D), b.reshape(1, D), wgate, wproj)


def _rope_tables(positions):
    inv = 1.0 / (ROPE_THETA ** (jnp.arange(0, HEAD_DIM, 2, dtype=F32) / HEAD_DIM))
    ang = positions.astype(F32).reshape(-1)[:, None] * inv
    cos, sin = jnp.cos(ang), jnp.sin(ang)
    cosf = jnp.concatenate([cos] * (LANES // (HEAD_DIM // 2)), axis=1)
    sinf = jnp.concatenate([-sin, sin] * (LANES // HEAD_DIM), axis=1)
    return cosf, sinf


def _q_col_scale(n_q, n):
    return jnp.concatenate([jnp.full((n_q,), Q_SCALE, F32), jnp.ones((n - n_q,), F32)])


def _fox_mixer(x, xb, B, S, w_in, b_f):
    D = D_MODEL
    wb = w_in[:, :3 * D].astype(BF16)
    qk = _proj(xb, wb[:, :2 * D], col_scale=_q_col_scale(D, 2 * D))
    vt = _proj_t(wb[:, 2 * D:].T, xb)
    fl = _proj(x, _pad_cols(w_in[:, 3 * D:], LANES), out_dtype=F32, tn=LANES, precision="split3")
    cp = _fox_gate(fl, b_f, B, S)
    return [_fox_attention(qk, cp, vt, B, S)]


def _nsa_mixer(x, xb, B, S, cosf, sinf, w_in, b_gate, pe_k, pe_v, ck_w1, ck_w2, cv_w1, cv_w2):
    D, G, HD = D_MODEL, NSA_KV_GROUPS, HEAD_DIM
    HG = NSA_HEADS_PER_GROUP
    kvw = G * HD
    wb = w_in[:, :D + 6 * kvw].astype(BF16)
    q_c, q_r = _proj(xb, wb[:, :D], mode="both", cos=cosf, sin=sinf, col_scale=_q_col_scale(D, D))
    w_rot = jnp.concatenate([wb[:, D + 2 * kvw:D + 3 * kvw], wb[:, D + 4 * kvw:D + 5 * kvw]], axis=1)
    k_rot = _proj(xb, w_rot, mode="rope", cos=cosf, sin=sinf)
    kvc = _proj(xb, wb[:, D:D + 2 * kvw])
    w_v = jnp.concatenate([wb[:, D + 3 * kvw:D + 4 * kvw], wb[:, D + 5 * kvw:D + 6 * kvw]], axis=1)
    vt = _proj_t(w_v.T, xb)
    gl = _proj(x, _pad_cols(w_in[:, D + 6 * kvw:], LANES), out_dtype=F32, tn=LANES, precision="split3")
    bg = _pad_cols(b_gate.reshape(1, -1), LANES)

    def grouped(t2d):
        t = t2d.reshape(B, S, G, HD).transpose(0, 2, 1, 3)
        return jnp.concatenate([t, t], axis=-1)

    n_chunks = S // NSA_CMP_STRIDE
    n_cmp = n_chunks - NSA_CMP_LEN // NSA_CMP_STRIDE + 1
    ncp = n_chunks

    def compress(t2d, pe, w1, w2):
        ch = t2d.reshape(B, n_chunks, NSA_CMP_STRIDE, G, HD).transpose(0, 1, 3, 2, 4)
        ch = ch.reshape(B, n_chunks, G, NSA_CMP_STRIDE * HD)
        flat = jnp.concatenate([ch[:, :n_cmp], ch[:, 1:n_cmp + 1]], axis=-1)
        flat = jnp.pad(flat, ((0, 0), (0, ncp - n_cmp), (0, 0), (0, 0))).reshape(B * ncp * G, -1)
        out = _nsa_compress(flat, pe.reshape(1, -1), w1.astype(BF16), _pad_cols(w2, LANES).astype(BF16))
        return out[:, :HD].reshape(B, ncp, G, HD).astype(BF16)

    kc = compress(kvc[:, :kvw], pe_k, ck_w1, ck_w2).transpose(0, 2, 1, 3)
    kk_c = jnp.concatenate([kc, kc], axis=-1)
    vt_c = compress(kvc[:, kvw:], pe_v, cv_w1, cv_w2).transpose(0, 2, 3, 1)

    n_sel = S // NSA_SEL_LEN
    assert n_sel <= LANES
    cmp_start = np.arange(ncp) * NSA_CMP_STRIDE
    sel_start = np.arange(LANES) * NSA_SEL_LEN
    overlap = ((cmp_start[:, None] < sel_start[None, :] + NSA_SEL_LEN)
               & (cmp_start[:, None] + NSA_CMP_LEN > sel_start[None, :])
               & (np.arange(ncp)[:, None] < n_cmp) & (np.arange(LANES)[None, :] < n_sel))
    overlap_t = jnp.asarray(overlap.T, BF16)
    ex = np.zeros((3, G, LANES, HG * HD), np.float32)
    for br in range(3):
        for g in range(G):
            for hg in range(HG):
                ex[br, g, (g * HG + hg) * 3 + br, hg * HD:(hg + 1) * HD] = 1.0
    ex = jnp.asarray(ex, BF16)

    o_c, selneg = _nsa_cmp_branch(q_c, kk_c, vt_c, overlap_t, gl, bg, ex[0], B, S)
    onehot = jnp.asarray((np.arange(S)[:, None] // NSA_SEL_LEN) == np.arange(LANES)[None, :], BF16)
    kk_s = grouped(k_rot[:, :kvw])
    kk_w = grouped(k_rot[:, kvw:])
    o_s = _nsa_kv_branch("sel", q_r, kk_s, vt[:kvw], gl, bg, ex[1], B, S, onehot=onehot, selneg=selneg)
    o_w = _nsa_kv_branch("win", q_r, kk_w, vt[kvw:], gl, bg, ex[2], B, S)
    return [o_c, o_s, o_w]


def _moba_mixer(xb, B, S, cosf, sinf, w_in):
    D, H, HD = D_MODEL, N_HEADS, HEAD_DIM
    wb = w_in.astype(BF16)
    qk_rot = _proj(xb, wb[:, :2 * D], mode="rope", cos=cosf, sin=sinf, col_scale=_q_col_scale(D, 2 * D))
    vt = _proj_t(wb[:, 2 * D:].T, xb)
    nblk = S // MOBA_BLOCK
    assert S % MOBA_BLOCK == 0 and 2 * nblk <= LANES and nblk % SUBLANES == 0
    kmean = _moba_kmean(qk_rot[:, D:]).reshape(B, nblk, H // 2, 2, HD)
    km = kmean.transpose(0, 2, 3, 1, 4)
    rmat_t = jnp.zeros((B, H // 2, LANES, 2, HD), F32)
    rmat_t = rmat_t.at[:, :, :nblk, 0].set(km[:, :, 0]).at[:, :, nblk:2 * nblk, 1].set(km[:, :, 1])
    selneg = _moba_select(qk_rot, rmat_t.reshape(B, H // 2, LANES, LANES), B, S)
    blk_of = np.arange(S) // MOBA_BLOCK
    oh = np.zeros((S, LANES), np.float32)
    oh[np.arange(S), blk_of] = 1.0
    oh[np.arange(S), nblk + blk_of] = 1.0
    return [_moba_attention(qk_rot, vt, jnp.asarray(oh, BF16), selneg, B, S)]


def _sc_invert_kernel(dest_hbm, tok_hbm, out_hbm, d_v, t_v, buf_v, *, seg, chunk, n_chunks, n_tok):
    lo = pl.multiple_of((lax.axis_index("s") * SC_CORES + lax.axis_index("c")) * seg, SC_LANES)

    @pl.loop(0, seg // SC_LANES)
    def _(j):
        row = lo + j * SC_LANES + lax.iota(jnp.int32, SC_LANES)
        buf_v[pl.ds(j * SC_LANES, SC_LANES)] = lax.rem(row, jnp.int32(n_tok))

    @pl.loop(0, n_chunks)
    def _(c):
        pltpu.sync_copy(dest_hbm.at[pl.ds(c * chunk, chunk)], d_v)
        pltpu.sync_copy(tok_hbm.at[pl.ds(c * chunk, chunk)], t_v)

        @pl.loop(0, chunk // SC_LANES)
        def _(j):
            d = d_v[pl.ds(j * SC_LANES, SC_LANES)] - lo
            mine = (d >= 0) & (d < seg)
            plsc.store_scatter(buf_v, [jnp.where(mine, d, 0)], t_v[pl.ds(j * SC_LANES, SC_LANES)], mask=mine)

    pltpu.sync_copy(buf_v, out_hbm.at[pl.ds(lo, seg)])


def _invert_dest(dest, tok, P, n_tok):
    R = dest.shape[0]
    n_workers = SC_CORES * SC_SUBCORES
    seg = P // n_workers
    chunk = min(SC_INDEX_CHUNK, R)
    assert P == seg * n_workers and seg % SC_LANES == 0 and R % chunk == 0
    mesh = plsc.VectorSubcoreMesh(core_axis_name="c", subcore_axis_name="s",
                                  num_cores=SC_CORES, num_subcores=SC_SUBCORES)
    return pl.kernel(
        functools.partial(_sc_invert_kernel, seg=seg, chunk=chunk, n_chunks=R // chunk, n_tok=n_tok),
        out_type=jax.ShapeDtypeStruct((P,), jnp.int32), mesh=mesh,
        scratch_types=[pltpu.VMEM((chunk,), jnp.int32), pltpu.VMEM((chunk,), jnp.int32),
                       pltpu.VMEM((seg,), jnp.int32)],
        compiler_params=pltpu.CompilerParams(needs_layout_passes=False),
        name="moe_invert")(dest, tok)


def _sc_mesh():
    return plsc.VectorSubcoreMesh(core_axis_name="c", subcore_axis_name="s",
                                  num_cores=SC_CORES, num_subcores=SC_SUBCORES)


def _gather_rows(table, idx):
    parts, V, _ = table.shape
    N = idx.shape[0]
    n_pieces = N * parts
    n_workers = SC_CORES * SC_SUBCORES
    assert n_pieces % (SC_WINDOW * n_workers) == 0
    pieces = (jnp.arange(parts, dtype=jnp.int32)[:, None] * V + idx[None, :]).reshape(1, n_pieces)

    def kernel_body(x_hbm, i_hbm, o_hbm):
        def body(i_vmem, o_vmem):
            pltpu.sync_copy(x_hbm.at[i_vmem.at[0]], o_vmem)

        pltpu.emit_pipeline(
            body, grid=(n_pieces // SC_WINDOW,),
            in_specs=[pl.BlockSpec((1, SC_WINDOW), lambda i: (0, i))],
            out_specs=[pl.BlockSpec((SC_WINDOW, SC_ROW_WORDS), lambda i: (i, 0))],
            core_axis_name=("c", "s"), dimension_semantics=(pltpu.PARALLEL,))(i_hbm, o_hbm)

    out = pl.kernel(kernel_body, out_type=jax.ShapeDtypeStruct((n_pieces, SC_ROW_WORDS), table.dtype),
                    mesh=_sc_mesh(), scratch_types=[], name="moe_gather")(
                        table.reshape(parts * V, SC_ROW_WORDS), pieces)
    return out.reshape(parts, N, SC_ROW_WORDS)


def _moe_dispatch(x1, x1p, router_w, router_b, wg, wu, wd, layer):
    T, D = x1.shape
    eidx_t, gw_t = _router(x1, router_w, router_b)
    rank_t, cnt = _expert_ranks(eidx_t)
    counts = cnt[:, 0].astype(jnp.int32)
    padded = (counts + MOE_BLOCK - 1) // MOE_BLOCK * MOE_BLOCK
    pends = jnp.cumsum(padded)
    pstarts = pends - padded
    R = T * TOP_K
    parts = D // 2 // SC_ROW_WORDS
    unit = math.lcm(EXPERT_SLOTS * MOE_BLOCK, SC_CORES * SC_SUBCORES * SC_LANES,
                    SC_CORES * SC_SUBCORES * SC_WINDOW // math.gcd(parts, SC_WINDOW))
    P = -(-(R + N_EXPERTS * (MOE_BLOCK - 1)) // unit) * unit
    NB = P // MOE_BLOCK
    blk_start = jnp.arange(NB, dtype=jnp.int32) * MOE_BLOCK
    blk_e = jnp.minimum(jnp.sum(pends[None, :] <= blk_start[:, None], axis=1), N_EXPERTS - 1).astype(jnp.int32)
    nb_used = (pends[-1] // MOE_BLOCK).astype(jnp.int32).reshape(1)
    dest_t = _expert_dest(eidx_t, rank_t, pstarts)

    tok = jnp.broadcast_to(jnp.arange(T, dtype=jnp.int32)[None, :], (TOP_K, T))
    buf_tok = _invert_dest(dest_t.reshape(-1), tok.reshape(-1), P, T)
    xs = _gather_rows(x1p, buf_tok)
    yb = _expert_ffn(blk_e, nb_used, xs, wg, wu, wd, layer)
    yg = _gather_rows(yb, dest_t.reshape(-1))
    return yg, gw_t.T


def kernel(x, p, positions, fox_w_in, fox_b_f, fox_w_out, nsa_w_in, nsa_b_gate, nsa_pe_k, nsa_pe_v,
           nsa_cmp_k_w1, nsa_cmp_k_w2, nsa_cmp_v_w1, nsa_cmp_v_w2, nsa_w_out, moba_w_in, moba_w_out,
           ln1_g, ln1_b, router_w, router_b, exp_w_gate, exp_w_up, exp_w_down,
           sh_w_gate, sh_w_up, sh_w_down, ln2_g, ln2_b, ple_w_gate, ple_w_proj):
    B, S, D = x.shape
    T = B * S
    depth = p.shape[0]
    cosf, sinf = _rope_tables(positions)
    xt = x.reshape(T, D)
    xtb = xt.astype(BF16)
    for i in range(depth):
        kind, j = i % N_MIXERS, i // N_MIXERS
        if kind == 0:
            o_list = _fox_mixer(xt, xtb, B, S, fox_w_in[j], fox_b_f[j])
            w_out = fox_w_out[j]
        elif kind == 1:
            o_list = _nsa_mixer(xt, xtb, B, S, cosf, sinf, nsa_w_in[j], nsa_b_gate[j], nsa_pe_k[j], nsa_pe_v[j],
                                nsa_cmp_k_w1[j], nsa_cmp_k_w2[j], nsa_cmp_v_w1[j], nsa_cmp_v_w2[j])
            w_out = nsa_w_out[j]
        else:
            o_list = _moba_mixer(xtb, B, S, cosf, sinf, moba_w_in[j])
            w_out = moba_w_out[j]
        x1, x1p = _outproj_ln(o_list, w_out.astype(BF16), xt, ln1_g[i], ln1_b[i])
        yg, gw = _moe_dispatch(x1, x1p, router_w[i], router_b[i], exp_w_gate, exp_w_up, exp_w_down, i)
        xt, xtb = _post_moe(x1, yg, gw, p[i].reshape(T, -1), sh_w_gate[i].astype(BF16), sh_w_up[i].astype(BF16),
                            sh_w_down[i].astype(BF16), ln2_g[i], ln2_b[i],
                            ple_w_gate[i].astype(BF16), ple_w_proj[i].astype(BF16))
    return xt.reshape(B, S, D)
```

```python
import functools
import math

import jax
import jax.numpy as jnp
import numpy as np
from jax import lax
from jax.experimental import pallas as pl
from jax.experimental.pallas import tpu as pltpu
from jax.experimental.pallas import tpu_sc as plsc

D_MODEL = 1024
DEPTH = 4
N_HEADS = 16
HEAD_DIM = 64
ATTN_SCALE = HEAD_DIM ** -0.5
LOG2E = 1.4426950408889634
Q_SCALE = ATTN_SCALE * LOG2E
ROPE_THETA = 10000.0
N_MIXERS = 3

NSA_KV_GROUPS = 4
NSA_HEADS_PER_GROUP = N_HEADS // NSA_KV_GROUPS
NSA_CMP_LEN = 32
NSA_CMP_STRIDE = 16
NSA_SEL_LEN = 64
NSA_SEL_TOPN = 16
NSA_WINDOW = 512
NSA_FORCE_BONUS = 1e4

MOBA_BLOCK = 256
MOBA_TOPK = 3

N_EXPERTS = 64
EXPERT_DIM = 256
TOP_K = 8
N_GROUPS = 8
TOPK_GROUPS = 4
ROUTED_SCALE = 2.5
MOE_BLOCK = 256
EXPERT_SLOTS = 2

DN_ALPHA = (2 * DEPTH) ** 0.25
LN_EPS = 1e-5
NEG = -1e30

SC_CORES = 2
SC_SUBCORES = 16
SC_LANES = 16
SC_ROW_WORDS = 256
SC_WINDOW = 128
SC_INDEX_CHUNK = 4096
LANES = 128
SUBLANES = 8
ATTN_TILE = 512
PAIR_TILE = 1024
KV_TILE = 1024
ROW_TILE = 512
VMEM_LIMIT = 48 * 1024 * 1024
C_SPLIT = 3
ONES_ROWS = 16

F32 = jnp.float32
BF16 = jnp.bfloat16
HIGHEST = lax.Precision.HIGHEST


def _cparams(sem):
    return pltpu.CompilerParams(dimension_semantics=sem, vmem_limit_bytes=VMEM_LIMIT)


def _log2(n):
    assert n & (n - 1) == 0
    return n.bit_length() - 1


def _nt_dot(a, b, precision=None):
    return lax.dot_general(a, b, (((1,), (1,)), ((), ())), preferred_element_type=F32,
                           precision=precision)


def _split_bf16(a):
    hi = a.astype(BF16)
    return hi, (a - hi.astype(F32)).astype(BF16)


def _dot3(x, w, nt=False):
    dot = _nt_dot if nt else functools.partial(jnp.dot, preferred_element_type=F32)
    xh, xl = _split_bf16(x)
    wh, wl = _split_bf16(w)
    return dot(xh, wh) + dot(xl, wh) + dot(xh, wl)


def _topk_rows(work, n):
    rows = lax.broadcasted_iota(jnp.int32, work.shape, 0).astype(F32)
    for _ in range(n):
        m = jnp.max(work, axis=0, keepdims=True)
        idx = jnp.min(jnp.where(work == m, rows, float(work.shape[0])), axis=0, keepdims=True)
        work = jnp.where(rows == idx, -jnp.inf, work)
    return jnp.where(work == -jnp.inf, 1.0, 0.0)


def _proj_kernel(*refs, mode, precision, scaled):
    refs = list(refs)
    x_ref, w_ref = refs[:2]
    del refs[:2]
    scale_ref = refs.pop(0) if scaled else None
    if mode == "none":
        (o_ref,) = refs
    elif mode == "rope":
        cos_ref, sin_ref, r_ref = refs
    else:
        cos_ref, sin_ref, o_ref, r_ref = refs
    x = x_ref[...]
    w = w_ref[...]
    if precision == "split3":
        acc = _dot3(x, w)
    else:
        if x.dtype != w.dtype:
            x = x.astype(w.dtype)
        acc = jnp.dot(x, w, preferred_element_type=F32, precision=precision)
    if scaled:
        acc = acc * scale_ref[...]
    if mode in ("none", "both"):
        o_ref[...] = acc.astype(o_ref.dtype)
    if mode in ("rope", "both"):
        tn = acc.shape[1]
        rep = tn // LANES
        cosf = jnp.concatenate([cos_ref[...]] * rep, axis=1)
        sinf = jnp.concatenate([sin_ref[...]] * rep, axis=1)
        lane = lax.broadcasted_iota(jnp.int32, acc.shape, 1)
        first_half = (lane & (HEAD_DIM // 2)) == 0
        swapped = jnp.where(first_half, pltpu.roll(acc, tn - HEAD_DIM // 2, 1),
                            pltpu.roll(acc, HEAD_DIM // 2, 1))
        r_ref[...] = (acc * cosf + swapped * sinf).astype(r_ref.dtype)


def _proj(x, w, *, mode="none", cos=None, sin=None, out_dtype=BF16, tn=512, precision=None, col_scale=None):
    T, K = x.shape
    N = w.shape[1]
    tm = min(ROW_TILE, T)
    tn = min(tn, N)
    assert T % tm == 0 and N % tn == 0
    grid = (T // tm, N // tn)
    in_specs = [pl.BlockSpec((tm, K), lambda i, j: (i, 0)),
                pl.BlockSpec((K, tn), lambda i, j: (0, j))]
    args = [x, w]
    if col_scale is not None:
        in_specs.append(pl.BlockSpec((1, tn), lambda i, j: (0, j)))
        args.append(col_scale.reshape(1, N).astype(F32))
    if mode != "none":
        in_specs += [pl.BlockSpec((tm, LANES), lambda i, j: (i, 0))] * 2
        args += [cos, sin]
    o_spec = pl.BlockSpec((tm, tn), lambda i, j: (i, j))
    o_shape = jax.ShapeDtypeStruct((T, N), out_dtype)
    if mode == "both":
        out_specs, out_shape = [o_spec, o_spec], [o_shape, o_shape]
    else:
        out_specs, out_shape = o_spec, o_shape
    return pl.pallas_call(
        functools.partial(_proj_kernel, mode=mode, precision=precision, scaled=col_scale is not None),
        grid=grid, in_specs=in_specs, out_specs=out_specs, out_shape=out_shape,
        compiler_params=_cparams(("parallel", "parallel")), name=f"proj_{mode}")(*args)


def _proj_t_kernel(wt_ref, x_ref, o_ref, *, precision):
    wt = wt_ref[...]
    x = x_ref[...]
    if x.dtype != wt.dtype:
        x = x.astype(wt.dtype)
    o_ref[...] = _nt_dot(wt, x, precision).astype(o_ref.dtype)


def _proj_t(wt, x, *, out_dtype=BF16, precision=None):
    N, K = wt.shape
    T = x.shape[0]
    tm = min(ROW_TILE, T)
    tn = min(512, N)
    assert T % tm == 0 and N % tn == 0
    return pl.pallas_call(
        functools.partial(_proj_t_kernel, precision=precision),
        grid=(T // tm, N // tn),
        in_specs=[pl.BlockSpec((tn, K), lambda i, j: (j, 0)), pl.BlockSpec((tm, K), lambda i, j: (i, 0))],
        out_specs=pl.BlockSpec((tn, tm), lambda i, j: (j, i)),
        out_shape=jax.ShapeDtypeStruct((N, T), out_dtype),
        compiler_params=_cparams(("parallel", "parallel")), name="proj_t")(wt, x)


def _pack_rows(v):
    n = v.shape[1] // 2
    lo = pltpu.bitcast(v[:, :n].astype(BF16).astype(F32), jnp.int32)
    hi = pltpu.bitcast(v[:, n:].astype(BF16).astype(F32), jnp.int32)
    return (hi & jnp.int32(-65536)) | lax.shift_right_logical(lo, jnp.int32(16))


def _unpack_rows(w):
    lo = pltpu.bitcast(lax.shift_left(w, jnp.int32(16)), F32)
    hi = pltpu.bitcast(w & jnp.int32(-65536), F32)
    return jnp.concatenate([lo, hi], axis=1)


def _store_pieces(ref, v):
    w = _pack_rows(v)
    for j in range(ref.shape[0]):
        ref[j] = w[:, j * SC_ROW_WORDS:(j + 1) * SC_ROW_WORDS]


def _load_pieces(ref):
    return _unpack_rows(jnp.concatenate([ref[j] for j in range(ref.shape[0])], axis=1))


def _piece_spec(d, rows, index_map):
    return pl.BlockSpec((d // 2 // SC_ROW_WORDS, rows, SC_ROW_WORDS), index_map)


def _piece_shape(d, rows):
    return jax.ShapeDtypeStruct((d // 2 // SC_ROW_WORDS, rows, SC_ROW_WORDS), jnp.int32)


def _pad_cols(w, n):
    return jnp.pad(w, ((0, 0), (0, n - w.shape[1])))


def _layer_norm(z, g, b):
    mu = jnp.mean(z, axis=-1, keepdims=True)
    zc = z - mu
    var = jnp.mean(zc * zc, axis=-1, keepdims=True)
    return zc * lax.rsqrt(var + LN_EPS) * g + b


def _outproj_ln_kernel(*refs, n_o):
    o_refs = refs[:n_o]
    w_ref, x_ref, g_ref, b_ref, y_ref, yp_ref = refs[n_o:]
    if n_o == 1:
        o = o_refs[0][...]
    else:
        o = o_refs[0][...].astype(F32)
        for r in o_refs[1:]:
            o = o + r[...].astype(F32)
        o = o.astype(BF16)
    mix = jnp.dot(o, w_ref[...], preferred_element_type=F32)
    z = DN_ALPHA * x_ref[...] + mix
    y = _layer_norm(z, g_ref[...], b_ref[...])
    y_ref[...] = y
    _store_pieces(yp_ref, y)


def _outproj_ln(o_list, w, x, g, b):
    T, D = x.shape
    tm = min(ROW_TILE, T)
    row = pl.BlockSpec((tm, D), lambda i: (i, 0))
    vec = pl.BlockSpec((1, D), lambda i: (0, 0))
    return pl.pallas_call(
        functools.partial(_outproj_ln_kernel, n_o=len(o_list)),
        grid=(T // tm,),
        in_specs=[row] * len(o_list) + [pl.BlockSpec((D, D), lambda i: (0, 0)), row, vec, vec],
        out_specs=[row, _piece_spec(D, tm, lambda i: (0, i, 0))],
        out_shape=[jax.ShapeDtypeStruct((T, D), F32), _piece_shape(D, T)],
        compiler_params=_cparams(("parallel",)), name="outproj_ln")(
            *o_list, w, x, g.reshape(1, D), b.reshape(1, D))


def _flash_init(m_ref, acc_ref):
    m_ref[...] = jnp.full(m_ref.shape, NEG, F32)
    acc_ref[...] = jnp.zeros(acc_ref.shape, F32)


def _flash_step(sT, lhs, m_ref, acc_ref, h, cols=slice(None)):
    m_prev = m_ref[h, :, cols]
    m_new = jnp.maximum(m_prev, jnp.max(sT, axis=0, keepdims=True))
    p = jnp.exp2(sT - m_new)
    alpha = jnp.exp2(m_prev - m_new)
    acc_ref[h, :, cols] = alpha * acc_ref[h, :, cols] + jnp.dot(lhs, p.astype(BF16), preferred_element_type=F32)
    m_ref[h, :, cols] = m_new


def _causal_t(tk, tq, shift=0, strict_lower=False):
    key = lax.broadcasted_iota(jnp.int32, (tk, tq), 0)
    qry = lax.broadcasted_iota(jnp.int32, (tk, tq), 1)
    return (key > qry) if strict_lower else (key <= qry + shift)


def _causal_sweep(i, tq, tk, step):
    r = tk // tq
    assert tk == r * tq
    n_full = i >> _log2(r)

    def body(j, carry):
        step(pl.multiple_of(j * tk, tk), tk, None)
        return carry

    lax.fori_loop(0, n_full, body, 0)
    if r == 1:
        step(pl.multiple_of(i * tq, tq), tq, _causal_t(tq, tq))
    else:
        for rem in range(r):
            @pl.when(i - n_full * r == rem)
            def _(rem=rem):
                size = (rem + 1) * tq
                step(pl.multiple_of(n_full * tk, tk), size, _causal_t(size, tq, shift=rem * tq))


def _staggered(units, scores, update):
    s = [scores(units[0])]
    for n, u in enumerate(units):
        if n + 1 < len(units):
            s.append(scores(units[n + 1]))
        update(u, s[n])


def _head_units(n_heads, tq):
    w = min(ATTN_TILE, tq)
    return [(h, slice(c * w, (c + 1) * w)) for h in range(n_heads) for c in range(tq // w)]


def _pair_units(tq):
    return _head_units(2, tq)


def _head_of_pair(q_pair, which):
    lane = lax.broadcasted_iota(jnp.int32, q_pair.shape, 1)
    keep = (lane < HEAD_DIM) if which == 0 else (lane >= HEAD_DIM)
    return jnp.where(keep, q_pair, jnp.zeros_like(q_pair))


def _pair_rows(a_top, a_bot):
    row = lax.broadcasted_iota(jnp.int32, a_top.shape, 0)
    return jnp.where(row < HEAD_DIM, a_top, a_bot)


def _pair_lhs(vt, which):
    row = lax.broadcasted_iota(jnp.int32, vt.shape, 0)
    keep = (row < HEAD_DIM) if which == 0 else (row >= HEAD_DIM)
    return jnp.where(keep, vt, jnp.ones_like(vt))


def _pair_finish(acc_ref):
    a0, a1 = acc_ref[0], acc_ref[1]
    return _pair_rows(a0 / a0[HEAD_DIM:HEAD_DIM + 1, :], a1 / a1[0:1, :])


def _fox_gate_kernel(fl_ref, bf_ref, tri_ref, ex_ref, cp_ref, carry_ref):
    @pl.when(pl.program_id(1) == 0)
    def _():
        carry_ref[...] = jnp.zeros(carry_ref.shape, F32)

    z = fl_ref[...] + bf_ref[...]
    log_f = jnp.minimum(z, 0.0) - jnp.log1p(jnp.exp(-jnp.abs(z)))
    c = jnp.dot(tri_ref[...], log_f, preferred_element_type=F32, precision=HIGHEST) + carry_ref[...]
    carry_ref[...] = c[-1:, :]
    out = jnp.zeros(c.shape, F32)
    rem = c * LOG2E
    for piece in range(C_SPLIT):
        part = rem.astype(BF16)
        rem = rem - part.astype(F32)
        out = out + jnp.dot(part, ex_ref[piece], preferred_element_type=F32)
    cp_ref[...] = out.astype(BF16)


def _fox_gate(fl, b_f, B, S):
    T = fl.shape[0]
    tm = min(ROW_TILE, S)
    ns = S // tm
    tri = jnp.asarray(np.tril(np.ones((tm, tm), np.float32)))
    ex = np.zeros((C_SPLIT, LANES, LANES), np.float32)
    for h in range(N_HEADS):
        for j in range(C_SPLIT):
            ex[j, h, C_SPLIT * h + j] = 1.0
    bf = _pad_cols(b_f.reshape(1, -1), LANES)
    return pl.pallas_call(
        _fox_gate_kernel, grid=(B, ns),
        in_specs=[pl.BlockSpec((tm, LANES), lambda b, s: (b * ns + s, 0)),
                  pl.BlockSpec((1, LANES), lambda b, s: (0, 0)),
                  pl.BlockSpec((tm, tm), lambda b, s: (0, 0)),
                  pl.BlockSpec((C_SPLIT, LANES, LANES), lambda b, s: (0, 0, 0))],
        out_specs=pl.BlockSpec((tm, LANES), lambda b, s: (b * ns + s, 0)),
        out_shape=jax.ShapeDtypeStruct((T, LANES), BF16),
        scratch_shapes=[pltpu.VMEM((1, LANES), F32)],
        compiler_params=_cparams(("parallel", "arbitrary")), name="fox_gate")(
            fl, bf, tri, jnp.asarray(ex, BF16))


def _fox_kernel(q_ref, k_ref, cp_ref, vt_ref, o_ref, m_ref, acc_ref, *, tq, tk):
    hp = pl.program_id(1)
    i = pl.program_id(2)
    _flash_init(m_ref, acc_ref)
    q = q_ref[...]
    lane = lax.broadcasted_iota(jnp.int32, q.shape, 1)
    qa = []
    for h in range(2):
        lo = C_SPLIT * (2 * hp + h)
        minus_one = jnp.where((lane >= lo) & (lane < lo + C_SPLIT), -1.0, 0.0).astype(q.dtype)
        qa.append(jnp.concatenate([_head_of_pair(q, h), minus_one], axis=1))

    def step(off, size, mask):
        k = jnp.concatenate([k_ref[pl.ds(off, size), :], cp_ref[pl.ds(off, size), :]], axis=1)
        vt = vt_ref[:, pl.ds(off, size)]

        lhs = [_pair_lhs(vt, h) for h in range(2)]

        def scores(u):
            sT = _nt_dot(k, qa[u[0]][u[1], :])
            return sT if mask is None else jnp.where(mask[:, u[1]], sT, NEG)

        _staggered(_pair_units(tq), scores,
                   lambda u, sT: _flash_step(sT, lhs[u[0]], m_ref, acc_ref, u[0], u[1]))

    _causal_sweep(i, tq, tk, step)
    o_ref[...] = _pair_finish(acc_ref).T.astype(o_ref.dtype)


def _fox_attention(qk, cp, vt, B, S):
    T = qk.shape[0]
    tq = min(PAIR_TILE, S)
    tk = min(KV_TILE, S)
    nq = S // tq
    npair = N_HEADS // 2
    ncol = D_MODEL // LANES
    return pl.pallas_call(
        functools.partial(_fox_kernel, tq=tq, tk=tk),
        grid=(B, npair, nq),
        in_specs=[pl.BlockSpec((tq, LANES), lambda b, hp, i: (b * nq + i, hp)),
                  pl.BlockSpec((S, LANES), lambda b, hp, i: (b, ncol + hp)),
                  pl.BlockSpec((S, LANES), lambda b, hp, i: (b, 0)),
                  pl.BlockSpec((LANES, S), lambda b, hp, i: (hp, b))],
        out_specs=pl.BlockSpec((tq, LANES), lambda b, hp, i: (b * nq + i, hp)),
        out_shape=jax.ShapeDtypeStruct((T, D_MODEL), BF16),
        scratch_shapes=[pltpu.VMEM((2, 1, tq), F32), pltpu.VMEM((2, LANES, tq), F32)],
        compiler_params=_cparams(("parallel", "parallel", "arbitrary")), name="fox_attn")(
            qk, qk, cp, vt)


def _gate_expand(gl_ref, bg_ref, ex_ref):
    sig = jax.nn.sigmoid(gl_ref[...] + bg_ref[...])
    hi = sig.astype(BF16)
    lo = (sig - hi.astype(F32)).astype(BF16)
    ex = ex_ref[...]
    return jnp.dot(hi, ex, preferred_element_type=F32) + jnp.dot(lo, ex, preferred_element_type=F32)


def _nsa_compress_kernel(f_ref, pe_ref, w1_ref, w2_ref, o_ref):
    blk = (f_ref[...].astype(F32) + pe_ref[...]).astype(BF16)
    h = jax.nn.gelu(jnp.dot(blk, w1_ref[...], preferred_element_type=F32))
    o_ref[...] = jnp.dot(h.astype(BF16), w2_ref[...], preferred_element_type=F32)


def _nsa_compress(flat, pe_flat, w1, w2p):
    M, K = flat.shape
    tm = min(ROW_TILE, M)
    Hc = w1.shape[1]
    return pl.pallas_call(
        _nsa_compress_kernel, grid=(M // tm,),
        in_specs=[pl.BlockSpec((tm, K), lambda i: (i, 0)), pl.BlockSpec((1, K), lambda i: (0, 0)),
                  pl.BlockSpec((K, Hc), lambda i: (0, 0)), pl.BlockSpec((Hc, LANES), lambda i: (0, 0))],
        out_specs=pl.BlockSpec((tm, LANES), lambda i: (i, 0)),
        out_shape=jax.ShapeDtypeStruct((M, LANES), F32),
        compiler_params=_cparams(("parallel",)), name="nsa_compress")(flat, pe_flat, w1, w2p)


def _nsa_cmp_kernel(q_ref, kk_ref, vt_ref, ovt_ref, gl_ref, bg_ref, ex_ref, o_ref, sn_ref, *, tq, ncp, nselp):
    i = pl.program_id(2)
    q = q_ref[...]
    kk = kk_ref[...]
    vt = vt_ref[...]
    t = i * tq + lax.broadcasted_iota(jnp.int32, (ncp, tq), 1)
    n = lax.broadcasted_iota(jnp.int32, (ncp, tq), 0)
    valid = n * NSA_CMP_STRIDE + (NSA_CMP_LEN - 1) <= t
    psum = jnp.zeros((ncp, tq), F32)
    outs = []
    for hg in range(NSA_HEADS_PER_GROUP):
        qh = _head_of_pair(q[:, LANES * (hg // 2):LANES * (hg // 2 + 1)], hg % 2)
        sT = jnp.where(valid, _nt_dot(kk, qh), NEG)
        m = jnp.max(sT, axis=0, keepdims=True)
        e = jnp.where(valid, jnp.exp2(sT - m), 0.0)
        p = e / jnp.maximum(jnp.sum(e, axis=0, keepdims=True), 1e-30)
        psum = psum + p
        outs.append(jnp.dot(vt, p.astype(BF16), preferred_element_type=F32))
    o = jnp.concatenate(outs, axis=0).T
    o_ref[...] = (o * _gate_expand(gl_ref, bg_ref, ex_ref)).astype(o_ref.dtype)
    hi = psum.astype(BF16)
    lo = (psum - hi.astype(F32)).astype(BF16)
    ovt = ovt_ref[...]
    imp = jnp.dot(ovt, hi, preferred_element_type=F32) + jnp.dot(ovt, lo, preferred_element_type=F32)
    jblk = lax.broadcasted_iota(jnp.int32, (nselp, tq), 0)
    cur = (i * tq + lax.broadcasted_iota(jnp.int32, (nselp, tq), 1)) >> _log2(NSA_SEL_LEN)
    forced = (jblk == 0) | (jblk == cur) | (jblk == cur - 1)
    pri = jnp.where(forced, imp + NSA_FORCE_BONUS, imp)
    past = jblk <= cur
    pri = jnp.where(past, pri, NEG)
    chosen = _topk_rows(pri, NSA_SEL_TOPN)
    sn = jnp.where((chosen > 0.0) & past, 0.0, NEG)
    sn_ref[...] = sn.T.astype(sn_ref.dtype)


def _nsa_cmp_branch(q_c, kk_c, vt_c, overlap_t, gl, bg, ex, B, S):
    T = q_c.shape[0]
    G = NSA_KV_GROUPS
    tq = min(ATTN_TILE, S)
    nq = S // tq
    ncp = kk_c.shape[2]
    nselp = overlap_t.shape[0]
    return pl.pallas_call(
        functools.partial(_nsa_cmp_kernel, tq=tq, ncp=ncp, nselp=nselp),
        grid=(B, G, nq),
        in_specs=[pl.BlockSpec((tq, 2 * LANES), lambda b, g, i: (b * nq + i, g)),
                  pl.BlockSpec((None, None, ncp, LANES), lambda b, g, i: (b, g, 0, 0)),
                  pl.BlockSpec((None, None, HEAD_DIM, ncp), lambda b, g, i: (b, g, 0, 0)),
                  pl.BlockSpec((nselp, ncp), lambda b, g, i: (0, 0)),
                  pl.BlockSpec((tq, LANES), lambda b, g, i: (b * nq + i, 0)),
                  pl.BlockSpec((1, LANES), lambda b, g, i: (0, 0)),
                  pl.BlockSpec((None, LANES, 2 * LANES), lambda b, g, i: (g, 0, 0))],
        out_specs=[pl.BlockSpec((tq, 2 * LANES), lambda b, g, i: (b * nq + i, g)),
                   pl.BlockSpec((None, None, tq, nselp), lambda b, g, i: (b, g, i, 0))],
        out_shape=[jax.ShapeDtypeStruct((T, D_MODEL), BF16),
                   jax.ShapeDtypeStruct((B, G, S, nselp), BF16)],
        compiler_params=_cparams(("parallel", "parallel", "parallel")), name="nsa_cmp")(
            q_c, kk_c, vt_c, overlap_t, gl, bg, ex)


def _nsa_kv_kernel(*refs, mode, tq, tk):
    if mode == "sel":
        q_ref, kk_ref, vt_ref, oh_ref, sn_ref, gl_ref, bg_ref, ex_ref, o_ref, m_ref, acc_ref = refs
    else:
        q_ref, kk_ref, vt_ref, gl_ref, bg_ref, ex_ref, o_ref, m_ref, acc_ref = refs
    i = pl.program_id(2)
    _flash_init(m_ref, acc_ref)
    q = q_ref[...]
    qh = []
    for hg in range(NSA_HEADS_PER_GROUP):
        qq = _head_of_pair(q[:, LANES * (hg // 2):LANES * (hg // 2 + 1)], hg % 2)
        if mode == "sel":
            qq = jnp.concatenate([qq, sn_ref[...]], axis=1)
        qh.append(qq)

    def step(off, size, mask):
        k = kk_ref[pl.ds(off, size), :]
        vt = vt_ref[:, pl.ds(off, size)]
        lhs = jnp.concatenate([vt, jnp.ones((ONES_ROWS, size), vt.dtype)], axis=0)
        if mode == "sel":
            k = jnp.concatenate([k, oh_ref[pl.ds(off, size), :]], axis=1)

        def scores(u):
            sT = _nt_dot(k, qh[u[0]][u[1], :])
            return sT if mask is None else jnp.where(mask[:, u[1]], sT, NEG)

        _staggered(_head_units(NSA_HEADS_PER_GROUP, tq), scores,
                   lambda u, sT: _flash_step(sT, lhs, m_ref, acc_ref, u[0], u[1]))

    if mode == "sel":
        _causal_sweep(i, tq, tk, step)
    else:
        @pl.when(i > 0)
        def _():
            step(pl.multiple_of((i - 1) * tq, tq), tq, _causal_t(tq, tq, strict_lower=True))
        step(pl.multiple_of(i * tq, tq), tq, _causal_t(tq, tq))
    outs = []
    for hg in range(NSA_HEADS_PER_GROUP):
        a = acc_ref[hg]
        outs.append(a[:HEAD_DIM] / a[HEAD_DIM:HEAD_DIM + 1, :])
    o_ref[...] = (jnp.concatenate(outs, axis=0).T * _gate_expand(gl_ref, bg_ref, ex_ref)).astype(o_ref.dtype)


def _nsa_kv_branch(mode, q_r, kk, vt, gl, bg, ex, B, S, onehot=None, selneg=None):
    T = q_r.shape[0]
    G = NSA_KV_GROUPS
    tq = min(PAIR_TILE if mode == "sel" else ATTN_TILE, S)
    tk = min(KV_TILE, S)
    assert NSA_WINDOW == tq or mode == "sel"
    nq = S // tq
    in_specs = [pl.BlockSpec((tq, 2 * LANES), lambda b, g, i: (b * nq + i, g)),
                pl.BlockSpec((None, None, S, LANES), lambda b, g, i: (b, g, 0, 0)),
                pl.BlockSpec((HEAD_DIM, S), lambda b, g, i: (g, b))]
    args = [q_r, kk, vt]
    if mode == "sel":
        in_specs += [pl.BlockSpec((S, LANES), lambda b, g, i: (0, 0)),
                     pl.BlockSpec((None, None, tq, LANES), lambda b, g, i: (b, g, i, 0))]
        args += [onehot, selneg]
    in_specs += [pl.BlockSpec((tq, LANES), lambda b, g, i: (b * nq + i, 0)),
                 pl.BlockSpec((1, LANES), lambda b, g, i: (0, 0)),
                 pl.BlockSpec((None, LANES, 2 * LANES), lambda b, g, i: (g, 0, 0))]
    args += [gl, bg, ex]
    nh = NSA_HEADS_PER_GROUP
    return pl.pallas_call(
        functools.partial(_nsa_kv_kernel, mode=mode, tq=tq, tk=tk),
        grid=(B, G, nq), in_specs=in_specs,
        out_specs=pl.BlockSpec((tq, 2 * LANES), lambda b, g, i: (b * nq + i, g)),
        out_shape=jax.ShapeDtypeStruct((T, D_MODEL), BF16),
        scratch_shapes=[pltpu.VMEM((nh, 1, tq), F32), pltpu.VMEM((nh, HEAD_DIM + ONES_ROWS, tq), F32)],
        compiler_params=_cparams(("parallel", "parallel", "arbitrary")), name=f"nsa_{mode}")(*args)


def _kmean_kernel(k_ref, o_ref, *, nblk):
    k = k_ref[...].astype(F32)
    o_ref[...] = jnp.mean(k.reshape(nblk, MOBA_BLOCK, k.shape[1]), axis=1)


def _moba_kmean(k_rot):
    T, D = k_rot.shape
    nblk = 8
    rows = nblk * MOBA_BLOCK
    assert T % rows == 0
    return pl.pallas_call(
        functools.partial(_kmean_kernel, nblk=nblk), grid=(T // rows,),
        in_specs=[pl.BlockSpec((rows, D), lambda i: (i, 0))],
        out_specs=pl.BlockSpec((nblk, D), lambda i: (i, 0)),
        out_shape=jax.ShapeDtypeStruct((T // MOBA_BLOCK, D), F32),
        compiler_params=_cparams(("parallel",)), name="moba_kmean")(k_rot)


def _moba_select_kernel(q_ref, r_ref, sn_ref, *, tq, nblk):
    i = pl.program_id(2)
    gsT = _nt_dot(r_ref[...], q_ref[...].astype(F32), HIGHEST)
    jblk = lax.broadcasted_iota(jnp.int32, (nblk, tq), 0)
    cb = (i * tq + lax.broadcasted_iota(jnp.int32, (nblk, tq), 1)) >> _log2(MOBA_BLOCK)
    past = jblk < cb
    parts = []
    for h in range(2):
        gs = jnp.where(past, gsT[h * nblk:(h + 1) * nblk, :], NEG)
        chosen = _topk_rows(gs, min(MOBA_TOPK, nblk))
        parts.append(jnp.where(((chosen > 0.0) & past) | (jblk == cb), 0.0, NEG))
    parts.append(jnp.zeros((LANES - 2 * nblk, tq), F32))
    sn_ref[...] = jnp.concatenate(parts, axis=0).T.astype(sn_ref.dtype)


def _moba_select(q_rot, rmat_t, B, S):
    tq = min(ATTN_TILE, S)
    nq = S // tq
    npair = N_HEADS // 2
    nblk = S // MOBA_BLOCK
    return pl.pallas_call(
        functools.partial(_moba_select_kernel, tq=tq, nblk=nblk), grid=(B, npair, nq),
        in_specs=[pl.BlockSpec((tq, LANES), lambda b, hp, i: (b * nq + i, hp)),
                  pl.BlockSpec((None, None, LANES, LANES), lambda b, hp, i: (b, hp, 0, 0))],
        out_specs=pl.BlockSpec((None, None, tq, LANES), lambda b, hp, i: (b, hp, i, 0)),
        out_shape=jax.ShapeDtypeStruct((B, npair, S, LANES), BF16),
        compiler_params=_cparams(("parallel", "parallel", "parallel")), name="moba_select")(q_rot, rmat_t)


def _moba_kernel(q_ref, k_ref, vt_ref, oh_ref, sn_ref, o_ref, m_ref, acc_ref, *, tq, tk, nblk):
    i = pl.program_id(2)
    _flash_init(m_ref, acc_ref)
    q = q_ref[...]
    sn = sn_ref[...]
    lane = lax.broadcasted_iota(jnp.int32, sn.shape, 1)
    qa = []
    for h in range(2):
        mine = (lane >= h * nblk) & (lane < (h + 1) * nblk)
        qa.append(jnp.concatenate([_head_of_pair(q, h), jnp.where(mine, sn, jnp.zeros_like(sn))], axis=1))

    def step(off, size, mask):
        k = jnp.concatenate([k_ref[pl.ds(off, size), :], oh_ref[pl.ds(off, size), :]], axis=1)
        vt = vt_ref[:, pl.ds(off, size)]

        lhs = [_pair_lhs(vt, h) for h in range(2)]

        def scores(u):
            sT = _nt_dot(k, qa[u[0]][u[1], :])
            return sT if mask is None else jnp.where(mask[:, u[1]], sT, NEG)

        _staggered(_pair_units(tq), scores,
                   lambda u, sT: _flash_step(sT, lhs[u[0]], m_ref, acc_ref, u[0], u[1]))

    _causal_sweep(i, tq, tk, step)
    o_ref[...] = _pair_finish(acc_ref).T.astype(o_ref.dtype)


def _moba_attention(qk_rot, vt, onehot2, selneg, B, S):
    T = qk_rot.shape[0]
    tq = min(PAIR_TILE, S)
    tk = min(KV_TILE, S)
    nq = S // tq
    npair = N_HEADS // 2
    ncol = D_MODEL // LANES
    nblk = S // MOBA_BLOCK
    return pl.pallas_call(
        functools.partial(_moba_kernel, tq=tq, tk=tk, nblk=nblk),
        grid=(B, npair, nq),
        in_specs=[pl.BlockSpec((tq, LANES), lambda b, hp, i: (b * nq + i, hp)),
                  pl.BlockSpec((S, LANES), lambda b, hp, i: (b, ncol + hp)),
                  pl.BlockSpec((LANES, S), lambda b, hp, i: (hp, b)),
                  pl.BlockSpec((S, LANES), lambda b, hp, i: (0, 0)),
                  pl.BlockSpec((None, None, tq, LANES), lambda b, hp, i: (b, hp, i, 0))],
        out_specs=pl.BlockSpec((tq, LANES), lambda b, hp, i: (b * nq + i, hp)),
        out_shape=jax.ShapeDtypeStruct((T, D_MODEL), BF16),
        scratch_shapes=[pltpu.VMEM((2, 1, tq), F32), pltpu.VMEM((2, LANES, tq), F32)],
        compiler_params=_cparams(("parallel", "parallel", "arbitrary")), name="moba_attn")(
            qk_rot, qk_rot, vt, onehot2, selneg)


def _router_kernel(wt_ref, x_ref, rb_ref, e_ref, g_ref):
    tm = x_ref.shape[0]
    gsz = N_EXPERTS // N_GROUPS
    scores = jax.nn.sigmoid(_dot3(wt_ref[...], x_ref[...], nt=True))
    biased = scores + rb_ref[...]
    member = lax.broadcasted_iota(jnp.int32, (gsz, tm), 0).astype(F32)
    gscore = []
    for g in range(N_GROUPS):
        v = biased[g * gsz:(g + 1) * gsz, :]
        m1 = jnp.max(v, axis=0, keepdims=True)
        i1 = jnp.min(jnp.where(v == m1, member, float(gsz)), axis=0, keepdims=True)
        m2 = jnp.max(jnp.where(member == i1, -jnp.inf, v), axis=0, keepdims=True)
        gscore.append(m1 + m2)
    gsel = _topk_rows(jnp.concatenate(gscore, axis=0), TOPK_GROUPS)
    emask = jnp.concatenate([jnp.broadcast_to(gsel[g:g + 1, :], (gsz, tm)) for g in range(N_GROUPS)], axis=0)
    work = jnp.where(emask > 0.0, biased, NEG)
    erow = lax.broadcasted_iota(jnp.int32, (N_EXPERTS, tm), 0).astype(F32)
    idxs, vals = [], []
    for _ in range(TOP_K):
        m = jnp.max(work, axis=0, keepdims=True)
        idx = jnp.min(jnp.where(work == m, erow, float(N_EXPERTS)), axis=0, keepdims=True)
        pick = erow == idx
        idxs.append(idx)
        vals.append(jnp.sum(jnp.where(pick, scores, 0.0), axis=0, keepdims=True))
        work = jnp.where(pick, -jnp.inf, work)
    gw = jnp.concatenate(vals, axis=0)
    e_ref[...] = jnp.concatenate(idxs, axis=0).astype(jnp.int32)
    g_ref[...] = gw / jnp.sum(gw, axis=0, keepdims=True) * ROUTED_SCALE


def _router(x1, router_w, router_b):
    T, D = x1.shape
    tm = min(ROW_TILE, T)
    return pl.pallas_call(
        _router_kernel, grid=(T // tm,),
        in_specs=[pl.BlockSpec((N_EXPERTS, D), lambda i: (0, 0)), pl.BlockSpec((tm, D), lambda i: (i, 0)),
                  pl.BlockSpec((N_EXPERTS, 1), lambda i: (0, 0))],
        out_specs=[pl.BlockSpec((TOP_K, tm), lambda i: (0, i))] * 2,
        out_shape=[jax.ShapeDtypeStruct((TOP_K, T), jnp.int32), jax.ShapeDtypeStruct((TOP_K, T), F32)],
        compiler_params=_cparams(("parallel",)), name="moe_router")(
            router_w.T, x1, router_b.reshape(N_EXPERTS, 1))


def _rank_kernel(e_ref, tri_ref, rank_ref, cnt_ref, carry_ref):
    @pl.when(pl.program_id(0) == 0)
    def _():
        carry_ref[...] = jnp.zeros(carry_ref.shape, F32)

    tm = e_ref.shape[1]
    erow = lax.broadcasted_iota(jnp.int32, (N_EXPERTS, tm), 0)
    tri = tri_ref[...]
    base = carry_ref[...]
    ranks = []
    for k in range(TOP_K):
        oh = erow == e_ref[k:k + 1, :]
        ohb = jnp.where(oh, 1.0, 0.0).astype(BF16)
        incl = jnp.dot(ohb, tri, preferred_element_type=F32)
        ranks.append(jnp.sum(jnp.where(oh, base + incl - 1.0, 0.0), axis=0, keepdims=True))
        base = base + incl[:, tm - 1:tm]
    carry_ref[...] = base
    rank_ref[...] = jnp.concatenate(ranks, axis=0).astype(jnp.int32)
    cnt_ref[...] = jnp.broadcast_to(base, cnt_ref.shape)


def _expert_ranks(eidx_t):
    K, T = eidx_t.shape
    tm = min(ROW_TILE, T)
    tri = jnp.asarray(np.triu(np.ones((tm, tm), np.float32)), BF16)
    return pl.pallas_call(
        _rank_kernel, grid=(T // tm,),
        in_specs=[pl.BlockSpec((K, tm), lambda i: (0, i)), pl.BlockSpec((tm, tm), lambda i: (0, 0))],
        out_specs=[pl.BlockSpec((K, tm), lambda i: (0, i)), pl.BlockSpec((N_EXPERTS, LANES), lambda i: (0, 0))],
        out_shape=[jax.ShapeDtypeStruct((K, T), jnp.int32), jax.ShapeDtypeStruct((N_EXPERTS, LANES), F32)],
        scratch_shapes=[pltpu.VMEM((N_EXPERTS, 1), F32)],
        compiler_params=_cparams(("arbitrary",)), name="moe_rank")(eidx_t, tri)


def _dest_kernel(e_ref, rank_ref, ps_ref, d_ref):
    tm = e_ref.shape[1]
    erow = lax.broadcasted_iota(jnp.int32, (N_EXPERTS, tm), 0)
    ps = ps_ref[...]
    rows = []
    for k in range(TOP_K):
        oh = erow == e_ref[k:k + 1, :]
        rows.append(jnp.sum(jnp.where(oh, ps, 0.0), axis=0, keepdims=True))
    d_ref[...] = jnp.concatenate(rows, axis=0).astype(jnp.int32) + rank_ref[...]


def _expert_dest(eidx_t, rank_t, pstarts):
    K, T = eidx_t.shape
    tm = min(ROW_TILE, T)
    blk = pl.BlockSpec((K, tm), lambda i: (0, i))
    return pl.pallas_call(
        _dest_kernel, grid=(T // tm,),
        in_specs=[blk, blk, pl.BlockSpec((N_EXPERTS, 1), lambda i: (0, 0))],
        out_specs=blk, out_shape=jax.ShapeDtypeStruct((K, T), jnp.int32),
        compiler_params=_cparams(("parallel",)), name="moe_dest")(
            eidx_t, rank_t, pstarts.astype(F32).reshape(N_EXPERTS, 1))


def _expert_kernel(be_ref, nb_ref, x_ref, *refs):
    n = EXPERT_SLOTS
    w_refs, (o_ref, wgu_ref, wdb_ref) = refs[:3 * n], refs[3 * n:]
    s = pl.program_id(0)
    E = wdb_ref.shape[1]
    R = MOE_BLOCK

    for j in range(n):
        b = s * n + j

        @pl.when((s == 0) | (be_ref[b] != be_ref[jnp.maximum(b - n, 0)]))
        def _(j=j):
            wg_ref, wu_ref, wd_ref = w_refs[3 * j:3 * j + 3]
            wgu_ref[j, :, :E] = wg_ref[...].astype(BF16)
            wgu_ref[j, :, E:] = wu_ref[...].astype(BF16)
            wdb_ref[j] = wd_ref[...].astype(BF16)

    @pl.when(s * n < nb_ref[0])
    def _():
        x = _load_pieces(x_ref).astype(BF16)
        ys = []

        def gate_up(j):
            return jnp.dot(x[j * R:(j + 1) * R], wgu_ref[j], preferred_element_type=F32)

        def down(j, gu):
            h = jax.nn.silu(gu[:, :E]) * gu[:, E:]
            ys.append(jnp.dot(h.astype(BF16), wdb_ref[j], preferred_element_type=F32))

        _staggered(list(range(n)), gate_up, down)
        _store_pieces(o_ref, jnp.concatenate(ys, axis=0))

    @pl.when(s * n >= nb_ref[0])
    def _():
        o_ref[...] = jnp.zeros(o_ref.shape, o_ref.dtype)


def _expert_ffn(blk_e, nb_used, xs, wg, wu, wd, layer):
    P = xs.shape[1]
    D = wg.shape[2]
    n = EXPERT_SLOTS
    rows = n * MOE_BLOCK
    assert P % rows == 0
    E = EXPERT_DIM
    w_specs = []
    for j in range(n):
        pick = functools.partial(lambda s, be, nb, j: (layer, be[s * n + j], 0, 0), j=j)
        w_specs += [pl.BlockSpec((None, None, D, E), pick), pl.BlockSpec((None, None, D, E), pick),
                    pl.BlockSpec((None, None, E, D), pick)]
    grid_spec = pltpu.PrefetchScalarGridSpec(
        num_scalar_prefetch=2, grid=(P // rows,),
        in_specs=[_piece_spec(D, rows, lambda s, be, nb: (0, s, 0))] + w_specs,
        out_specs=_piece_spec(D, rows, lambda s, be, nb: (0, s, 0)),
        scratch_shapes=[pltpu.VMEM((n, D, 2 * E), BF16), pltpu.VMEM((n, E, D), BF16)])
    return pl.pallas_call(
        _expert_kernel, grid_spec=grid_spec, out_shape=_piece_shape(D, P),
        compiler_params=_cparams(("arbitrary",)), name="moe_experts")(
            blk_e, nb_used, xs, *([wg, wu, wd] * n))


def _post_moe_kernel(x_ref, *refs):
    y_refs = refs[:TOP_K]
    gw_ref, p_ref, sg_ref, su_ref, sd_ref, g_ref, b_ref, wg_ref, wp_ref, o_ref, ob_ref = refs[TOP_K:]
    x = x_ref[...]
    xb = x.astype(BF16)
    h = jax.nn.silu(jnp.dot(xb, sg_ref[...], preferred_element_type=F32)) * jnp.dot(
        xb, su_ref[...], preferred_element_type=F32)
    ffn = jnp.dot(h.astype(BF16), sd_ref[...], preferred_element_type=F32)
    gw = gw_ref[...]
    for k in range(TOP_K):
        ffn = ffn + gw[:, k:k + 1] * _load_pieces(y_refs[k])
    z = DN_ALPHA * x + ffn
    x2 = _layer_norm(z, g_ref[...], b_ref[...])
    gate = jax.nn.sigmoid(jnp.dot(x2.astype(BF16), wg_ref[...], preferred_element_type=F32))
    proj = jnp.dot(p_ref[...].astype(BF16), wp_ref[...], preferred_element_type=F32)
    out = x2 + gate * proj
    o_ref[...] = out
    ob_ref[...] = out.astype(BF16)


def _post_moe(x1, yg, gw, p, sg, su, sd, g, b, wgate, wproj):
    T, D = x1.shape
    tm = min(ROW_TILE // 2, T)
    nt = T // tm
    PD = p.shape[1]
    SD = sg.shape[1]
    row = pl.BlockSpec((tm, D), lambda i: (i, 0))
    vec = pl.BlockSpec((1, D), lambda i: (0, 0))
    full = lambda r, c: pl.BlockSpec((r, c), lambda i: (0, 0))
    return pl.pallas_call(
        _post_moe_kernel, grid=(T // tm,),
        in_specs=[row] + [_piece_spec(D, tm, functools.partial(lambda i, k: (0, k * nt + i, 0), k=k))
                          for k in range(TOP_K)]
                 + [pl.BlockSpec((tm, TOP_K), lambda i: (i, 0)),
                    pl.BlockSpec((tm, PD), lambda i: (i, 0)), full(D, SD), full(D, SD),
                  full(SD, D), vec, vec, full(D, D), full(PD, D)],
        out_specs=[row, row],
        out_shape=[jax.ShapeDtypeStruct((T, D), F32), jax.ShapeDtypeStruct((T, D), BF16)],
        compiler_params=_cparams(("parallel",)), name="post_moe")(
            x1, *([yg] * TOP_K), gw, p, sg, su, sd, g.reshape(1, D), b.reshape(1, D), wgate, wproj)


def _rope_tables(positions):
    inv = 1.0 / (ROPE_THETA ** (jnp.arange(0, HEAD_DIM, 2, dtype=F32) / HEAD_DIM))
    ang = positions.astype(F32).reshape(-1)[:, None] * inv
    cos, sin = jnp.cos(ang), jnp.sin(ang)
    cosf = jnp.concatenate([cos] * (LANES // (HEAD_DIM // 2)), axis=1)
    sinf = jnp.concatenate([-sin, sin] * (LANES // HEAD_DIM), axis=1)
    return cosf, sinf


def _q_col_scale(n_q, n):
    return jnp.concatenate([jnp.full((n_q,), Q_SCALE, F32), jnp.ones((n - n_q,), F32)])


def _fox_mixer(x, xb, B, S, w_in, b_f):
    D = D_MODEL
    wb = w_in[:, :3 * D].astype(BF16)
    qk = _proj(xb, wb[:, :2 * D], col_scale=_q_col_scale(D, 2 * D))
    vt = _proj_t(wb[:, 2 * D:].T, xb)
    fl = _proj(x, _pad_cols(w_in[:, 3 * D:], LANES), out_dtype=F32, tn=LANES, precision="split3")
    cp = _fox_gate(fl, b_f, B, S)
    return [_fox_attention(qk, cp, vt, B, S)]


def _nsa_mixer(x, xb, B, S, cosf, sinf, w_in, b_gate, pe_k, pe_v, ck_w1, ck_w2, cv_w1, cv_w2):
    D, G, HD = D_MODEL, NSA_KV_GROUPS, HEAD_DIM
    HG = NSA_HEADS_PER_GROUP
    kvw = G * HD
    wb = w_in[:, :D + 6 * kvw].astype(BF16)
    q_c, q_r = _proj(xb, wb[:, :D], mode="both", cos=cosf, sin=sinf, col_scale=_q_col_scale(D, D))
    w_rot = jnp.concatenate([wb[:, D + 2 * kvw:D + 3 * kvw], wb[:, D + 4 * kvw:D + 5 * kvw]], axis=1)
    k_rot = _proj(xb, w_rot, mode="rope", cos=cosf, sin=sinf)
    kvc = _proj(xb, wb[:, D:D + 2 * kvw])
    w_v = jnp.concatenate([wb[:, D + 3 * kvw:D + 4 * kvw], wb[:, D + 5 * kvw:D + 6 * kvw]], axis=1)
    vt = _proj_t(w_v.T, xb)
    gl = _proj(x, _pad_cols(w_in[:, D + 6 * kvw:], LANES), out_dtype=F32, tn=LANES, precision="split3")
    bg = _pad_cols(b_gate.reshape(1, -1), LANES)

    def grouped(t2d):
        t = t2d.reshape(B, S, G, HD).transpose(0, 2, 1, 3)
        return jnp.concatenate([t, t], axis=-1)

    n_chunks = S // NSA_CMP_STRIDE
    n_cmp = n_chunks - NSA_CMP_LEN // NSA_CMP_STRIDE + 1
    ncp = n_chunks

    def compress(t2d, pe, w1, w2):
        ch = t2d.reshape(B, n_chunks, NSA_CMP_STRIDE, G, HD).transpose(0, 1, 3, 2, 4)
        ch = ch.reshape(B, n_chunks, G, NSA_CMP_STRIDE * HD)
        flat = jnp.concatenate([ch[:, :n_cmp], ch[:, 1:n_cmp + 1]], axis=-1)
        flat = jnp.pad(flat, ((0, 0), (0, ncp - n_cmp), (0, 0), (0, 0))).reshape(B * ncp * G, -1)
        out = _nsa_compress(flat, pe.reshape(1, -1), w1.astype(BF16), _pad_cols(w2, LANES).astype(BF16))
        return out[:, :HD].reshape(B, ncp, G, HD).astype(BF16)

    kc = compress(kvc[:, :kvw], pe_k, ck_w1, ck_w2).transpose(0, 2, 1, 3)
    kk_c = jnp.concatenate([kc, kc], axis=-1)
    vt_c = compress(kvc[:, kvw:], pe_v, cv_w1, cv_w2).transpose(0, 2, 3, 1)

    n_sel = S // NSA_SEL_LEN
    assert n_sel <= LANES
    cmp_start = np.arange(ncp) * NSA_CMP_STRIDE
    sel_start = np.arange(LANES) * NSA_SEL_LEN
    overlap = ((cmp_start[:, None] < sel_start[None, :] + NSA_SEL_LEN)
               & (cmp_start[:, None] + NSA_CMP_LEN > sel_start[None, :])
               & (np.arange(ncp)[:, None] < n_cmp) & (np.arange(LANES)[None, :] < n_sel))
    overlap_t = jnp.asarray(overlap.T, BF16)
    ex = np.zeros((3, G, LANES, HG * HD), np.float32)
    for br in range(3):
        for g in range(G):
            for hg in range(HG):
                ex[br, g, (g * HG + hg) * 3 + br, hg * HD:(hg + 1) * HD] = 1.0
    ex = jnp.asarray(ex, BF16)

    o_c, selneg = _nsa_cmp_branch(q_c, kk_c, vt_c, overlap_t, gl, bg, ex[0], B, S)
    onehot = jnp.asarray((np.arange(S)[:, None] // NSA_SEL_LEN) == np.arange(LANES)[None, :], BF16)
    kk_s = grouped(k_rot[:, :kvw])
    kk_w = grouped(k_rot[:, kvw:])
    o_s = _nsa_kv_branch("sel", q_r, kk_s, vt[:kvw], gl, bg, ex[1], B, S, onehot=onehot, selneg=selneg)
    o_w = _nsa_kv_branch("win", q_r, kk_w, vt[kvw:], gl, bg, ex[2], B, S)
    return [o_c, o_s, o_w]


def _moba_mixer(xb, B, S, cosf, sinf, w_in):
    D, H, HD = D_MODEL, N_HEADS, HEAD_DIM
    wb = w_in.astype(BF16)
    qk_rot = _proj(xb, wb[:, :2 * D], mode="rope", cos=cosf, sin=sinf, col_scale=_q_col_scale(D, 2 * D))
    vt = _proj_t(wb[:, 2 * D:].T, xb)
    nblk = S // MOBA_BLOCK
    assert S % MOBA_BLOCK == 0 and 2 * nblk <= LANES and nblk % SUBLANES == 0
    kmean = _moba_kmean(qk_rot[:, D:]).reshape(B, nblk, H // 2, 2, HD)
    km = kmean.transpose(0, 2, 3, 1, 4)
    rmat_t = jnp.zeros((B, H // 2, LANES, 2, HD), F32)
    rmat_t = rmat_t.at[:, :, :nblk, 0].set(km[:, :, 0]).at[:, :, nblk:2 * nblk, 1].set(km[:, :, 1])
    selneg = _moba_select(qk_rot, rmat_t.reshape(B, H // 2, LANES, LANES), B, S)
    blk_of = np.arange(S) // MOBA_BLOCK
    oh = np.zeros((S, LANES), np.float32)
    oh[np.arange(S), blk_of] = 1.0
    oh[np.arange(S), nblk + blk_of] = 1.0
    return [_moba_attention(qk_rot, vt, jnp.asarray(oh, BF16), selneg, B, S)]


def _sc_invert_kernel(dest_hbm, tok_hbm, out_hbm, d_v, t_v, buf_v, *, seg, chunk, n_chunks, n_tok):
    lo = pl.multiple_of((lax.axis_index("s") * SC_CORES + lax.axis_index("c")) * seg, SC_LANES)

    @pl.loop(0, seg // SC_LANES)
    def _(j):
        row = lo + j * SC_LANES + lax.iota(jnp.int32, SC_LANES)
        buf_v[pl.ds(j * SC_LANES, SC_LANES)] = lax.rem(row, jnp.int32(n_tok))

    @pl.loop(0, n_chunks)
    def _(c):
        pltpu.sync_copy(dest_hbm.at[pl.ds(c * chunk, chunk)], d_v)
        pltpu.sync_copy(tok_hbm.at[pl.ds(c * chunk, chunk)], t_v)

        @pl.loop(0, chunk // SC_LANES)
        def _(j):
            d = d_v[pl.ds(j * SC_LANES, SC_LANES)] - lo
            mine = (d >= 0) & (d < seg)
            plsc.store_scatter(buf_v, [jnp.where(mine, d, 0)], t_v[pl.ds(j * SC_LANES, SC_LANES)], mask=mine)

    pltpu.sync_copy(buf_v, out_hbm.at[pl.ds(lo, seg)])


def _invert_dest(dest, tok, P, n_tok):
    R = dest.shape[0]
    n_workers = SC_CORES * SC_SUBCORES
    seg = P // n_workers
    chunk = min(SC_INDEX_CHUNK, R)
    assert P == seg * n_workers and seg % SC_LANES == 0 and R % chunk == 0
    mesh = plsc.VectorSubcoreMesh(core_axis_name="c", subcore_axis_name="s",
                                  num_cores=SC_CORES, num_subcores=SC_SUBCORES)
    return pl.kernel(
        functools.partial(_sc_invert_kernel, seg=seg, chunk=chunk, n_chunks=R // chunk, n_tok=n_tok),
        out_type=jax.ShapeDtypeStruct((P,), jnp.int32), mesh=mesh,
        scratch_types=[pltpu.VMEM((chunk,), jnp.int32), pltpu.VMEM((chunk,), jnp.int32),
                       pltpu.VMEM((seg,), jnp.int32)],
        compiler_params=pltpu.CompilerParams(needs_layout_passes=False),
        name="moe_invert")(dest, tok)


def _sc_mesh():
    return plsc.VectorSubcoreMesh(core_axis_name="c", subcore_axis_name="s",
                                  num_cores=SC_CORES, num_subcores=SC_SUBCORES)


def _gather_rows(table, idx):
    parts, V, _ = table.shape
    N = idx.shape[0]
    n_pieces = N * parts
    n_workers = SC_CORES * SC_SUBCORES
    assert n_pieces % (SC_WINDOW * n_workers) == 0
    pieces = (jnp.arange(parts, dtype=jnp.int32)[:, None] * V + idx[None, :]).reshape(1, n_pieces)

    def kernel_body(x_hbm, i_hbm, o_hbm):
        def body(i_vmem, o_vmem):
            pltpu.sync_copy(x_hbm.at[i_vmem.at[0]], o_vmem)

        pltpu.emit_pipeline(
            body, grid=(n_pieces // SC_WINDOW,),
            in_specs=[pl.BlockSpec((1, SC_WINDOW), lambda i: (0, i))],
            out_specs=[pl.BlockSpec((SC_WINDOW, SC_ROW_WORDS), lambda i: (i, 0))],
            core_axis_name=("c", "s"), dimension_semantics=(pltpu.PARALLEL,))(i_hbm, o_hbm)

    out = pl.kernel(kernel_body, out_type=jax.ShapeDtypeStruct((n_pieces, SC_ROW_WORDS), table.dtype),
                    mesh=_sc_mesh(), scratch_types=[], name="moe_gather")(
                        table.reshape(parts * V, SC_ROW_WORDS), pieces)
    return out.reshape(parts, N, SC_ROW_WORDS)


def _moe_dispatch(x1, x1p, router_w, router_b, wg, wu, wd, layer):
    T, D = x1.shape
    eidx_t, gw_t = _router(x1, router_w, router_b)
    rank_t, cnt = _expert_ranks(eidx_t)
    counts = cnt[:, 0].astype(jnp.int32)
    padded = (counts + MOE_BLOCK - 1) // MOE_BLOCK * MOE_BLOCK
    pends = jnp.cumsum(padded)
    pstarts = pends - padded
    R = T * TOP_K
    parts = D // 2 // SC_ROW_WORDS
    unit = math.lcm(EXPERT_SLOTS * MOE_BLOCK, SC_CORES * SC_SUBCORES * SC_LANES,
                    SC_CORES * SC_SUBCORES * SC_WINDOW // math.gcd(parts, SC_WINDOW))
    P = -(-(R + N_EXPERTS * (MOE_BLOCK - 1)) // unit) * unit
    NB = P // MOE_BLOCK
    blk_start = jnp.arange(NB, dtype=jnp.int32) * MOE_BLOCK
    blk_e = jnp.minimum(jnp.sum(pends[None, :] <= blk_start[:, None], axis=1), N_EXPERTS - 1).astype(jnp.int32)
    nb_used = (pends[-1] // MOE_BLOCK).astype(jnp.int32).reshape(1)
    dest_t = _expert_dest(eidx_t, rank_t, pstarts)

    tok = jnp.broadcast_to(jnp.arange(T, dtype=jnp.int32)[None, :], (TOP_K, T))
    buf_tok = _invert_dest(dest_t.reshape(-1), tok.reshape(-1), P, T)
    xs = _gather_rows(x1p, buf_tok)
    yb = _expert_ffn(blk_e, nb_used, xs, wg, wu, wd, layer)
    yg = _gather_rows(yb, dest_t.reshape(-1))
    return yg, gw_t.T


def kernel(x, p, positions, fox_w_in, fox_b_f, fox_w_out, nsa_w_in, nsa_b_gate, nsa_pe_k, nsa_pe_v,
           nsa_cmp_k_w1, nsa_cmp_k_w2, nsa_cmp_v_w1, nsa_cmp_v_w2, nsa_w_out, moba_w_in, moba_w_out,
           ln1_g, ln1_b, router_w, router_b, exp_w_gate, exp_w_up, exp_w_down,
           sh_w_gate, sh_w_up, sh_w_down, ln2_g, ln2_b, ple_w_gate, ple_w_proj):
    B, S, D = x.shape
    T = B * S
    depth = p.shape[0]
    cosf, sinf = _rope_tables(positions)
    xt = x.reshape(T, D)
    xtb = xt.astype(BF16)
    for i in range(depth):
        kind, j = i % N_MIXERS, i // N_MIXERS
        if kind == 0:
            o_list = _fox_mixer(xt, xtb, B, S, fox_w_in[j], fox_b_f[j])
            w_out = fox_w_out[j]
        elif kind == 1:
            o_list = _nsa_mixer(xt, xtb, B, S, cosf, sinf, nsa_w_in[j], nsa_b_gate[j], nsa_pe_k[j], nsa_pe_v[j],
                                nsa_cmp_k_w1[j], nsa_cmp_k_w2[j], nsa_cmp_v_w1[j], nsa_cmp_v_w2[j])
            w_out = nsa_w_out[j]
        else:
            o_list = _moba_mixer(xtb, B, S, cosf, sinf, moba_w_in[j])
            w_out = moba_w_out[j]
        x1, x1p = _outproj_ln(o_list, w_out.astype(BF16), xt, ln1_g[i], ln1_b[i])
        yg, gw = _moe_dispatch(x1, x1p, router_w[i], router_b[i], exp_w_gate, exp_w_up, exp_w_down, i)
        xt, xtb = _post_moe(x1, yg, gw, p[i].reshape(T, -1), sh_w_gate[i].astype(BF16), sh_w_up[i].astype(BF16),
                            sh_w_down[i].astype(BF16), ln2_g[i], ln2_b[i],
                            ple_w_gate[i].astype(BF16), ple_w_proj[i].astype(BF16))
    return xt.reshape(B, S, D)
```

```python
import functools
import math

import jax
import jax.numpy as jnp
import numpy as np
from jax import lax
from jax.experimental import pallas as pl
from jax.experimental.pallas import tpu as pltpu
from jax.experimental.pallas import tpu_sc as plsc

D_MODEL = 1024
DEPTH = 4
N_HEADS = 16
HEAD_DIM = 64
ATTN_SCALE = HEAD_DIM ** -0.5
LOG2E = 1.4426950408889634
Q_SCALE = ATTN_SCALE * LOG2E
ROPE_THETA = 10000.0
N_MIXERS = 3

NSA_KV_GROUPS = 4
NSA_HEADS_PER_GROUP = N_HEADS // NSA_KV_GROUPS
NSA_CMP_LEN = 32
NSA_CMP_STRIDE = 16
NSA_SEL_LEN = 64
NSA_SEL_TOPN = 16
NSA_WINDOW = 512
NSA_FORCE_BONUS = 1e4

MOBA_BLOCK = 256
MOBA_TOPK = 3

N_EXPERTS = 64
EXPERT_DIM = 256
TOP_K = 8
N_GROUPS = 8
TOPK_GROUPS = 4
ROUTED_SCALE = 2.5
MOE_BLOCK = 256
EXPERT_SLOTS = 2

DN_ALPHA = (2 * DEPTH) ** 0.25
LN_EPS = 1e-5
NEG = -1e30

SC_CORES = 2
SC_SUBCORES = 16
SC_LANES = 16
SC_ROW_WORDS = 256
SC_WINDOW = 128
SC_INDEX_CHUNK = 4096
LANES = 128
SUBLANES = 8
ATTN_TILE = 512
PAIR_TILE = 1024
KV_TILE = 1024
ROW_TILE = 512
VMEM_LIMIT = 48 * 1024 * 1024
C_SPLIT = 3
ONES_ROWS = 16

F32 = jnp.float32
BF16 = jnp.bfloat16
HIGHEST = lax.Precision.HIGHEST


def _cparams(sem):
    return pltpu.CompilerParams(dimension_semantics=sem, vmem_limit_bytes=VMEM_LIMIT)


def _log2(n):
    assert n & (n - 1) == 0
    return n.bit_length() - 1


def _nt_dot(a, b, precision=None):
    return lax.dot_general(a, b, (((1,), (1,)), ((), ())), preferred_element_type=F32,
                           precision=precision)


def _split_bf16(a):
    hi = a.astype(BF16)
    return hi, (a - hi.astype(F32)).astype(BF16)


def _dot3(x, w, nt=False):
    dot = _nt_dot if nt else functools.partial(jnp.dot, preferred_element_type=F32)
    xh, xl = _split_bf16(x)
    wh, wl = _split_bf16(w)
    return dot(xh, wh) + dot(xl, wh) + dot(xh, wl)


def _topk_rows(work, n):
    rows = lax.broadcasted_iota(jnp.int32, work.shape, 0).astype(F32)
    for _ in range(n):
        m = jnp.max(work, axis=0, keepdims=True)
        idx = jnp.min(jnp.where(work == m, rows, float(work.shape[0])), axis=0, keepdims=True)
        work = jnp.where(rows == idx, -jnp.inf, work)
    return jnp.where(work == -jnp.inf, 1.0, 0.0)


def _proj_kernel(*refs, mode, precision, scaled):
    refs = list(refs)
    x_ref, w_ref = refs[:2]
    del refs[:2]
    scale_ref = refs.pop(0) if scaled else None
    if mode == "none":
        (o_ref,) = refs
    elif mode == "rope":
        cos_ref, sin_ref, r_ref = refs
    else:
        cos_ref, sin_ref, o_ref, r_ref = refs
    x = x_ref[...]
    w = w_ref[...]
    if precision == "split3":
        acc = _dot3(x, w)
    else:
        if x.dtype != w.dtype:
            x = x.astype(w.dtype)
        acc = jnp.dot(x, w, preferred_element_type=F32, precision=precision)
    if scaled:
        acc = acc * scale_ref[...]
    if mode in ("none", "both"):
        o_ref[...] = acc.astype(o_ref.dtype)
    if mode in ("rope", "both"):
        tn = acc.shape[1]
        rep = tn // LANES
        cosf = jnp.concatenate([cos_ref[...]] * rep, axis=1)
        sinf = jnp.concatenate([sin_ref[...]] * rep, axis=1)
        lane = lax.broadcasted_iota(jnp.int32, acc.shape, 1)
        first_half = (lane & (HEAD_DIM // 2)) == 0
        swapped = jnp.where(first_half, pltpu.roll(acc, tn - HEAD_DIM // 2, 1),
                            pltpu.roll(acc, HEAD_DIM // 2, 1))
        r_ref[...] = (acc * cosf + swapped * sinf).astype(r_ref.dtype)


def _proj(x, w, *, mode="none", cos=None, sin=None, out_dtype=BF16, tn=512, precision=None, col_scale=None):
    T, K = x.shape
    N = w.shape[1]
    tm = min(ROW_TILE, T)
    tn = min(tn, N)
    assert T % tm == 0 and N % tn == 0
    grid = (T // tm, N // tn)
    in_specs = [pl.BlockSpec((tm, K), lambda i, j: (i, 0)),
                pl.BlockSpec((K, tn), lambda i, j: (0, j))]
    args = [x, w]
    if col_scale is not None:
        in_specs.append(pl.BlockSpec((1, tn), lambda i, j: (0, j)))
        args.append(col_scale.reshape(1, N).astype(F32))
    if mode != "none":
        in_specs += [pl.BlockSpec((tm, LANES), lambda i, j: (i, 0))] * 2
        args += [cos, sin]
    o_spec = pl.BlockSpec((tm, tn), lambda i, j: (i, j))
    o_shape = jax.ShapeDtypeStruct((T, N), out_dtype)
    if mode == "both":
        out_specs, out_shape = [o_spec, o_spec], [o_shape, o_shape]
    else:
        out_specs, out_shape = o_spec, o_shape
    return pl.pallas_call(
        functools.partial(_proj_kernel, mode=mode, precision=precision, scaled=col_scale is not None),
        grid=grid, in_specs=in_specs, out_specs=out_specs, out_shape=out_shape,
        compiler_params=_cparams(("parallel", "parallel")), name=f"proj_{mode}")(*args)


def _proj_t_kernel(wt_ref, x_ref, o_ref, *, precision):
    wt = wt_ref[...]
    x = x_ref[...]
    if x.dtype != wt.dtype:
        x = x.astype(wt.dtype)
    o_ref[...] = _nt_dot(wt, x, precision).astype(o_ref.dtype)


def _proj_t(wt, x, *, out_dtype=BF16, precision=None):
    N, K = wt.shape
    T = x.shape[0]
    tm = min(ROW_TILE, T)
    tn = min(512, N)
    assert T % tm == 0 and N % tn == 0
    return pl.pallas_call(
        functools.partial(_proj_t_kernel, precision=precision),
        grid=(T // tm, N // tn),
        in_specs=[pl.BlockSpec((tn, K), lambda i, j: (j, 0)), pl.BlockSpec((tm, K), lambda i, j: (i, 0))],
        out_specs=pl.BlockSpec((tn, tm), lambda i, j: (j, i)),
        out_shape=jax.ShapeDtypeStruct((N, T), out_dtype),
        compiler_params=_cparams(("parallel", "parallel")), name="proj_t")(wt, x)


def _pack_rows(v):
    n = v.shape[1] // 2
    lo = pltpu.bitcast(v[:, :n].astype(BF16).astype(F32), jnp.int32)
    hi = pltpu.bitcast(v[:, n:].astype(BF16).astype(F32), jnp.int32)
    return (hi & jnp.int32(-65536)) | lax.shift_right_logical(lo, jnp.int32(16))


def _unpack_rows(w):
    lo = pltpu.bitcast(lax.shift_left(w, jnp.int32(16)), F32)
    hi = pltpu.bitcast(w & jnp.int32(-65536), F32)
    return jnp.concatenate([lo, hi], axis=1)


def _store_pieces(ref, v):
    w = _pack_rows(v)
    for j in range(ref.shape[0]):
        ref[j] = w[:, j * SC_ROW_WORDS:(j + 1) * SC_ROW_WORDS]


def _load_pieces(ref):
    return _unpack_rows(jnp.concatenate([ref[j] for j in range(ref.shape[0])], axis=1))


def _piece_spec(d, rows, index_map):
    return pl.BlockSpec((d // 2 // SC_ROW_WORDS, rows, SC_ROW_WORDS), index_map)


def _piece_shape(d, rows):
    return jax.ShapeDtypeStruct((d // 2 // SC_ROW_WORDS, rows, SC_ROW_WORDS), jnp.int32)


def _pad_cols(w, n):
    return jnp.pad(w, ((0, 0), (0, n - w.shape[1])))


def _layer_norm(z, g, b):
    mu = jnp.mean(z, axis=-1, keepdims=True)
    zc = z - mu
    var = jnp.mean(zc * zc, axis=-1, keepdims=True)
    return zc * lax.rsqrt(var + LN_EPS) * g + b


def _outproj_ln_kernel(*refs, n_o):
    o_refs = refs[:n_o]
    w_ref, x_ref, g_ref, b_ref, y_ref, yp_ref = refs[n_o:]
    if n_o == 1:
        o = o_refs[0][...]
    else:
        o = o_refs[0][...].astype(F32)
        for r in o_refs[1:]:
            o = o + r[...].astype(F32)
        o = o.astype(BF16)
    mix = jnp.dot(o, w_ref[...], preferred_element_type=F32)
    z = DN_ALPHA * x_ref[...] + mix
    y = _layer_norm(z, g_ref[...], b_ref[...])
    y_ref[...] = y
    _store_pieces(yp_ref, y)


def _outproj_ln(o_list, w, x, g, b):
    T, D = x.shape
    tm = min(ROW_TILE, T)
    row = pl.BlockSpec((tm, D), lambda i: (i, 0))
    vec = pl.BlockSpec((1, D), lambda i: (0, 0))
    return pl.pallas_call(
        functools.partial(_outproj_ln_kernel, n_o=len(o_list)),
        grid=(T // tm,),
        in_specs=[row] * len(o_list) + [pl.BlockSpec((D, D), lambda i: (0, 0)), row, vec, vec],
        out_specs=[row, _piece_spec(D, tm, lambda i: (0, i, 0))],
        out_shape=[jax.ShapeDtypeStruct((T, D), F32), _piece_shape(D, T)],
        compiler_params=_cparams(("parallel",)), name="outproj_ln")(
            *o_list, w, x, g.reshape(1, D), b.reshape(1, D))


def _flash_init(m_ref, acc_ref):
    m_ref[...] = jnp.full(m_ref.shape, NEG, F32)
    acc_ref[...] = jnp.zeros(acc_ref.shape, F32)


def _flash_step(sT, lhs, m_ref, acc_ref, h, cols=slice(None)):
    m_prev = m_ref[h, :, cols]
    m_new = jnp.maximum(m_prev, jnp.max(sT, axis=0, keepdims=True))
    p = jnp.exp2(sT - m_new)
    alpha = jnp.exp2(m_prev - m_new)
    acc_ref[h, :, cols] = alpha * acc_ref[h, :, cols] + jnp.dot(lhs, p.astype(BF16), preferred_element_type=F32)
    m_ref[h, :, cols] = m_new


def _causal_t(tk, tq, shift=0, strict_lower=False):
    key = lax.broadcasted_iota(jnp.int32, (tk, tq), 0)
    qry = lax.broadcasted_iota(jnp.int32, (tk, tq), 1)
    return (key > qry) if strict_lower else (key <= qry + shift)


def _causal_sweep(i, tq, tk, step):
    r = tk // tq
    assert tk == r * tq
    n_full = i >> _log2(r)

    def body(j, carry):
        step(pl.multiple_of(j * tk, tk), tk, None)
        return carry

    lax.fori_loop(0, n_full, body, 0)
    if r == 1:
        step(pl.multiple_of(i * tq, tq), tq, _causal_t(tq, tq))
    else:
        for rem in range(r):
            @pl.when(i - n_full * r == rem)
            def _(rem=rem):
                size = (rem + 1) * tq
                step(pl.multiple_of(n_full * tk, tk), size, _causal_t(size, tq, shift=rem * tq))


def _staggered(units, scores, update):
    s = [scores(units[0])]
    for n, u in enumerate(units):
        if n + 1 < len(units):
            s.append(scores(units[n + 1]))
        update(u, s[n])


def _head_units(n_heads, tq):
    w = min(ATTN_TILE, tq)
    return [(h, slice(c * w, (c + 1) * w)) for h in range(n_heads) for c in range(tq // w)]


def _pair_units(tq):
    return _head_units(2, tq)


def _head_of_pair(q_pair, which):
    lane = lax.broadcasted_iota(jnp.int32, q_pair.shape, 1)
    keep = (lane < HEAD_DIM) if which == 0 else (lane >= HEAD_DIM)
    return jnp.where(keep, q_pair, jnp.zeros_like(q_pair))


def _pair_rows(a_top, a_bot):
    row = lax.broadcasted_iota(jnp.int32, a_top.shape, 0)
    return jnp.where(row < HEAD_DIM, a_top, a_bot)


def _pair_lhs(vt, which):
    row = lax.broadcasted_iota(jnp.int32, vt.shape, 0)
    keep = (row < HEAD_DIM) if which == 0 else (row >= HEAD_DIM)
    return jnp.where(keep, vt, jnp.ones_like(vt))


def _pair_finish(acc_ref):
    a0, a1 = acc_ref[0], acc_ref[1]
    return _pair_rows(a0 / a0[HEAD_DIM:HEAD_DIM + 1, :], a1 / a1[0:1, :])


def _fox_gate_kernel(fl_ref, bf_ref, tri_ref, ex_ref, cp_ref, carry_ref):
    @pl.when(pl.program_id(1) == 0)
    def _():
        carry_ref[...] = jnp.zeros(carry_ref.shape, F32)

    z = fl_ref[...] + bf_ref[...]
    log_f = jnp.minimum(z, 0.0) - jnp.log1p(jnp.exp(-jnp.abs(z)))
    c = jnp.dot(tri_ref[...], log_f, preferred_element_type=F32, precision=HIGHEST) + carry_ref[...]
    carry_ref[...] = c[-1:, :]
    out = jnp.zeros(c.shape, F32)
    rem = c * LOG2E
    for piece in range(C_SPLIT):
        part = rem.astype(BF16)
        rem = rem - part.astype(F32)
        out = out + jnp.dot(part, ex_ref[piece], preferred_element_type=F32)
    cp_ref[...] = out.astype(BF16)


def _fox_gate(fl, b_f, B, S):
    T = fl.shape[0]
    tm = min(ROW_TILE, S)
    ns = S // tm
    tri = jnp.asarray(np.tril(np.ones((tm, tm), np.float32)))
    ex = np.zeros((C_SPLIT, LANES, LANES), np.float32)
    for h in range(N_HEADS):
        for j in range(C_SPLIT):
            ex[j, h, C_SPLIT * h + j] = 1.0
    bf = _pad_cols(b_f.reshape(1, -1), LANES)
    return pl.pallas_call(
        _fox_gate_kernel, grid=(B, ns),
        in_specs=[pl.BlockSpec((tm, LANES), lambda b, s: (b * ns + s, 0)),
                  pl.BlockSpec((1, LANES), lambda b, s: (0, 0)),
                  pl.BlockSpec((tm, tm), lambda b, s: (0, 0)),
                  pl.BlockSpec((C_SPLIT, LANES, LANES), lambda b, s: (0, 0, 0))],
        out_specs=pl.BlockSpec((tm, LANES), lambda b, s: (b * ns + s, 0)),
        out_shape=jax.ShapeDtypeStruct((T, LANES), BF16),
        scratch_shapes=[pltpu.VMEM((1, LANES), F32)],
        compiler_params=_cparams(("parallel", "arbitrary")), name="fox_gate")(
            fl, bf, tri, jnp.asarray(ex, BF16))


def _fox_kernel(q_ref, k_ref, cp_ref, vt_ref, o_ref, m_ref, acc_ref, *, tq, tk):
    hp = pl.program_id(1)
    i = pl.program_id(2)
    _flash_init(m_ref, acc_ref)
    q = q_ref[...]
    lane = lax.broadcasted_iota(jnp.int32, q.shape, 1)
    qa = []
    for h in range(2):
        lo = C_SPLIT * (2 * hp + h)
        minus_one = jnp.where((lane >= lo) & (lane < lo + C_SPLIT), -1.0, 0.0).astype(q.dtype)
        qa.append(jnp.concatenate([_head_of_pair(q, h), minus_one], axis=1))

    def step(off, size, mask):
        k = jnp.concatenate([k_ref[pl.ds(off, size), :], cp_ref[pl.ds(off, size), :]], axis=1)
        vt = vt_ref[:, pl.ds(off, size)]

        lhs = [_pair_lhs(vt, h) for h in range(2)]

        def scores(u):
            sT = _nt_dot(k, qa[u[0]][u[1], :])
            return sT if mask is None else jnp.where(mask[:, u[1]], sT, NEG)

        _staggered(_pair_units(tq), scores,
                   lambda u, sT: _flash_step(sT, lhs[u[0]], m_ref, acc_ref, u[0], u[1]))

    _causal_sweep(i, tq, tk, step)
    o_ref[...] = _pair_finish(acc_ref).T.astype(o_ref.dtype)


def _fox_attention(qk, cp, vt, B, S):
    T = qk.shape[0]
    tq = min(PAIR_TILE, S)
    tk = min(KV_TILE, S)
    nq = S // tq
    npair = N_HEADS // 2
    ncol = D_MODEL // LANES
    return pl.pallas_call(
        functools.partial(_fox_kernel, tq=tq, tk=tk),
        grid=(B, npair, nq),
        in_specs=[pl.BlockSpec((tq, LANES), lambda b, hp, i: (b * nq + i, hp)),
                  pl.BlockSpec((S, LANES), lambda b, hp, i: (b, ncol + hp)),
                  pl.BlockSpec((S, LANES), lambda b, hp, i: (b, 0)),
                  pl.BlockSpec((LANES, S), lambda b, hp, i: (hp, b))],
        out_specs=pl.BlockSpec((tq, LANES), lambda b, hp, i: (b * nq + i, hp)),
        out_shape=jax.ShapeDtypeStruct((T, D_MODEL), BF16),
        scratch_shapes=[pltpu.VMEM((2, 1, tq), F32), pltpu.VMEM((2, LANES, tq), F32)],
        compiler_params=_cparams(("parallel", "parallel", "arbitrary")), name="fox_attn")(
            qk, qk, cp, vt)


def _gate_expand(gl_ref, bg_ref, ex_ref):
    sig = jax.nn.sigmoid(gl_ref[...] + bg_ref[...])
    hi = sig.astype(BF16)
    lo = (sig - hi.astype(F32)).astype(BF16)
    ex = ex_ref[...]
    return jnp.dot(hi, ex, preferred_element_type=F32) + jnp.dot(lo, ex, preferred_element_type=F32)


def _nsa_compress_kernel(f_ref, pe_ref, w1_ref, w2_ref, o_ref):
    blk = (f_ref[...].astype(F32) + pe_ref[...]).astype(BF16)
    h = jax.nn.gelu(jnp.dot(blk, w1_ref[...], preferred_element_type=F32))
    o_ref[...] = jnp.dot(h.astype(BF16), w2_ref[...], preferred_element_type=F32)


def _nsa_compress(flat, pe_flat, w1, w2p):
    M, K = flat.shape
    tm = min(ROW_TILE, M)
    Hc = w1.shape[1]
    return pl.pallas_call(
        _nsa_compress_kernel, grid=(M // tm,),
        in_specs=[pl.BlockSpec((tm, K), lambda i: (i, 0)), pl.BlockSpec((1, K), lambda i: (0, 0)),
                  pl.BlockSpec((K, Hc), lambda i: (0, 0)), pl.BlockSpec((Hc, LANES), lambda i: (0, 0))],
        out_specs=pl.BlockSpec((tm, LANES), lambda i: (i, 0)),
        out_shape=jax.ShapeDtypeStruct((M, LANES), F32),
        compiler_params=_cparams(("parallel",)), name="nsa_compress")(flat, pe_flat, w1, w2p)


def _nsa_cmp_kernel(q_ref, kk_ref, vt_ref, ovt_ref, gl_ref, bg_ref, ex_ref, o_ref, sn_ref, *, tq, ncp, nselp):
    i = pl.program_id(2)
    q = q_ref[...]
    kk = kk_ref[...]
    vt = vt_ref[...]
    t = i * tq + lax.broadcasted_iota(jnp.int32, (ncp, tq), 1)
    n = lax.broadcasted_iota(jnp.int32, (ncp, tq), 0)
    valid = n * NSA_CMP_STRIDE + (NSA_CMP_LEN - 1) <= t
    psum = jnp.zeros((ncp, tq), F32)
    outs = []
    for hg in range(NSA_HEADS_PER_GROUP):
        qh = _head_of_pair(q[:, LANES * (hg // 2):LANES * (hg // 2 + 1)], hg % 2)
        sT = jnp.where(valid, _nt_dot(kk, qh), NEG)
        m = jnp.max(sT, axis=0, keepdims=True)
        e = jnp.where(valid, jnp.exp2(sT - m), 0.0)
        p = e / jnp.maximum(jnp.sum(e, axis=0, keepdims=True), 1e-30)
        psum = psum + p
        outs.append(jnp.dot(vt, p.astype(BF16), preferred_element_type=F32))
    o = jnp.concatenate(outs, axis=0).T
    o_ref[...] = (o * _gate_expand(gl_ref, bg_ref, ex_ref)).astype(o_ref.dtype)
    hi = psum.astype(BF16)
    lo = (psum - hi.astype(F32)).astype(BF16)
    ovt = ovt_ref[...]
    imp = jnp.dot(ovt, hi, preferred_element_type=F32) + jnp.dot(ovt, lo, preferred_element_type=F32)
    jblk = lax.broadcasted_iota(jnp.int32, (nselp, tq), 0)
    cur = (i * tq + lax.broadcasted_iota(jnp.int32, (nselp, tq), 1)) >> _log2(NSA_SEL_LEN)
    forced = (jblk == 0) | (jblk == cur) | (jblk == cur - 1)
    pri = jnp.where(forced, imp + NSA_FORCE_BONUS, imp)
    past = jblk <= cur
    pri = jnp.where(past, pri, NEG)
    chosen = _topk_rows(pri, NSA_SEL_TOPN)
    sn = jnp.where((chosen > 0.0) & past, 0.0, NEG)
    sn_ref[...] = sn.T.astype(sn_ref.dtype)


def _nsa_cmp_branch(q_c, kk_c, vt_c, overlap_t, gl, bg, ex, B, S):
    T = q_c.shape[0]
    G = NSA_KV_GROUPS
    tq = min(ATTN_TILE, S)
    nq = S // tq
    ncp = kk_c.shape[2]
    nselp = overlap_t.shape[0]
    return pl.pallas_call(
        functools.partial(_nsa_cmp_kernel, tq=tq, ncp=ncp, nselp=nselp),
        grid=(B, G, nq),
        in_specs=[pl.BlockSpec((tq, 2 * LANES), lambda b, g, i: (b * nq + i, g)),
                  pl.BlockSpec((None, None, ncp, LANES), lambda b, g, i: (b, g, 0, 0)),
                  pl.BlockSpec((None, None, HEAD_DIM, ncp), lambda b, g, i: (b, g, 0, 0)),
                  pl.BlockSpec((nselp, ncp), lambda b, g, i: (0, 0)),
                  pl.BlockSpec((tq, LANES), lambda b, g, i: (b * nq + i, 0)),
                  pl.BlockSpec((1, LANES), lambda b, g, i: (0, 0)),
                  pl.BlockSpec((None, LANES, 2 * LANES), lambda b, g, i: (g, 0, 0))],
        out_specs=[pl.BlockSpec((tq, 2 * LANES), lambda b, g, i: (b * nq + i, g)),
                   pl.BlockSpec((None, None, tq, nselp), lambda b, g, i: (b, g, i, 0))],
        out_shape=[jax.ShapeDtypeStruct((T, D_MODEL), BF16),
                   jax.ShapeDtypeStruct((B, G, S, nselp), BF16)],
        compiler_params=_cparams(("parallel", "parallel", "parallel")), name="nsa_cmp")(
            q_c, kk_c, vt_c, overlap_t, gl, bg, ex)


def _nsa_kv_kernel(*refs, mode, tq, tk):
    if mode == "sel":
        q_ref, kk_ref, vt_ref, oh_ref, sn_ref, gl_ref, bg_ref, ex_ref, o_ref, m_ref, acc_ref = refs
    else:
        q_ref, kk_ref, vt_ref, gl_ref, bg_ref, ex_ref, o_ref, m_ref, acc_ref = refs
    i = pl.program_id(2)
    _flash_init(m_ref, acc_ref)
    q = q_ref[...]
    qh = []
    for hg in range(NSA_HEADS_PER_GROUP):
        qq = _head_of_pair(q[:, LANES * (hg // 2):LANES * (hg // 2 + 1)], hg % 2)
        if mode == "sel":
            qq = jnp.concatenate([qq, sn_ref[...]], axis=1)
        qh.append(qq)

    def step(off, size, mask):
        k = kk_ref[pl.ds(off, size), :]
        vt = vt_ref[:, pl.ds(off, size)]
        lhs = jnp.concatenate([vt, jnp.ones((ONES_ROWS, size), vt.dtype)], axis=0)
        if mode == "sel":
            k = jnp.concatenate([k, oh_ref[pl.ds(off, size), :]], axis=1)

        def scores(u):
            sT = _nt_dot(k, qh[u[0]][u[1], :])
            return sT if mask is None else jnp.where(mask[:, u[1]], sT, NEG)

        _staggered(_head_units(NSA_HEADS_PER_GROUP, tq), scores,
                   lambda u, sT: _flash_step(sT, lhs, m_ref, acc_ref, u[0], u[1]))

    if mode == "sel":
        _causal_sweep(i, tq, tk, step)
    else:
        @pl.when(i > 0)
        def _():
            step(pl.multiple_of((i - 1) * tq, tq), tq, _causal_t(tq, tq, strict_lower=True))
        step(pl.multiple_of(i * tq, tq), tq, _causal_t(tq, tq))
    outs = []
    for hg in range(NSA_HEADS_PER_GROUP):
        a = acc_ref[hg]
        outs.append(a[:HEAD_DIM] / a[HEAD_DIM:HEAD_DIM + 1, :])
    o_ref[...] = (jnp.concatenate(outs, axis=0).T * _gate_expand(gl_ref, bg_ref, ex_ref)).astype(o_ref.dtype)


def _nsa_kv_branch(mode, q_r, kk, vt, gl, bg, ex, B, S, onehot=None, selneg=None):
    T = q_r.shape[0]
    G = NSA_KV_GROUPS
    tq = min(PAIR_TILE if mode == "sel" else ATTN_TILE, S)
    tk = min(KV_TILE, S)
    assert NSA_WINDOW == tq or mode == "sel"
    nq = S // tq
    in_specs = [pl.BlockSpec((tq, 2 * LANES), lambda b, g, i: (b * nq + i, g)),
                pl.BlockSpec((None, None, S, LANES), lambda b, g, i: (b, g, 0, 0)),
                pl.BlockSpec((HEAD_DIM, S), lambda b, g, i: (g, b))]
    args = [q_r, kk, vt]
    if mode == "sel":
        in_specs += [pl.BlockSpec((S, LANES), lambda b, g, i: (0, 0)),
                     pl.BlockSpec((None, None, tq, LANES), lambda b, g, i: (b, g, i, 0))]
        args += [onehot, selneg]
    in_specs += [pl.BlockSpec((tq, LANES), lambda b, g, i: (b * nq + i, 0)),
                 pl.BlockSpec((1, LANES), lambda b, g, i: (0, 0)),
                 pl.BlockSpec((None, LANES, 2 * LANES), lambda b, g, i: (g, 0, 0))]
    args += [gl, bg, ex]
    nh = NSA_HEADS_PER_GROUP
    return pl.pallas_call(
        functools.partial(_nsa_kv_kernel, mode=mode, tq=tq, tk=tk),
        grid=(B, G, nq), in_specs=in_specs,
        out_specs=pl.BlockSpec((tq, 2 * LANES), lambda b, g, i: (b * nq + i, g)),
        out_shape=jax.ShapeDtypeStruct((T, D_MODEL), BF16),
        scratch_shapes=[pltpu.VMEM((nh, 1, tq), F32), pltpu.VMEM((nh, HEAD_DIM + ONES_ROWS, tq), F32)],
        compiler_params=_cparams(("parallel", "parallel", "arbitrary")), name=f"nsa_{mode}")(*args)


def _kmean_kernel(k_ref, o_ref, *, nblk):
    k = k_ref[...].astype(F32)
    o_ref[...] = jnp.mean(k.reshape(nblk, MOBA_BLOCK, k.shape[1]), axis=1)


def _moba_kmean(k_rot):
    T, D = k_rot.shape
    nblk = 8
    rows = nblk * MOBA_BLOCK
    assert T % rows == 0
    return pl.pallas_call(
        functools.partial(_kmean_kernel, nblk=nblk), grid=(T // rows,),
        in_specs=[pl.BlockSpec((rows, D), lambda i: (i, 0))],
        out_specs=pl.BlockSpec((nblk, D), lambda i: (i, 0)),
        out_shape=jax.ShapeDtypeStruct((T // MOBA_BLOCK, D), F32),
        compiler_params=_cparams(("parallel",)), name="moba_kmean")(k_rot)


def _moba_select_kernel(q_ref, r_ref, sn_ref, *, tq, nblk):
    i = pl.program_id(2)
    gsT = _nt_dot(r_ref[...], q_ref[...].astype(F32), HIGHEST)
    jblk = lax.broadcasted_iota(jnp.int32, (nblk, tq), 0)
    cb = (i * tq + lax.broadcasted_iota(jnp.int32, (nblk, tq), 1)) >> _log2(MOBA_BLOCK)
    past = jblk < cb
    parts = []
    for h in range(2):
        gs = jnp.where(past, gsT[h * nblk:(h + 1) * nblk, :], NEG)
        chosen = _topk_rows(gs, min(MOBA_TOPK, nblk))
        parts.append(jnp.where(((chosen > 0.0) & past) | (jblk == cb), 0.0, NEG))
    parts.append(jnp.zeros((LANES - 2 * nblk, tq), F32))
    sn_ref[...] = jnp.concatenate(parts, axis=0).T.astype(sn_ref.dtype)


def _moba_select(q_rot, rmat_t, B, S):
    tq = min(ATTN_TILE, S)
    nq = S // tq
    npair = N_HEADS // 2
    nblk = S // MOBA_BLOCK
    return pl.pallas_call(
        functools.partial(_moba_select_kernel, tq=tq, nblk=nblk), grid=(B, npair, nq),
        in_specs=[pl.BlockSpec((tq, LANES), lambda b, hp, i: (b * nq + i, hp)),
                  pl.BlockSpec((None, None, LANES, LANES), lambda b, hp, i: (b, hp, 0, 0))],
        out_specs=pl.BlockSpec((None, None, tq, LANES), lambda b, hp, i: (b, hp, i, 0)),
        out_shape=jax.ShapeDtypeStruct((B, npair, S, LANES), BF16),
        compiler_params=_cparams(("parallel", "parallel", "parallel")), name="moba_select")(q_rot, rmat_t)


def _moba_kernel(q_ref, k_ref, vt_ref, oh_ref, sn_ref, o_ref, m_ref, acc_ref, *, tq, tk, nblk):
    i = pl.program_id(2)
    _flash_init(m_ref, acc_ref)
    q = q_ref[...]
    sn = sn_ref[...]
    lane = lax.broadcasted_iota(jnp.int32, sn.shape, 1)
    qa = []
    for h in range(2):
        mine = (lane >= h * nblk) & (lane < (h + 1) * nblk)
        qa.append(jnp.concatenate([_head_of_pair(q, h), jnp.where(mine, sn, jnp.zeros_like(sn))], axis=1))

    def step(off, size, mask):
        k = jnp.concatenate([k_ref[pl.ds(off, size), :], oh_ref[pl.ds(off, size), :]], axis=1)
        vt = vt_ref[:, pl.ds(off, size)]

        lhs = [_pair_lhs(vt, h) for h in range(2)]

        def scores(u):
            sT = _nt_dot(k, qa[u[0]][u[1], :])
            return sT if mask is None else jnp.where(mask[:, u[1]], sT, NEG)

        _staggered(_pair_units(tq), scores,
                   lambda u, sT: _flash_step(sT, lhs[u[0]], m_ref, acc_ref, u[0], u[1]))

    _causal_sweep(i, tq, tk, step)
    o_ref[...] = _pair_finish(acc_ref).T.astype(o_ref.dtype)


def _moba_attention(qk_rot, vt, onehot2, selneg, B, S):
    T = qk_rot.shape[0]
    tq = min(PAIR_TILE, S)
    tk = min(KV_TILE, S)
    nq = S // tq
    npair = N_HEADS // 2
    ncol = D_MODEL // LANES
    nblk = S // MOBA_BLOCK
    return pl.pallas_call(
        functools.partial(_moba_kernel, tq=tq, tk=tk, nblk=nblk),
        grid=(B, npair, nq),
        in_specs=[pl.BlockSpec((tq, LANES), lambda b, hp, i: (b * nq + i, hp)),
                  pl.BlockSpec((S, LANES), lambda b, hp, i: (b, ncol + hp)),
                  pl.BlockSpec((LANES, S), lambda b, hp, i: (hp, b)),
                  pl.BlockSpec((S, LANES), lambda b, hp, i: (0, 0)),
                  pl.BlockSpec((None, None, tq, LANES), lambda b, hp, i: (b, hp, i, 0))],
        out_specs=pl.BlockSpec((tq, LANES), lambda b, hp, i: (b * nq + i, hp)),
        out_shape=jax.ShapeDtypeStruct((T, D_MODEL), BF16),
        scratch_shapes=[pltpu.VMEM((2, 1, tq), F32), pltpu.VMEM((2, LANES, tq), F32)],
        compiler_params=_cparams(("parallel", "parallel", "arbitrary")), name="moba_attn")(
            qk_rot, qk_rot, vt, onehot2, selneg)


def _router_kernel(wt_ref, x_ref, rb_ref, e_ref, g_ref):
    tm = x_ref.shape[0]
    gsz = N_EXPERTS // N_GROUPS
    scores = jax.nn.sigmoid(_dot3(wt_ref[...], x_ref[...], nt=True))
    biased = scores + rb_ref[...]
    member = lax.broadcasted_iota(jnp.int32, (gsz, tm), 0).astype(F32)
    gscore = []
    for g in range(N_GROUPS):
        v = biased[g * gsz:(g + 1) * gsz, :]
        m1 = jnp.max(v, axis=0, keepdims=True)
        i1 = jnp.min(jnp.where(v == m1, member, float(gsz)), axis=0, keepdims=True)
        m2 = jnp.max(jnp.where(member == i1, -jnp.inf, v), axis=0, keepdims=True)
        gscore.append(m1 + m2)
    gsel = _topk_rows(jnp.concatenate(gscore, axis=0), TOPK_GROUPS)
    emask = jnp.concatenate([jnp.broadcast_to(gsel[g:g + 1, :], (gsz, tm)) for g in range(N_GROUPS)], axis=0)
    work = jnp.where(emask > 0.0, biased, NEG)
    erow = lax.broadcasted_iota(jnp.int32, (N_EXPERTS, tm), 0).astype(F32)
    idxs, vals = [], []
    for _ in range(TOP_K):
        m = jnp.max(work, axis=0, keepdims=True)
        idx = jnp.min(jnp.where(work == m, erow, float(N_EXPERTS)), axis=0, keepdims=True)
        pick = erow == idx
        idxs.append(idx)
        vals.append(jnp.sum(jnp.where(pick, scores, 0.0), axis=0, keepdims=True))
        work = jnp.where(pick, -jnp.inf, work)
    gw = jnp.concatenate(vals, axis=0)
    e_ref[...] = jnp.concatenate(idxs, axis=0).astype(jnp.int32)
    g_ref[...] = gw / jnp.sum(gw, axis=0, keepdims=True) * ROUTED_SCALE


def _router(x1, router_w, router_b):
    T, D = x1.shape
    tm = min(ROW_TILE, T)
    return pl.pallas_call(
        _router_kernel, grid=(T // tm,),
        in_specs=[pl.BlockSpec((N_EXPERTS, D), lambda i: (0, 0)), pl.BlockSpec((tm, D), lambda i: (i, 0)),
                  pl.BlockSpec((N_EXPERTS, 1), lambda i: (0, 0))],
        out_specs=[pl.BlockSpec((TOP_K, tm), lambda i: (0, i))] * 2,
        out_shape=[jax.ShapeDtypeStruct((TOP_K, T), jnp.int32), jax.ShapeDtypeStruct((TOP_K, T), F32)],
        compiler_params=_cparams(("parallel",)), name="moe_router")(
            router_w.T, x1, router_b.reshape(N_EXPERTS, 1))


def _rank_kernel(e_ref, tri_ref, rank_ref, cnt_ref, carry_ref):
    @pl.when(pl.program_id(0) == 0)
    def _():
        carry_ref[...] = jnp.zeros(carry_ref.shape, F32)

    tm = e_ref.shape[1]
    erow = lax.broadcasted_iota(jnp.int32, (N_EXPERTS, tm), 0)
    tri = tri_ref[...]
    base = carry_ref[...]
    ranks = []
    for k in range(TOP_K):
        oh = erow == e_ref[k:k + 1, :]
        ohb = jnp.where(oh, 1.0, 0.0).astype(BF16)
        incl = jnp.dot(ohb, tri, preferred_element_type=F32)
        ranks.append(jnp.sum(jnp.where(oh, base + incl - 1.0, 0.0), axis=0, keepdims=True))
        base = base + incl[:, tm - 1:tm]
    carry_ref[...] = base
    rank_ref[...] = jnp.concatenate(ranks, axis=0).astype(jnp.int32)
    cnt_ref[...] = jnp.broadcast_to(base, cnt_ref.shape)


def _expert_ranks(eidx_t):
    K, T = eidx_t.shape
    tm = min(ROW_TILE, T)
    tri = jnp.asarray(np.triu(np.ones((tm, tm), np.float32)), BF16)
    return pl.pallas_call(
        _rank_kernel, grid=(T // tm,),
        in_specs=[pl.BlockSpec((K, tm), lambda i: (0, i)), pl.BlockSpec((tm, tm), lambda i: (0, 0))],
        out_specs=[pl.BlockSpec((K, tm), lambda i: (0, i)), pl.BlockSpec((N_EXPERTS, LANES), lambda i: (0, 0))],
        out_shape=[jax.ShapeDtypeStruct((K, T), jnp.int32), jax.ShapeDtypeStruct((N_EXPERTS, LANES), F32)],
        scratch_shapes=[pltpu.VMEM((N_EXPERTS, 1), F32)],
        compiler_params=_cparams(("arbitrary",)), name="moe_rank")(eidx_t, tri)


def _dest_kernel(e_ref, rank_ref, ps_ref, d_ref):
    tm = e_ref.shape[1]
    erow = lax.broadcasted_iota(jnp.int32, (N_EXPERTS, tm), 0)
    ps = ps_ref[...]
    rows = []
    for k in range(TOP_K):
        oh = erow == e_ref[k:k + 1, :]
        rows.append(jnp.sum(jnp.where(oh, ps, 0.0), axis=0, keepdims=True))
    d_ref[...] = jnp.concatenate(rows, axis=0).astype(jnp.int32) + rank_ref[...]


def _expert_dest(eidx_t, rank_t, pstarts):
    K, T = eidx_t.shape
    tm = min(ROW_TILE, T)
    blk = pl.BlockSpec((K, tm), lambda i: (0, i))
    return pl.pallas_call(
        _dest_kernel, grid=(T // tm,),
        in_specs=[blk, blk, pl.BlockSpec((N_EXPERTS, 1), lambda i: (0, 0))],
        out_specs=blk, out_shape=jax.ShapeDtypeStruct((K, T), jnp.int32),
        compiler_params=_cparams(("parallel",)), name="moe_dest")(
            eidx_t, rank_t, pstarts.astype(F32).reshape(N_EXPERTS, 1))


def _expert_kernel(be_ref, nb_ref, x_ref, *refs):
    n = EXPERT_SLOTS
    w_refs, (o_ref, wgu_ref, wdb_ref) = refs[:3 * n], refs[3 * n:]
    s = pl.program_id(0)
    E = wdb_ref.shape[1]
    R = MOE_BLOCK

    for j in range(n):
        b = s * n + j

        @pl.when((s == 0) | (be_ref[b] != be_ref[jnp.maximum(b - n, 0)]))
        def _(j=j):
            wg_ref, wu_ref, wd_ref = w_refs[3 * j:3 * j + 3]
            wgu_ref[j, :, :E] = wg_ref[...].astype(BF16)
            wgu_ref[j, :, E:] = wu_ref[...].astype(BF16)
            wdb_ref[j] = wd_ref[...].astype(BF16)

    @pl.when(s * n < nb_ref[0])
    def _():
        x = _load_pieces(x_ref).astype(BF16)
        ys = []

        def gate_up(j):
            return jnp.dot(x[j * R:(j + 1) * R], wgu_ref[j], preferred_element_type=F32)

        def down(j, gu):
            h = jax.nn.silu(gu[:, :E]) * gu[:, E:]
            ys.append(jnp.dot(h.astype(BF16), wdb_ref[j], preferred_element_type=F32))

        _staggered(list(range(n)), gate_up, down)
        _store_pieces(o_ref, jnp.concatenate(ys, axis=0))

    @pl.when(s * n >= nb_ref[0])
    def _():
        o_ref[...] = jnp.zeros(o_ref.shape, o_ref.dtype)


def _expert_ffn(blk_e, nb_used, xs, wg, wu, wd, layer):
    P = xs.shape[1]
    D = wg.shape[2]
    n = EXPERT_SLOTS
    rows = n * MOE_BLOCK
    assert P % rows == 0
    E = EXPERT_DIM
    w_specs = []
    for j in range(n):
        pick = functools.partial(lambda s, be, nb, j: (layer, be[s * n + j], 0, 0), j=j)
        w_specs += [pl.BlockSpec((None, None, D, E), pick), pl.BlockSpec((None, None, D, E), pick),
                    pl.BlockSpec((None, None, E, D), pick)]
    grid_spec = pltpu.PrefetchScalarGridSpec(
        num_scalar_prefetch=2, grid=(P // rows,),
        in_specs=[_piece_spec(D, rows, lambda s, be, nb: (0, s, 0))] + w_specs,
        out_specs=_piece_spec(D, rows, lambda s, be, nb: (0, s, 0)),
        scratch_shapes=[pltpu.VMEM((n, D, 2 * E), BF16), pltpu.VMEM((n, E, D), BF16)])
    return pl.pallas_call(
        _expert_kernel, grid_spec=grid_spec, out_shape=_piece_shape(D, P),
        compiler_params=_cparams(("arbitrary",)), name="moe_experts")(
            blk_e, nb_used, xs, *([wg, wu, wd] * n))


def _post_moe_kernel(x_ref, *refs):
    y_refs = refs[:TOP_K]
    gw_ref, p_ref, sg_ref, su_ref, sd_ref, g_ref, b_ref, wg_ref, wp_ref, o_ref, ob_ref = refs[TOP_K:]
    x = x_ref[...]
    xb = x.astype(BF16)
    h = jax.nn.silu(jnp.dot(xb, sg_ref[...], preferred_element_type=F32)) * jnp.dot(
        xb, su_ref[...], preferred_element_type=F32)
    ffn = jnp.dot(h.astype(BF16), sd_ref[...], preferred_element_type=F32)
    gw = gw_ref[...]
    for k in range(TOP_K):
        ffn = ffn + gw[:, k:k + 1] * _load_pieces(y_refs[k])
    z = DN_ALPHA * x + ffn
    x2 = _layer_norm(z, g_ref[...], b_ref[...])
    gate = jax.nn.sigmoid(jnp.dot(x2.astype(BF16), wg_ref[...], preferred_element_type=F32))
    proj = jnp.dot(p_ref[...].astype(BF16), wp_ref[...], preferred_element_type=F32)
    out = x2 + gate * proj
    o_ref[...] = out
    ob_ref[...] = out.astype(BF16)


def _post_moe(x1, yg, gw, p, sg, su, sd, g, b, wgate, wproj):
    T, D = x1.shape
    tm = min(ROW_TILE // 2, T)
    nt = T // tm
    PD = p.shape[1]
    SD = sg.shape[1]
    row = pl.BlockSpec((tm, D), lambda i: (i, 0))
    vec = pl.BlockSpec((1, D), lambda i: (0, 0))
    full = lambda r, c: pl.BlockSpec((r, c), lambda i: (0, 0))
    return pl.pallas_call(
        _post_moe_kernel, grid=(T // tm,),
        in_specs=[row] + [_piece_spec(D, tm, functools.partial(lambda i, k: (0, k * nt + i, 0), k=k))
                          for k in range(TOP_K)]
                 + [pl.BlockSpec((tm, TOP_K), lambda i: (i, 0)),
                    pl.BlockSpec((tm, PD), lambda i: (i, 0)), full(D, SD), full(D, SD),
                  full(SD, D), vec, vec, full(D, D), full(PD, D)],
        out_specs=[row, row],
        out_shape=[jax.ShapeDtypeStruct((T, D), F32), jax.ShapeDtypeStruct((T, D), BF16)],
        compiler_params=_cparams(("parallel",)), name="post_moe")(
            x1, *([yg] * TOP_K), gw, p, sg, su, sd, g.reshape(1, D), b.reshape(1, D), wgate, wproj)


def _shared_ple_kernel(x_ref, p_ref, sg_ref, su_ref, sd_ref, wp_ref, s_ref, j_ref):
    xb = x_ref[...].astype(BF16)
    h = jax.nn.silu(jnp.dot(xb, sg_ref[...], preferred_element_type=F32)) * jnp.dot(
        xb, su_ref[...], preferred_element_type=F32)
    s_ref[...] = jnp.dot(h.astype(BF16), sd_ref[...], preferred_element_type=F32)
    j_ref[...] = jnp.dot(p_ref[...].astype(BF16), wp_ref[...], preferred_element_type=F32)


def _shared_ple(x1, p, sg, su, sd, wproj):
    T, D = x1.shape
    tm = min(ROW_TILE, T)
    PD, SD = p.shape[1], sg.shape[1]
    row = pl.BlockSpec((tm, D), lambda i: (i, 0))
    full = lambda r, c: pl.BlockSpec((r, c), lambda i: (0, 0))
    return pl.pallas_call(
        _shared_ple_kernel, grid=(T // tm,),
        in_specs=[row, pl.BlockSpec((tm, PD), lambda i: (i, 0)), full(D, SD), full(D, SD), full(SD, D), full(PD, D)],
        out_specs=[row, row], out_shape=[jax.ShapeDtypeStruct((T, D), F32)] * 2,
        compiler_params=_cparams(("parallel",)), name="shared_ple")(x1, p, sg, su, sd, wproj)


def _combine_kernel(x_ref, *refs):
    y_refs = refs[:TOP_K]
    gw_ref, s_ref, j_ref, g_ref, b_ref, wg_ref, o_ref, ob_ref = refs[TOP_K:]
    x = x_ref[...]
    ffn = s_ref[...]
    gw = gw_ref[...]
    for k in range(TOP_K):
        ffn = ffn + gw[:, k:k + 1] * _load_pieces(y_refs[k])
    x2 = _layer_norm(DN_ALPHA * x + ffn, g_ref[...], b_ref[...])
    gate = jax.nn.sigmoid(jnp.dot(x2.astype(BF16), wg_ref[...], preferred_element_type=F32))
    out = x2 + gate * j_ref[...]
    o_ref[...] = out
    ob_ref[...] = out.astype(BF16)


def _combine(x1, yg, gw, shared, proj, g, b, wgate):
    T, D = x1.shape
    tm = min(ROW_TILE // 2, T)
    nt = T // tm
    row = pl.BlockSpec((tm, D), lambda i: (i, 0))
    vec = pl.BlockSpec((1, D), lambda i: (0, 0))
    return pl.pallas_call(
        _combine_kernel, grid=(T // tm,),
        in_specs=[row] + [_piece_spec(D, tm, functools.partial(lambda i, k: (0, k * nt + i, 0), k=k))
                          for k in range(TOP_K)]
                 + [pl.BlockSpec((tm, TOP_K), lambda i: (i, 0)), row, row, vec, vec,
                    pl.BlockSpec((D, D), lambda i: (0, 0))],
        out_specs=[row, row],
        out_shape=[jax.ShapeDtypeStruct((T, D), F32), jax.ShapeDtypeStruct((T, D), BF16)],
        compiler_params=_cparams(("parallel",)), name="moe_combine")(
            x1, *([yg] * TOP_K), gw, shared, proj, g.reshape(1, D), b.reshape(1, D), wgate)


def _rope_tables(positions):
    inv = 1.0 / (ROPE_THETA ** (jnp.arange(0, HEAD_DIM, 2, dtype=F32) / HEAD_DIM))
    ang = positions.astype(F32).reshape(-1)[:, None] * inv
    cos, sin = jnp.cos(ang), jnp.sin(ang)
    cosf = jnp.concatenate([cos] * (LANES // (HEAD_DIM // 2)), axis=1)
    sinf = jnp.concatenate([-sin, sin] * (LANES // HEAD_DIM), axis=1)
    return cosf, sinf


def _q_col_scale(n_q, n):
    return jnp.concatenate([jnp.full((n_q,), Q_SCALE, F32), jnp.ones((n - n_q,), F32)])


def _fox_mixer(x, xb, B, S, w_in, b_f):
    D = D_MODEL
    wb = w_in[:, :3 * D].astype(BF16)
    qk = _proj(xb, wb[:, :2 * D], col_scale=_q_col_scale(D, 2 * D))
    vt = _proj_t(wb[:, 2 * D:].T, xb)
    fl = _proj(x, _pad_cols(w_in[:, 3 * D:], LANES), out_dtype=F32, tn=LANES, precision="split3")
    cp = _fox_gate(fl, b_f, B, S)
    return [_fox_attention(qk, cp, vt, B, S)]


def _nsa_mixer(x, xb, B, S, cosf, sinf, w_in, b_gate, pe_k, pe_v, ck_w1, ck_w2, cv_w1, cv_w2):
    D, G, HD = D_MODEL, NSA_KV_GROUPS, HEAD_DIM
    HG = NSA_HEADS_PER_GROUP
    kvw = G * HD
    wb = w_in[:, :D + 6 * kvw].astype(BF16)
    q_c, q_r = _proj(xb, wb[:, :D], mode="both", cos=cosf, sin=sinf, col_scale=_q_col_scale(D, D))
    w_rot = jnp.concatenate([wb[:, D + 2 * kvw:D + 3 * kvw], wb[:, D + 4 * kvw:D + 5 * kvw]], axis=1)
    k_rot = _proj(xb, w_rot, mode="rope", cos=cosf, sin=sinf)
    kvc = _proj(xb, wb[:, D:D + 2 * kvw])
    w_v = jnp.concatenate([wb[:, D + 3 * kvw:D + 4 * kvw], wb[:, D + 5 * kvw:D + 6 * kvw]], axis=1)
    vt = _proj_t(w_v.T, xb)
    gl = _proj(x, _pad_cols(w_in[:, D + 6 * kvw:], LANES), out_dtype=F32, tn=LANES, precision="split3")
    bg = _pad_cols(b_gate.reshape(1, -1), LANES)

    def grouped(t2d):
        t = t2d.reshape(B, S, G, HD).transpose(0, 2, 1, 3)
        return jnp.concatenate([t, t], axis=-1)

    n_chunks = S // NSA_CMP_STRIDE
    n_cmp = n_chunks - NSA_CMP_LEN // NSA_CMP_STRIDE + 1
    ncp = n_chunks

    def compress(t2d, pe, w1, w2):
        ch = t2d.reshape(B, n_chunks, NSA_CMP_STRIDE, G, HD).transpose(0, 1, 3, 2, 4)
        ch = ch.reshape(B, n_chunks, G, NSA_CMP_STRIDE * HD)
        flat = jnp.concatenate([ch[:, :n_cmp], ch[:, 1:n_cmp + 1]], axis=-1)
        flat = jnp.pad(flat, ((0, 0), (0, ncp - n_cmp), (0, 0), (0, 0))).reshape(B * ncp * G, -1)
        out = _nsa_compress(flat, pe.reshape(1, -1), w1.astype(BF16), _pad_cols(w2, LANES).astype(BF16))
        return out[:, :HD].reshape(B, ncp, G, HD).astype(BF16)

    kc = compress(kvc[:, :kvw], pe_k, ck_w1, ck_w2).transpose(0, 2, 1, 3)
    kk_c = jnp.concatenate([kc, kc], axis=-1)
    vt_c = compress(kvc[:, kvw:], pe_v, cv_w1, cv_w2).transpose(0, 2, 3, 1)

    n_sel = S // NSA_SEL_LEN
    assert n_sel <= LANES
    cmp_start = np.arange(ncp) * NSA_CMP_STRIDE
    sel_start = np.arange(LANES) * NSA_SEL_LEN
    overlap = ((cmp_start[:, None] < sel_start[None, :] + NSA_SEL_LEN)
               & (cmp_start[:, None] + NSA_CMP_LEN > sel_start[None, :])
               & (np.arange(ncp)[:, None] < n_cmp) & (np.arange(LANES)[None, :] < n_sel))
    overlap_t = jnp.asarray(overlap.T, BF16)
    ex = np.zeros((3, G, LANES, HG * HD), np.float32)
    for br in range(3):
        for g in range(G):
            for hg in range(HG):
                ex[br, g, (g * HG + hg) * 3 + br, hg * HD:(hg + 1) * HD] = 1.0
    ex = jnp.asarray(ex, BF16)

    o_c, selneg = _nsa_cmp_branch(q_c, kk_c, vt_c, overlap_t, gl, bg, ex[0], B, S)
    onehot = jnp.asarray((np.arange(S)[:, None] // NSA_SEL_LEN) == np.arange(LANES)[None, :], BF16)
    kk_s = grouped(k_rot[:, :kvw])
    kk_w = grouped(k_rot[:, kvw:])
    o_s = _nsa_kv_branch("sel", q_r, kk_s, vt[:kvw], gl, bg, ex[1], B, S, onehot=onehot, selneg=selneg)
    o_w = _nsa_kv_branch("win", q_r, kk_w, vt[kvw:], gl, bg, ex[2], B, S)
    return [o_c, o_s, o_w]


def _moba_mixer(xb, B, S, cosf, sinf, w_in):
    D, H, HD = D_MODEL, N_HEADS, HEAD_DIM
    wb = w_in.astype(BF16)
    qk_rot = _proj(xb, wb[:, :2 * D], mode="rope", cos=cosf, sin=sinf, col_scale=_q_col_scale(D, 2 * D))
    vt = _proj_t(wb[:, 2 * D:].T, xb)
    nblk = S // MOBA_BLOCK
    assert S % MOBA_BLOCK == 0 and 2 * nblk <= LANES and nblk % SUBLANES == 0
    kmean = _moba_kmean(qk_rot[:, D:]).reshape(B, nblk, H // 2, 2, HD)
    km = kmean.transpose(0, 2, 3, 1, 4)
    rmat_t = jnp.zeros((B, H // 2, LANES, 2, HD), F32)
    rmat_t = rmat_t.at[:, :, :nblk, 0].set(km[:, :, 0]).at[:, :, nblk:2 * nblk, 1].set(km[:, :, 1])
    selneg = _moba_select(qk_rot, rmat_t.reshape(B, H // 2, LANES, LANES), B, S)
    blk_of = np.arange(S) // MOBA_BLOCK
    oh = np.zeros((S, LANES), np.float32)
    oh[np.arange(S), blk_of] = 1.0
    oh[np.arange(S), nblk + blk_of] = 1.0
    return [_moba_attention(qk_rot, vt, jnp.asarray(oh, BF16), selneg, B, S)]


def _sc_invert_kernel(dest_hbm, tok_hbm, out_hbm, d_v, t_v, buf_v, *, seg, chunk, n_chunks, n_tok):
    lo = pl.multiple_of((lax.axis_index("s") * SC_CORES + lax.axis_index("c")) * seg, SC_LANES)

    @pl.loop(0, seg // SC_LANES)
    def _(j):
        row = lo + j * SC_LANES + lax.iota(jnp.int32, SC_LANES)
        buf_v[pl.ds(j * SC_LANES, SC_LANES)] = lax.rem(row, jnp.int32(n_tok))

    @pl.loop(0, n_chunks)
    def _(c):
        pltpu.sync_copy(dest_hbm.at[pl.ds(c * chunk, chunk)], d_v)
        pltpu.sync_copy(tok_hbm.at[pl.ds(c * chunk, chunk)], t_v)

        @pl.loop(0, chunk // SC_LANES)
        def _(j):
            d = d_v[pl.ds(j * SC_LANES, SC_LANES)] - lo
            mine = (d >= 0) & (d < seg)
            plsc.store_scatter(buf_v, [jnp.where(mine, d, 0)], t_v[pl.ds(j * SC_LANES, SC_LANES)], mask=mine)

    pltpu.sync_copy(buf_v, out_hbm.at[pl.ds(lo, seg)])


def _invert_dest(dest, tok, P, n_tok):
    R = dest.shape[0]
    n_workers = SC_CORES * SC_SUBCORES
    seg = P // n_workers
    chunk = min(SC_INDEX_CHUNK, R)
    assert P == seg * n_workers and seg % SC_LANES == 0 and R % chunk == 0
    mesh = plsc.VectorSubcoreMesh(core_axis_name="c", subcore_axis_name="s",
                                  num_cores=SC_CORES, num_subcores=SC_SUBCORES)
    return pl.kernel(
        functools.partial(_sc_invert_kernel, seg=seg, chunk=chunk, n_chunks=R // chunk, n_tok=n_tok),
        out_type=jax.ShapeDtypeStruct((P,), jnp.int32), mesh=mesh,
        scratch_types=[pltpu.VMEM((chunk,), jnp.int32), pltpu.VMEM((chunk,), jnp.int32),
                       pltpu.VMEM((seg,), jnp.int32)],
        compiler_params=pltpu.CompilerParams(needs_layout_passes=False),
        name="moe_invert")(dest, tok)


def _sc_mesh():
    return plsc.VectorSubcoreMesh(core_axis_name="c", subcore_axis_name="s",
                                  num_cores=SC_CORES, num_subcores=SC_SUBCORES)


def _gather_rows(table, idx):
    parts, V, _ = table.shape
    N = idx.shape[0]
    n_pieces = N * parts
    n_workers = SC_CORES * SC_SUBCORES
    assert n_pieces % (SC_WINDOW * n_workers) == 0
    pieces = (jnp.arange(parts, dtype=jnp.int32)[:, None] * V + idx[None, :]).reshape(1, n_pieces)

    def kernel_body(x_hbm, i_hbm, o_hbm):
        def body(i_vmem, o_vmem):
            pltpu.sync_copy(x_hbm.at[i_vmem.at[0]], o_vmem)

        pltpu.emit_pipeline(
            body, grid=(n_pieces // SC_WINDOW,),
            in_specs=[pl.BlockSpec((1, SC_WINDOW), lambda i: (0, i))],
            out_specs=[pl.BlockSpec((SC_WINDOW, SC_ROW_WORDS), lambda i: (i, 0))],
            core_axis_name=("c", "s"), dimension_semantics=(pltpu.PARALLEL,))(i_hbm, o_hbm)

    out = pl.kernel(kernel_body, out_type=jax.ShapeDtypeStruct((n_pieces, SC_ROW_WORDS), table.dtype),
                    mesh=_sc_mesh(), scratch_types=[], name="moe_gather")(
                        table.reshape(parts * V, SC_ROW_WORDS), pieces)
    return out.reshape(parts, N, SC_ROW_WORDS)


def _moe_dispatch(x1, x1p, router_w, router_b, wg, wu, wd, layer):
    T, D = x1.shape
    eidx_t, gw_t = _router(x1, router_w, router_b)
    rank_t, cnt = _expert_ranks(eidx_t)
    counts = cnt[:, 0].astype(jnp.int32)
    padded = (counts + MOE_BLOCK - 1) // MOE_BLOCK * MOE_BLOCK
    pends = jnp.cumsum(padded)
    pstarts = pends - padded
    R = T * TOP_K
    parts = D // 2 // SC_ROW_WORDS
    unit = math.lcm(EXPERT_SLOTS * MOE_BLOCK, SC_CORES * SC_SUBCORES * SC_LANES,
                    SC_CORES * SC_SUBCORES * SC_WINDOW // math.gcd(parts, SC_WINDOW))
    P = -(-(R + N_EXPERTS * (MOE_BLOCK - 1)) // unit) * unit
    NB = P // MOE_BLOCK
    blk_start = jnp.arange(NB, dtype=jnp.int32) * MOE_BLOCK
    blk_e = jnp.minimum(jnp.sum(pends[None, :] <= blk_start[:, None], axis=1), N_EXPERTS - 1).astype(jnp.int32)
    nb_used = (pends[-1] // MOE_BLOCK).astype(jnp.int32).reshape(1)
    dest_t = _expert_dest(eidx_t, rank_t, pstarts)

    tok = jnp.broadcast_to(jnp.arange(T, dtype=jnp.int32)[None, :], (TOP_K, T))
    buf_tok = _invert_dest(dest_t.reshape(-1), tok.reshape(-1), P, T)
    xs = _gather_rows(x1p, buf_tok)
    yb = _expert_ffn(blk_e, nb_used, xs, wg, wu, wd, layer)
    yg = _gather_rows(yb, dest_t.reshape(-1))
    return yg, gw_t.T


def kernel(x, p, positions, fox_w_in, fox_b_f, fox_w_out, nsa_w_in, nsa_b_gate, nsa_pe_k, nsa_pe_v,
           nsa_cmp_k_w1, nsa_cmp_k_w2, nsa_cmp_v_w1, nsa_cmp_v_w2, nsa_w_out, moba_w_in, moba_w_out,
           ln1_g, ln1_b, router_w, router_b, exp_w_gate, exp_w_up, exp_w_down,
           sh_w_gate, sh_w_up, sh_w_down, ln2_g, ln2_b, ple_w_gate, ple_w_proj):
    B, S, D = x.shape
    T = B * S
    depth = p.shape[0]
    cosf, sinf = _rope_tables(positions)
    xt = x.reshape(T, D)
    xtb = xt.astype(BF16)
    for i in range(depth):
        kind, j = i % N_MIXERS, i // N_MIXERS
        if kind == 0:
            o_list = _fox_mixer(xt, xtb, B, S, fox_w_in[j], fox_b_f[j])
            w_out = fox_w_out[j]
        elif kind == 1:
            o_list = _nsa_mixer(xt, xtb, B, S, cosf, sinf, nsa_w_in[j], nsa_b_gate[j], nsa_pe_k[j], nsa_pe_v[j],
                                nsa_cmp_k_w1[j], nsa_cmp_k_w2[j], nsa_cmp_v_w1[j], nsa_cmp_v_w2[j])
            w_out = nsa_w_out[j]
        else:
            o_list = _moba_mixer(xtb, B, S, cosf, sinf, moba_w_in[j])
            w_out = moba_w_out[j]
        x1, x1p = _outproj_ln(o_list, w_out.astype(BF16), xt, ln1_g[i], ln1_b[i])
        shared, proj = _shared_ple(x1, p[i].reshape(T, -1), sh_w_gate[i].astype(BF16), sh_w_up[i].astype(BF16),
                                   sh_w_down[i].astype(BF16), ple_w_proj[i].astype(BF16))
        yg, gw = _moe_dispatch(x1, x1p, router_w[i], router_b[i], exp_w_gate, exp_w_up, exp_w_down, i)
        xt, xtb = _combine(x1, yg, gw, shared, proj, ln2_g[i], ln2_b[i], ple_w_gate[i].astype(BF16))
    return xt.reshape(B, S, D)
```
